```python
import math
import jax
import jax.numpy as jnp
from jax import lax
import numpy as np

D_MODEL = 2048
BATCH = 1
SEQ = 16384
DEPTH = 4

N_A_LAYERS = DEPTH // 2
N_B_LAYERS = DEPTH - N_A_LAYERS
RMS_EPS = 1e-6

GDN_HEADS = 16
GDN_DK = 128
GDN_DV = 128
GDN_CONV = 4
GDN_CHUNK = 64
GDN_QK_W = GDN_HEADS * GDN_DK
GDN_V_W = GDN_HEADS * GDN_DV
GDN_CONV_W = 2 * GDN_QK_W + GDN_V_W
GDN_PROJ = GDN_CONV_W + GDN_V_W + 2 * GDN_HEADS

DSA_GROUPS = ((128, 1), (512, 4), (2048, 16))
N_GROUPS = len(DSA_GROUPS)
DSA_HEADS = 16
DSA_DH = 128
DSA_SPAN = 128
DSA_GROUP_W = DSA_HEADS * DSA_DH

NUM_BUCKETS = 32
MAX_DISTANCE = 2048

D_FF = -(-8 * D_MODEL // (3 * 256)) * 256

kernel_name = 'yoco_gdn_dilated_swa_hybrid'


def rms_norm(x, gain):
    xf = x.astype(jnp.float32)
    y = xf * lax.rsqrt(jnp.mean(xf * xf, axis=-1, keepdims=True) + RMS_EPS)
    return (y * gain.astype(jnp.float32)).astype(x.dtype)


def l2norm(x):
    return x * lax.rsqrt(jnp.sum(x * x, axis=-1, keepdims=True) + RMS_EPS)


def swiglu_ffn(h, w_gate_up, w_down):
    gate, up = jnp.split(h @ w_gate_up, 2, axis=-1)
    return (jax.nn.silu(gate) * up) @ w_down


def causal_conv_silu(x, w):
    K, C = w.shape
    y = lax.conv_general_dilated(x, w[:, None, :].astype(x.dtype), window_strides=(1,),
                                 padding=((K - 1, 0),), dimension_numbers=('NWC', 'WIO', 'NWC'),
                                 feature_group_count=C)
    return jax.nn.silu(y)


def chunk_gated_delta_rule(q, k, v, g, beta):
    B, S, H, Dk = q.shape
    Dv = v.shape[-1]
    C = GDN_CHUNK
    N = S // C

    def to_chunks(t):
        t = t.reshape((B, N, C, H) + t.shape[3:])
        return jnp.moveaxis(t, (1, 3), (0, 2))

    qc = to_chunks(q * (Dk ** -0.5))
    kc = to_chunks(k)
    vc = to_chunks(v)
    gc = to_chunks(g)
    bc = to_chunks(beta)
    gcum = jnp.cumsum(gc, axis=-1)
    idx = jnp.arange(C)
    causal = idx[:, None] >= idx[None, :]
    strict = idx[:, None] > idx[None, :]
    decay = jnp.exp(jnp.where(causal, gcum[..., :, None] - gcum[..., None, :], -jnp.inf))
    kb = kc * bc[..., None]
    low = jnp.where(strict, jnp.einsum('nbhid,nbhjd->nbhij', kb, kc) * decay, 0.0)
    a_mat = low + jnp.eye(C, dtype=jnp.float32)
    rhs = jnp.concatenate([vc * bc[..., None], kb * jnp.exp(gcum)[..., None]], axis=-1)
    sol = lax.linalg.triangular_solve(a_mat, rhs, left_side=True, lower=True, unit_diagonal=True)
    u = sol[..., :Dv]
    w = sol[..., Dv:]
    intra = jnp.einsum('nbhid,nbhjd->nbhij', qc, kc) * decay
    q_dec = qc * jnp.exp(gcum)[..., None]
    k_dec = kc * jnp.exp(gcum[..., -1:] - gcum)[..., None]
    g_last = jnp.exp(gcum[..., -1])

    def step(state, inp):
        q_i, k_i, u_i, w_i, a_i, gl = inp
        v_new = u_i - jnp.einsum('bhck,bhkv->bhcv', w_i, state)
        o = jnp.einsum('bhck,bhkv->bhcv', q_i, state) + jnp.einsum('bhij,bhjv->bhiv', a_i, v_new)
        state = state * gl[..., None, None] + jnp.einsum('bhck,bhcv->bhkv', k_i, v_new)
        return state, o

    state0 = jnp.zeros((B, H, Dk, Dv), jnp.float32)
    _, o = lax.scan(step, state0, (q_dec, k_dec, u, w, intra, g_last))
    return jnp.moveaxis(o, (0, 2), (1, 3)).reshape(B, S, H, Dv)


def gdn_mixer(h, w_in, conv_w, a_log, dt_bias, out_norm, w_out):
    B, S, _ = h.shape
    proj = h @ w_in
    qkv = causal_conv_silu(proj[..., :GDN_CONV_W], conv_w).astype(jnp.float32)
    z = proj[..., GDN_CONV_W:GDN_CONV_W + GDN_V_W]
    b = proj[..., GDN_CONV_W + GDN_V_W:GDN_CONV_W + GDN_V_W + GDN_HEADS]
    a = proj[..., GDN_CONV_W + GDN_V_W + GDN_HEADS:]
    q = l2norm(qkv[..., :GDN_QK_W].reshape(B, S, GDN_HEADS, GDN_DK))
    k = l2norm(qkv[..., GDN_QK_W:2 * GDN_QK_W].reshape(B, S, GDN_HEADS, GDN_DK))
    v = qkv[..., 2 * GDN_QK_W:].reshape(B, S, GDN_HEADS, GDN_DV)
    beta = jax.nn.sigmoid(b.astype(jnp.float32))
    g = -jnp.exp(a_log.astype(jnp.float32)) * jax.nn.softplus(a.astype(jnp.float32) + dt_bias.astype(jnp.float32))
    o = chunk_gated_delta_rule(q, k, v, g, beta)
    zf = z.astype(jnp.float32).reshape(B, S, GDN_HEADS, GDN_DV)
    o = rms_norm(o, out_norm) * jax.nn.silu(zf)
    return o.reshape(B, S, GDN_V_W).astype(h.dtype) @ w_out


def t5_bucket(dist):
    max_exact = NUM_BUCKETS // 2
    d_f = jnp.maximum(dist, 1).astype(jnp.float32)
    large = max_exact + (jnp.log(d_f / max_exact) / math.log(MAX_DISTANCE / max_exact)
                         * (NUM_BUCKETS - max_exact)).astype(jnp.int32)
    large = jnp.minimum(large, NUM_BUCKETS - 1)
    return jnp.where(dist < max_exact, dist, large)


def dilated_attention(q, k, v, bias_table, dilation):
    B, S, H, Dh = q.shape
    W = DSA_SPAN
    L = S // dilation
    nb = -(-L // W)
    Lp = nb * W

    def to_sub(t):
        t = t.reshape(B, L, dilation, H, Dh).transpose(0, 2, 1, 3, 4)
        t = jnp.pad(t, ((0, 0), (0, 0), (0, Lp - L), (0, 0), (0, 0)))
        return t.reshape(B, dilation, nb, W, H, Dh)

    def with_prev(t):
        prev = jnp.pad(t, ((0, 0), (0, 0), (1, 0), (0, 0), (0, 0), (0, 0)))[:, :, :-1]
        return jnp.concatenate([prev, t], axis=3)

    qb = to_sub(q)
    kk = with_prev(to_sub(k))
    vv = with_prev(to_sub(v))
    s = jnp.einsum('brnqhd,brnkhd->brnhqk', qb, kk, preferred_element_type=jnp.float32) * (Dh ** -0.5)
    qi = jnp.arange(W)[:, None]
    kj = jnp.arange(2 * W)[None, :]
    rel = qi + W - kj
    band = (rel >= 0) & (rel <= W)
    bucket = t5_bucket(jnp.clip(rel, 0, W) * dilation)
    bias = jnp.take(bias_table, bucket, axis=0).transpose(2, 0, 1).astype(jnp.float32)
    key_pos = jnp.arange(nb)[:, None] * W - W + kj
    valid = band[None] & (key_pos >= 0)[:, None, :]
    s = jnp.where(valid[None, None, :, None], s + bias[None, None, None], -jnp.inf)
    lse = jax.nn.logsumexp(s, axis=-1)
    p = jnp.exp(s - lse[..., None])
    o = jnp.einsum('brnhqk,brnkhd->brnqhd', p, vv.astype(jnp.float32))
    o = o.reshape(B, dilation, Lp, H, Dh)[:, :, :L].transpose(0, 2, 1, 3, 4).reshape(B, S, H, Dh)
    lse = jnp.swapaxes(lse, -1, -2).reshape(B, dilation, Lp, H)[:, :, :L]
    lse = lse.transpose(0, 2, 1, 3).reshape(B, S, H)
    return o, lse


def dsa_mixer(h, w_q, k_all, v_all, rel_bias, w_out):
    B, S, _ = h.shape
    q_all = (h @ w_q).reshape(B, S, N_GROUPS, DSA_HEADS, DSA_DH)
    outs = []
    lses = []
    for gi, (window, dilation) in enumerate(DSA_GROUPS):
        o, lse = dilated_attention(q_all[:, :, gi], k_all[:, :, gi], v_all[:, :, gi],
                                   rel_bias[:, gi * DSA_HEADS:(gi + 1) * DSA_HEADS], dilation)
        outs.append(o)
        lses.append(lse)
    wts = jax.nn.softmax(jnp.stack(lses, axis=0), axis=0)
    o = jnp.sum(wts[..., None] * jnp.stack(outs, axis=0), axis=0)
    return o.reshape(B, S, DSA_GROUP_W).astype(h.dtype) @ w_out


def shared_kv(x, kv_norm, kv_w):
    B, S, _ = x.shape
    kv = (rms_norm(x, kv_norm) @ kv_w).reshape(B, S, 2, N_GROUPS, DSA_HEADS, DSA_DH)
    return kv[:, :, 0], kv[:, :, 1]


def setup_inputs(seed: int = 0) -> dict:
    key = jax.random.key(seed)
    ks = jax.random.split(key, 16)
    f32 = jnp.float32

    def dense(k, shape):
        return jax.random.normal(k, shape, f32) * shape[-2] ** -0.5

    x = jax.random.normal(ks[0], (BATCH, SEQ, D_MODEL), f32)
    norm_gains = 1.0 + 0.02 * jax.random.normal(ks[1], (DEPTH, 4, D_MODEL), f32)
    ffn_w_gate_up = dense(ks[2], (DEPTH, D_MODEL, 2 * D_FF))
    ffn_w_down = dense(ks[3], (DEPTH, D_FF, D_MODEL))
    gdn_w_in = dense(ks[4], (N_A_LAYERS, D_MODEL, GDN_PROJ))
    gdn_conv_w = jax.random.normal(ks[5], (N_A_LAYERS, GDN_CONV, GDN_CONV_W), f32) * GDN_CONV ** -0.5
    gdn_a_log = jnp.log(jax.random.uniform(ks[6], (N_A_LAYERS, GDN_HEADS), f32, 1.0, 16.0))
    dt = jnp.exp(jax.random.uniform(ks[7], (N_A_LAYERS, GDN_HEADS), f32, math.log(1e-3), math.log(1e-1)))
    gdn_dt_bias = dt + jnp.log(-jnp.expm1(-dt))
    gdn_out_norm = 1.0 + 0.02 * jax.random.normal(ks[8], (N_A_LAYERS, GDN_DV), f32)
    gdn_w_out = dense(ks[9], (N_A_LAYERS, GDN_V_W, D_MODEL))
    kv_norm = 1.0 + 0.02 * jax.random.normal(ks[10], (D_MODEL,), f32)
    kv_w = dense(ks[11], (D_MODEL, 2 * N_GROUPS * DSA_GROUP_W))
    dsa_w_q = dense(ks[12], (N_B_LAYERS, D_MODEL, N_GROUPS * DSA_GROUP_W))
    dsa_w_out = dense(ks[13], (N_B_LAYERS, DSA_GROUP_W, D_MODEL))
    rel_bias = 0.1 * jax.random.normal(ks[14], (NUM_BUCKETS, N_GROUPS * DSA_HEADS), f32)
    return {'x': x, 'norm_gains': norm_gains, 'ffn_w_gate_up': ffn_w_gate_up, 'ffn_w_down': ffn_w_down,
            'gdn_w_in': gdn_w_in, 'gdn_conv_w': gdn_conv_w, 'gdn_a_log': gdn_a_log,
            'gdn_dt_bias': gdn_dt_bias, 'gdn_out_norm': gdn_out_norm, 'gdn_w_out': gdn_w_out,
            'kv_norm': kv_norm, 'kv_w': kv_w, 'dsa_w_q': dsa_w_q, 'dsa_w_out': dsa_w_out,
            'rel_bias': rel_bias}


def reference(x, norm_gains, ffn_w_gate_up, ffn_w_down, gdn_w_in, gdn_conv_w, gdn_a_log,
              gdn_dt_bias, gdn_out_norm, gdn_w_out, kv_norm, kv_w, dsa_w_q, dsa_w_out, rel_bias):
    k_all = None
    v_all = None
    for layer in range(DEPTH):
        gains = norm_gains[layer]
        h = rms_norm(x, gains[0])
        if layer < N_A_LAYERS:
            m = gdn_mixer(h, gdn_w_in[layer], gdn_conv_w[layer], gdn_a_log[layer], gdn_dt_bias[layer],
                          gdn_out_norm[layer], gdn_w_out[layer])
        else:
            j = layer - N_A_LAYERS
            m = dsa_mixer(h, dsa_w_q[j], k_all, v_all, rel_bias, dsa_w_out[j])
        x = x + rms_norm(m, gains[1])
        f = swiglu_ffn(rms_norm(x, gains[2]), ffn_w_gate_up[layer], ffn_w_down[layer])
        x = x + rms_norm(f, gains[3])
        if layer == N_A_LAYERS - 1:
            k_all, v_all = shared_kv(x, kv_norm, kv_w)
    return x
```

```python
import functools
import math

import jax
import jax.numpy as jnp
from jax import lax
from jax.experimental import pallas as pl
from jax.experimental.pallas import tpu as pltpu

F32 = jnp.float32
BF16 = jnp.bfloat16

RMS_EPS = 1e-6
D_MODEL = 2048

GDN_HEADS = 16
GDN_DK = 128
GDN_DV = 128
GDN_CONV = 4
GDN_QK_W = GDN_HEADS * GDN_DK
GDN_V_W = GDN_HEADS * GDN_DV
GDN_CONV_W = 2 * GDN_QK_W + GDN_V_W
GDN_CHUNK = 128
GDN_HEAD_BLOCK = 16
GDN_INV_BASE = 8
CONV_HALO_ROWS = 8

DSA_GROUPS = ((128, 1), (512, 4), (2048, 16))
N_GROUPS = len(DSA_GROUPS)
DSA_HEADS = 16
DSA_DH = 128
DSA_SPAN = 128
DSA_GROUP_W = DSA_HEADS * DSA_DH
NUM_BUCKETS = 32
MAX_DISTANCE = 2048
MASK_VALUE = -1e30

LANES = 128
VMEM_LIMIT_BYTES = 56 * 1024 * 1024


def _compiler_params(semantics):
    return pltpu.CompilerParams(dimension_semantics=semantics, vmem_limit_bytes=VMEM_LIMIT_BYTES)


def _rms(x, gain):
    ms = jnp.mean(x * x, axis=-1, keepdims=True)
    return x * lax.rsqrt(ms + RMS_EPS) * gain


def _sigmoid(x):
    return 1.0 / (1.0 + jnp.exp(-x))


def _bdot(a, b):
    return jnp.dot(a.astype(BF16), b.astype(BF16), preferred_element_type=F32)


def _bdot_nt(a, b):
    return lax.dot_general(a.astype(BF16), b.astype(BF16), (((1,), (1,)), ((), ())),
                           preferred_element_type=F32)


def _bdot_tn(a, b):
    return lax.dot_general(a.astype(BF16), b.astype(BF16), (((0,), (0,)), ((), ())),
                           preferred_element_type=F32)


def _norm_matmul_kernel(x_ref, g_ref, w_ref, o_ref, xn_ref):
    @pl.when(pl.program_id(1) == 0)
    def _():
        xn_ref[...] = _rms(x_ref[...], g_ref[...]).astype(BF16)

    o_ref[...] = jnp.dot(xn_ref[...], w_ref[...], preferred_element_type=F32).astype(o_ref.dtype)


def _norm_matmul(x, gain, w, *, tm, tn, out_dtype):
    s, d = x.shape
    n = w.shape[1]
    return pl.pallas_call(
        _norm_matmul_kernel,
        grid=(s // tm, n // tn),
        in_specs=[pl.BlockSpec((tm, d), lambda i, j: (i, 0)),
                  pl.BlockSpec((1, d), lambda i, j: (0, 0)),
                  pl.BlockSpec((d, tn), lambda i, j: (0, j))],
        out_specs=pl.BlockSpec((tm, tn), lambda i, j: (i, j)),
        out_shape=jax.ShapeDtypeStruct((s, n), out_dtype),
        scratch_shapes=[pltpu.VMEM((tm, d), BF16)],
        compiler_params=_compiler_params(("parallel", "arbitrary")),
        name="norm_matmul",
    )(x, gain.reshape(1, d), w)


def _gdn_gates_kernel(x_ref, g_ref, w_ref, alog_ref, dtb_ref, o_ref):
    xn = _rms(x_ref[...], g_ref[...]).astype(BF16)
    p = jnp.dot(xn, w_ref[...], preferred_element_type=F32)
    lane = lax.broadcasted_iota(jnp.int32, p.shape, 1)
    beta = _sigmoid(p)
    a = p + dtb_ref[...]
    softplus = jnp.maximum(a, 0.0) + jnp.log(1.0 + jnp.exp(-jnp.abs(a)))
    g = -jnp.exp(alog_ref[...]) * softplus
    o_ref[...] = jnp.where(lane < GDN_HEADS, beta, jnp.where(lane < 2 * GDN_HEADS, g, 0.0))


def _gdn_gates(x, gain, w_ba, a_log, dt_bias, *, tm):
    s, d = x.shape
    pad = LANES - 2 * GDN_HEADS
    w = jnp.pad(w_ba, ((0, 0), (0, pad))).astype(BF16)
    zeros_h = jnp.zeros((GDN_HEADS,), F32)
    alog = jnp.pad(jnp.concatenate([zeros_h, a_log.astype(F32)]), (0, pad)).reshape(1, LANES)
    dtb = jnp.pad(jnp.concatenate([zeros_h, dt_bias.astype(F32)]), (0, pad)).reshape(1, LANES)
    return pl.pallas_call(
        _gdn_gates_kernel,
        grid=(s // tm,),
        in_specs=[pl.BlockSpec((tm, d), lambda i: (i, 0)),
                  pl.BlockSpec((1, d), lambda i: (0, 0)),
                  pl.BlockSpec((d, LANES), lambda i: (0, 0)),
                  pl.BlockSpec((1, LANES), lambda i: (0, 0)),
                  pl.BlockSpec((1, LANES), lambda i: (0, 0))],
        out_specs=pl.BlockSpec((tm, LANES), lambda i: (i, 0)),
        out_shape=jax.ShapeDtypeStruct((s, LANES), F32),
        compiler_params=_compiler_params(("parallel",)),
        name="gdn_gates",
    )(x, gain.reshape(1, d), w, alog, dtb)


def _unit_lower_inverse(low, row, col):
    c = low.shape[0]
    eye = (row == col).astype(F32)
    blk = GDN_INV_BASE
    nil = jnp.where((row // blk) == (col // blk), low, 0.0)
    inv = eye - nil
    power = nil
    span = 2
    while span < blk:
        power = _bdot(power, power)
        inv = inv + _bdot(inv, power)
        span *= 2
    while blk < c:
        off = jnp.where(((row // (2 * blk)) == (col // (2 * blk))) & ((row // blk) != (col // blk)), low, 0.0)
        inv = inv - _bdot(inv, _bdot(off, inv))
        blk *= 2
    return inv


def _gdn_kernel(q_ref, k_ref, v_ref, z_ref, qh_ref, kh_ref, vh_ref, cwq_ref, cwk_ref, cwv_ref,
                gb_ref, on_ref, o_ref, state_ref):
    c_len = q_ref.shape[0]
    step = pl.program_id(1)

    @pl.when(step == 0)
    def _():
        state_ref[...] = jnp.zeros_like(state_ref)

    halo_on = (step > 0).astype(F32)

    def conv_silu(x_ref, h_ref, w_ref):
        cur = x_ref[...].astype(F32)
        halo = h_ref[...].astype(F32) * halo_on
        full = jnp.concatenate([halo, cur], axis=0)
        w = w_ref[...]
        y = w[GDN_CONV - 1:GDN_CONV, :] * cur
        for j in range(GDN_CONV - 1):
            off = CONV_HALO_ROWS - (GDN_CONV - 1) + j
            y = y + w[j:j + 1, :] * full[off:off + c_len, :]
        return y * _sigmoid(y)

    q_all = conv_silu(q_ref, qh_ref, cwq_ref)
    k_all = conv_silu(k_ref, kh_ref, cwk_ref)
    v_all = conv_silu(v_ref, vh_ref, cwv_ref)

    row = lax.broadcasted_iota(jnp.int32, (c_len, c_len), 0)
    col = lax.broadcasted_iota(jnp.int32, (c_len, c_len), 1)
    causal = row >= col
    strict = row > col

    gb = gb_ref[...]
    tri = causal.astype(BF16)
    gb_hi = gb.astype(BF16)
    gb_lo = (gb - gb_hi.astype(F32)).astype(BF16)
    gcum = (jnp.dot(tri, gb_hi, preferred_element_type=F32)
            + jnp.dot(tri, gb_lo, preferred_element_type=F32))
    gcum_t = gcum.T

    out_gain = on_ref[...]
    for i in range(GDN_HEAD_BLOCK):
        hs = slice(i * GDN_DK, (i + 1) * GDN_DK)
        qh = q_all[:, hs]
        kh = k_all[:, hs]
        vh = v_all[:, hs]
        qn = qh * (lax.rsqrt(jnp.sum(qh * qh, axis=-1, keepdims=True) + RMS_EPS) * (GDN_DK ** -0.5))
        kn = kh * lax.rsqrt(jnp.sum(kh * kh, axis=-1, keepdims=True) + RMS_EPS)

        beta = gb[:, i:i + 1]
        gcol = gcum[:, GDN_HEADS + i:GDN_HEADS + i + 1]
        grow = gcum_t[GDN_HEADS + i:GDN_HEADS + i + 1, :]
        decay = jnp.exp(jnp.where(causal, gcol - grow, MASK_VALUE))
        e_col = jnp.exp(gcol)
        g_last = gcol[c_len - 1:c_len, :]
        e_last = jnp.exp(g_last)
        k_scale = jnp.exp(g_last - gcol)

        kb = kn * beta
        kk_qk = _bdot_nt(jnp.concatenate([kb, qn], axis=0), kn)
        low = jnp.where(strict, kk_qk[:c_len] * decay, 0.0)
        intra = kk_qk[c_len:] * decay
        inv = _unit_lower_inverse(low, row, col)

        rhs = jnp.concatenate([vh * beta, kb * e_col], axis=1)
        sol = _bdot(inv, rhs)
        u = sol[:, :GDN_DV]
        w = sol[:, GDN_DV:]

        state = state_ref[i]
        ws_qs = _bdot(jnp.concatenate([w, qn * e_col], axis=0), state)
        v_new = u - ws_qs[:c_len]
        o = ws_qs[c_len:] + _bdot(intra, v_new)
        state_ref[i] = state * e_last + _bdot_tn(kn * k_scale, v_new)

        zh = z_ref[:, hs].astype(F32)
        o_ref[:, hs] = (_rms(o, out_gain) * (zh * _sigmoid(zh))).astype(o_ref.dtype)


def _gdn(proj, gb, conv_w, out_norm):
    s = proj.shape[0]
    c = GDN_CHUNK
    hb = GDN_HEAD_BLOCK
    ng = GDN_HEADS // hb
    bw = hb * GDN_DK
    halo_blocks = c // CONV_HALO_ROWS

    def cur(sec):
        return pl.BlockSpec((c, bw), lambda g, t, sec=sec: (t, sec * ng + g))

    def halo(sec):
        return pl.BlockSpec((CONV_HALO_ROWS, bw),
                            lambda g, t, sec=sec: (jnp.maximum(t * halo_blocks - 1, 0), sec * ng + g))

    def cw(sec):
        return pl.BlockSpec((GDN_CONV, bw), lambda g, t, sec=sec: (0, sec * ng + g))

    return pl.pallas_call(
        _gdn_kernel,
        grid=(ng, s // c),
        in_specs=[cur(0), cur(1), cur(2), cur(3), halo(0), halo(1), halo(2), cw(0), cw(1), cw(2),
                  pl.BlockSpec((c, LANES), lambda g, t: (t, g)),
                  pl.BlockSpec((1, GDN_DV), lambda g, t: (0, 0))],
        out_specs=pl.BlockSpec((c, bw), lambda g, t: (t, g)),
        out_shape=jax.ShapeDtypeStruct((s, GDN_V_W), BF16),
        scratch_shapes=[pltpu.VMEM((hb, GDN_DK, GDN_DV), F32)],
        compiler_params=_compiler_params(("parallel", "arbitrary")),
        name="gdn_delta_rule",
    )(proj, proj, proj, proj, proj, proj, proj, conv_w, conv_w, conv_w, gb, out_norm.reshape(1, GDN_DV))


def _matmul_norm_res_kernel(a_ref, w_ref, g_ref, x_ref, o_ref):
    m = jnp.dot(a_ref[...], w_ref[...], preferred_element_type=F32)
    o_ref[...] = x_ref[...] + _rms(m, g_ref[...])


def _matmul_norm_res(a, w, gain, x, *, tm):
    s, k = a.shape
    d = w.shape[1]
    return pl.pallas_call(
        _matmul_norm_res_kernel,
        grid=(s // tm,),
        in_specs=[pl.BlockSpec((tm, k), lambda i: (i, 0)),
                  pl.BlockSpec((k, d), lambda i: (0, 0)),
                  pl.BlockSpec((1, d), lambda i: (0, 0)),
                  pl.BlockSpec((tm, d), lambda i: (i, 0))],
        out_specs=pl.BlockSpec((tm, d), lambda i: (i, 0)),
        out_shape=jax.ShapeDtypeStruct((s, d), F32),
        compiler_params=_compiler_params(("parallel",)),
        name="matmul_norm_res",
    )(a, w, gain.reshape(1, d), x)


def _ffn_kernel(x_ref, gpre_ref, wg_ref, wu_ref, wd_ref, gpost_ref, o_ref, xn_ref, acc_ref):
    j = pl.program_id(1)

    @pl.when(j == 0)
    def _():
        xn_ref[...] = _rms(x_ref[...], gpre_ref[...]).astype(BF16)
        acc_ref[...] = jnp.zeros_like(acc_ref)

    xn = xn_ref[...]
    gate = jnp.dot(xn, wg_ref[...], preferred_element_type=F32)
    up = jnp.dot(xn, wu_ref[...], preferred_element_type=F32)
    act = (gate * _sigmoid(gate) * up).astype(BF16)
    acc_ref[...] += jnp.dot(act, wd_ref[...], preferred_element_type=F32)

    @pl.when(j == pl.num_programs(1) - 1)
    def _():
        o_ref[...] = x_ref[...] + _rms(acc_ref[...], gpost_ref[...])


def _ffn(x, g_pre, w_gate_up, w_down, g_post, *, tm, tf):
    s, d = x.shape
    d_ff = w_down.shape[0]
    nf = d_ff // tf
    return pl.pallas_call(
        _ffn_kernel,
        grid=(s // tm, nf),
        in_specs=[pl.BlockSpec((tm, d), lambda i, j: (i, 0)),
                  pl.BlockSpec((1, d), lambda i, j: (0, 0)),
                  pl.BlockSpec((d, tf), lambda i, j: (0, j)),
                  pl.BlockSpec((d, tf), lambda i, j: (0, nf + j)),
                  pl.BlockSpec((tf, d), lambda i, j: (j, 0)),
                  pl.BlockSpec((1, d), lambda i, j: (0, 0))],
        out_specs=pl.BlockSpec((tm, d), lambda i, j: (i, 0)),
        out_shape=jax.ShapeDtypeStruct((s, d), F32),
        scratch_shapes=[pltpu.VMEM((tm, d), BF16), pltpu.VMEM((tm, d), F32)],
        compiler_params=_compiler_params(("parallel", "arbitrary")),
        name="swiglu_ffn",
    )(x, g_pre.reshape(1, d), w_gate_up, w_gate_up, w_down, g_post.reshape(1, d))


def _t5_bucket(dist):
    max_exact = NUM_BUCKETS // 2
    d_f = jnp.maximum(dist, 1).astype(F32)
    large = max_exact + (jnp.log(d_f / max_exact) / math.log(MAX_DISTANCE / max_exact)
                         * (NUM_BUCKETS - max_exact)).astype(jnp.int32)
    large = jnp.minimum(large, NUM_BUCKETS - 1)
    return jnp.where(dist < max_exact, dist, large)


def _band_buckets(dilation):
    w = DSA_SPAN
    qi = jnp.arange(w)[:, None]
    kj = jnp.arange(2 * w)[None, :]
    rel = qi + w - kj
    band = (rel >= 0) & (rel <= w)
    return jnp.where(band, _t5_bucket(jnp.clip(rel, 0, w) * dilation), -1).astype(jnp.int32)


def _dsa_kernel(tab_ref, bucket_ref, q_ref, kp_ref, kc_ref, vp_ref, vc_ref, o_ref, lse_ref, bias_ref, *, group):
    n = pl.program_id(1)
    w = DSA_SPAN

    @pl.when((pl.program_id(0) == 0) & (n == 0))
    def _():
        bucket = bucket_ref[...]

        def per_head(h, carry):
            b = jnp.full(bucket.shape, MASK_VALUE, F32)
            for t in range(NUM_BUCKETS):
                b = jnp.where(bucket == t, tab_ref[t, group * DSA_HEADS + h], b)
            bias_ref[h] = b
            return carry

        lax.fori_loop(0, DSA_HEADS, per_head, 0)

    first = jnp.where(n == 0, MASK_VALUE, 0.0).astype(F32)
    key_col = lax.broadcasted_iota(jnp.int32, (w, 2 * w), 1)
    prev_mask = jnp.where(key_col < w, first, 0.0)
    lane = lax.broadcasted_iota(jnp.int32, (w, LANES), 1)
    lse_all = jnp.zeros((w, LANES), F32)
    scale = DSA_DH ** -0.5
    for h in range(DSA_HEADS):
        hs = slice(h * DSA_DH, (h + 1) * DSA_DH)
        q = q_ref[:, hs]
        k = jnp.concatenate([kp_ref[:, hs], kc_ref[:, hs]], axis=0)
        v = jnp.concatenate([vp_ref[:, hs], vc_ref[:, hs]], axis=0)
        s = _bdot_nt(q, k) * scale + bias_ref[h] + prev_mask
        m = jnp.max(s, axis=-1, keepdims=True)
        p = jnp.exp(s - m)
        l = jnp.sum(p, axis=-1, keepdims=True)
        o = _bdot(p, v) / l
        o_ref[:, hs] = o.astype(o_ref.dtype)
        lse_all = jnp.where(lane == h, m + jnp.log(l), lse_all)
    lse_ref[...] = lse_all


def _dsa_group(q_all, kv, rel_bias, group):
    s = q_all.shape[0]
    dilation = DSA_GROUPS[group][1]
    w = DSA_SPAN
    gw = DSA_GROUP_W
    sub_len = s // dilation
    nb = sub_len // w
    qv = q_all.reshape(sub_len, dilation * N_GROUPS * gw)
    kvv = kv.reshape(sub_len, dilation * 2 * N_GROUPS * gw)

    def kv_spec(is_v, prev):
        def imap(r, n):
            row = jnp.maximum(n - 1, 0) if prev else n
            return (row, r * 2 * N_GROUPS + is_v * N_GROUPS + group)
        return pl.BlockSpec((w, gw), imap)

    o, lse = pl.pallas_call(
        functools.partial(_dsa_kernel, group=group),
        grid=(dilation, nb),
        in_specs=[pl.BlockSpec(memory_space=pltpu.SMEM),
                  pl.BlockSpec((w, 2 * w), lambda r, n: (0, 0)),
                  pl.BlockSpec((w, gw), lambda r, n: (n, r * N_GROUPS + group)),
                  kv_spec(0, True), kv_spec(0, False), kv_spec(1, True), kv_spec(1, False)],
        out_specs=[pl.BlockSpec((w, gw), lambda r, n: (n, r)),
                   pl.BlockSpec((w, LANES), lambda r, n: (n, r))],
        out_shape=[jax.ShapeDtypeStruct((sub_len, dilation * gw), BF16),
                   jax.ShapeDtypeStruct((sub_len, dilation * LANES), F32)],
        scratch_shapes=[pltpu.VMEM((DSA_HEADS, w, 2 * w), F32)],
        compiler_params=_compiler_params(("arbitrary", "arbitrary")),
        name=f"dsa_attention_g{group}",
    )(rel_bias, _band_buckets(dilation), qv, kvv, kvv, kvv, kvv)
    return o.reshape(s, gw), lse.reshape(s, LANES)


def _dsa_out_kernel(o0_ref, o1_ref, o2_ref, l0_ref, l1_ref, l2_ref, w_ref, g_ref, x_ref, out_ref, a_ref):
    lses = [l0_ref[...], l1_ref[...], l2_ref[...]]
    top = jnp.maximum(jnp.maximum(lses[0], lses[1]), lses[2])
    es = [jnp.exp(l - top) for l in lses]
    inv_den = 1.0 / (es[0] + es[1] + es[2])
    wts = [e * inv_den for e in es]
    o_refs = [o0_ref, o1_ref, o2_ref]
    for h in range(DSA_HEADS):
        hs = slice(h * DSA_DH, (h + 1) * DSA_DH)
        merged = wts[0][:, h:h + 1] * o_refs[0][:, hs].astype(F32)
        for g in range(1, N_GROUPS):
            merged = merged + wts[g][:, h:h + 1] * o_refs[g][:, hs].astype(F32)
        a_ref[:, hs] = merged.astype(BF16)
    m = jnp.dot(a_ref[...], w_ref[...], preferred_element_type=F32)
    out_ref[...] = x_ref[...] + _rms(m, g_ref[...])


def _dsa_out(os_, lses, w, gain, x, *, tm):
    s, k = os_[0].shape
    d = w.shape[1]
    o_spec = pl.BlockSpec((tm, k), lambda i: (i, 0))
    l_spec = pl.BlockSpec((tm, LANES), lambda i: (i, 0))
    return pl.pallas_call(
        _dsa_out_kernel,
        grid=(s // tm,),
        in_specs=[o_spec, o_spec, o_spec, l_spec, l_spec, l_spec,
                  pl.BlockSpec((k, d), lambda i: (0, 0)),
                  pl.BlockSpec((1, d), lambda i: (0, 0)),
                  pl.BlockSpec((tm, d), lambda i: (i, 0))],
        out_specs=pl.BlockSpec((tm, d), lambda i: (i, 0)),
        out_shape=jax.ShapeDtypeStruct((s, d), F32),
        scratch_shapes=[pltpu.VMEM((tm, k), BF16)],
        compiler_params=_compiler_params(("parallel",)),
        name="dsa_merge_out",
    )(*os_, *lses, w, gain.reshape(1, d), x)


def _tiles(s):
    return dict(tm_proj=min(1024, s), tn_proj=1024, tm_out=min(512, s), tm_ffn=min(512, s), tf_ffn=512)


def kernel(x, norm_gains, ffn_w_gate_up, ffn_w_down, gdn_w_in, gdn_conv_w, gdn_a_log, gdn_dt_bias,
           gdn_out_norm, gdn_w_out, kv_norm, kv_w, dsa_w_q, dsa_w_out, rel_bias):
    b, s, d = x.shape
    assert b == 1 and d == D_MODEL and s % (DSA_SPAN * DSA_GROUPS[-1][1]) == 0
    t = _tiles(s)
    depth = norm_gains.shape[0]
    n_a = gdn_w_in.shape[0]
    xs = x.reshape(s, d)
    kv = None
    main_w = GDN_CONV_W + GDN_V_W
    for layer in range(depth):
        gains = norm_gains[layer]
        if layer < n_a:
            w_in = gdn_w_in[layer]
            proj = _norm_matmul(xs, gains[0], w_in[:, :main_w].astype(BF16),
                                tm=t["tm_proj"], tn=t["tn_proj"], out_dtype=F32)
            gb = _gdn_gates(xs, gains[0], w_in[:, main_w:], gdn_a_log[layer], gdn_dt_bias[layer],
                            tm=t["tm_proj"])
            o = _gdn(proj, gb, gdn_conv_w[layer], gdn_out_norm[layer])
            xs = _matmul_norm_res(o, gdn_w_out[layer].astype(BF16), gains[1], xs, tm=t["tm_out"])
        else:
            j = layer - n_a
            q_all = _norm_matmul(xs, gains[0], dsa_w_q[j].astype(BF16),
                                 tm=t["tm_proj"], tn=t["tn_proj"], out_dtype=BF16)
            outs = [_dsa_group(q_all, kv, rel_bias, g) for g in range(N_GROUPS)]
            xs = _dsa_out([o for o, _ in outs], [l for _, l in outs], dsa_w_out[j].astype(BF16),
                          gains[1], xs, tm=t["tm_out"])
        xs = _ffn(xs, gains[2], ffn_w_gate_up[layer].astype(BF16), ffn_w_down[layer].astype(BF16),
                  gains[3], tm=t["tm_ffn"], tf=t["tf_ffn"])
        if layer == n_a - 1:
            kv = _norm_matmul(xs, kv_norm, kv_w.astype(BF16), tm=t["tm_proj"], tn=t["tn_proj"], out_dtype=BF16)
    return xs.reshape(b, s, d)
```

```python
import functools
import math

import jax
import jax.numpy as jnp
from jax import lax
from jax.experimental import pallas as pl
from jax.experimental.pallas import tpu as pltpu

F32 = jnp.float32
BF16 = jnp.bfloat16

RMS_EPS = 1e-6
D_MODEL = 2048

GDN_HEADS = 16
GDN_DK = 128
GDN_DV = 128
GDN_CONV = 4
GDN_QK_W = GDN_HEADS * GDN_DK
GDN_V_W = GDN_HEADS * GDN_DV
GDN_CONV_W = 2 * GDN_QK_W + GDN_V_W
GDN_CHUNK = 128
GDN_HEAD_BLOCK = 16
GDN_INV_BASE = 8
CONV_HALO_ROWS = 8

DSA_GROUPS = ((128, 1), (512, 4), (2048, 16))
N_GROUPS = len(DSA_GROUPS)
DSA_HEADS = 16
DSA_DH = 128
DSA_SPAN = 128
DSA_GROUP_W = DSA_HEADS * DSA_DH
NUM_BUCKETS = 32
MAX_DISTANCE = 2048
MASK_VALUE = -1e30

DSA_RESIDUES = DSA_GROUPS[-1][1]

LANES = 128
BF16_SUBLANES = 16
VMEM_LIMIT_BYTES = 56 * 1024 * 1024


def _compiler_params(semantics):
    return pltpu.CompilerParams(dimension_semantics=semantics, vmem_limit_bytes=VMEM_LIMIT_BYTES)


def _rms(x, gain):
    ms = jnp.mean(x * x, axis=-1, keepdims=True)
    return x * lax.rsqrt(ms + RMS_EPS) * gain


def _sigmoid(x):
    return 1.0 / (1.0 + jnp.exp(-x))


def _bdot(a, b):
    return jnp.dot(a.astype(BF16), b.astype(BF16), preferred_element_type=F32)


def _bdot_nt(a, b):
    return lax.dot_general(a.astype(BF16), b.astype(BF16), (((1,), (1,)), ((), ())),
                           preferred_element_type=F32)


def _bdot_tn(a, b):
    return lax.dot_general(a.astype(BF16), b.astype(BF16), (((0,), (0,)), ((), ())),
                           preferred_element_type=F32)


def _norm_matmul_kernel(x_ref, g_ref, w_ref, o_ref, xn_ref):
    @pl.when(pl.program_id(1) == 0)
    def _():
        xn_ref[...] = _rms(x_ref[...], g_ref[...]).astype(BF16)

    o_ref[...] = jnp.dot(xn_ref[...], w_ref[...], preferred_element_type=F32).astype(o_ref.dtype)


def _norm_matmul(x, gain, w, *, tm, tn, out_dtype):
    s, d = x.shape
    n = w.shape[1]
    return pl.pallas_call(
        _norm_matmul_kernel,
        grid=(s // tm, n // tn),
        in_specs=[pl.BlockSpec((tm, d), lambda i, j: (i, 0)),
                  pl.BlockSpec((1, d), lambda i, j: (0, 0)),
                  pl.BlockSpec((d, tn), lambda i, j: (0, j))],
        out_specs=pl.BlockSpec((tm, tn), lambda i, j: (i, j)),
        out_shape=jax.ShapeDtypeStruct((s, n), out_dtype),
        scratch_shapes=[pltpu.VMEM((tm, d), BF16)],
        compiler_params=_compiler_params(("parallel", "arbitrary")),
        name="norm_matmul",
    )(x, gain.reshape(1, d), w)


def _gdn_gates_kernel(x_ref, g_ref, w_ref, alog_ref, dtb_ref, o_ref):
    xn = _rms(x_ref[...], g_ref[...]).astype(BF16)
    p = jnp.dot(xn, w_ref[...], preferred_element_type=F32)
    lane = lax.broadcasted_iota(jnp.int32, p.shape, 1)
    beta = _sigmoid(p)
    a = p + dtb_ref[...]
    softplus = jnp.maximum(a, 0.0) + jnp.log(1.0 + jnp.exp(-jnp.abs(a)))
    g = -jnp.exp(alog_ref[...]) * softplus
    o_ref[...] = jnp.where(lane < GDN_HEADS, beta, jnp.where(lane < 2 * GDN_HEADS, g, 0.0))


def _gdn_gates(x, gain, w_ba, a_log, dt_bias, *, tm):
    s, d = x.shape
    pad = LANES - 2 * GDN_HEADS
    w = jnp.pad(w_ba, ((0, 0), (0, pad))).astype(BF16)
    zeros_h = jnp.zeros((GDN_HEADS,), F32)
    alog = jnp.pad(jnp.concatenate([zeros_h, a_log.astype(F32)]), (0, pad)).reshape(1, LANES)
    dtb = jnp.pad(jnp.concatenate([zeros_h, dt_bias.astype(F32)]), (0, pad)).reshape(1, LANES)
    return pl.pallas_call(
        _gdn_gates_kernel,
        grid=(s // tm,),
        in_specs=[pl.BlockSpec((tm, d), lambda i: (i, 0)),
                  pl.BlockSpec((1, d), lambda i: (0, 0)),
                  pl.BlockSpec((d, LANES), lambda i: (0, 0)),
                  pl.BlockSpec((1, LANES), lambda i: (0, 0)),
                  pl.BlockSpec((1, LANES), lambda i: (0, 0))],
        out_specs=pl.BlockSpec((tm, LANES), lambda i: (i, 0)),
        out_shape=jax.ShapeDtypeStruct((s, LANES), F32),
        compiler_params=_compiler_params(("parallel",)),
        name="gdn_gates",
    )(x, gain.reshape(1, d), w, alog, dtb)


def _unit_lower_inverse(lows, row, col):
    c = lows[0].shape[0]
    eye = (row == col).astype(F32)
    blk = GDN_INV_BASE
    diag_mask = (row // blk) == (col // blk)
    powers = [jnp.where(diag_mask, low, 0.0) for low in lows]
    invs = [eye - p for p in powers]
    span = 2
    while span < blk:
        powers = [_bdot(p, p) for p in powers]
        invs = [x + _bdot(x, p) for x, p in zip(invs, powers)]
        span *= 2
    while blk < c:
        off_mask = ((row // (2 * blk)) == (col // (2 * blk))) & ((row // blk) != (col // blk))
        offs = [_bdot(jnp.where(off_mask, low, 0.0), x) for low, x in zip(lows, invs)]
        invs = [x - _bdot(x, t) for x, t in zip(invs, offs)]
        blk *= 2
    return invs


def _gdn_kernel(q_ref, k_ref, v_ref, z_ref, qh_ref, kh_ref, vh_ref, cwq_ref, cwk_ref, cwv_ref,
                gb_ref, on_ref, o_ref, state_ref):
    c_len = q_ref.shape[0]
    step = pl.program_id(1)

    @pl.when(step == 0)
    def _():
        state_ref[...] = jnp.zeros_like(state_ref)

    halo_on = (step > 0).astype(F32)

    def conv_silu(x_ref, h_ref, w_ref):
        cur = x_ref[...].astype(F32)
        halo = h_ref[...].astype(F32) * halo_on
        full = jnp.concatenate([halo, cur], axis=0)
        w = w_ref[...]
        y = w[GDN_CONV - 1:GDN_CONV, :] * cur
        for j in range(GDN_CONV - 1):
            off = CONV_HALO_ROWS - (GDN_CONV - 1) + j
            y = y + w[j:j + 1, :] * full[off:off + c_len, :]
        return y * _sigmoid(y)

    q_all = conv_silu(q_ref, qh_ref, cwq_ref)
    k_all = conv_silu(k_ref, kh_ref, cwk_ref)
    v_all = conv_silu(v_ref, vh_ref, cwv_ref)

    row = lax.broadcasted_iota(jnp.int32, (c_len, c_len), 0)
    col = lax.broadcasted_iota(jnp.int32, (c_len, c_len), 1)
    causal = row >= col
    strict = row > col

    gb = gb_ref[...]
    tri = causal.astype(BF16)
    gb_hi = gb.astype(BF16)
    gb_lo = (gb - gb_hi.astype(F32)).astype(BF16)
    gcum = (jnp.dot(tri, gb_hi, preferred_element_type=F32)
            + jnp.dot(tri, gb_lo, preferred_element_type=F32))
    gcum_t = gcum.T

    heads = range(GDN_HEAD_BLOCK)

    def head_slices(x):
        return [x[:, i * GDN_DK:(i + 1) * GDN_DK] for i in heads]

    q_all, k_all, v_all = head_slices(q_all), head_slices(k_all), head_slices(v_all)
    qn = [q * (lax.rsqrt(jnp.sum(q * q, axis=-1, keepdims=True) + RMS_EPS) * (GDN_DK ** -0.5)) for q in q_all]
    kn = [k * lax.rsqrt(jnp.sum(k * k, axis=-1, keepdims=True) + RMS_EPS) for k in k_all]
    beta = [gb[:, i:i + 1] for i in heads]
    gcol = [gcum[:, GDN_HEADS + i:GDN_HEADS + i + 1] for i in heads]
    grow = [gcum_t[GDN_HEADS + i:GDN_HEADS + i + 1, :] for i in heads]
    decay = [jnp.exp(jnp.where(causal, gc - gr, MASK_VALUE)) for gc, gr in zip(gcol, grow)]
    e_col = [jnp.exp(gc) for gc in gcol]
    g_last = [gc[c_len - 1:c_len, :] for gc in gcol]
    kb = [k * b for k, b in zip(kn, beta)]
    kk_qk = [_bdot_nt(jnp.concatenate([kb[i], qn[i]], axis=0), kn[i]) for i in heads]
    low = [jnp.where(strict, kk_qk[i][:c_len] * decay[i], 0.0) for i in heads]
    intra = [kk_qk[i][c_len:] * decay[i] for i in heads]
    inv = _unit_lower_inverse(low, row, col)
    sol = [_bdot(inv[i], jnp.concatenate([v_all[i] * beta[i], kb[i] * e_col[i]], axis=1)) for i in heads]
    state = [state_ref[i] for i in heads]
    ws_qs = [_bdot(jnp.concatenate([sol[i][:, GDN_DV:], qn[i] * e_col[i]], axis=0), state[i]) for i in heads]
    v_new = [sol[i][:, :GDN_DV] - ws_qs[i][:c_len] for i in heads]
    o = [ws_qs[i][c_len:] + _bdot(intra[i], v_new[i]) for i in heads]
    for i in heads:
        k_dec = kn[i] * jnp.exp(g_last[i] - gcol[i])
        state_ref[i] = state[i] * jnp.exp(g_last[i]) + _bdot_tn(k_dec, v_new[i])
    out_gain = on_ref[...]
    for i in heads:
        hs = slice(i * GDN_DV, (i + 1) * GDN_DV)
        zh = z_ref[:, hs].astype(F32)
        o_ref[:, hs] = (_rms(o[i], out_gain) * (zh * _sigmoid(zh))).astype(o_ref.dtype)


def _gdn(proj, gb, conv_w, out_norm):
    s = proj.shape[0]
    c = GDN_CHUNK
    hb = GDN_HEAD_BLOCK
    ng = GDN_HEADS // hb
    bw = hb * GDN_DK
    halo_blocks = c // CONV_HALO_ROWS

    def cur(sec):
        return pl.BlockSpec((c, bw), lambda g, t, sec=sec: (t, sec * ng + g))

    def halo(sec):
        return pl.BlockSpec((CONV_HALO_ROWS, bw),
                            lambda g, t, sec=sec: (jnp.maximum(t * halo_blocks - 1, 0), sec * ng + g))

    def cw(sec):
        return pl.BlockSpec((GDN_CONV, bw), lambda g, t, sec=sec: (0, sec * ng + g))

    return pl.pallas_call(
        _gdn_kernel,
        grid=(ng, s // c),
        in_specs=[cur(0), cur(1), cur(2), cur(3), halo(0), halo(1), halo(2), cw(0), cw(1), cw(2),
                  pl.BlockSpec((c, LANES), lambda g, t: (t, g)),
                  pl.BlockSpec((1, GDN_DV), lambda g, t: (0, 0))],
        out_specs=pl.BlockSpec((c, bw), lambda g, t: (t, g)),
        out_shape=jax.ShapeDtypeStruct((s, GDN_V_W), BF16),
        scratch_shapes=[pltpu.VMEM((hb, GDN_DK, GDN_DV), F32)],
        compiler_params=_compiler_params(("parallel", "arbitrary")),
        name="gdn_delta_rule",
    )(proj, proj, proj, proj, proj, proj, proj, conv_w, conv_w, conv_w, gb, out_norm.reshape(1, GDN_DV))


def _matmul_norm_res_kernel(a_ref, w_ref, g_ref, x_ref, o_ref):
    m = jnp.dot(a_ref[...], w_ref[...], preferred_element_type=F32)
    o_ref[...] = x_ref[...] + _rms(m, g_ref[...])


def _matmul_norm_res(a, w, gain, x, *, tm):
    s, k = a.shape
    d = w.shape[1]
    return pl.pallas_call(
        _matmul_norm_res_kernel,
        grid=(s // tm,),
        in_specs=[pl.BlockSpec((tm, k), lambda i: (i, 0)),
                  pl.BlockSpec((k, d), lambda i: (0, 0)),
                  pl.BlockSpec((1, d), lambda i: (0, 0)),
                  pl.BlockSpec((tm, d), lambda i: (i, 0))],
        out_specs=pl.BlockSpec((tm, d), lambda i: (i, 0)),
        out_shape=jax.ShapeDtypeStruct((s, d), F32),
        compiler_params=_compiler_params(("parallel",)),
        name="matmul_norm_res",
    )(a, w, gain.reshape(1, d), x)


def _ffn_kernel(x_ref, gpre_ref, wg_ref, wu_ref, wd_ref, gpost_ref, o_ref, xn_ref, acc_ref):
    j = pl.program_id(1)

    @pl.when(j == 0)
    def _():
        xn_ref[...] = _rms(x_ref[...], gpre_ref[...]).astype(BF16)
        acc_ref[...] = jnp.zeros_like(acc_ref)

    xn = xn_ref[...]
    gate = jnp.dot(xn, wg_ref[...], preferred_element_type=F32)
    up = jnp.dot(xn, wu_ref[...], preferred_element_type=F32)
    act = (gate * _sigmoid(gate) * up).astype(BF16)
    acc_ref[...] += jnp.dot(act, wd_ref[...], preferred_element_type=F32)

    @pl.when(j == pl.num_programs(1) - 1)
    def _():
        o_ref[...] = x_ref[...] + _rms(acc_ref[...], gpost_ref[...])


def _ffn(x, g_pre, w_gate_up, w_down, g_post, *, tm, tf):
    s, d = x.shape
    d_ff = w_down.shape[0]
    nf = d_ff // tf
    return pl.pallas_call(
        _ffn_kernel,
        grid=(s // tm, nf),
        in_specs=[pl.BlockSpec((tm, d), lambda i, j: (i, 0)),
                  pl.BlockSpec((1, d), lambda i, j: (0, 0)),
                  pl.BlockSpec((d, tf), lambda i, j: (0, j)),
                  pl.BlockSpec((d, tf), lambda i, j: (0, nf + j)),
                  pl.BlockSpec((tf, d), lambda i, j: (j, 0)),
                  pl.BlockSpec((1, d), lambda i, j: (0, 0))],
        out_specs=pl.BlockSpec((tm, d), lambda i, j: (i, 0)),
        out_shape=jax.ShapeDtypeStruct((s, d), F32),
        scratch_shapes=[pltpu.VMEM((tm, d), BF16), pltpu.VMEM((tm, d), F32)],
        compiler_params=_compiler_params(("parallel", "arbitrary")),
        name="swiglu_ffn",
    )(x, g_pre.reshape(1, d), w_gate_up, w_gate_up, w_down, g_post.reshape(1, d))


def _t5_bucket(dist):
    max_exact = NUM_BUCKETS // 2
    d_f = jnp.maximum(dist, 1).astype(F32)
    large = max_exact + (jnp.log(d_f / max_exact) / math.log(MAX_DISTANCE / max_exact)
                         * (NUM_BUCKETS - max_exact)).astype(jnp.int32)
    large = jnp.minimum(large, NUM_BUCKETS - 1)
    return jnp.where(dist < max_exact, dist, large)


def _tile_offsets(group):
    w = DSA_SPAN
    classes = DSA_RESIDUES // DSA_GROUPS[group][1]
    rows = w // classes
    i = jnp.arange(w)
    return (i % rows) * classes + i // rows


def _band_buckets(group):
    w = DSA_SPAN
    off = _tile_offsets(group)
    rel = off[:, None] + w - jnp.concatenate([off, off + w])[None, :]
    band = (rel >= 0) & (rel <= w)
    return jnp.where(band, _t5_bucket(jnp.clip(rel, 0, w) * DSA_GROUPS[group][1]), -1).astype(jnp.int32)


def _dsa_tiles_per_step(group):
    classes = DSA_RESIDUES // DSA_GROUPS[group][1]
    return max(1, BF16_SUBLANES * classes // DSA_SPAN)


def _dsa_kernel(tab_ref, bucket_ref, q_ref, kp_ref, kc_ref, vp_ref, vc_ref, o_ref, lse_ref, bias_ref, *, group):
    w = DSA_SPAN
    n_tiles = _dsa_tiles_per_step(group)
    classes = q_ref.shape[0]
    rows = q_ref.shape[1] // n_tiles
    step = pl.program_id(1)

    @pl.when((pl.program_id(0) == 0) & (step == 0))
    def _():
        bucket = bucket_ref[...]

        def per_head(h, carry):
            b = jnp.full(bucket.shape, MASK_VALUE, F32)
            for t in range(NUM_BUCKETS):
                b = jnp.where(bucket == t, tab_ref[t, group * DSA_HEADS + h], b)
            bias_ref[h] = b
            return carry

        lax.fori_loop(0, DSA_HEADS, per_head, 0)

    def tiles(ref, cols):
        blk = ref[:, :, cols]
        if n_tiles == 1:
            return [blk.reshape(w, blk.shape[-1])]
        blk = blk.astype(F32)
        return [blk[:, t * rows:(t + 1) * rows, :].reshape(w, blk.shape[-1]).astype(ref.dtype)
                for t in range(n_tiles)]

    def store(ref, cols, parts):
        parts = [p.reshape(classes, rows, p.shape[-1]) for p in parts]
        val = parts[0] if n_tiles == 1 else jnp.concatenate(parts, axis=1)
        ref[:, :, cols] = val.astype(ref.dtype)

    first = jnp.where(step == 0, MASK_VALUE, 0.0).astype(F32)
    key_col = lax.broadcasted_iota(jnp.int32, (w, 2 * w), 1)
    first_mask = jnp.where(key_col < w, first, 0.0)
    lane = lax.broadcasted_iota(jnp.int32, (w, LANES), 1)
    lse_all = [jnp.zeros((w, LANES), F32) for _ in range(n_tiles)]
    scale = DSA_DH ** -0.5
    for h in range(DSA_HEADS):
        hs = slice(h * DSA_DH, (h + 1) * DSA_DH)
        q_t = tiles(q_ref, hs)
        k_t = tiles(kc_ref, hs)
        v_t = tiles(vc_ref, hs)
        k_prev = [tiles(kp_ref, hs)[-1]] + k_t[:-1]
        v_prev = [tiles(vp_ref, hs)[-1]] + v_t[:-1]
        outs = []
        for t in range(n_tiles):
            k = jnp.concatenate([k_prev[t], k_t[t]], axis=0)
            v = jnp.concatenate([v_prev[t], v_t[t]], axis=0)
            s = _bdot_nt(q_t[t], k) * scale + bias_ref[h]
            if t == 0:
                s = s + first_mask
            m = jnp.max(s, axis=-1, keepdims=True)
            p = jnp.exp(s - m)
            l = jnp.sum(p, axis=-1, keepdims=True)
            outs.append(_bdot(p, v) / l)
            lse_all[t] = jnp.where(lane == h, m + jnp.log(l), lse_all[t])
        store(o_ref, hs, outs)
    store(lse_ref, slice(None), lse_all)


def _dsa_group(q_all, kv, rel_bias, group):
    s = q_all.shape[0]
    dilation = DSA_GROUPS[group][1]
    w = DSA_SPAN
    gw = DSA_GROUP_W
    classes = DSA_RESIDUES // dilation
    class_len = s // DSA_RESIDUES
    blk_rows = _dsa_tiles_per_step(group) * w // classes
    nb = class_len // blk_rows

    def view(a):
        return a.reshape(classes, dilation, class_len, a.shape[-1])

    def spec(width, col, prev=False):
        def imap(r, n):
            return (0, r, jnp.maximum(n - 1, 0) if prev else n, col)
        return pl.BlockSpec((classes, None, blk_rows, width), imap)

    o, lse = pl.pallas_call(
        functools.partial(_dsa_kernel, group=group),
        grid=(dilation, nb),
        in_specs=[pl.BlockSpec(memory_space=pltpu.SMEM),
                  pl.BlockSpec((w, 2 * w), lambda r, n: (0, 0)),
                  spec(gw, group),
                  spec(gw, group, prev=True), spec(gw, group),
                  spec(gw, N_GROUPS + group, prev=True), spec(gw, N_GROUPS + group)],
        out_specs=[spec(gw, 0), spec(LANES, 0)],
        out_shape=[jax.ShapeDtypeStruct((classes, dilation, class_len, gw), BF16),
                   jax.ShapeDtypeStruct((classes, dilation, class_len, LANES), F32)],
        scratch_shapes=[pltpu.VMEM((DSA_HEADS, w, 2 * w), F32)],
        compiler_params=_compiler_params(("arbitrary", "arbitrary")),
        name=f"dsa_attention_g{group}",
    )(rel_bias, _band_buckets(group), view(q_all), view(kv), view(kv), view(kv), view(kv))
    return o.reshape(s, gw), lse.reshape(s, LANES)


def _to_residue_major_kernel(x_ref, o_ref):
    rows = o_ref.shape[1]
    x = x_ref[...].reshape(rows, DSA_RESIDUES, x_ref.shape[-1])
    o_ref[...] = pltpu.einshape("lrd->rld", x)


def _from_residue_major_kernel(x_ref, o_ref):
    rows = x_ref.shape[1]
    o_ref[...] = pltpu.einshape("rld->lrd", x_ref[...]).reshape(rows * DSA_RESIDUES, x_ref.shape[-1])


def _residue_major(x, *, rows, inverse):
    s, d = x.shape
    class_len = s // DSA_RESIDUES
    natural = pl.BlockSpec((rows * DSA_RESIDUES, d), lambda i: (i, 0))
    major = pl.BlockSpec((DSA_RESIDUES, rows, d), lambda i: (0, i, 0))
    if inverse:
        body, specs, arg, shape = (_from_residue_major_kernel, (major, natural),
                                   x.reshape(DSA_RESIDUES, class_len, d), (s, d))
    else:
        body, specs, arg, shape = _to_residue_major_kernel, (natural, major), x, (DSA_RESIDUES, class_len, d)
    out = pl.pallas_call(
        body,
        grid=(class_len // rows,),
        in_specs=[specs[0]],
        out_specs=specs[1],
        out_shape=jax.ShapeDtypeStruct(shape, x.dtype),
        compiler_params=_compiler_params(("parallel",)),
        name="from_residue_major" if inverse else "to_residue_major",
    )(arg)
    return out.reshape(s, d)


def _dsa_out_kernel(o0_ref, o1_ref, o2_ref, l0_ref, l1_ref, l2_ref, w_ref, g_ref, x_ref, out_ref, a_ref):
    lses = [l0_ref[...], l1_ref[...], l2_ref[...]]
    top = jnp.maximum(jnp.maximum(lses[0], lses[1]), lses[2])
    es = [jnp.exp(l - top) for l in lses]
    inv_den = 1.0 / (es[0] + es[1] + es[2])
    wts = [e * inv_den for e in es]
    o_refs = [o0_ref, o1_ref, o2_ref]
    for h in range(DSA_HEADS):
        hs = slice(h * DSA_DH, (h + 1) * DSA_DH)
        merged = wts[0][:, h:h + 1] * o_refs[0][:, hs].astype(F32)
        for g in range(1, N_GROUPS):
            merged = merged + wts[g][:, h:h + 1] * o_refs[g][:, hs].astype(F32)
        a_ref[:, hs] = merged.astype(BF16)
    m = jnp.dot(a_ref[...], w_ref[...], preferred_element_type=F32)
    out_ref[...] = x_ref[...] + _rms(m, g_ref[...])


def _dsa_out(os_, lses, w, gain, x, *, tm):
    s, k = os_[0].shape
    d = w.shape[1]
    o_spec = pl.BlockSpec((tm, k), lambda i: (i, 0))
    l_spec = pl.BlockSpec((tm, LANES), lambda i: (i, 0))
    return pl.pallas_call(
        _dsa_out_kernel,
        grid=(s // tm,),
        in_specs=[o_spec, o_spec, o_spec, l_spec, l_spec, l_spec,
                  pl.BlockSpec((k, d), lambda i: (0, 0)),
                  pl.BlockSpec((1, d), lambda i: (0, 0)),
                  pl.BlockSpec((tm, d), lambda i: (i, 0))],
        out_specs=pl.BlockSpec((tm, d), lambda i: (i, 0)),
        out_shape=jax.ShapeDtypeStruct((s, d), F32),
        scratch_shapes=[pltpu.VMEM((tm, k), BF16)],
        compiler_params=_compiler_params(("parallel",)),
        name="dsa_merge_out",
    )(*os_, *lses, w, gain.reshape(1, d), x)


def _tiles(s):
    return dict(tm_proj=min(1024, s), tn_proj=1024, tm_out=min(512, s), tm_ffn=min(512, s), tf_ffn=512,
                perm_rows=32)


def kernel(x, norm_gains, ffn_w_gate_up, ffn_w_down, gdn_w_in, gdn_conv_w, gdn_a_log, gdn_dt_bias,
           gdn_out_norm, gdn_w_out, kv_norm, kv_w, dsa_w_q, dsa_w_out, rel_bias):
    b, s, d = x.shape
    assert b == 1 and d == D_MODEL and s % (DSA_SPAN * DSA_GROUPS[-1][1]) == 0
    t = _tiles(s)
    depth = norm_gains.shape[0]
    n_a = gdn_w_in.shape[0]
    xs = x.reshape(s, d)
    kv = None
    main_w = GDN_CONV_W + GDN_V_W
    for layer in range(depth):
        gains = norm_gains[layer]
        if layer < n_a:
            w_in = gdn_w_in[layer]
            proj = _norm_matmul(xs, gains[0], w_in[:, :main_w].astype(BF16),
                                tm=t["tm_proj"], tn=t["tn_proj"], out_dtype=F32)
            gb = _gdn_gates(xs, gains[0], w_in[:, main_w:], gdn_a_log[layer], gdn_dt_bias[layer],
                            tm=t["tm_proj"])
            o = _gdn(proj, gb, gdn_conv_w[layer], gdn_out_norm[layer])
            xs = _matmul_norm_res(o, gdn_w_out[layer].astype(BF16), gains[1], xs, tm=t["tm_out"])
        else:
            j = layer - n_a
            q_all = _norm_matmul(xs, gains[0], dsa_w_q[j].astype(BF16),
                                 tm=t["tm_proj"], tn=t["tn_proj"], out_dtype=BF16)
            outs = [_dsa_group(q_all, kv, rel_bias, g) for g in range(N_GROUPS)]
            xs = _dsa_out([o for o, _ in outs], [l for _, l in outs], dsa_w_out[j].astype(BF16),
                          gains[1], xs, tm=t["tm_out"])
        xs = _ffn(xs, gains[2], ffn_w_gate_up[layer].astype(BF16), ffn_w_down[layer].astype(BF16),
                  gains[3], tm=t["tm_ffn"], tf=t["tf_ffn"])
        if layer == n_a - 1 and depth > n_a:
            xs = _residue_major(xs, rows=t["perm_rows"], inverse=False)
            kv = _norm_matmul(xs, kv_norm, kv_w.astype(BF16), tm=t["tm_proj"], tn=t["tn_proj"], out_dtype=BF16)
    if depth > n_a:
        xs = _residue_major(xs, rows=t["perm_rows"], inverse=True)
    return xs.reshape(b, s, d)
```

```python
import functools
import math

import jax
import jax.numpy as jnp
from jax import lax
from jax.experimental import pallas as pl
from jax.experimental.pallas import tpu as pltpu

F32 = jnp.float32
BF16 = jnp.bfloat16

RMS_EPS = 1e-6
D_MODEL = 2048

GDN_HEADS = 16
GDN_DK = 128
GDN_DV = 128
GDN_CONV = 4
GDN_QK_W = GDN_HEADS * GDN_DK
GDN_V_W = GDN_HEADS * GDN_DV
GDN_CONV_W = 2 * GDN_QK_W + GDN_V_W
GDN_CHUNK = 128
GDN_HEAD_BLOCK = 16
GDN_INV_BASE = 8
CONV_HALO_ROWS = 8

DSA_GROUPS = ((128, 1), (512, 4), (2048, 16))
N_GROUPS = len(DSA_GROUPS)
DSA_HEADS = 16
DSA_DH = 128
DSA_SPAN = 128
DSA_GROUP_W = DSA_HEADS * DSA_DH
NUM_BUCKETS = 32
MAX_DISTANCE = 2048
MASK_VALUE = -1e30
LOG2_E = math.log2(math.e)
LN_2 = math.log(2.0)

DSA_RESIDUES = DSA_GROUPS[-1][1]

LANES = 128
BF16_SUBLANES = 16
VMEM_LIMIT_BYTES = 56 * 1024 * 1024


def _compiler_params(semantics):
    return pltpu.CompilerParams(dimension_semantics=semantics, vmem_limit_bytes=VMEM_LIMIT_BYTES)


def _rms(x, gain):
    ms = jnp.mean(x * x, axis=-1, keepdims=True)
    return x * lax.rsqrt(ms + RMS_EPS) * gain


def _sigmoid(x):
    return 1.0 / (1.0 + jnp.exp(-x))


def _silu(x):
    h = 0.5 * x
    return h + h * jnp.tanh(h)


def _bdot(a, b):
    return jnp.dot(a.astype(BF16), b.astype(BF16), preferred_element_type=F32)


def _bdot_nt(a, b):
    return lax.dot_general(a.astype(BF16), b.astype(BF16), (((1,), (1,)), ((), ())),
                           preferred_element_type=F32)


def _bdot_tn(a, b):
    return lax.dot_general(a.astype(BF16), b.astype(BF16), (((0,), (0,)), ((), ())),
                           preferred_element_type=F32)


def _norm_matmul_kernel(x_ref, g_ref, w_ref, o_ref, xn_ref, *, out_scale):
    @pl.when(pl.program_id(1) == 0)
    def _():
        xn_ref[...] = _rms(x_ref[...], g_ref[...]).astype(BF16)

    y = jnp.dot(xn_ref[...], w_ref[...], preferred_element_type=F32)
    if out_scale != 1.0:
        y = y * out_scale
    o_ref[...] = y.astype(o_ref.dtype)


def _norm_matmul(x, gain, w, *, tm, tn, out_dtype, out_scale=1.0):
    s, d = x.shape
    n = w.shape[1]
    return pl.pallas_call(
        functools.partial(_norm_matmul_kernel, out_scale=out_scale),
        grid=(s // tm, n // tn),
        in_specs=[pl.BlockSpec((tm, d), lambda i, j: (i, 0)),
                  pl.BlockSpec((1, d), lambda i, j: (0, 0)),
                  pl.BlockSpec((d, tn), lambda i, j: (0, j))],
        out_specs=pl.BlockSpec((tm, tn), lambda i, j: (i, j)),
        out_shape=jax.ShapeDtypeStruct((s, n), out_dtype),
        scratch_shapes=[pltpu.VMEM((tm, d), BF16)],
        compiler_params=_compiler_params(("parallel", "arbitrary")),
        name="norm_matmul",
    )(x, gain.reshape(1, d), w)


def _gdn_in_proj_kernel(x_ref, g_ref, w_ref, cw_ref, o_ref, xn_ref, halo_ref, raw_ref, *, qk_blocks, v_blocks):
    i = pl.program_id(0)
    j = pl.program_id(1)
    tm = x_ref.shape[0]
    tn = w_ref.shape[1]

    @pl.when(j == 0)
    def _():
        xn_ref[...] = _rms(x_ref[...], g_ref[...]).astype(BF16)

    @pl.when((i == 0) & (j == 0))
    def _():
        halo_ref[...] = jnp.zeros_like(halo_ref)

    raw_ref[...] = jnp.dot(xn_ref[...], w_ref[...], preferred_element_type=F32)

    def conv_silu():
        raw = raw_ref[...]
        halo = jnp.where(i > 0, halo_ref[j], 0.0)
        halo_ref[j] = raw[tm - CONV_HALO_ROWS:, :]
        cw = cw_ref[...]
        halo_row = lax.broadcasted_iota(jnp.int32, halo.shape, 0)
        y = cw[GDN_CONV - 1:GDN_CONV, :] * raw
        for t in range(GDN_CONV - 1):
            back = GDN_CONV - 1 - t
            shifted = pltpu.roll(raw, back, axis=0)
            top = jnp.where(halo_row < back, pltpu.roll(halo, back, axis=0), shifted[:CONV_HALO_ROWS, :])
            shifted = jnp.concatenate([top, shifted[CONV_HALO_ROWS:, :]], axis=0)
            y = y + cw[t:t + 1, :] * shifted
        return _silu(y)

    @pl.when(j < 2 * qk_blocks)
    def _():
        y = conv_silu()
        c = jnp.where(j < qk_blocks, float(GDN_DK), 1.0).astype(F32)
        for h in range(tn // GDN_DK):
            hs = slice(h * GDN_DK, (h + 1) * GDN_DK)
            yh = y[:, hs]
            ss = jnp.sum(yh * yh, axis=-1, keepdims=True)
            o_ref[:, hs] = (yh * lax.rsqrt(ss * c + RMS_EPS * c)).astype(o_ref.dtype)

    @pl.when((j >= 2 * qk_blocks) & (j < 2 * qk_blocks + v_blocks))
    def _():
        o_ref[...] = conv_silu().astype(o_ref.dtype)

    @pl.when(j >= 2 * qk_blocks + v_blocks)
    def _():
        o_ref[...] = _silu(raw_ref[...]).astype(o_ref.dtype)


def _gdn_in_proj(x, gain, w_in, conv_w, *, tm, tn):
    s, d = x.shape
    n = GDN_CONV_W + GDN_V_W
    conv_blocks = GDN_CONV_W // tn
    return pl.pallas_call(
        functools.partial(_gdn_in_proj_kernel, qk_blocks=GDN_QK_W // tn, v_blocks=GDN_V_W // tn),
        grid=(s // tm, n // tn),
        in_specs=[pl.BlockSpec((tm, d), lambda i, j: (i, 0)),
                  pl.BlockSpec((1, d), lambda i, j: (0, 0)),
                  pl.BlockSpec((d, tn), lambda i, j: (0, j)),
                  pl.BlockSpec((GDN_CONV, tn), lambda i, j: (0, jnp.minimum(j, conv_blocks - 1)))],
        out_specs=pl.BlockSpec((tm, tn), lambda i, j: (i, j)),
        out_shape=jax.ShapeDtypeStruct((s, n), BF16),
        scratch_shapes=[pltpu.VMEM((tm, d), BF16),
                        pltpu.VMEM((conv_blocks, CONV_HALO_ROWS, tn), F32),
                        pltpu.VMEM((tm, tn), F32)],
        compiler_params=_compiler_params(("arbitrary", "arbitrary")),
        name="gdn_in_proj",
    )(x, gain.reshape(1, d), w_in, conv_w)


def _gdn_gates_kernel(x_ref, g_ref, w_ref, alog_ref, dtb_ref, o_ref):
    xn = _rms(x_ref[...], g_ref[...]).astype(BF16)
    p = jnp.dot(xn, w_ref[...], preferred_element_type=F32)
    lane = lax.broadcasted_iota(jnp.int32, p.shape, 1)
    beta = _sigmoid(p)
    a = p + dtb_ref[...]
    softplus = jnp.maximum(a, 0.0) + jnp.log(1.0 + jnp.exp(-jnp.abs(a)))
    g = -jnp.exp(alog_ref[...]) * softplus
    o_ref[...] = jnp.where(lane < GDN_HEADS, beta, jnp.where(lane < 2 * GDN_HEADS, g, 0.0))


def _gdn_gates(x, gain, w_ba, a_log, dt_bias, *, tm):
    s, d = x.shape
    pad = LANES - 2 * GDN_HEADS
    w = jnp.pad(w_ba, ((0, 0), (0, pad))).astype(BF16)
    zeros_h = jnp.zeros((GDN_HEADS,), F32)
    alog = jnp.pad(jnp.concatenate([zeros_h, a_log.astype(F32)]), (0, pad)).reshape(1, LANES)
    dtb = jnp.pad(jnp.concatenate([zeros_h, dt_bias.astype(F32)]), (0, pad)).reshape(1, LANES)
    return pl.pallas_call(
        _gdn_gates_kernel,
        grid=(s // tm,),
        in_specs=[pl.BlockSpec((tm, d), lambda i: (i, 0)),
                  pl.BlockSpec((1, d), lambda i: (0, 0)),
                  pl.BlockSpec((d, LANES), lambda i: (0, 0)),
                  pl.BlockSpec((1, LANES), lambda i: (0, 0)),
                  pl.BlockSpec((1, LANES), lambda i: (0, 0))],
        out_specs=pl.BlockSpec((tm, LANES), lambda i: (i, 0)),
        out_shape=jax.ShapeDtypeStruct((s, LANES), F32),
        compiler_params=_compiler_params(("parallel",)),
        name="gdn_gates",
    )(x, gain.reshape(1, d), w, alog, dtb)


def _unit_lower_inverse(lows, row, col):
    c = lows[0].shape[0]
    eye = (row == col).astype(F32)
    blk = GDN_INV_BASE
    diag_mask = (row // blk) == (col // blk)
    powers = [jnp.where(diag_mask, low, 0.0) for low in lows]
    invs = [eye - p for p in powers]
    span = 2
    while span < blk:
        powers = [_bdot(p, p) for p in powers]
        invs = [x + _bdot(x, p) for x, p in zip(invs, powers)]
        span *= 2
    while blk < c:
        off_mask = ((row // (2 * blk)) == (col // (2 * blk))) & ((row // blk) != (col // blk))
        offs = [_bdot(jnp.where(off_mask, low, 0.0), x) for low, x in zip(lows, invs)]
        invs = [x - _bdot(x, t) for x, t in zip(invs, offs)]
        blk *= 2
    return invs


def _gdn_kernel(q_ref, k_ref, v_ref, z_ref, gb_ref, on_ref, o_ref, state_ref):
    c_len = q_ref.shape[0]
    step = pl.program_id(1)

    @pl.when(step == 0)
    def _():
        state_ref[...] = jnp.zeros_like(state_ref)

    row = lax.broadcasted_iota(jnp.int32, (c_len, c_len), 0)
    col = lax.broadcasted_iota(jnp.int32, (c_len, c_len), 1)
    causal = row >= col
    strict = row > col

    gb = gb_ref[...]
    tri = causal.astype(BF16)
    gb_hi = gb.astype(BF16)
    gb_lo = (gb - gb_hi.astype(F32)).astype(BF16)
    gcum = (jnp.dot(tri, gb_hi, preferred_element_type=F32)
            + jnp.dot(tri, gb_lo, preferred_element_type=F32))
    gcum_t = gcum.T

    heads = range(GDN_HEAD_BLOCK)

    def head_slices(ref):
        return [ref[:, i * GDN_DK:(i + 1) * GDN_DK].astype(F32) for i in heads]

    qn, kn, v_all = head_slices(q_ref), head_slices(k_ref), head_slices(v_ref)
    beta = [gb[:, i:i + 1] for i in heads]
    gcol = [gcum[:, GDN_HEADS + i:GDN_HEADS + i + 1] for i in heads]
    grow = [gcum_t[GDN_HEADS + i:GDN_HEADS + i + 1, :] for i in heads]
    decay = [jnp.exp(jnp.where(causal, gc - gr, MASK_VALUE)) for gc, gr in zip(gcol, grow)]
    e_col = [jnp.exp(gc) for gc in gcol]
    g_last = [gc[c_len - 1:c_len, :] for gc in gcol]
    kb = [k * b for k, b in zip(kn, beta)]
    kk_qk = [_bdot_nt(jnp.concatenate([kb[i], qn[i]], axis=0), kn[i]) for i in heads]
    low = [jnp.where(strict, kk_qk[i][:c_len] * decay[i], 0.0) for i in heads]
    intra = [kk_qk[i][c_len:] * decay[i] for i in heads]
    inv = _unit_lower_inverse(low, row, col)
    sol = [_bdot(inv[i], jnp.concatenate([v_all[i] * beta[i], kb[i] * e_col[i]], axis=1)) for i in heads]
    state = [state_ref[i] for i in heads]
    ws_qs = [_bdot(jnp.concatenate([sol[i][:, GDN_DV:], qn[i] * e_col[i]], axis=0), state[i]) for i in heads]
    v_new = [sol[i][:, :GDN_DV] - ws_qs[i][:c_len] for i in heads]
    o = [ws_qs[i][c_len:] + _bdot(intra[i], v_new[i]) for i in heads]
    for i in heads:
        k_dec = kn[i] * jnp.exp(g_last[i] - gcol[i])
        state_ref[i] = state[i] * jnp.exp(g_last[i]) + _bdot_tn(k_dec, v_new[i])
    out_gain = on_ref[...]
    for i in heads:
        hs = slice(i * GDN_DV, (i + 1) * GDN_DV)
        o_ref[:, hs] = (_rms(o[i], out_gain) * z_ref[:, hs].astype(F32)).astype(o_ref.dtype)


def _gdn(proj, gb, out_norm):
    s = proj.shape[0]
    c = GDN_CHUNK
    hb = GDN_HEAD_BLOCK
    ng = GDN_HEADS // hb
    bw = hb * GDN_DK

    def cur(sec):
        return pl.BlockSpec((c, bw), lambda g, t, sec=sec: (t, sec * ng + g))

    return pl.pallas_call(
        _gdn_kernel,
        grid=(ng, s // c),
        in_specs=[cur(0), cur(1), cur(2), cur(3),
                  pl.BlockSpec((c, LANES), lambda g, t: (t, g)),
                  pl.BlockSpec((1, GDN_DV), lambda g, t: (0, 0))],
        out_specs=pl.BlockSpec((c, bw), lambda g, t: (t, g)),
        out_shape=jax.ShapeDtypeStruct((s, GDN_V_W), BF16),
        scratch_shapes=[pltpu.VMEM((hb, GDN_DK, GDN_DV), F32)],
        compiler_params=_compiler_params(("parallel", "arbitrary")),
        name="gdn_delta_rule",
    )(proj, proj, proj, proj, gb, out_norm.reshape(1, GDN_DV))


def _matmul_norm_res_kernel(a_ref, w_ref, g_ref, x_ref, o_ref):
    m = jnp.dot(a_ref[...], w_ref[...], preferred_element_type=F32)
    o_ref[...] = x_ref[...] + _rms(m, g_ref[...])


def _matmul_norm_res(a, w, gain, x, *, tm):
    s, k = a.shape
    d = w.shape[1]
    return pl.pallas_call(
        _matmul_norm_res_kernel,
        grid=(s // tm,),
        in_specs=[pl.BlockSpec((tm, k), lambda i: (i, 0)),
                  pl.BlockSpec((k, d), lambda i: (0, 0)),
                  pl.BlockSpec((1, d), lambda i: (0, 0)),
                  pl.BlockSpec((tm, d), lambda i: (i, 0))],
        out_specs=pl.BlockSpec((tm, d), lambda i: (i, 0)),
        out_shape=jax.ShapeDtypeStruct((s, d), F32),
        compiler_params=_compiler_params(("parallel",)),
        name="matmul_norm_res",
    )(a, w, gain.reshape(1, d), x)


def _ffn_kernel(x_ref, gpre_ref, wg_ref, wu_ref, wd_ref, gpost_ref, o_ref, xn_ref, acc_ref):
    j = pl.program_id(1)

    @pl.when(j == 0)
    def _():
        xn_ref[...] = _rms(x_ref[...], gpre_ref[...]).astype(BF16)
        acc_ref[...] = jnp.zeros_like(acc_ref)

    xn = xn_ref[...]
    gate = jnp.dot(xn, wg_ref[...], preferred_element_type=F32)
    up = jnp.dot(xn, wu_ref[...], preferred_element_type=F32)
    act = (gate * _sigmoid(gate) * up).astype(BF16)
    acc_ref[...] += jnp.dot(act, wd_ref[...], preferred_element_type=F32)

    @pl.when(j == pl.num_programs(1) - 1)
    def _():
        o_ref[...] = x_ref[...] + _rms(acc_ref[...], gpost_ref[...])


def _ffn(x, g_pre, w_gate_up, w_down, g_post, *, tm, tf):
    s, d = x.shape
    d_ff = w_down.shape[0]
    nf = d_ff // tf
    return pl.pallas_call(
        _ffn_kernel,
        grid=(s // tm, nf),
        in_specs=[pl.BlockSpec((tm, d), lambda i, j: (i, 0)),
                  pl.BlockSpec((1, d), lambda i, j: (0, 0)),
                  pl.BlockSpec((d, tf), lambda i, j: (0, j)),
                  pl.BlockSpec((d, tf), lambda i, j: (0, nf + j)),
                  pl.BlockSpec((tf, d), lambda i, j: (j, 0)),
                  pl.BlockSpec((1, d), lambda i, j: (0, 0))],
        out_specs=pl.BlockSpec((tm, d), lambda i, j: (i, 0)),
        out_shape=jax.ShapeDtypeStruct((s, d), F32),
        scratch_shapes=[pltpu.VMEM((tm, d), BF16), pltpu.VMEM((tm, d), F32)],
        compiler_params=_compiler_params(("parallel", "arbitrary")),
        name="swiglu_ffn",
    )(x, g_pre.reshape(1, d), w_gate_up, w_gate_up, w_down, g_post.reshape(1, d))


def _t5_bucket(dist):
    max_exact = NUM_BUCKETS // 2
    d_f = jnp.maximum(dist, 1).astype(F32)
    large = max_exact + (jnp.log(d_f / max_exact) / math.log(MAX_DISTANCE / max_exact)
                         * (NUM_BUCKETS - max_exact)).astype(jnp.int32)
    large = jnp.minimum(large, NUM_BUCKETS - 1)
    return jnp.where(dist < max_exact, dist, large)


def _tile_offsets(group):
    w = DSA_SPAN
    classes = DSA_RESIDUES // DSA_GROUPS[group][1]
    rows = w // classes
    i = jnp.arange(w)
    return (i % rows) * classes + i // rows


def _band_buckets(group):
    w = DSA_SPAN
    off = _tile_offsets(group)
    rel = off[:, None] + w - jnp.concatenate([off, off + w])[None, :]
    band = (rel >= 0) & (rel <= w)
    return jnp.where(band, _t5_bucket(jnp.clip(rel, 0, w) * DSA_GROUPS[group][1]), -1).astype(jnp.int32)


def _dsa_tiles_per_step(group):
    classes = DSA_RESIDUES // DSA_GROUPS[group][1]
    return max(1, BF16_SUBLANES * classes // DSA_SPAN)


def _dsa_kernel(tab_ref, bucket_ref, q_ref, kp_ref, kc_ref, vp_ref, vc_ref, o_ref, lse_ref, bias_ref, *, group):
    w = DSA_SPAN
    n_tiles = _dsa_tiles_per_step(group)
    classes = q_ref.shape[0]
    rows = q_ref.shape[1] // n_tiles
    step = pl.program_id(1)

    @pl.when((pl.program_id(0) == 0) & (step == 0))
    def _():
        bucket = bucket_ref[...]
        key_col = lax.broadcasted_iota(jnp.int32, (w, 2 * w), 1)

        def per_head(h, carry):
            b = jnp.full(bucket.shape, MASK_VALUE, F32)
            for t in range(NUM_BUCKETS):
                b = jnp.where(bucket == t, tab_ref[t, group * DSA_HEADS + h] * LOG2_E, b)
            bias_ref[0, h] = b
            bias_ref[1, h] = jnp.where(key_col < w, MASK_VALUE, b)
            return carry

        lax.fori_loop(0, DSA_HEADS, per_head, 0)

    def tiles(ref, cols):
        blk = ref[:, :, cols]
        if n_tiles == 1:
            return [blk.reshape(w, blk.shape[-1])]
        blk = blk.astype(F32)
        return [blk[:, t * rows:(t + 1) * rows, :].reshape(w, blk.shape[-1]).astype(ref.dtype)
                for t in range(n_tiles)]

    def store(ref, cols, parts):
        parts = [p.reshape(classes, rows, p.shape[-1]) for p in parts]
        val = parts[0] if n_tiles == 1 else jnp.concatenate(parts, axis=1)
        ref[:, :, cols] = val.astype(ref.dtype)

    first = jnp.where(step == 0, 1, 0)
    lane = lax.broadcasted_iota(jnp.int32, (w, LANES), 1)
    lse_all = [jnp.zeros((w, LANES), F32) for _ in range(n_tiles)]
    for h in range(DSA_HEADS):
        hs = slice(h * DSA_DH, (h + 1) * DSA_DH)
        q_t = tiles(q_ref, hs)
        k_t = tiles(kc_ref, hs)
        v_t = tiles(vc_ref, hs)
        k_prev = [tiles(kp_ref, hs)[-1]] + k_t[:-1]
        v_prev = [tiles(vp_ref, hs)[-1]] + v_t[:-1]
        outs = []
        for t in range(n_tiles):
            k = jnp.concatenate([k_prev[t], k_t[t]], axis=0)
            v = jnp.concatenate([v_prev[t], v_t[t]], axis=0)
            s = _bdot_nt(q_t[t], k) + bias_ref[first if t == 0 else 0, h]
            m = jnp.max(s, axis=-1, keepdims=True)
            p = jnp.exp2(s - m)
            l = jnp.sum(p, axis=-1, keepdims=True)
            outs.append(_bdot(p, v) / l)
            lse_all[t] = jnp.where(lane == h, (m + jnp.log2(l)) * LN_2, lse_all[t])
        store(o_ref, hs, outs)
    store(lse_ref, slice(None), lse_all)


def _dsa_group(q_all, kv, rel_bias, group):
    s = q_all.shape[0]
    dilation = DSA_GROUPS[group][1]
    w = DSA_SPAN
    gw = DSA_GROUP_W
    classes = DSA_RESIDUES // dilation
    class_len = s // DSA_RESIDUES
    blk_rows = _dsa_tiles_per_step(group) * w // classes
    nb = class_len // blk_rows

    def view(a):
        return a.reshape(classes, dilation, class_len, a.shape[-1])

    def spec(width, col, prev=False):
        def imap(r, n):
            return (0, r, jnp.maximum(n - 1, 0) if prev else n, col)
        return pl.BlockSpec((classes, None, blk_rows, width), imap)

    o, lse = pl.pallas_call(
        functools.partial(_dsa_kernel, group=group),
        grid=(dilation, nb),
        in_specs=[pl.BlockSpec(memory_space=pltpu.SMEM),
                  pl.BlockSpec((w, 2 * w), lambda r, n: (0, 0)),
                  spec(gw, group),
                  spec(gw, group, prev=True), spec(gw, group),
                  spec(gw, N_GROUPS + group, prev=True), spec(gw, N_GROUPS + group)],
        out_specs=[spec(gw, 0), spec(LANES, 0)],
        out_shape=[jax.ShapeDtypeStruct((classes, dilation, class_len, gw), BF16),
                   jax.ShapeDtypeStruct((classes, dilation, class_len, LANES), F32)],
        scratch_shapes=[pltpu.VMEM((2, DSA_HEADS, w, 2 * w), F32)],
        compiler_params=_compiler_params(("arbitrary", "arbitrary")),
        name=f"dsa_attention_g{group}",
    )(rel_bias, _band_buckets(group), view(q_all), view(kv), view(kv), view(kv), view(kv))
    return o.reshape(s, gw), lse.reshape(s, LANES)


def _to_residue_major_kernel(x_ref, o_ref):
    rows = o_ref.shape[1]
    x = x_ref[...].reshape(rows, DSA_RESIDUES, x_ref.shape[-1])
    o_ref[...] = pltpu.einshape("lrd->rld", x)


def _from_residue_major_kernel(x_ref, o_ref):
    rows = x_ref.shape[1]
    o_ref[...] = pltpu.einshape("rld->lrd", x_ref[...]).reshape(rows * DSA_RESIDUES, x_ref.shape[-1])


def _residue_major(x, *, rows, inverse):
    s, d = x.shape
    class_len = s // DSA_RESIDUES
    natural = pl.BlockSpec((rows * DSA_RESIDUES, d), lambda i: (i, 0))
    major = pl.BlockSpec((DSA_RESIDUES, rows, d), lambda i: (0, i, 0))
    if inverse:
        body, specs, arg, shape = (_from_residue_major_kernel, (major, natural),
                                   x.reshape(DSA_RESIDUES, class_len, d), (s, d))
    else:
        body, specs, arg, shape = _to_residue_major_kernel, (natural, major), x, (DSA_RESIDUES, class_len, d)
    out = pl.pallas_call(
        body,
        grid=(class_len // rows,),
        in_specs=[specs[0]],
        out_specs=specs[1],
        out_shape=jax.ShapeDtypeStruct(shape, x.dtype),
        compiler_params=_compiler_params(("parallel",)),
        name="from_residue_major" if inverse else "to_residue_major",
    )(arg)
    return out.reshape(s, d)


def _dsa_out_kernel(o0_ref, o1_ref, o2_ref, l0_ref, l1_ref, l2_ref, w_ref, g_ref, x_ref, out_ref, a_ref):
    lses = [l0_ref[...], l1_ref[...], l2_ref[...]]
    top = jnp.maximum(jnp.maximum(lses[0], lses[1]), lses[2])
    es = [jnp.exp(l - top) for l in lses]
    inv_den = 1.0 / (es[0] + es[1] + es[2])
    wts = [e * inv_den for e in es]
    o_refs = [o0_ref, o1_ref, o2_ref]
    for h in range(DSA_HEADS):
        hs = slice(h * DSA_DH, (h + 1) * DSA_DH)
        merged = wts[0][:, h:h + 1] * o_refs[0][:, hs].astype(F32)
        for g in range(1, N_GROUPS):
            merged = merged + wts[g][:, h:h + 1] * o_refs[g][:, hs].astype(F32)
        a_ref[:, hs] = merged.astype(BF16)
    m = jnp.dot(a_ref[...], w_ref[...], preferred_element_type=F32)
    out_ref[...] = x_ref[...] + _rms(m, g_ref[...])


def _dsa_out(os_, lses, w, gain, x, *, tm):
    s, k = os_[0].shape
    d = w.shape[1]
    o_spec = pl.BlockSpec((tm, k), lambda i: (i, 0))
    l_spec = pl.BlockSpec((tm, LANES), lambda i: (i, 0))
    return pl.pallas_call(
        _dsa_out_kernel,
        grid=(s // tm,),
        in_specs=[o_spec, o_spec, o_spec, l_spec, l_spec, l_spec,
                  pl.BlockSpec((k, d), lambda i: (0, 0)),
                  pl.BlockSpec((1, d), lambda i: (0, 0)),
                  pl.BlockSpec((tm, d), lambda i: (i, 0))],
        out_specs=pl.BlockSpec((tm, d), lambda i: (i, 0)),
        out_shape=jax.ShapeDtypeStruct((s, d), F32),
        scratch_shapes=[pltpu.VMEM((tm, k), BF16)],
        compiler_params=_compiler_params(("parallel",)),
        name="dsa_merge_out",
    )(*os_, *lses, w, gain.reshape(1, d), x)


def _tiles(s):
    return dict(tm_proj=min(1024, s), tn_proj=1024, tm_out=min(512, s), tm_ffn=min(512, s), tf_ffn=512,
                perm_rows=32)


def kernel(x, norm_gains, ffn_w_gate_up, ffn_w_down, gdn_w_in, gdn_conv_w, gdn_a_log, gdn_dt_bias,
           gdn_out_norm, gdn_w_out, kv_norm, kv_w, dsa_w_q, dsa_w_out, rel_bias):
    b, s, d = x.shape
    assert b == 1 and d == D_MODEL and s % (DSA_SPAN * DSA_GROUPS[-1][1]) == 0
    t = _tiles(s)
    depth = norm_gains.shape[0]
    n_a = gdn_w_in.shape[0]
    xs = x.reshape(s, d)
    kv = None
    main_w = GDN_CONV_W + GDN_V_W
    for layer in range(depth):
        gains = norm_gains[layer]
        if layer < n_a:
            w_in = gdn_w_in[layer]
            proj = _gdn_in_proj(xs, gains[0], w_in.astype(BF16), gdn_conv_w[layer],
                                tm=t["tm_proj"], tn=t["tn_proj"])
            gb = _gdn_gates(xs, gains[0], w_in[:, main_w:], gdn_a_log[layer], gdn_dt_bias[layer],
                            tm=t["tm_proj"])
            o = _gdn(proj, gb, gdn_out_norm[layer])
            xs = _matmul_norm_res(o, gdn_w_out[layer].astype(BF16), gains[1], xs, tm=t["tm_out"])
        else:
            j = layer - n_a
            q_all = _norm_matmul(xs, gains[0], dsa_w_q[j].astype(BF16),
                                 tm=t["tm_proj"], tn=t["tn_proj"], out_dtype=BF16,
                                 out_scale=DSA_DH ** -0.5 * LOG2_E)
            outs = [_dsa_group(q_all, kv, rel_bias, g) for g in range(N_GROUPS)]
            xs = _dsa_out([o for o, _ in outs], [l for _, l in outs], dsa_w_out[j].astype(BF16),
                          gains[1], xs, tm=t["tm_out"])
        xs = _ffn(xs, gains[2], ffn_w_gate_up[layer].astype(BF16), ffn_w_down[layer].astype(BF16),
                  gains[3], tm=t["tm_ffn"], tf=t["tf_ffn"])
        if layer == n_a - 1 and depth > n_a:
            xs = _residue_major(xs, rows=t["perm_rows"], inverse=False)
            kv = _norm_matmul(xs, kv_norm, kv_w.astype(BF16), tm=t["tm_proj"], tn=t["tn_proj"], out_dtype=BF16)
    if depth > n_a:
        xs = _residue_major(xs, rows=t["perm_rows"], inverse=True)
    return xs.reshape(b, s, d)
```

```python
import functools
import math

import jax
import jax.numpy as jnp
from jax import lax
from jax.experimental import pallas as pl
from jax.experimental.pallas import tpu as pltpu

F32 = jnp.float32
BF16 = jnp.bfloat16

RMS_EPS = 1e-6
D_MODEL = 2048

GDN_HEADS = 16
GDN_DK = 128
GDN_DV = 128
GDN_CONV = 4
GDN_QK_W = GDN_HEADS * GDN_DK
GDN_V_W = GDN_HEADS * GDN_DV
GDN_CONV_W = 2 * GDN_QK_W + GDN_V_W
GDN_CHUNK = 128
GDN_HEAD_BLOCK = 16
GDN_INV_BASE = 8
CONV_HALO_ROWS = 8

DSA_GROUPS = ((128, 1), (512, 4), (2048, 16))
N_GROUPS = len(DSA_GROUPS)
DSA_HEADS = 16
DSA_DH = 128
DSA_SPAN = 128
DSA_GROUP_W = DSA_HEADS * DSA_DH
NUM_BUCKETS = 32
MAX_DISTANCE = 2048
MASK_VALUE = -1e30
LOG2_E = math.log2(math.e)
LN_2 = math.log(2.0)

DSA_RESIDUES = DSA_GROUPS[-1][1]

LANES = 128
BF16_SUBLANES = 16
VMEM_LIMIT_BYTES = 56 * 1024 * 1024


def _compiler_params(semantics):
    return pltpu.CompilerParams(dimension_semantics=semantics, vmem_limit_bytes=VMEM_LIMIT_BYTES)


def _rms(x, gain):
    ms = jnp.mean(x * x, axis=-1, keepdims=True)
    return x * lax.rsqrt(ms + RMS_EPS) * gain


def _sigmoid(x):
    return 1.0 / (1.0 + jnp.exp(-x))


def _silu(x):
    h = 0.5 * x
    return h + h * jnp.tanh(h)


def _bdot(a, b):
    return jnp.dot(a.astype(BF16), b.astype(BF16), preferred_element_type=F32)


def _bdot_nt(a, b):
    return lax.dot_general(a.astype(BF16), b.astype(BF16), (((1,), (1,)), ((), ())),
                           preferred_element_type=F32)


def _bdot_tn(a, b):
    return lax.dot_general(a.astype(BF16), b.astype(BF16), (((0,), (0,)), ((), ())),
                           preferred_element_type=F32)


def _norm_matmul_kernel(x_ref, g_ref, w_ref, o_ref, xn_ref, *, out_scale):
    @pl.when(pl.program_id(1) == 0)
    def _():
        xn_ref[...] = _rms(x_ref[...], g_ref[...]).astype(BF16)

    y = jnp.dot(xn_ref[...], w_ref[...], preferred_element_type=F32)
    if out_scale != 1.0:
        y = y * out_scale
    o_ref[...] = y.astype(o_ref.dtype)


def _norm_matmul(x, gain, w, *, tm, tn, out_dtype, out_scale=1.0):
    s, d = x.shape
    n = w.shape[1]
    return pl.pallas_call(
        functools.partial(_norm_matmul_kernel, out_scale=out_scale),
        grid=(s // tm, n // tn),
        in_specs=[pl.BlockSpec((tm, d), lambda i, j: (i, 0)),
                  pl.BlockSpec((1, d), lambda i, j: (0, 0)),
                  pl.BlockSpec((d, tn), lambda i, j: (0, j))],
        out_specs=pl.BlockSpec((tm, tn), lambda i, j: (i, j)),
        out_shape=jax.ShapeDtypeStruct((s, n), out_dtype),
        scratch_shapes=[pltpu.VMEM((tm, d), BF16)],
        compiler_params=_compiler_params(("parallel", "arbitrary")),
        name="norm_matmul",
    )(x, gain.reshape(1, d), w)


def _ordered_after(x, anchor):
    rows, lanes = BF16_SUBLANES, LANES
    a = lax.bitcast_convert_type(jnp.abs(anchor[:rows, :lanes].astype(F32)), jnp.int32)
    zero = lax.shift_right_logical(a, 31)
    tile = lax.bitcast_convert_type(x[:rows, :lanes].astype(F32), jnp.int32) | zero
    tile = lax.bitcast_convert_type(tile, F32).astype(x.dtype)
    top = jnp.concatenate([tile, x[:rows, lanes:]], axis=1)
    return jnp.concatenate([top, x[rows:, :]], axis=0)


def _gdn_in_proj_kernel(x_ref, g_ref, w_ref, cw_ref, o_ref, xn_ref, halo_ref, raw_ref,
                        *, qk_blocks, v_blocks, row_chunks):
    i = pl.program_id(0)
    j = pl.program_id(1)
    tm = x_ref.shape[0]
    tn = w_ref.shape[1]

    @pl.when(j == 0)
    def _():
        xn_ref[...] = _rms(x_ref[...], g_ref[...]).astype(BF16)

    @pl.when((i == 0) & (j == 0))
    def _():
        halo_ref[...] = jnp.zeros_like(halo_ref)

    chunks = []
    for rows in row_chunks:
        chunks.append((sum(r for _, r in chunks), rows))
    assert sum(row_chunks) == tm
    n_chunks = len(chunks)

    def conv_silu(r0, rows):
        raw = raw_ref[pl.ds(r0, rows), :]
        if r0 == 0:
            halo = jnp.where(i > 0, halo_ref[j], 0.0)
        else:
            halo = raw_ref[pl.ds(r0 - CONV_HALO_ROWS, CONV_HALO_ROWS), :]
        cw = cw_ref[...]
        groups = jnp.concatenate([halo, raw], axis=0).reshape(rows // CONV_HALO_ROWS + 1, CONV_HALO_ROWS, tn)
        in_group = lax.broadcasted_iota(jnp.int32, (rows // CONV_HALO_ROWS, CONV_HALO_ROWS, tn), 1)
        y = cw[GDN_CONV - 1:GDN_CONV, :] * raw
        for t in range(GDN_CONV - 1):
            back = GDN_CONV - 1 - t
            rot = pltpu.roll(groups, back, axis=1)
            shifted = jnp.where(in_group < back, rot[:-1], rot[1:]).reshape(rows, tn)
            y = y + cw[t:t + 1, :] * shifted
        return _silu(y)

    def finish_qk(r0, rows):
        y = conv_silu(r0, rows)
        c = jnp.where(j < qk_blocks, float(GDN_DK), 1.0).astype(F32)
        out = []
        for h in range(tn // GDN_DK):
            yh = y[:, h * GDN_DK:(h + 1) * GDN_DK]
            ss = jnp.sum(yh * yh, axis=-1, keepdims=True)
            out.append((yh * lax.rsqrt(ss * c + RMS_EPS * c)).astype(o_ref.dtype))
        return jnp.concatenate(out, axis=1)

    def finish_v(r0, rows):
        return conv_silu(r0, rows).astype(o_ref.dtype)

    def finish_z(r0, rows):
        return _silu(raw_ref[pl.ds(r0, rows), :]).astype(o_ref.dtype)

    def run(finish):
        done = []
        for c in range(n_chunks + 1):
            if c < n_chunks:
                rows = pl.ds(*chunks[c])
                lhs = xn_ref[rows, :]
                if c >= 2:
                    lhs = _ordered_after(lhs, done[c - 2])
                raw_ref[rows, :] = jnp.dot(lhs, w_ref[...], preferred_element_type=F32)
            if c >= 1:
                out = finish(*chunks[c - 1])
                o_ref[pl.ds(*chunks[c - 1]), :] = out
                done.append(out)

    @pl.when(j < 2 * qk_blocks)
    def _():
        run(finish_qk)
        halo_ref[j] = raw_ref[pl.ds(tm - CONV_HALO_ROWS, CONV_HALO_ROWS), :]

    @pl.when((j >= 2 * qk_blocks) & (j < 2 * qk_blocks + v_blocks))
    def _():
        run(finish_v)
        halo_ref[j] = raw_ref[pl.ds(tm - CONV_HALO_ROWS, CONV_HALO_ROWS), :]

    @pl.when(j >= 2 * qk_blocks + v_blocks)
    def _():
        run(finish_z)


def _gdn_in_proj(x, gain, w_in, conv_w, *, tm, tn, row_chunks):
    s, d = x.shape
    n = GDN_CONV_W + GDN_V_W
    conv_blocks = GDN_CONV_W // tn
    return pl.pallas_call(
        functools.partial(_gdn_in_proj_kernel, qk_blocks=GDN_QK_W // tn, v_blocks=GDN_V_W // tn,
                          row_chunks=row_chunks),
        grid=(s // tm, n // tn),
        in_specs=[pl.BlockSpec((tm, d), lambda i, j: (i, 0)),
                  pl.BlockSpec((1, d), lambda i, j: (0, 0)),
                  pl.BlockSpec((d, tn), lambda i, j: (0, j)),
                  pl.BlockSpec((GDN_CONV, tn), lambda i, j: (0, jnp.minimum(j, conv_blocks - 1)))],
        out_specs=pl.BlockSpec((tm, tn), lambda i, j: (i, j)),
        out_shape=jax.ShapeDtypeStruct((s, n), BF16),
        scratch_shapes=[pltpu.VMEM((tm, d), BF16),
                        pltpu.VMEM((conv_blocks, CONV_HALO_ROWS, tn), F32),
                        pltpu.VMEM((tm, tn), F32)],
        compiler_params=_compiler_params(("arbitrary", "arbitrary")),
        name="gdn_in_proj",
    )(x, gain.reshape(1, d), w_in, conv_w)


def _gdn_gates_kernel(x_ref, g_ref, w_ref, alog_ref, dtb_ref, o_ref):
    xn = _rms(x_ref[...], g_ref[...]).astype(BF16)
    p = jnp.dot(xn, w_ref[...], preferred_element_type=F32)
    lane = lax.broadcasted_iota(jnp.int32, p.shape, 1)
    beta = _sigmoid(p)
    a = p + dtb_ref[...]
    softplus = jnp.maximum(a, 0.0) + jnp.log(1.0 + jnp.exp(-jnp.abs(a)))
    g = -jnp.exp(alog_ref[...]) * softplus
    o_ref[...] = jnp.where(lane < GDN_HEADS, beta, jnp.where(lane < 2 * GDN_HEADS, g, 0.0))


def _gdn_gates(x, gain, w_ba, a_log, dt_bias, *, tm):
    s, d = x.shape
    pad = LANES - 2 * GDN_HEADS
    w = jnp.pad(w_ba, ((0, 0), (0, pad))).astype(BF16)
    zeros_h = jnp.zeros((GDN_HEADS,), F32)
    alog = jnp.pad(jnp.concatenate([zeros_h, a_log.astype(F32)]), (0, pad)).reshape(1, LANES)
    dtb = jnp.pad(jnp.concatenate([zeros_h, dt_bias.astype(F32)]), (0, pad)).reshape(1, LANES)
    return pl.pallas_call(
        _gdn_gates_kernel,
        grid=(s // tm,),
        in_specs=[pl.BlockSpec((tm, d), lambda i: (i, 0)),
                  pl.BlockSpec((1, d), lambda i: (0, 0)),
                  pl.BlockSpec((d, LANES), lambda i: (0, 0)),
                  pl.BlockSpec((1, LANES), lambda i: (0, 0)),
                  pl.BlockSpec((1, LANES), lambda i: (0, 0))],
        out_specs=pl.BlockSpec((tm, LANES), lambda i: (i, 0)),
        out_shape=jax.ShapeDtypeStruct((s, LANES), F32),
        compiler_params=_compiler_params(("parallel",)),
        name="gdn_gates",
    )(x, gain.reshape(1, d), w, alog, dtb)


def _unit_lower_inverse(lows, row, col):
    c = lows[0].shape[0]
    eye = (row == col).astype(F32)
    blk = GDN_INV_BASE
    diag_mask = (row // blk) == (col // blk)
    powers = [jnp.where(diag_mask, low, 0.0) for low in lows]
    invs = [eye - p for p in powers]
    span = 2
    while span < blk:
        powers = [_bdot(p, p) for p in powers]
        invs = [x + _bdot(x, p) for x, p in zip(invs, powers)]
        span *= 2
    while blk < c:
        off_mask = ((row // (2 * blk)) == (col // (2 * blk))) & ((row // blk) != (col // blk))
        offs = [_bdot(jnp.where(off_mask, low, 0.0), x) for low, x in zip(lows, invs)]
        invs = [x - _bdot(x, t) for x, t in zip(invs, offs)]
        blk *= 2
    return invs


def _gdn_kernel(q_ref, k_ref, v_ref, z_ref, gb_ref, on_ref, o_ref, state_ref):
    c_len = q_ref.shape[0]
    step = pl.program_id(1)

    @pl.when(step == 0)
    def _():
        state_ref[...] = jnp.zeros_like(state_ref)

    row = lax.broadcasted_iota(jnp.int32, (c_len, c_len), 0)
    col = lax.broadcasted_iota(jnp.int32, (c_len, c_len), 1)
    causal = row >= col
    strict = row > col

    gb = gb_ref[...]
    tri = causal.astype(BF16)
    gb_hi = gb.astype(BF16)
    gb_lo = (gb - gb_hi.astype(F32)).astype(BF16)
    gcum = (jnp.dot(tri, gb_hi, preferred_element_type=F32)
            + jnp.dot(tri, gb_lo, preferred_element_type=F32))
    gcum_t = gcum.T

    heads = range(GDN_HEAD_BLOCK)

    def head_slices(ref):
        return [ref[:, i * GDN_DK:(i + 1) * GDN_DK].astype(F32) for i in heads]

    qn, kn, v_all = head_slices(q_ref), head_slices(k_ref), head_slices(v_ref)
    beta = [gb[:, i:i + 1] for i in heads]
    gcol = [gcum[:, GDN_HEADS + i:GDN_HEADS + i + 1] for i in heads]
    grow = [gcum_t[GDN_HEADS + i:GDN_HEADS + i + 1, :] for i in heads]
    decay = [jnp.exp(jnp.where(causal, gc - gr, MASK_VALUE)) for gc, gr in zip(gcol, grow)]
    e_col = [jnp.exp(gc) for gc in gcol]
    g_last = [gc[c_len - 1:c_len, :] for gc in gcol]
    kb = [k * b for k, b in zip(kn, beta)]
    kk_qk = [_bdot_nt(jnp.concatenate([kb[i], qn[i]], axis=0), kn[i]) for i in heads]
    low = [jnp.where(strict, kk_qk[i][:c_len] * decay[i], 0.0) for i in heads]
    intra = [kk_qk[i][c_len:] * decay[i] for i in heads]
    inv = _unit_lower_inverse(low, row, col)
    sol = [_bdot(inv[i], jnp.concatenate([v_all[i] * beta[i], kb[i] * e_col[i]], axis=1)) for i in heads]
    state = [state_ref[i] for i in heads]
    ws_qs = [_bdot(jnp.concatenate([sol[i][:, GDN_DV:], qn[i] * e_col[i]], axis=0), state[i]) for i in heads]
    v_new = [sol[i][:, :GDN_DV] - ws_qs[i][:c_len] for i in heads]
    o = [ws_qs[i][c_len:] + _bdot(intra[i], v_new[i]) for i in heads]
    for i in heads:
        k_dec = kn[i] * jnp.exp(g_last[i] - gcol[i])
        state_ref[i] = state[i] * jnp.exp(g_last[i]) + _bdot_tn(k_dec, v_new[i])
    out_gain = on_ref[...]
    for i in heads:
        hs = slice(i * GDN_DV, (i + 1) * GDN_DV)
        o_ref[:, hs] = (_rms(o[i], out_gain) * z_ref[:, hs].astype(F32)).astype(o_ref.dtype)


def _gdn(proj, gb, out_norm):
    s = proj.shape[0]
    c = GDN_CHUNK
    hb = GDN_HEAD_BLOCK
    ng = GDN_HEADS // hb
    bw = hb * GDN_DK

    def cur(sec):
        return pl.BlockSpec((c, bw), lambda g, t, sec=sec: (t, sec * ng + g))

    return pl.pallas_call(
        _gdn_kernel,
        grid=(ng, s // c),
        in_specs=[cur(0), cur(1), cur(2), cur(3),
                  pl.BlockSpec((c, LANES), lambda g, t: (t, g)),
                  pl.BlockSpec((1, GDN_DV), lambda g, t: (0, 0))],
        out_specs=pl.BlockSpec((c, bw), lambda g, t: (t, g)),
        out_shape=jax.ShapeDtypeStruct((s, GDN_V_W), BF16),
        scratch_shapes=[pltpu.VMEM((hb, GDN_DK, GDN_DV), F32)],
        compiler_params=_compiler_params(("parallel", "arbitrary")),
        name="gdn_delta_rule",
    )(proj, proj, proj, proj, gb, out_norm.reshape(1, GDN_DV))


def _overlapped_row_chunks(n_chunks, lhs_rows, w_ref, finish_rows):
    done = []
    prev = None
    for c in range(n_chunks + 1):
        cur = None
        if c < n_chunks:
            lhs = lhs_rows(c)
            if c >= 2:
                lhs = _ordered_after(lhs, done[c - 2])
            cur = jnp.dot(lhs, w_ref[...], preferred_element_type=F32)
        if c >= 1:
            done.append(finish_rows(c - 1, prev))
        prev = cur


def _matmul_norm_res_kernel(a_ref, w_ref, g_ref, x_ref, o_ref, *, row_chunk):
    def rows(c):
        return pl.ds(c * row_chunk, row_chunk)

    def finish(c, m):
        out = x_ref[rows(c), :] + _rms(m, g_ref[...])
        o_ref[rows(c), :] = out
        return out

    _overlapped_row_chunks(a_ref.shape[0] // row_chunk, lambda c: a_ref[rows(c), :], w_ref, finish)


def _matmul_norm_res(a, w, gain, x, *, tm, row_chunk):
    s, k = a.shape
    d = w.shape[1]
    return pl.pallas_call(
        functools.partial(_matmul_norm_res_kernel, row_chunk=row_chunk),
        grid=(s // tm,),
        in_specs=[pl.BlockSpec((tm, k), lambda i: (i, 0)),
                  pl.BlockSpec((k, d), lambda i: (0, 0)),
                  pl.BlockSpec((1, d), lambda i: (0, 0)),
                  pl.BlockSpec((tm, d), lambda i: (i, 0))],
        out_specs=pl.BlockSpec((tm, d), lambda i: (i, 0)),
        out_shape=jax.ShapeDtypeStruct((s, d), F32),
        compiler_params=_compiler_params(("parallel",)),
        name="matmul_norm_res",
    )(a, w, gain.reshape(1, d), x)


def _ffn_kernel(x_ref, gpre_ref, wg_ref, wu_ref, wd_ref, gpost_ref, o_ref, xn_ref, acc_ref):
    j = pl.program_id(1)

    @pl.when(j == 0)
    def _():
        xn_ref[...] = _rms(x_ref[...], gpre_ref[...]).astype(BF16)
        acc_ref[...] = jnp.zeros_like(acc_ref)

    xn = xn_ref[...]
    gate = jnp.dot(xn, wg_ref[...], preferred_element_type=F32)
    up = jnp.dot(xn, wu_ref[...], preferred_element_type=F32)
    act = (gate * _sigmoid(gate) * up).astype(BF16)
    acc_ref[...] += jnp.dot(act, wd_ref[...], preferred_element_type=F32)

    @pl.when(j == pl.num_programs(1) - 1)
    def _():
        o_ref[...] = x_ref[...] + _rms(acc_ref[...], gpost_ref[...])


def _ffn(x, g_pre, w_gate_up, w_down, g_post, *, tm, tf):
    s, d = x.shape
    d_ff = w_down.shape[0]
    nf = d_ff // tf
    return pl.pallas_call(
        _ffn_kernel,
        grid=(s // tm, nf),
        in_specs=[pl.BlockSpec((tm, d), lambda i, j: (i, 0)),
                  pl.BlockSpec((1, d), lambda i, j: (0, 0)),
                  pl.BlockSpec((d, tf), lambda i, j: (0, j)),
                  pl.BlockSpec((d, tf), lambda i, j: (0, nf + j)),
                  pl.BlockSpec((tf, d), lambda i, j: (j, 0)),
                  pl.BlockSpec((1, d), lambda i, j: (0, 0))],
        out_specs=pl.BlockSpec((tm, d), lambda i, j: (i, 0)),
        out_shape=jax.ShapeDtypeStruct((s, d), F32),
        scratch_shapes=[pltpu.VMEM((tm, d), BF16), pltpu.VMEM((tm, d), F32)],
        compiler_params=_compiler_params(("parallel", "arbitrary")),
        name="swiglu_ffn",
    )(x, g_pre.reshape(1, d), w_gate_up, w_gate_up, w_down, g_post.reshape(1, d))


def _t5_bucket(dist):
    max_exact = NUM_BUCKETS // 2
    d_f = jnp.maximum(dist, 1).astype(F32)
    large = max_exact + (jnp.log(d_f / max_exact) / math.log(MAX_DISTANCE / max_exact)
                         * (NUM_BUCKETS - max_exact)).astype(jnp.int32)
    large = jnp.minimum(large, NUM_BUCKETS - 1)
    return jnp.where(dist < max_exact, dist, large)


def _tile_offsets(group):
    w = DSA_SPAN
    classes = DSA_RESIDUES // DSA_GROUPS[group][1]
    rows = w // classes
    i = jnp.arange(w)
    return (i % rows) * classes + i // rows


def _band_buckets(group):
    w = DSA_SPAN
    off = _tile_offsets(group)
    rel = off[:, None] + w - jnp.concatenate([off, off + w])[None, :]
    band = (rel >= 0) & (rel <= w)
    return jnp.where(band, _t5_bucket(jnp.clip(rel, 0, w) * DSA_GROUPS[group][1]), -1).astype(jnp.int32)


def _dsa_tiles_per_step(group):
    classes = DSA_RESIDUES // DSA_GROUPS[group][1]
    return max(1, BF16_SUBLANES * classes // DSA_SPAN)


def _dsa_kernel(tab_ref, bucket_ref, q_ref, kp_ref, kc_ref, vp_ref, vc_ref, o_ref, lse_ref, bias_ref, *, group):
    w = DSA_SPAN
    n_tiles = _dsa_tiles_per_step(group)
    classes = q_ref.shape[0]
    rows = q_ref.shape[1] // n_tiles
    step = pl.program_id(1)

    @pl.when((pl.program_id(0) == 0) & (step == 0))
    def _():
        bucket = bucket_ref[...]
        key_col = lax.broadcasted_iota(jnp.int32, (w, 2 * w), 1)

        def per_head(h, carry):
            b = jnp.full(bucket.shape, MASK_VALUE, F32)
            for t in range(NUM_BUCKETS):
                b = jnp.where(bucket == t, tab_ref[t, group * DSA_HEADS + h] * LOG2_E, b)
            bias_ref[0, h] = b
            bias_ref[1, h] = jnp.where(key_col < w, MASK_VALUE, b)
            return carry

        lax.fori_loop(0, DSA_HEADS, per_head, 0)

    def tiles(ref, cols):
        blk = ref[:, :, cols]
        if n_tiles == 1:
            return [blk.reshape(w, blk.shape[-1])]
        blk = blk.astype(F32)
        return [blk[:, t * rows:(t + 1) * rows, :].reshape(w, blk.shape[-1]).astype(ref.dtype)
                for t in range(n_tiles)]

    def store(ref, cols, parts):
        parts = [p.reshape(classes, rows, p.shape[-1]) for p in parts]
        val = parts[0] if n_tiles == 1 else jnp.concatenate(parts, axis=1)
        ref[:, :, cols] = val.astype(ref.dtype)

    first = jnp.where(step == 0, 1, 0)
    lane = lax.broadcasted_iota(jnp.int32, (w, LANES), 1)
    lse_all = [jnp.zeros((w, LANES), F32) for _ in range(n_tiles)]
    for h in range(DSA_HEADS):
        hs = slice(h * DSA_DH, (h + 1) * DSA_DH)
        q_t = tiles(q_ref, hs)
        k_t = tiles(kc_ref, hs)
        v_t = tiles(vc_ref, hs)
        k_prev = [tiles(kp_ref, hs)[-1]] + k_t[:-1]
        v_prev = [tiles(vp_ref, hs)[-1]] + v_t[:-1]
        outs = []
        for t in range(n_tiles):
            k = jnp.concatenate([k_prev[t], k_t[t]], axis=0)
            v = jnp.concatenate([v_prev[t], v_t[t]], axis=0)
            s = _bdot_nt(q_t[t], k) + bias_ref[first if t == 0 else 0, h]
            m = jnp.max(s, axis=-1, keepdims=True)
            p = jnp.exp2(s - m)
            l = jnp.sum(p, axis=-1, keepdims=True)
            outs.append(_bdot(p, v) / l)
            lse_all[t] = jnp.where(lane == h, (m + jnp.log2(l)) * LN_2, lse_all[t])
        store(o_ref, hs, outs)
    store(lse_ref, slice(None), lse_all)


def _dsa_group(q_all, kv, rel_bias, group):
    s = q_all.shape[0]
    dilation = DSA_GROUPS[group][1]
    w = DSA_SPAN
    gw = DSA_GROUP_W
    classes = DSA_RESIDUES // dilation
    class_len = s // DSA_RESIDUES
    blk_rows = _dsa_tiles_per_step(group) * w // classes
    nb = class_len // blk_rows

    def view(a):
        return a.reshape(classes, dilation, class_len, a.shape[-1])

    def spec(width, col, prev=False):
        def imap(r, n):
            return (0, r, jnp.maximum(n - 1, 0) if prev else n, col)
        return pl.BlockSpec((classes, None, blk_rows, width), imap)

    o, lse = pl.pallas_call(
        functools.partial(_dsa_kernel, group=group),
        grid=(dilation, nb),
        in_specs=[pl.BlockSpec(memory_space=pltpu.SMEM),
                  pl.BlockSpec((w, 2 * w), lambda r, n: (0, 0)),
                  spec(gw, group),
                  spec(gw, group, prev=True), spec(gw, group),
                  spec(gw, N_GROUPS + group, prev=True), spec(gw, N_GROUPS + group)],
        out_specs=[spec(gw, 0), spec(LANES, 0)],
        out_shape=[jax.ShapeDtypeStruct((classes, dilation, class_len, gw), BF16),
                   jax.ShapeDtypeStruct((classes, dilation, class_len, LANES), F32)],
        scratch_shapes=[pltpu.VMEM((2, DSA_HEADS, w, 2 * w), F32)],
        compiler_params=_compiler_params(("arbitrary", "arbitrary")),
        name=f"dsa_attention_g{group}",
    )(rel_bias, _band_buckets(group), view(q_all), view(kv), view(kv), view(kv), view(kv))
    return o.reshape(s, gw), lse.reshape(s, LANES)


def _to_residue_major_kernel(x_ref, o_ref):
    rows = o_ref.shape[1]
    x = x_ref[...].reshape(rows, DSA_RESIDUES, x_ref.shape[-1])
    o_ref[...] = pltpu.einshape("lrd->rld", x)


def _from_residue_major_kernel(x_ref, o_ref):
    rows = x_ref.shape[1]
    o_ref[...] = pltpu.einshape("rld->lrd", x_ref[...]).reshape(rows * DSA_RESIDUES, x_ref.shape[-1])


def _residue_major(x, *, rows, inverse):
    s, d = x.shape
    class_len = s // DSA_RESIDUES
    natural = pl.BlockSpec((rows * DSA_RESIDUES, d), lambda i: (i, 0))
    major = pl.BlockSpec((DSA_RESIDUES, rows, d), lambda i: (0, i, 0))
    if inverse:
        body, specs, arg, shape = (_from_residue_major_kernel, (major, natural),
                                   x.reshape(DSA_RESIDUES, class_len, d), (s, d))
    else:
        body, specs, arg, shape = _to_residue_major_kernel, (natural, major), x, (DSA_RESIDUES, class_len, d)
    out = pl.pallas_call(
        body,
        grid=(class_len // rows,),
        in_specs=[specs[0]],
        out_specs=specs[1],
        out_shape=jax.ShapeDtypeStruct(shape, x.dtype),
        compiler_params=_compiler_params(("parallel",)),
        name="from_residue_major" if inverse else "to_residue_major",
    )(arg)
    return out.reshape(s, d)


def _dsa_out_kernel(o0_ref, o1_ref, o2_ref, l0_ref, l1_ref, l2_ref, w_ref, g_ref, x_ref, out_ref, *, row_chunk):
    o_refs = [o0_ref, o1_ref, o2_ref]
    l_refs = [l0_ref, l1_ref, l2_ref]

    def rows(c):
        return pl.ds(c * row_chunk, row_chunk)

    def merged_rows(c):
        lses = [l[rows(c), :] for l in l_refs]
        top = jnp.maximum(jnp.maximum(lses[0], lses[1]), lses[2])
        es = [jnp.exp(l - top) for l in lses]
        inv_den = 1.0 / (es[0] + es[1] + es[2])
        wts = [e * inv_den for e in es]
        heads = []
        for h in range(DSA_HEADS):
            hs = slice(h * DSA_DH, (h + 1) * DSA_DH)
            merged = wts[0][:, h:h + 1] * o_refs[0][rows(c), hs].astype(F32)
            for g in range(1, N_GROUPS):
                merged = merged + wts[g][:, h:h + 1] * o_refs[g][rows(c), hs].astype(F32)
            heads.append(merged.astype(BF16))
        return jnp.concatenate(heads, axis=1)

    def finish(c, m):
        out = x_ref[rows(c), :] + _rms(m, g_ref[...])
        out_ref[rows(c), :] = out
        return out

    _overlapped_row_chunks(x_ref.shape[0] // row_chunk, merged_rows, w_ref, finish)


def _dsa_out(os_, lses, w, gain, x, *, tm, row_chunk):
    s, k = os_[0].shape
    d = w.shape[1]
    o_spec = pl.BlockSpec((tm, k), lambda i: (i, 0))
    l_spec = pl.BlockSpec((tm, LANES), lambda i: (i, 0))
    return pl.pallas_call(
        functools.partial(_dsa_out_kernel, row_chunk=row_chunk),
        grid=(s // tm,),
        in_specs=[o_spec, o_spec, o_spec, l_spec, l_spec, l_spec,
                  pl.BlockSpec((k, d), lambda i: (0, 0)),
                  pl.BlockSpec((1, d), lambda i: (0, 0)),
                  pl.BlockSpec((tm, d), lambda i: (i, 0))],
        out_specs=pl.BlockSpec((tm, d), lambda i: (i, 0)),
        out_shape=jax.ShapeDtypeStruct((s, d), F32),
        compiler_params=_compiler_params(("parallel",)),
        name="dsa_merge_out",
    )(*os_, *lses, w, gain.reshape(1, d), x)


def _tiles(s):
    tm_proj = min(1024, s)
    eighth = tm_proj // 8
    return dict(tm_proj=tm_proj, tn_proj=1024, tm_out=min(512, s), out_chunk=128, tm_ffn=min(512, s), tf_ffn=512,
                perm_rows=32, gdn_proj_chunks=(2 * eighth, 2 * eighth, 2 * eighth, eighth, eighth))


def kernel(x, norm_gains, ffn_w_gate_up, ffn_w_down, gdn_w_in, gdn_conv_w, gdn_a_log, gdn_dt_bias,
           gdn_out_norm, gdn_w_out, kv_norm, kv_w, dsa_w_q, dsa_w_out, rel_bias):
    b, s, d = x.shape
    assert b == 1 and d == D_MODEL and s % (DSA_SPAN * DSA_GROUPS[-1][1]) == 0
    t = _tiles(s)
    depth = norm_gains.shape[0]
    n_a = gdn_w_in.shape[0]
    xs = x.reshape(s, d)
    kv = None
    main_w = GDN_CONV_W + GDN_V_W
    for layer in range(depth):
        gains = norm_gains[layer]
        if layer < n_a:
            w_in = gdn_w_in[layer]
            proj = _gdn_in_proj(xs, gains[0], w_in.astype(BF16), gdn_conv_w[layer],
                                tm=t["tm_proj"], tn=t["tn_proj"], row_chunks=t["gdn_proj_chunks"])
            gb = _gdn_gates(xs, gains[0], w_in[:, main_w:], gdn_a_log[layer], gdn_dt_bias[layer],
                            tm=t["tm_proj"])
            o = _gdn(proj, gb, gdn_out_norm[layer])
            xs = _matmul_norm_res(o, gdn_w_out[layer].astype(BF16), gains[1], xs,
                                  tm=t["tm_out"], row_chunk=t["out_chunk"])
        else:
            j = layer - n_a
            q_all = _norm_matmul(xs, gains[0], dsa_w_q[j].astype(BF16),
                                 tm=t["tm_proj"], tn=t["tn_proj"], out_dtype=BF16,
                                 out_scale=DSA_DH ** -0.5 * LOG2_E)
            outs = [_dsa_group(q_all, kv, rel_bias, g) for g in range(N_GROUPS)]
            xs = _dsa_out([o for o, _ in outs], [l for _, l in outs], dsa_w_out[j].astype(BF16),
                          gains[1], xs, tm=t["tm_out"], row_chunk=t["out_chunk"])
        xs = _ffn(xs, gains[2], ffn_w_gate_up[layer].astype(BF16), ffn_w_down[layer].astype(BF16),
                  gains[3], tm=t["tm_ffn"], tf=t["tf_ffn"])
        if layer == n_a - 1 and depth > n_a:
            xs = _residue_major(xs, rows=t["perm_rows"], inverse=False)
            kv = _norm_matmul(xs, kv_norm, kv_w.astype(BF16), tm=t["tm_proj"], tn=t["tn_proj"], out_dtype=BF16)
    if depth > n_a:
        xs = _residue_major(xs, rows=t["perm_rows"], inverse=True)
    return xs.reshape(b, s, d)
```

```python
import functools
import math

import jax
import jax.numpy as jnp
from jax import lax
from jax.experimental import pallas as pl
from jax.experimental.pallas import tpu as pltpu

F32 = jnp.float32
BF16 = jnp.bfloat16

RMS_EPS = 1e-6
D_MODEL = 2048

GDN_HEADS = 16
GDN_DK = 128
GDN_DV = 128
GDN_CONV = 4
GDN_QK_W = GDN_HEADS * GDN_DK
GDN_V_W = GDN_HEADS * GDN_DV
GDN_CONV_W = 2 * GDN_QK_W + GDN_V_W
GDN_CHUNK = 128
GDN_HEAD_BLOCK = 16
GDN_INV_BASE = 8
CONV_HALO_ROWS = 8

DSA_GROUPS = ((128, 1), (512, 4), (2048, 16))
N_GROUPS = len(DSA_GROUPS)
DSA_HEADS = 16
DSA_DH = 128
DSA_SPAN = 128
DSA_GROUP_W = DSA_HEADS * DSA_DH
NUM_BUCKETS = 32
MAX_DISTANCE = 2048
MASK_VALUE = -1e30
LOG2_E = math.log2(math.e)
LN_2 = math.log(2.0)

DSA_RESIDUES = DSA_GROUPS[-1][1]

LANES = 128
BF16_SUBLANES = 16
VMEM_LIMIT_BYTES = 56 * 1024 * 1024


def _compiler_params(semantics):
    return pltpu.CompilerParams(dimension_semantics=semantics, vmem_limit_bytes=VMEM_LIMIT_BYTES)


def _rms(x, gain):
    ms = jnp.mean(x * x, axis=-1, keepdims=True)
    return x * lax.rsqrt(ms + RMS_EPS) * gain


def _sigmoid(x):
    return 1.0 / (1.0 + jnp.exp(-x))


def _silu(x):
    h = 0.5 * x
    return h + h * jnp.tanh(h)


def _bdot(a, b):
    return jnp.dot(a.astype(BF16), b.astype(BF16), preferred_element_type=F32)


def _bdot_nt(a, b):
    return lax.dot_general(a.astype(BF16), b.astype(BF16), (((1,), (1,)), ((), ())),
                           preferred_element_type=F32)


def _bdot_tn(a, b):
    return lax.dot_general(a.astype(BF16), b.astype(BF16), (((0,), (0,)), ((), ())),
                           preferred_element_type=F32)


def _norm_matmul_kernel(x_ref, g_ref, w_ref, o_ref, xn_ref, *, out_scale):
    @pl.when(pl.program_id(1) == 0)
    def _():
        xn_ref[...] = _rms(x_ref[...], g_ref[...]).astype(BF16)

    y = jnp.dot(xn_ref[...], w_ref[...], preferred_element_type=F32)
    if out_scale != 1.0:
        y = y * out_scale
    o_ref[...] = y.astype(o_ref.dtype)


def _norm_matmul(x, gain, w, layer, *, tm, tn, out_dtype, out_scale=1.0):
    s, d = x.shape
    n = w.shape[2]
    return pl.pallas_call(
        functools.partial(_norm_matmul_kernel, out_scale=out_scale),
        grid=(s // tm, n // tn),
        in_specs=[pl.BlockSpec((tm, d), lambda i, j: (i, 0)),
                  pl.BlockSpec((1, d), lambda i, j: (0, 0)),
                  pl.BlockSpec((None, d, tn), lambda i, j: (layer, 0, j))],
        out_specs=pl.BlockSpec((tm, tn), lambda i, j: (i, j)),
        out_shape=jax.ShapeDtypeStruct((s, n), out_dtype),
        scratch_shapes=[pltpu.VMEM((tm, d), BF16)],
        compiler_params=_compiler_params(("parallel", "arbitrary")),
        name="norm_matmul",
    )(x, gain.reshape(1, d), w)


def _ordered_after(x, anchor):
    rows, lanes = BF16_SUBLANES, LANES
    a = lax.bitcast_convert_type(jnp.abs(anchor[:rows, :lanes].astype(F32)), jnp.int32)
    zero = lax.shift_right_logical(a, 31)
    tile = lax.bitcast_convert_type(x[:rows, :lanes].astype(F32), jnp.int32) | zero
    tile = lax.bitcast_convert_type(tile, F32).astype(x.dtype)
    top = jnp.concatenate([tile, x[:rows, lanes:]], axis=1)
    return jnp.concatenate([top, x[rows:, :]], axis=0)


def _gdn_in_proj_kernel(x_ref, g_ref, w_ref, cw_ref, o_ref, xn_ref, halo_ref, raw_ref,
                        *, qk_blocks, v_blocks, row_chunks):
    i = pl.program_id(0)
    j = pl.program_id(1)
    tm = x_ref.shape[0]
    tn = w_ref.shape[1]

    @pl.when(j == 0)
    def _():
        xn_ref[...] = _rms(x_ref[...], g_ref[...]).astype(BF16)

    @pl.when((i == 0) & (j == 0))
    def _():
        halo_ref[...] = jnp.zeros_like(halo_ref)

    chunks = []
    for rows in row_chunks:
        chunks.append((sum(r for _, r in chunks), rows))
    assert sum(row_chunks) == tm
    n_chunks = len(chunks)

    def conv_silu(r0, rows):
        raw = raw_ref[pl.ds(r0, rows), :]
        if r0 == 0:
            halo = jnp.where(i > 0, halo_ref[j], 0.0)
        else:
            halo = raw_ref[pl.ds(r0 - CONV_HALO_ROWS, CONV_HALO_ROWS), :]
        cw = cw_ref[...]
        groups = jnp.concatenate([halo, raw], axis=0).reshape(rows // CONV_HALO_ROWS + 1, CONV_HALO_ROWS, tn)
        in_group = lax.broadcasted_iota(jnp.int32, (rows // CONV_HALO_ROWS, CONV_HALO_ROWS, tn), 1)
        y = cw[GDN_CONV - 1:GDN_CONV, :] * raw
        for t in range(GDN_CONV - 1):
            back = GDN_CONV - 1 - t
            rot = pltpu.roll(groups, back, axis=1)
            shifted = jnp.where(in_group < back, rot[:-1], rot[1:]).reshape(rows, tn)
            y = y + cw[t:t + 1, :] * shifted
        return _silu(y)

    def finish_qk(r0, rows):
        y = conv_silu(r0, rows)
        c = jnp.where(j < qk_blocks, float(GDN_DK), 1.0).astype(F32)
        out = []
        for h in range(tn // GDN_DK):
            yh = y[:, h * GDN_DK:(h + 1) * GDN_DK]
            ss = jnp.sum(yh * yh, axis=-1, keepdims=True)
            out.append((yh * lax.rsqrt(ss * c + RMS_EPS * c)).astype(o_ref.dtype))
        return jnp.concatenate(out, axis=1)

    def finish_v(r0, rows):
        return conv_silu(r0, rows).astype(o_ref.dtype)

    def finish_z(r0, rows):
        return _silu(raw_ref[pl.ds(r0, rows), :]).astype(o_ref.dtype)

    def run(finish):
        done = []
        for c in range(n_chunks + 1):
            if c < n_chunks:
                rows = pl.ds(*chunks[c])
                lhs = xn_ref[rows, :]
                if c >= 2:
                    lhs = _ordered_after(lhs, done[c - 2])
                raw_ref[rows, :] = jnp.dot(lhs, w_ref[...], preferred_element_type=F32)
            if c >= 1:
                out = finish(*chunks[c - 1])
                o_ref[pl.ds(*chunks[c - 1]), :] = out
                done.append(out)

    @pl.when(j < 2 * qk_blocks)
    def _():
        run(finish_qk)
        halo_ref[j] = raw_ref[pl.ds(tm - CONV_HALO_ROWS, CONV_HALO_ROWS), :]

    @pl.when((j >= 2 * qk_blocks) & (j < 2 * qk_blocks + v_blocks))
    def _():
        run(finish_v)
        halo_ref[j] = raw_ref[pl.ds(tm - CONV_HALO_ROWS, CONV_HALO_ROWS), :]

    @pl.when(j >= 2 * qk_blocks + v_blocks)
    def _():
        run(finish_z)


def _gdn_in_proj(x, gain, w_in, conv_w, layer, *, tm, tn, row_chunks):
    s, d = x.shape
    n = GDN_CONV_W + GDN_V_W
    conv_blocks = GDN_CONV_W // tn
    return pl.pallas_call(
        functools.partial(_gdn_in_proj_kernel, qk_blocks=GDN_QK_W // tn, v_blocks=GDN_V_W // tn,
                          row_chunks=row_chunks),
        grid=(s // tm, n // tn),
        in_specs=[pl.BlockSpec((tm, d), lambda i, j: (i, 0)),
                  pl.BlockSpec((1, d), lambda i, j: (0, 0)),
                  pl.BlockSpec((None, d, tn), lambda i, j: (layer, 0, j)),
                  pl.BlockSpec((None, GDN_CONV, tn), lambda i, j: (layer, 0, jnp.minimum(j, conv_blocks - 1)))],
        out_specs=pl.BlockSpec((tm, tn), lambda i, j: (i, j)),
        out_shape=jax.ShapeDtypeStruct((s, n), BF16),
        scratch_shapes=[pltpu.VMEM((tm, d), BF16),
                        pltpu.VMEM((conv_blocks, CONV_HALO_ROWS, tn), F32),
                        pltpu.VMEM((tm, tn), F32)],
        compiler_params=_compiler_params(("arbitrary", "arbitrary")),
        name="gdn_in_proj",
    )(x, gain.reshape(1, d), w_in, conv_w)


def _gdn_gates_kernel(x_ref, g_ref, w_ref, alog_ref, dtb_ref, o_ref):
    xn = _rms(x_ref[...], g_ref[...]).astype(BF16)
    p = jnp.dot(xn, w_ref[...], preferred_element_type=F32)
    lane = lax.broadcasted_iota(jnp.int32, p.shape, 1)
    beta = _sigmoid(p)
    a = p + dtb_ref[...]
    softplus = jnp.maximum(a, 0.0) + jnp.log(1.0 + jnp.exp(-jnp.abs(a)))
    g = -jnp.exp(alog_ref[...]) * softplus
    o_ref[...] = jnp.where(lane < GDN_HEADS, beta, jnp.where(lane < 2 * GDN_HEADS, g, 0.0))


def _gdn_gates(x, gain, w_ba, a_log, dt_bias, *, tm):
    s, d = x.shape
    pad = LANES - 2 * GDN_HEADS
    w = jnp.pad(w_ba, ((0, 0), (0, pad))).astype(BF16)
    zeros_h = jnp.zeros((GDN_HEADS,), F32)
    alog = jnp.pad(jnp.concatenate([zeros_h, a_log.astype(F32)]), (0, pad)).reshape(1, LANES)
    dtb = jnp.pad(jnp.concatenate([zeros_h, dt_bias.astype(F32)]), (0, pad)).reshape(1, LANES)
    return pl.pallas_call(
        _gdn_gates_kernel,
        grid=(s // tm,),
        in_specs=[pl.BlockSpec((tm, d), lambda i: (i, 0)),
                  pl.BlockSpec((1, d), lambda i: (0, 0)),
                  pl.BlockSpec((d, LANES), lambda i: (0, 0)),
                  pl.BlockSpec((1, LANES), lambda i: (0, 0)),
                  pl.BlockSpec((1, LANES), lambda i: (0, 0))],
        out_specs=pl.BlockSpec((tm, LANES), lambda i: (i, 0)),
        out_shape=jax.ShapeDtypeStruct((s, LANES), F32),
        compiler_params=_compiler_params(("parallel",)),
        name="gdn_gates",
    )(x, gain.reshape(1, d), w, alog, dtb)


def _unit_lower_inverse(lows, row, col):
    c = lows[0].shape[0]
    eye = (row == col).astype(F32)
    blk = GDN_INV_BASE
    diag_mask = (row // blk) == (col // blk)
    powers = [jnp.where(diag_mask, low, 0.0) for low in lows]
    invs = [eye - p for p in powers]
    span = 2
    while span < blk:
        powers = [_bdot(p, p) for p in powers]
        invs = [x + _bdot(x, p) for x, p in zip(invs, powers)]
        span *= 2
    while blk < c:
        off_mask = ((row // (2 * blk)) == (col // (2 * blk))) & ((row // blk) != (col // blk))
        offs = [_bdot(jnp.where(off_mask, low, 0.0), x) for low, x in zip(lows, invs)]
        invs = [x - _bdot(x, t) for x, t in zip(invs, offs)]
        blk *= 2
    return invs


def _gdn_kernel(q_ref, k_ref, v_ref, z_ref, gb_ref, on_ref, o_ref, state_ref):
    c_len = q_ref.shape[0]
    step = pl.program_id(1)

    @pl.when(step == 0)
    def _():
        state_ref[...] = jnp.zeros_like(state_ref)

    row = lax.broadcasted_iota(jnp.int32, (c_len, c_len), 0)
    col = lax.broadcasted_iota(jnp.int32, (c_len, c_len), 1)
    causal = row >= col
    strict = row > col

    gb = gb_ref[...]
    tri = causal.astype(BF16)
    gb_hi = gb.astype(BF16)
    gb_lo = (gb - gb_hi.astype(F32)).astype(BF16)
    gcum = (jnp.dot(tri, gb_hi, preferred_element_type=F32)
            + jnp.dot(tri, gb_lo, preferred_element_type=F32))
    gcum_t = gcum.T

    heads = range(GDN_HEAD_BLOCK)

    def head_slices(ref):
        return [ref[:, i * GDN_DK:(i + 1) * GDN_DK].astype(F32) for i in heads]

    qn, kn, v_all = head_slices(q_ref), head_slices(k_ref), head_slices(v_ref)
    beta = [gb[:, i:i + 1] for i in heads]
    gcol = [gcum[:, GDN_HEADS + i:GDN_HEADS + i + 1] for i in heads]
    grow = [gcum_t[GDN_HEADS + i:GDN_HEADS + i + 1, :] for i in heads]
    decay = [jnp.exp(jnp.where(causal, gc - gr, MASK_VALUE)) for gc, gr in zip(gcol, grow)]
    e_col = [jnp.exp(gc) for gc in gcol]
    g_last = [gc[c_len - 1:c_len, :] for gc in gcol]
    kb = [k * b for k, b in zip(kn, beta)]
    kk_qk = [_bdot_nt(jnp.concatenate([kb[i], qn[i]], axis=0), kn[i]) for i in heads]
    low = [jnp.where(strict, kk_qk[i][:c_len] * decay[i], 0.0) for i in heads]
    intra = [kk_qk[i][c_len:] * decay[i] for i in heads]
    inv = _unit_lower_inverse(low, row, col)
    sol = [_bdot(inv[i], jnp.concatenate([v_all[i] * beta[i], kb[i] * e_col[i]], axis=1)) for i in heads]
    state = [state_ref[i] for i in heads]
    ws_qs = [_bdot(jnp.concatenate([sol[i][:, GDN_DV:], qn[i] * e_col[i]], axis=0), state[i]) for i in heads]
    v_new = [sol[i][:, :GDN_DV] - ws_qs[i][:c_len] for i in heads]
    o = [ws_qs[i][c_len:] + _bdot(intra[i], v_new[i]) for i in heads]
    for i in heads:
        k_dec = kn[i] * jnp.exp(g_last[i] - gcol[i])
        state_ref[i] = state[i] * jnp.exp(g_last[i]) + _bdot_tn(k_dec, v_new[i])
    out_gain = on_ref[...]
    for i in heads:
        hs = slice(i * GDN_DV, (i + 1) * GDN_DV)
        o_ref[:, hs] = (_rms(o[i], out_gain) * z_ref[:, hs].astype(F32)).astype(o_ref.dtype)


def _gdn(proj, gb, out_norm):
    s = proj.shape[0]
    c = GDN_CHUNK
    hb = GDN_HEAD_BLOCK
    ng = GDN_HEADS // hb
    bw = hb * GDN_DK

    def cur(sec):
        return pl.BlockSpec((c, bw), lambda g, t, sec=sec: (t, sec * ng + g))

    return pl.pallas_call(
        _gdn_kernel,
        grid=(ng, s // c),
        in_specs=[cur(0), cur(1), cur(2), cur(3),
                  pl.BlockSpec((c, LANES), lambda g, t: (t, g)),
                  pl.BlockSpec((1, GDN_DV), lambda g, t: (0, 0))],
        out_specs=pl.BlockSpec((c, bw), lambda g, t: (t, g)),
        out_shape=jax.ShapeDtypeStruct((s, GDN_V_W), BF16),
        scratch_shapes=[pltpu.VMEM((hb, GDN_DK, GDN_DV), F32)],
        compiler_params=_compiler_params(("parallel", "arbitrary")),
        name="gdn_delta_rule",
    )(proj, proj, proj, proj, gb, out_norm.reshape(1, GDN_DV))


def _overlapped_row_chunks(n_chunks, lhs_rows, w_ref, finish_rows):
    done = []
    prev = None
    for c in range(n_chunks + 1):
        cur = None
        if c < n_chunks:
            lhs = lhs_rows(c)
            if c >= 2:
                lhs = _ordered_after(lhs, done[c - 2])
            cur = jnp.dot(lhs, w_ref[...], preferred_element_type=F32)
        if c >= 1:
            done.append(finish_rows(c - 1, prev))
        prev = cur


def _matmul_norm_res_kernel(a_ref, w_ref, g_ref, x_ref, o_ref, *, row_chunk):
    def rows(c):
        return pl.ds(c * row_chunk, row_chunk)

    def finish(c, m):
        out = x_ref[rows(c), :] + _rms(m, g_ref[...])
        o_ref[rows(c), :] = out
        return out

    _overlapped_row_chunks(a_ref.shape[0] // row_chunk, lambda c: a_ref[rows(c), :], w_ref, finish)


def _matmul_norm_res(a, w, gain, x, layer, *, tm, row_chunk):
    s, k = a.shape
    d = w.shape[2]
    return pl.pallas_call(
        functools.partial(_matmul_norm_res_kernel, row_chunk=row_chunk),
        grid=(s // tm,),
        in_specs=[pl.BlockSpec((tm, k), lambda i: (i, 0)),
                  pl.BlockSpec((None, k, d), lambda i: (layer, 0, 0)),
                  pl.BlockSpec((1, d), lambda i: (0, 0)),
                  pl.BlockSpec((tm, d), lambda i: (i, 0))],
        out_specs=pl.BlockSpec((tm, d), lambda i: (i, 0)),
        out_shape=jax.ShapeDtypeStruct((s, d), F32),
        compiler_params=_compiler_params(("parallel",)),
        name="matmul_norm_res",
    )(a, w, gain.reshape(1, d), x)


def _ffn_kernel(x_ref, gpre_ref, wg_ref, wu_ref, wd_ref, gpost_ref, o_ref, xn_ref, acc_ref):
    j = pl.program_id(1)

    @pl.when(j == 0)
    def _():
        xn_ref[...] = _rms(x_ref[...], gpre_ref[...]).astype(BF16)
        acc_ref[...] = jnp.zeros_like(acc_ref)

    xn = xn_ref[...]
    gate = jnp.dot(xn, wg_ref[...], preferred_element_type=F32)
    up = jnp.dot(xn, wu_ref[...], preferred_element_type=F32)
    act = (_silu(gate) * up).astype(BF16)
    acc_ref[...] += jnp.dot(act, wd_ref[...], preferred_element_type=F32)

    @pl.when(j == pl.num_programs(1) - 1)
    def _():
        o_ref[...] = x_ref[...] + _rms(acc_ref[...], gpost_ref[...])


def _ffn(x, g_pre, w_gate_up, w_down, g_post, layer, *, tm, tf):
    s, d = x.shape
    d_ff = w_down.shape[1]
    nf = d_ff // tf
    return pl.pallas_call(
        _ffn_kernel,
        grid=(s // tm, nf),
        in_specs=[pl.BlockSpec((tm, d), lambda i, j: (i, 0)),
                  pl.BlockSpec((1, d), lambda i, j: (0, 0)),
                  pl.BlockSpec((None, d, tf), lambda i, j: (layer, 0, j)),
                  pl.BlockSpec((None, d, tf), lambda i, j: (layer, 0, nf + j)),
                  pl.BlockSpec((None, tf, d), lambda i, j: (layer, j, 0)),
                  pl.BlockSpec((1, d), lambda i, j: (0, 0))],
        out_specs=pl.BlockSpec((tm, d), lambda i, j: (i, 0)),
        out_shape=jax.ShapeDtypeStruct((s, d), F32),
        scratch_shapes=[pltpu.VMEM((tm, d), BF16), pltpu.VMEM((tm, d), F32)],
        compiler_params=_compiler_params(("parallel", "arbitrary")),
        name="swiglu_ffn",
    )(x, g_pre.reshape(1, d), w_gate_up, w_gate_up, w_down, g_post.reshape(1, d))


def _t5_bucket(dist):
    max_exact = NUM_BUCKETS // 2
    d_f = jnp.maximum(dist, 1).astype(F32)
    large = max_exact + (jnp.log(d_f / max_exact) / math.log(MAX_DISTANCE / max_exact)
                         * (NUM_BUCKETS - max_exact)).astype(jnp.int32)
    large = jnp.minimum(large, NUM_BUCKETS - 1)
    return jnp.where(dist < max_exact, dist, large)


def _tile_offsets(group):
    w = DSA_SPAN
    classes = DSA_RESIDUES // DSA_GROUPS[group][1]
    rows = w // classes
    i = jnp.arange(w)
    return (i % rows) * classes + i // rows


def _band_buckets(group):
    w = DSA_SPAN
    off = _tile_offsets(group)
    rel = off[:, None] + w - jnp.concatenate([off, off + w])[None, :]
    band = (rel >= 0) & (rel <= w)
    return jnp.where(band, _t5_bucket(jnp.clip(rel, 0, w) * DSA_GROUPS[group][1]), -1).astype(jnp.int32)


def _dsa_min_tiles(group):
    classes = DSA_RESIDUES // DSA_GROUPS[group][1]
    return max(1, BF16_SUBLANES * classes // DSA_SPAN)


def _dsa_kernel(tab_ref, bucket_ref, q_ref, kp_ref, kc_ref, vp_ref, vc_ref, o_ref, lse_ref, bias_ref, *, group):
    w = DSA_SPAN
    classes = q_ref.shape[0]
    rows = w // classes
    n_tiles = q_ref.shape[1] // rows
    step = pl.program_id(1)

    @pl.when((pl.program_id(0) == 0) & (step == 0))
    def _():
        bucket = bucket_ref[...]
        key_col = lax.broadcasted_iota(jnp.int32, (w, 2 * w), 1)

        def per_head(h, carry):
            b = jnp.full(bucket.shape, MASK_VALUE, F32)
            for t in range(NUM_BUCKETS):
                b = jnp.where(bucket == t, tab_ref[t, group * DSA_HEADS + h] * LOG2_E, b)
            bias_ref[0, h] = b
            bias_ref[1, h] = jnp.where(key_col < w, MASK_VALUE, b)
            return carry

        lax.fori_loop(0, DSA_HEADS, per_head, 0)

    def tiles(ref, cols):
        blk = ref[:, :, cols]
        if rows % BF16_SUBLANES:
            blk = blk.astype(F32)
        return [blk[:, t * rows:(t + 1) * rows, :].reshape(w, blk.shape[-1]).astype(ref.dtype)
                for t in range(ref.shape[1] // rows)]

    def store(ref, cols, parts):
        parts = [p.reshape(classes, rows, p.shape[-1]) for p in parts]
        ref[:, :, cols] = jnp.concatenate(parts, axis=1).astype(ref.dtype)

    first = jnp.where(step == 0, 1, 0)
    lane = lax.broadcasted_iota(jnp.int32, (w, LANES), 1)
    lse_all = [jnp.zeros((w, LANES), F32) for _ in range(n_tiles)]
    for h in range(DSA_HEADS):
        hs = slice(h * DSA_DH, (h + 1) * DSA_DH)
        q_t = tiles(q_ref, hs)
        k_t = tiles(kc_ref, hs)
        v_t = tiles(vc_ref, hs)
        k_prev = [tiles(kp_ref, hs)[-1]] + k_t[:-1]
        v_prev = [tiles(vp_ref, hs)[-1]] + v_t[:-1]
        outs = []
        for t in range(n_tiles):
            k = jnp.concatenate([k_prev[t], k_t[t]], axis=0)
            v = jnp.concatenate([v_prev[t], v_t[t]], axis=0)
            s = _bdot_nt(q_t[t], k) + bias_ref[first if t == 0 else 0, h]
            m = jnp.max(s, axis=-1, keepdims=True)
            p = jnp.exp2(s - m)
            l = jnp.sum(p, axis=-1, keepdims=True)
            outs.append(_bdot(p, v) / l)
            lse_all[t] = jnp.where(lane == h, (m + jnp.log2(l)) * LN_2, lse_all[t])
        store(o_ref, hs, outs)
    store(lse_ref, slice(None), lse_all)


def _dsa_group(q_all, kv, rel_bias, group, *, tiles_per_step):
    s = q_all.shape[0]
    dilation = DSA_GROUPS[group][1]
    w = DSA_SPAN
    gw = DSA_GROUP_W
    classes = DSA_RESIDUES // dilation
    class_len = s // DSA_RESIDUES
    rows = w // classes
    prev_tiles = _dsa_min_tiles(group)
    n_tiles = max(prev_tiles, min(tiles_per_step, class_len // rows))
    assert n_tiles % prev_tiles == 0 and class_len % (n_tiles * rows) == 0

    def view(a):
        return a.reshape(classes, dilation, class_len, a.shape[-1])

    def cur(width, col):
        return pl.BlockSpec((classes, None, n_tiles * rows, width), lambda r, n: (0, r, n, col))

    def prev(width, col):
        ratio = n_tiles // prev_tiles
        return pl.BlockSpec((classes, None, prev_tiles * rows, width),
                            lambda r, n: (0, r, jnp.maximum(n * ratio - 1, 0), col))

    o, lse = pl.pallas_call(
        functools.partial(_dsa_kernel, group=group),
        grid=(dilation, class_len // (n_tiles * rows)),
        in_specs=[pl.BlockSpec(memory_space=pltpu.SMEM),
                  pl.BlockSpec((w, 2 * w), lambda r, n: (0, 0)),
                  cur(gw, group),
                  prev(gw, group), cur(gw, group),
                  prev(gw, N_GROUPS + group), cur(gw, N_GROUPS + group)],
        out_specs=[cur(gw, 0), cur(LANES, 0)],
        out_shape=[jax.ShapeDtypeStruct((classes, dilation, class_len, gw), BF16),
                   jax.ShapeDtypeStruct((classes, dilation, class_len, LANES), F32)],
        scratch_shapes=[pltpu.VMEM((2, DSA_HEADS, w, 2 * w), F32)],
        compiler_params=_compiler_params(("arbitrary", "arbitrary")),
        name=f"dsa_attention_g{group}",
    )(rel_bias, _band_buckets(group), view(q_all), view(kv), view(kv), view(kv), view(kv))
    return o.reshape(s, gw), lse.reshape(s, LANES)


def _to_residue_major_kernel(x_ref, o_ref):
    rows = o_ref.shape[1]
    x = x_ref[...].reshape(rows, DSA_RESIDUES, x_ref.shape[-1])
    o_ref[...] = pltpu.einshape("lrd->rld", x)


def _from_residue_major_kernel(x_ref, o_ref):
    rows = x_ref.shape[1]
    o_ref[...] = pltpu.einshape("rld->lrd", x_ref[...]).reshape(rows * DSA_RESIDUES, x_ref.shape[-1])


def _residue_major(x, *, rows, inverse):
    s, d = x.shape
    class_len = s // DSA_RESIDUES
    natural = pl.BlockSpec((rows * DSA_RESIDUES, d), lambda i: (i, 0))
    major = pl.BlockSpec((DSA_RESIDUES, rows, d), lambda i: (0, i, 0))
    if inverse:
        body, specs, arg, shape = (_from_residue_major_kernel, (major, natural),
                                   x.reshape(DSA_RESIDUES, class_len, d), (s, d))
    else:
        body, specs, arg, shape = _to_residue_major_kernel, (natural, major), x, (DSA_RESIDUES, class_len, d)
    out = pl.pallas_call(
        body,
        grid=(class_len // rows,),
        in_specs=[specs[0]],
        out_specs=specs[1],
        out_shape=jax.ShapeDtypeStruct(shape, x.dtype),
        compiler_params=_compiler_params(("parallel",)),
        name="from_residue_major" if inverse else "to_residue_major",
    )(arg)
    return out.reshape(s, d)


def _dsa_out_kernel(o0_ref, o1_ref, o2_ref, l0_ref, l1_ref, l2_ref, w_ref, g_ref, x_ref, out_ref, *, row_chunk):
    o_refs = [o0_ref, o1_ref, o2_ref]
    l_refs = [l0_ref, l1_ref, l2_ref]

    def rows(c):
        return pl.ds(c * row_chunk, row_chunk)

    def merged_rows(c):
        lses = [l[rows(c), :] for l in l_refs]
        top = jnp.maximum(jnp.maximum(lses[0], lses[1]), lses[2])
        es = [jnp.exp(l - top) for l in lses]
        inv_den = 1.0 / (es[0] + es[1] + es[2])
        wts = [e * inv_den for e in es]
        heads = []
        for h in range(DSA_HEADS):
            hs = slice(h * DSA_DH, (h + 1) * DSA_DH)
            merged = wts[0][:, h:h + 1] * o_refs[0][rows(c), hs].astype(F32)
            for g in range(1, N_GROUPS):
                merged = merged + wts[g][:, h:h + 1] * o_refs[g][rows(c), hs].astype(F32)
            heads.append(merged.astype(BF16))
        return jnp.concatenate(heads, axis=1)

    def finish(c, m):
        out = x_ref[rows(c), :] + _rms(m, g_ref[...])
        out_ref[rows(c), :] = out
        return out

    _overlapped_row_chunks(x_ref.shape[0] // row_chunk, merged_rows, w_ref, finish)


def _dsa_out(os_, lses, w, gain, x, layer, *, tm, row_chunk):
    s, k = os_[0].shape
    d = w.shape[2]
    o_spec = pl.BlockSpec((tm, k), lambda i: (i, 0))
    l_spec = pl.BlockSpec((tm, LANES), lambda i: (i, 0))
    return pl.pallas_call(
        functools.partial(_dsa_out_kernel, row_chunk=row_chunk),
        grid=(s // tm,),
        in_specs=[o_spec, o_spec, o_spec, l_spec, l_spec, l_spec,
                  pl.BlockSpec((None, k, d), lambda i: (layer, 0, 0)),
                  pl.BlockSpec((1, d), lambda i: (0, 0)),
                  pl.BlockSpec((tm, d), lambda i: (i, 0))],
        out_specs=pl.BlockSpec((tm, d), lambda i: (i, 0)),
        out_shape=jax.ShapeDtypeStruct((s, d), F32),
        compiler_params=_compiler_params(("parallel",)),
        name="dsa_merge_out",
    )(*os_, *lses, w, gain.reshape(1, d), x)


def _tiles(s):
    tm_proj = min(1024, s)
    eighth = tm_proj // 8
    return dict(tm_proj=tm_proj, tn_proj=1024, tm_out=min(512, s), out_chunk=128, tm_ffn=min(512, s), tf_ffn=512, dsa_tiles=4,
                perm_rows=32, gdn_proj_chunks=(2 * eighth, 2 * eighth, 2 * eighth, eighth, eighth))


def kernel(x, norm_gains, ffn_w_gate_up, ffn_w_down, gdn_w_in, gdn_conv_w, gdn_a_log, gdn_dt_bias,
           gdn_out_norm, gdn_w_out, kv_norm, kv_w, dsa_w_q, dsa_w_out, rel_bias):
    b, s, d = x.shape
    assert b == 1 and d == D_MODEL and s % (DSA_SPAN * DSA_GROUPS[-1][1]) == 0
    t = _tiles(s)
    depth = norm_gains.shape[0]
    n_a = gdn_w_in.shape[0]
    xs = x.reshape(s, d)
    kv = None
    main_w = GDN_CONV_W + GDN_V_W
    ffn_w_gate_up, ffn_w_down = ffn_w_gate_up.astype(BF16), ffn_w_down.astype(BF16)
    gdn_w_in_bf, gdn_w_out = gdn_w_in.astype(BF16), gdn_w_out.astype(BF16)
    dsa_w_q, dsa_w_out = dsa_w_q.astype(BF16), dsa_w_out.astype(BF16)
    for layer in range(depth):
        gains = norm_gains[layer]
        if layer < n_a:
            proj = _gdn_in_proj(xs, gains[0], gdn_w_in_bf, gdn_conv_w, layer,
                                tm=t["tm_proj"], tn=t["tn_proj"], row_chunks=t["gdn_proj_chunks"])
            gb = _gdn_gates(xs, gains[0], gdn_w_in[layer][:, main_w:], gdn_a_log[layer], gdn_dt_bias[layer],
                            tm=t["tm_proj"])
            o = _gdn(proj, gb, gdn_out_norm[layer])
            xs = _matmul_norm_res(o, gdn_w_out, gains[1], xs, layer, tm=t["tm_out"], row_chunk=t["out_chunk"])
        else:
            j = layer - n_a
            q_all = _norm_matmul(xs, gains[0], dsa_w_q, j, tm=t["tm_proj"], tn=t["tn_proj"], out_dtype=BF16,
                                 out_scale=DSA_DH ** -0.5 * LOG2_E)
            outs = [_dsa_group(q_all, kv, rel_bias, g, tiles_per_step=t["dsa_tiles"]) for g in range(N_GROUPS)]
            xs = _dsa_out([o for o, _ in outs], [l for _, l in outs], dsa_w_out, gains[1], xs, j,
                          tm=t["tm_out"], row_chunk=t["out_chunk"])
        xs = _ffn(xs, gains[2], ffn_w_gate_up, ffn_w_down, gains[3], layer, tm=t["tm_ffn"], tf=t["tf_ffn"])
        if layer == n_a - 1 and depth > n_a:
            xs = _residue_major(xs, rows=t["perm_rows"], inverse=False)
            kv = _norm_matmul(xs, kv_norm, kv_w.astype(BF16)[None], 0, tm=t["tm_proj"], tn=t["tn_proj"],
                              out_dtype=BF16)
    if depth > n_a:
        xs = _residue_major(xs, rows=t["perm_rows"], inverse=True)
    return xs.reshape(b, s, d)
```

```python
import functools
import math

import jax
import jax.numpy as jnp
from jax import lax
from jax.experimental import pallas as pl
from jax.experimental.pallas import tpu as pltpu

F32 = jnp.float32
BF16 = jnp.bfloat16

RMS_EPS = 1e-6
D_MODEL = 2048

GDN_HEADS = 16
GDN_DK = 128
GDN_DV = 128
GDN_CONV = 4
GDN_QK_W = GDN_HEADS * GDN_DK
GDN_V_W = GDN_HEADS * GDN_DV
GDN_CONV_W = 2 * GDN_QK_W + GDN_V_W
GDN_CHUNK = 128
GDN_HEAD_BLOCK = 16
GDN_INV_BASE = 8
CONV_HALO_ROWS = 8

DSA_GROUPS = ((128, 1), (512, 4), (2048, 16))
N_GROUPS = len(DSA_GROUPS)
DSA_HEADS = 16
DSA_DH = 128
DSA_SPAN = 128
DSA_GROUP_W = DSA_HEADS * DSA_DH
NUM_BUCKETS = 32
MAX_DISTANCE = 2048
MASK_VALUE = -1e30
LOG2_E = math.log2(math.e)
LN_2 = math.log(2.0)

DSA_RESIDUES = DSA_GROUPS[-1][1]

LANES = 128
BF16_SUBLANES = 16
VMEM_LIMIT_BYTES = 56 * 1024 * 1024


def _compiler_params(semantics):
    return pltpu.CompilerParams(dimension_semantics=semantics, vmem_limit_bytes=VMEM_LIMIT_BYTES)


def _rms(x, gain):
    ms = jnp.mean(x * x, axis=-1, keepdims=True)
    return x * lax.rsqrt(ms + RMS_EPS) * gain


def _sigmoid(x):
    return 1.0 / (1.0 + jnp.exp(-x))


def _silu(x):
    h = 0.5 * x
    return h + h * jnp.tanh(h)


def _bdot(a, b):
    return jnp.dot(a.astype(BF16), b.astype(BF16), preferred_element_type=F32)


def _bdot_nt(a, b):
    return lax.dot_general(a.astype(BF16), b.astype(BF16), (((1,), (1,)), ((), ())),
                           preferred_element_type=F32)


def _bdot_tn(a, b):
    return lax.dot_general(a.astype(BF16), b.astype(BF16), (((0,), (0,)), ((), ())),
                           preferred_element_type=F32)


def _norm_matmul_kernel(x_ref, g_ref, w_ref, o_ref, xn_ref, *, out_scale):
    @pl.when(pl.program_id(1) == 0)
    def _():
        xn_ref[...] = _rms(x_ref[...], g_ref[...]).astype(BF16)

    y = jnp.dot(xn_ref[...], w_ref[...], preferred_element_type=F32)
    if out_scale != 1.0:
        y = y * out_scale
    o_ref[...] = y.astype(o_ref.dtype)


def _norm_matmul(x, gain, w, layer, *, tm, tn, out_dtype, out_scale=1.0):
    s, d = x.shape
    n = w.shape[2]
    return pl.pallas_call(
        functools.partial(_norm_matmul_kernel, out_scale=out_scale),
        grid=(s // tm, n // tn),
        in_specs=[pl.BlockSpec((tm, d), lambda i, j: (i, 0)),
                  pl.BlockSpec((1, d), lambda i, j: (0, 0)),
                  pl.BlockSpec((None, d, tn), lambda i, j: (layer, 0, j))],
        out_specs=pl.BlockSpec((tm, tn), lambda i, j: (i, j)),
        out_shape=jax.ShapeDtypeStruct((s, n), out_dtype),
        scratch_shapes=[pltpu.VMEM((tm, d), BF16)],
        compiler_params=_compiler_params(("parallel", "arbitrary")),
        name="norm_matmul",
    )(x, gain.reshape(1, d), w)


def _ordered_after(x, anchor):
    rows, lanes = BF16_SUBLANES, LANES
    a = lax.bitcast_convert_type(jnp.abs(anchor[:rows, :lanes].astype(F32)), jnp.int32)
    zero = lax.shift_right_logical(a, 31)
    tile = lax.bitcast_convert_type(x[:rows, :lanes].astype(F32), jnp.int32) | zero
    tile = lax.bitcast_convert_type(tile, F32).astype(x.dtype)
    top = jnp.concatenate([tile, x[:rows, lanes:]], axis=1)
    return jnp.concatenate([top, x[rows:, :]], axis=0)


def _gdn_in_proj_kernel(x_ref, g_ref, w_ref, cw_ref, wba_ref, alog_ref, dtb_ref, o_ref, gb_ref,
                        xn_ref, halo_ref, raw_ref, *, qk_blocks, v_blocks, row_chunks):
    i = pl.program_id(0)
    j = pl.program_id(1)
    tm = x_ref.shape[0]
    tn = w_ref.shape[1]

    @pl.when(j == 0)
    def _():
        xn = _rms(x_ref[...], g_ref[...]).astype(BF16)
        xn_ref[...] = xn
        gb_ref[...] = _gdn_gate_values(jnp.dot(xn, wba_ref[...], preferred_element_type=F32),
                                       alog_ref[...], dtb_ref[...])

    @pl.when((i == 0) & (j == 0))
    def _():
        halo_ref[...] = jnp.zeros_like(halo_ref)

    chunks = []
    for rows in row_chunks:
        chunks.append((sum(r for _, r in chunks), rows))
    assert sum(row_chunks) == tm
    n_chunks = len(chunks)

    def conv_silu(r0, rows):
        raw = raw_ref[pl.ds(r0, rows), :]
        if r0 == 0:
            halo = jnp.where(i > 0, halo_ref[j], 0.0)
        else:
            halo = raw_ref[pl.ds(r0 - CONV_HALO_ROWS, CONV_HALO_ROWS), :]
        cw = cw_ref[...]
        groups = jnp.concatenate([halo, raw], axis=0).reshape(rows // CONV_HALO_ROWS + 1, CONV_HALO_ROWS, tn)
        in_group = lax.broadcasted_iota(jnp.int32, (rows // CONV_HALO_ROWS, CONV_HALO_ROWS, tn), 1)
        y = cw[GDN_CONV - 1:GDN_CONV, :] * raw
        for t in range(GDN_CONV - 1):
            back = GDN_CONV - 1 - t
            rot = pltpu.roll(groups, back, axis=1)
            shifted = jnp.where(in_group < back, rot[:-1], rot[1:]).reshape(rows, tn)
            y = y + cw[t:t + 1, :] * shifted
        return _silu(y)

    def finish_qk(r0, rows):
        y = conv_silu(r0, rows)
        c = jnp.where(j < qk_blocks, float(GDN_DK), 1.0).astype(F32)
        out = []
        for h in range(tn // GDN_DK):
            yh = y[:, h * GDN_DK:(h + 1) * GDN_DK]
            ss = jnp.sum(yh * yh, axis=-1, keepdims=True)
            out.append((yh * lax.rsqrt(ss * c + RMS_EPS * c)).astype(o_ref.dtype))
        return jnp.concatenate(out, axis=1)

    def finish_v(r0, rows):
        return conv_silu(r0, rows).astype(o_ref.dtype)

    def finish_z(r0, rows):
        return _silu(raw_ref[pl.ds(r0, rows), :]).astype(o_ref.dtype)

    def run(finish):
        done = []
        for c in range(n_chunks + 1):
            if c < n_chunks:
                rows = pl.ds(*chunks[c])
                lhs = xn_ref[rows, :]
                if c >= 2:
                    lhs = _ordered_after(lhs, done[c - 2])
                raw_ref[rows, :] = jnp.dot(lhs, w_ref[...], preferred_element_type=F32)
            if c >= 1:
                out = finish(*chunks[c - 1])
                o_ref[pl.ds(*chunks[c - 1]), :] = out
                done.append(out)

    @pl.when(j < 2 * qk_blocks)
    def _():
        run(finish_qk)
        halo_ref[j] = raw_ref[pl.ds(tm - CONV_HALO_ROWS, CONV_HALO_ROWS), :]

    @pl.when((j >= 2 * qk_blocks) & (j < 2 * qk_blocks + v_blocks))
    def _():
        run(finish_v)
        halo_ref[j] = raw_ref[pl.ds(tm - CONV_HALO_ROWS, CONV_HALO_ROWS), :]

    @pl.when(j >= 2 * qk_blocks + v_blocks)
    def _():
        run(finish_z)


def _gdn_in_proj(x, gain, w_in, conv_w, layer, w_ba, a_log, dt_bias, *, tm, tn, row_chunks):
    s, d = x.shape
    n = GDN_CONV_W + GDN_V_W
    conv_blocks = GDN_CONV_W // tn
    gate_w, alog, dtb = _gdn_gate_params(w_ba, a_log, dt_bias)
    lane_row = pl.BlockSpec((1, LANES), lambda i, j: (0, 0))
    return pl.pallas_call(
        functools.partial(_gdn_in_proj_kernel, qk_blocks=GDN_QK_W // tn, v_blocks=GDN_V_W // tn,
                          row_chunks=row_chunks),
        grid=(s // tm, n // tn),
        in_specs=[pl.BlockSpec((tm, d), lambda i, j: (i, 0)),
                  pl.BlockSpec((1, d), lambda i, j: (0, 0)),
                  pl.BlockSpec((None, d, tn), lambda i, j: (layer, 0, j)),
                  pl.BlockSpec((None, GDN_CONV, tn), lambda i, j: (layer, 0, jnp.minimum(j, conv_blocks - 1))),
                  pl.BlockSpec((d, LANES), lambda i, j: (0, 0)), lane_row, lane_row],
        out_specs=[pl.BlockSpec((tm, tn), lambda i, j: (i, j)),
                   pl.BlockSpec((tm, LANES), lambda i, j: (i, 0))],
        out_shape=[jax.ShapeDtypeStruct((s, n), BF16), jax.ShapeDtypeStruct((s, LANES), F32)],
        scratch_shapes=[pltpu.VMEM((tm, d), BF16),
                        pltpu.VMEM((conv_blocks, CONV_HALO_ROWS, tn), F32),
                        pltpu.VMEM((tm, tn), F32)],
        compiler_params=_compiler_params(("arbitrary", "arbitrary")),
        name="gdn_in_proj",
    )(x, gain.reshape(1, d), w_in, conv_w, gate_w, alog, dtb)


def _gdn_gate_values(p, alog, dtb):
    lane = lax.broadcasted_iota(jnp.int32, p.shape, 1)
    beta = _sigmoid(p)
    a = p + dtb
    softplus = jnp.maximum(a, 0.0) + jnp.log(1.0 + jnp.exp(-jnp.abs(a)))
    g = -jnp.exp(alog) * softplus
    return jnp.where(lane < GDN_HEADS, beta, jnp.where(lane < 2 * GDN_HEADS, g, 0.0))


def _gdn_gate_params(w_ba, a_log, dt_bias):
    pad = LANES - 2 * GDN_HEADS
    w = jnp.pad(w_ba, ((0, 0), (0, pad))).astype(BF16)
    zeros_h = jnp.zeros((GDN_HEADS,), F32)
    alog = jnp.pad(jnp.concatenate([zeros_h, a_log.astype(F32)]), (0, pad)).reshape(1, LANES)
    dtb = jnp.pad(jnp.concatenate([zeros_h, dt_bias.astype(F32)]), (0, pad)).reshape(1, LANES)
    return w, alog, dtb


def _unit_lower_inverse(lows, row, col):
    c = lows[0].shape[0]
    eye = (row == col).astype(F32)
    blk = GDN_INV_BASE
    diag_mask = (row // blk) == (col // blk)
    powers = [jnp.where(diag_mask, low, 0.0) for low in lows]
    invs = [eye - p for p in powers]
    span = 2
    while span < blk:
        powers = [_bdot(p, p) for p in powers]
        invs = [x + _bdot(x, p) for x, p in zip(invs, powers)]
        span *= 2
    while blk < c:
        off_mask = ((row // (2 * blk)) == (col // (2 * blk))) & ((row // blk) != (col // blk))
        offs = [_bdot(jnp.where(off_mask, low, 0.0), x) for low, x in zip(lows, invs)]
        invs = [x - _bdot(x, t) for x, t in zip(invs, offs)]
        blk *= 2
    return invs


def _gdn_kernel(q_ref, k_ref, v_ref, z_ref, gb_ref, on_ref, o_ref, state_ref):
    c_len = q_ref.shape[0]
    step = pl.program_id(1)

    @pl.when(step == 0)
    def _():
        state_ref[...] = jnp.zeros_like(state_ref)

    row = lax.broadcasted_iota(jnp.int32, (c_len, c_len), 0)
    col = lax.broadcasted_iota(jnp.int32, (c_len, c_len), 1)
    causal = row >= col
    strict = row > col

    gb = gb_ref[...]
    tri = causal.astype(BF16)
    gb_hi = gb.astype(BF16)
    gb_lo = (gb - gb_hi.astype(F32)).astype(BF16)
    gcum = (jnp.dot(tri, gb_hi, preferred_element_type=F32)
            + jnp.dot(tri, gb_lo, preferred_element_type=F32))
    gcum_t = gcum.T

    heads = range(GDN_HEAD_BLOCK)

    def head_slices(ref):
        return [ref[:, i * GDN_DK:(i + 1) * GDN_DK].astype(F32) for i in heads]

    qn, kn, v_all = head_slices(q_ref), head_slices(k_ref), head_slices(v_ref)
    beta = [gb[:, i:i + 1] for i in heads]
    gcol = [gcum[:, GDN_HEADS + i:GDN_HEADS + i + 1] for i in heads]
    grow = [gcum_t[GDN_HEADS + i:GDN_HEADS + i + 1, :] for i in heads]
    decay = [jnp.exp(jnp.where(causal, gc - gr, MASK_VALUE)) for gc, gr in zip(gcol, grow)]
    e_col = [jnp.exp(gc) for gc in gcol]
    g_last = [gc[c_len - 1:c_len, :] for gc in gcol]
    kb = [k * b for k, b in zip(kn, beta)]
    kk_qk = [_bdot_nt(jnp.concatenate([kb[i], qn[i]], axis=0), kn[i]) for i in heads]
    low = [jnp.where(strict, kk_qk[i][:c_len] * decay[i], 0.0) for i in heads]
    intra = [kk_qk[i][c_len:] * decay[i] for i in heads]
    inv = _unit_lower_inverse(low, row, col)
    sol = [_bdot(inv[i], jnp.concatenate([v_all[i] * beta[i], kb[i] * e_col[i]], axis=1)) for i in heads]
    state = [state_ref[i] for i in heads]
    ws_qs = [_bdot(jnp.concatenate([sol[i][:, GDN_DV:], qn[i] * e_col[i]], axis=0), state[i]) for i in heads]
    v_new = [sol[i][:, :GDN_DV] - ws_qs[i][:c_len] for i in heads]
    o = [ws_qs[i][c_len:] + _bdot(intra[i], v_new[i]) for i in heads]
    for i in heads:
        k_dec = kn[i] * jnp.exp(g_last[i] - gcol[i])
        state_ref[i] = state[i] * jnp.exp(g_last[i]) + _bdot_tn(k_dec, v_new[i])
    out_gain = on_ref[...]
    for i in heads:
        hs = slice(i * GDN_DV, (i + 1) * GDN_DV)
        o_ref[:, hs] = (_rms(o[i], out_gain) * z_ref[:, hs].astype(F32)).astype(o_ref.dtype)


def _gdn(proj, gb, out_norm):
    s = proj.shape[0]
    c = GDN_CHUNK
    hb = GDN_HEAD_BLOCK
    ng = GDN_HEADS // hb
    bw = hb * GDN_DK

    def cur(sec):
        return pl.BlockSpec((c, bw), lambda g, t, sec=sec: (t, sec * ng + g))

    return pl.pallas_call(
        _gdn_kernel,
        grid=(ng, s // c),
        in_specs=[cur(0), cur(1), cur(2), cur(3),
                  pl.BlockSpec((c, LANES), lambda g, t: (t, g)),
                  pl.BlockSpec((1, GDN_DV), lambda g, t: (0, 0))],
        out_specs=pl.BlockSpec((c, bw), lambda g, t: (t, g)),
        out_shape=jax.ShapeDtypeStruct((s, GDN_V_W), BF16),
        scratch_shapes=[pltpu.VMEM((hb, GDN_DK, GDN_DV), F32)],
        compiler_params=_compiler_params(("parallel", "arbitrary")),
        name="gdn_delta_rule",
    )(proj, proj, proj, proj, gb, out_norm.reshape(1, GDN_DV))


def _overlapped_row_chunks(n_chunks, lhs_rows, w_ref, finish_rows):
    done = []
    prev = None
    for c in range(n_chunks + 1):
        cur = None
        if c < n_chunks:
            lhs = lhs_rows(c)
            if c >= 2:
                lhs = _ordered_after(lhs, done[c - 2])
            cur = jnp.dot(lhs, w_ref[...], preferred_element_type=F32)
        if c >= 1:
            done.append(finish_rows(c - 1, prev))
        prev = cur


def _matmul_norm_res_kernel(a_ref, w_ref, g_ref, x_ref, o_ref, *, row_chunk):
    def rows(c):
        return pl.ds(c * row_chunk, row_chunk)

    def finish(c, m):
        out = x_ref[rows(c), :] + _rms(m, g_ref[...])
        o_ref[rows(c), :] = out
        return out

    _overlapped_row_chunks(a_ref.shape[0] // row_chunk, lambda c: a_ref[rows(c), :], w_ref, finish)


def _matmul_norm_res(a, w, gain, x, layer, *, tm, row_chunk):
    s, k = a.shape
    d = w.shape[2]
    return pl.pallas_call(
        functools.partial(_matmul_norm_res_kernel, row_chunk=row_chunk),
        grid=(s // tm,),
        in_specs=[pl.BlockSpec((tm, k), lambda i: (i, 0)),
                  pl.BlockSpec((None, k, d), lambda i: (layer, 0, 0)),
                  pl.BlockSpec((1, d), lambda i: (0, 0)),
                  pl.BlockSpec((tm, d), lambda i: (i, 0))],
        out_specs=pl.BlockSpec((tm, d), lambda i: (i, 0)),
        out_shape=jax.ShapeDtypeStruct((s, d), F32),
        compiler_params=_compiler_params(("parallel",)),
        name="matmul_norm_res",
    )(a, w, gain.reshape(1, d), x)


def _ffn_kernel(x_ref, gpre_ref, wg_ref, wu_ref, wd_ref, gpost_ref, o_ref, xn_ref, acc_ref):
    j = pl.program_id(1)

    @pl.when(j == 0)
    def _():
        xn_ref[...] = _rms(x_ref[...], gpre_ref[...]).astype(BF16)
        acc_ref[...] = jnp.zeros_like(acc_ref)

    xn = xn_ref[...]
    gate = jnp.dot(xn, wg_ref[...], preferred_element_type=F32)
    up = jnp.dot(xn, wu_ref[...], preferred_element_type=F32)
    act = (_silu(gate) * up).astype(BF16)
    acc_ref[...] += jnp.dot(act, wd_ref[...], preferred_element_type=F32)

    @pl.when(j == pl.num_programs(1) - 1)
    def _():
        o_ref[...] = x_ref[...] + _rms(acc_ref[...], gpost_ref[...])


def _ffn(x, g_pre, w_gate_up, w_down, g_post, layer, *, tm, tf):
    s, d = x.shape
    d_ff = w_down.shape[1]
    nf = d_ff // tf
    return pl.pallas_call(
        _ffn_kernel,
        grid=(s // tm, nf),
        in_specs=[pl.BlockSpec((tm, d), lambda i, j: (i, 0)),
                  pl.BlockSpec((1, d), lambda i, j: (0, 0)),
                  pl.BlockSpec((None, d, tf), lambda i, j: (layer, 0, j)),
                  pl.BlockSpec((None, d, tf), lambda i, j: (layer, 0, nf + j)),
                  pl.BlockSpec((None, tf, d), lambda i, j: (layer, j, 0)),
                  pl.BlockSpec((1, d), lambda i, j: (0, 0))],
        out_specs=pl.BlockSpec((tm, d), lambda i, j: (i, 0)),
        out_shape=jax.ShapeDtypeStruct((s, d), F32),
        scratch_shapes=[pltpu.VMEM((tm, d), BF16), pltpu.VMEM((tm, d), F32)],
        compiler_params=_compiler_params(("parallel", "arbitrary")),
        name="swiglu_ffn",
    )(x, g_pre.reshape(1, d), w_gate_up, w_gate_up, w_down, g_post.reshape(1, d))


def _t5_bucket(dist):
    max_exact = NUM_BUCKETS // 2
    d_f = jnp.maximum(dist, 1).astype(F32)
    large = max_exact + (jnp.log(d_f / max_exact) / math.log(MAX_DISTANCE / max_exact)
                         * (NUM_BUCKETS - max_exact)).astype(jnp.int32)
    large = jnp.minimum(large, NUM_BUCKETS - 1)
    return jnp.where(dist < max_exact, dist, large)


def _tile_offsets(group):
    w = DSA_SPAN
    classes = DSA_RESIDUES // DSA_GROUPS[group][1]
    rows = w // classes
    i = jnp.arange(w)
    return (i % rows) * classes + i // rows


def _band_buckets(group):
    w = DSA_SPAN
    off = _tile_offsets(group)
    rel = off[:, None] + w - jnp.concatenate([off, off + w])[None, :]
    band = (rel >= 0) & (rel <= w)
    return jnp.where(band, _t5_bucket(jnp.clip(rel, 0, w) * DSA_GROUPS[group][1]), -1).astype(jnp.int32)


def _dsa_min_tiles(group):
    classes = DSA_RESIDUES // DSA_GROUPS[group][1]
    return max(1, BF16_SUBLANES * classes // DSA_SPAN)


def _dsa_kernel(tab_ref, bucket_ref, q_ref, kp_ref, kc_ref, vp_ref, vc_ref, o_ref, lse_ref, bias_ref, *, group):
    w = DSA_SPAN
    classes = q_ref.shape[0]
    rows = w // classes
    n_tiles = q_ref.shape[1] // rows
    step = pl.program_id(1)

    @pl.when((pl.program_id(0) == 0) & (step == 0))
    def _():
        bucket = bucket_ref[...]
        key_col = lax.broadcasted_iota(jnp.int32, (w, 2 * w), 1)

        def per_head(h, carry):
            b = jnp.full(bucket.shape, MASK_VALUE, F32)
            for t in range(NUM_BUCKETS):
                b = jnp.where(bucket == t, tab_ref[t, group * DSA_HEADS + h] * LOG2_E, b)
            bias_ref[0, h] = b
            bias_ref[1, h] = jnp.where(key_col < w, MASK_VALUE, b)
            return carry

        lax.fori_loop(0, DSA_HEADS, per_head, 0)

    def tiles(ref, cols):
        blk = ref[:, :, cols]
        if rows % BF16_SUBLANES:
            blk = blk.astype(F32)
        return [blk[:, t * rows:(t + 1) * rows, :].reshape(w, blk.shape[-1]).astype(ref.dtype)
                for t in range(ref.shape[1] // rows)]

    def store(ref, cols, parts):
        parts = [p.reshape(classes, rows, p.shape[-1]) for p in parts]
        ref[:, :, cols] = jnp.concatenate(parts, axis=1).astype(ref.dtype)

    first = jnp.where(step == 0, 1, 0)
    lane = lax.broadcasted_iota(jnp.int32, (w, LANES), 1)
    lse_all = [jnp.zeros((w, LANES), F32) for _ in range(n_tiles)]
    for h in range(DSA_HEADS):
        hs = slice(h * DSA_DH, (h + 1) * DSA_DH)
        q_t = tiles(q_ref, hs)
        k_t = tiles(kc_ref, hs)
        v_t = tiles(vc_ref, hs)
        k_prev = [tiles(kp_ref, hs)[-1]] + k_t[:-1]
        v_prev = [tiles(vp_ref, hs)[-1]] + v_t[:-1]
        outs = []
        for t in range(n_tiles):
            k = jnp.concatenate([k_prev[t], k_t[t]], axis=0)
            v = jnp.concatenate([v_prev[t], v_t[t]], axis=0)
            s = _bdot_nt(q_t[t], k) + bias_ref[first if t == 0 else 0, h]
            m = jnp.max(s, axis=-1, keepdims=True)
            p = jnp.exp2(s - m)
            l = jnp.sum(p, axis=-1, keepdims=True)
            outs.append(_bdot(p, v) / l)
            lse_all[t] = jnp.where(lane == h, (m + jnp.log2(l)) * LN_2, lse_all[t])
        store(o_ref, hs, outs)
    store(lse_ref, slice(None), lse_all)


def _dsa_group(q_all, kv, rel_bias, group, *, tiles_per_step):
    s = q_all.shape[0]
    dilation = DSA_GROUPS[group][1]
    w = DSA_SPAN
    gw = DSA_GROUP_W
    classes = DSA_RESIDUES // dilation
    class_len = s // DSA_RESIDUES
    rows = w // classes
    prev_tiles = _dsa_min_tiles(group)
    n_tiles = max(prev_tiles, min(tiles_per_step, class_len // rows))
    assert n_tiles % prev_tiles == 0 and class_len % (n_tiles * rows) == 0

    def view(a):
        return a.reshape(classes, dilation, class_len, a.shape[-1])

    def cur(width, col):
        return pl.BlockSpec((classes, None, n_tiles * rows, width), lambda r, n: (0, r, n, col))

    def prev(width, col):
        ratio = n_tiles // prev_tiles
        return pl.BlockSpec((classes, None, prev_tiles * rows, width),
                            lambda r, n: (0, r, jnp.maximum(n * ratio - 1, 0), col))

    o, lse = pl.pallas_call(
        functools.partial(_dsa_kernel, group=group),
        grid=(dilation, class_len // (n_tiles * rows)),
        in_specs=[pl.BlockSpec(memory_space=pltpu.SMEM),
                  pl.BlockSpec((w, 2 * w), lambda r, n: (0, 0)),
                  cur(gw, group),
                  prev(gw, group), cur(gw, group),
                  prev(gw, N_GROUPS + group), cur(gw, N_GROUPS + group)],
        out_specs=[cur(gw, 0), cur(LANES, 0)],
        out_shape=[jax.ShapeDtypeStruct((classes, dilation, class_len, gw), BF16),
                   jax.ShapeDtypeStruct((classes, dilation, class_len, LANES), F32)],
        scratch_shapes=[pltpu.VMEM((2, DSA_HEADS, w, 2 * w), F32)],
        compiler_params=_compiler_params(("arbitrary", "arbitrary")),
        name=f"dsa_attention_g{group}",
    )(rel_bias, _band_buckets(group), view(q_all), view(kv), view(kv), view(kv), view(kv))
    return o.reshape(s, gw), lse.reshape(s, LANES)


def _to_residue_major_kernel(x_ref, o_ref):
    rows = o_ref.shape[1]
    x = x_ref[...].reshape(rows, DSA_RESIDUES, x_ref.shape[-1])
    o_ref[...] = pltpu.einshape("lrd->rld", x)


def _from_residue_major_kernel(x_ref, o_ref):
    rows = x_ref.shape[1]
    o_ref[...] = pltpu.einshape("rld->lrd", x_ref[...]).reshape(rows * DSA_RESIDUES, x_ref.shape[-1])


def _residue_major(x, *, rows, inverse):
    s, d = x.shape
    class_len = s // DSA_RESIDUES
    natural = pl.BlockSpec((rows * DSA_RESIDUES, d), lambda i: (i, 0))
    major = pl.BlockSpec((DSA_RESIDUES, rows, d), lambda i: (0, i, 0))
    if inverse:
        body, specs, arg, shape = (_from_residue_major_kernel, (major, natural),
                                   x.reshape(DSA_RESIDUES, class_len, d), (s, d))
    else:
        body, specs, arg, shape = _to_residue_major_kernel, (natural, major), x, (DSA_RESIDUES, class_len, d)
    out = pl.pallas_call(
        body,
        grid=(class_len // rows,),
        in_specs=[specs[0]],
        out_specs=specs[1],
        out_shape=jax.ShapeDtypeStruct(shape, x.dtype),
        compiler_params=_compiler_params(("parallel",)),
        name="from_residue_major" if inverse else "to_residue_major",
    )(arg)
    return out.reshape(s, d)


def _dsa_out_kernel(o0_ref, o1_ref, o2_ref, l0_ref, l1_ref, l2_ref, w_ref, g_ref, x_ref, out_ref, *, row_chunk):
    o_refs = [o0_ref, o1_ref, o2_ref]
    l_refs = [l0_ref, l1_ref, l2_ref]

    def rows(c):
        return pl.ds(c * row_chunk, row_chunk)

    def merged_rows(c):
        lses = [l[rows(c), :] for l in l_refs]
        top = jnp.maximum(jnp.maximum(lses[0], lses[1]), lses[2])
        es = [jnp.exp(l - top) for l in lses]
        inv_den = 1.0 / (es[0] + es[1] + es[2])
        wts = [e * inv_den for e in es]
        heads = []
        for h in range(DSA_HEADS):
            hs = slice(h * DSA_DH, (h + 1) * DSA_DH)
            merged = wts[0][:, h:h + 1] * o_refs[0][rows(c), hs].astype(F32)
            for g in range(1, N_GROUPS):
                merged = merged + wts[g][:, h:h + 1] * o_refs[g][rows(c), hs].astype(F32)
            heads.append(merged.astype(BF16))
        return jnp.concatenate(heads, axis=1)

    def finish(c, m):
        out = x_ref[rows(c), :] + _rms(m, g_ref[...])
        out_ref[rows(c), :] = out
        return out

    _overlapped_row_chunks(x_ref.shape[0] // row_chunk, merged_rows, w_ref, finish)


def _dsa_out(os_, lses, w, gain, x, layer, *, tm, row_chunk):
    s, k = os_[0].shape
    d = w.shape[2]
    o_spec = pl.BlockSpec((tm, k), lambda i: (i, 0))
    l_spec = pl.BlockSpec((tm, LANES), lambda i: (i, 0))
    return pl.pallas_call(
        functools.partial(_dsa_out_kernel, row_chunk=row_chunk),
        grid=(s // tm,),
        in_specs=[o_spec, o_spec, o_spec, l_spec, l_spec, l_spec,
                  pl.BlockSpec((None, k, d), lambda i: (layer, 0, 0)),
                  pl.BlockSpec((1, d), lambda i: (0, 0)),
                  pl.BlockSpec((tm, d), lambda i: (i, 0))],
        out_specs=pl.BlockSpec((tm, d), lambda i: (i, 0)),
        out_shape=jax.ShapeDtypeStruct((s, d), F32),
        compiler_params=_compiler_params(("parallel",)),
        name="dsa_merge_out",
    )(*os_, *lses, w, gain.reshape(1, d), x)


def _tiles(s):
    tm_proj = min(1024, s)
    eighth = tm_proj // 8
    return dict(tm_proj=tm_proj, tn_proj=2048, tn_gdn=1024, tm_out=min(512, s), out_chunk=128, tm_ffn=min(512, s), tf_ffn=512, dsa_tiles=4,
                perm_rows=32, gdn_proj_chunks=(2 * eighth, 2 * eighth, 2 * eighth, eighth, eighth))


def kernel(x, norm_gains, ffn_w_gate_up, ffn_w_down, gdn_w_in, gdn_conv_w, gdn_a_log, gdn_dt_bias,
           gdn_out_norm, gdn_w_out, kv_norm, kv_w, dsa_w_q, dsa_w_out, rel_bias):
    b, s, d = x.shape
    assert b == 1 and d == D_MODEL and s % (DSA_SPAN * DSA_GROUPS[-1][1]) == 0
    t = _tiles(s)
    depth = norm_gains.shape[0]
    n_a = gdn_w_in.shape[0]
    xs = x.reshape(s, d)
    kv = None
    main_w = GDN_CONV_W + GDN_V_W
    ffn_w_gate_up, ffn_w_down = ffn_w_gate_up.astype(BF16), ffn_w_down.astype(BF16)
    gdn_w_in_bf, gdn_w_out = gdn_w_in[:, :, :main_w].astype(BF16), gdn_w_out.astype(BF16)
    dsa_w_q, dsa_w_out = dsa_w_q.astype(BF16), dsa_w_out.astype(BF16)
    for layer in range(depth):
        gains = norm_gains[layer]
        if layer < n_a:
            proj, gb = _gdn_in_proj(xs, gains[0], gdn_w_in_bf, gdn_conv_w, layer,
                                    gdn_w_in[layer][:, main_w:], gdn_a_log[layer], gdn_dt_bias[layer],
                                    tm=t["tm_proj"], tn=t["tn_gdn"], row_chunks=t["gdn_proj_chunks"])
            o = _gdn(proj, gb, gdn_out_norm[layer])
            xs = _matmul_norm_res(o, gdn_w_out, gains[1], xs, layer, tm=t["tm_out"], row_chunk=t["out_chunk"])
        else:
            j = layer - n_a
            q_all = _norm_matmul(xs, gains[0], dsa_w_q, j, tm=t["tm_proj"], tn=t["tn_proj"], out_dtype=BF16,
                                 out_scale=DSA_DH ** -0.5 * LOG2_E)
            outs = [_dsa_group(q_all, kv, rel_bias, g, tiles_per_step=t["dsa_tiles"]) for g in range(N_GROUPS)]
            xs = _dsa_out([o for o, _ in outs], [l for _, l in outs], dsa_w_out, gains[1], xs, j,
                          tm=t["tm_out"], row_chunk=t["out_chunk"])
        xs = _ffn(xs, gains[2], ffn_w_gate_up, ffn_w_down, gains[3], layer, tm=t["tm_ffn"], tf=t["tf_ffn"])
        if layer == n_a - 1 and depth > n_a:
            xs = _residue_major(xs, rows=t["perm_rows"], inverse=False)
            kv = _norm_matmul(xs, kv_norm, kv_w.astype(BF16)[None], 0, tm=t["tm_proj"], tn=t["tn_proj"],
                              out_dtype=BF16)
    if depth > n_a:
        xs = _residue_major(xs, rows=t["perm_rows"], inverse=True)
    return xs.reshape(b, s, d)
```

```python
import functools
import math

import jax
import jax.numpy as jnp
from jax import lax
from jax.experimental import pallas as pl
from jax.experimental.pallas import tpu as pltpu

F32 = jnp.float32
BF16 = jnp.bfloat16

RMS_EPS = 1e-6
D_MODEL = 2048

GDN_HEADS = 16
GDN_DK = 128
GDN_DV = 128
GDN_CONV = 4
GDN_QK_W = GDN_HEADS * GDN_DK
GDN_V_W = GDN_HEADS * GDN_DV
GDN_CONV_W = 2 * GDN_QK_W + GDN_V_W
GDN_CHUNK = 128
GDN_HEAD_BLOCK = 16
GDN_INV_BASE = 8
CONV_HALO_ROWS = 8

DSA_GROUPS = ((128, 1), (512, 4), (2048, 16))
N_GROUPS = len(DSA_GROUPS)
DSA_HEADS = 16
DSA_DH = 128
DSA_SPAN = 128
DSA_GROUP_W = DSA_HEADS * DSA_DH
NUM_BUCKETS = 32
MAX_DISTANCE = 2048
MASK_VALUE = -1e30
LOG2_E = math.log2(math.e)
LN_2 = math.log(2.0)

DSA_RESIDUES = DSA_GROUPS[-1][1]

LANES = 128
BF16_SUBLANES = 16
VMEM_BYTES = 64 * 1024 * 1024
VMEM_LIMIT_BYTES = VMEM_BYTES * 7 // 8
FFN_VMEM_LIMIT_BYTES = VMEM_BYTES * 31 // 32


def _compiler_params(semantics, vmem_limit_bytes=VMEM_LIMIT_BYTES):
    return pltpu.CompilerParams(dimension_semantics=semantics, vmem_limit_bytes=vmem_limit_bytes)


def _rms(x, gain):
    ms = jnp.mean(x * x, axis=-1, keepdims=True)
    return x * lax.rsqrt(ms + RMS_EPS) * gain


def _sigmoid(x):
    return 1.0 / (1.0 + jnp.exp(-x))


def _silu(x):
    h = 0.5 * x
    return h + h * jnp.tanh(h)


def _bdot(a, b):
    return jnp.dot(a.astype(BF16), b.astype(BF16), preferred_element_type=F32)


def _bdot_nt(a, b):
    return lax.dot_general(a.astype(BF16), b.astype(BF16), (((1,), (1,)), ((), ())),
                           preferred_element_type=F32)


def _bdot_tn(a, b):
    return lax.dot_general(a.astype(BF16), b.astype(BF16), (((0,), (0,)), ((), ())),
                           preferred_element_type=F32)


def _norm_matmul_kernel(x_ref, g_ref, w_ref, o_ref, xn_ref, *, out_scale):
    @pl.when(pl.program_id(1) == 0)
    def _():
        xn_ref[...] = _rms(x_ref[...], g_ref[...]).astype(BF16)

    y = jnp.dot(xn_ref[...], w_ref[...], preferred_element_type=F32)
    if out_scale != 1.0:
        y = y * out_scale
    o_ref[...] = y.astype(o_ref.dtype)


def _norm_matmul(x, gain, w, layer, *, tm, tn, out_dtype, out_scale=1.0):
    s, d = x.shape
    n = w.shape[2]
    return pl.pallas_call(
        functools.partial(_norm_matmul_kernel, out_scale=out_scale),
        grid=(s // tm, n // tn),
        in_specs=[pl.BlockSpec((tm, d), lambda i, j: (i, 0)),
                  pl.BlockSpec((1, d), lambda i, j: (0, 0)),
                  pl.BlockSpec((None, d, tn), lambda i, j: (layer, 0, j))],
        out_specs=pl.BlockSpec((tm, tn), lambda i, j: (i, j)),
        out_shape=jax.ShapeDtypeStruct((s, n), out_dtype),
        scratch_shapes=[pltpu.VMEM((tm, d), BF16)],
        compiler_params=_compiler_params(("parallel", "arbitrary")),
        name="norm_matmul",
    )(x, gain.reshape(1, d), w)


def _ordered_after(x, anchor):
    rows, lanes = BF16_SUBLANES, LANES
    a = lax.bitcast_convert_type(jnp.abs(anchor[:rows, :lanes].astype(F32)), jnp.int32)
    zero = lax.shift_right_logical(a, 31)
    tile = lax.bitcast_convert_type(x[:rows, :lanes].astype(F32), jnp.int32) | zero
    tile = lax.bitcast_convert_type(tile, F32).astype(x.dtype)
    top = jnp.concatenate([tile, x[:rows, lanes:]], axis=1)
    return jnp.concatenate([top, x[rows:, :]], axis=0)


def _gdn_in_proj_kernel(x_ref, g_ref, w_ref, cw_ref, wba_ref, alog_ref, dtb_ref, o_ref, gb_ref,
                        xn_ref, halo_ref, raw_ref, *, qk_blocks, v_blocks, row_chunks):
    i = pl.program_id(0)
    j = pl.program_id(1)
    tm = x_ref.shape[0]
    tn = w_ref.shape[1]

    @pl.when(j == 0)
    def _():
        xn = _rms(x_ref[...], g_ref[...]).astype(BF16)
        xn_ref[...] = xn
        gb_ref[...] = _gdn_gate_values(jnp.dot(xn, wba_ref[...], preferred_element_type=F32),
                                       alog_ref[...], dtb_ref[...])

    @pl.when((i == 0) & (j == 0))
    def _():
        halo_ref[...] = jnp.zeros_like(halo_ref)

    chunks = []
    for rows in row_chunks:
        chunks.append((sum(r for _, r in chunks), rows))
    assert sum(row_chunks) == tm
    n_chunks = len(chunks)

    def conv_silu(r0, rows):
        raw = raw_ref[pl.ds(r0, rows), :]
        if r0 == 0:
            halo = jnp.where(i > 0, halo_ref[j], 0.0)
        else:
            halo = raw_ref[pl.ds(r0 - CONV_HALO_ROWS, CONV_HALO_ROWS), :]
        cw = cw_ref[...]
        groups = jnp.concatenate([halo, raw], axis=0).reshape(rows // CONV_HALO_ROWS + 1, CONV_HALO_ROWS, tn)
        in_group = lax.broadcasted_iota(jnp.int32, (rows // CONV_HALO_ROWS, CONV_HALO_ROWS, tn), 1)
        y = cw[GDN_CONV - 1:GDN_CONV, :] * raw
        for t in range(GDN_CONV - 1):
            back = GDN_CONV - 1 - t
            rot = pltpu.roll(groups, back, axis=1)
            shifted = jnp.where(in_group < back, rot[:-1], rot[1:]).reshape(rows, tn)
            y = y + cw[t:t + 1, :] * shifted
        return _silu(y)

    def finish_qk(r0, rows):
        y = conv_silu(r0, rows)
        c = jnp.where(j < qk_blocks, float(GDN_DK), 1.0).astype(F32)
        out = []
        for h in range(tn // GDN_DK):
            yh = y[:, h * GDN_DK:(h + 1) * GDN_DK]
            ss = jnp.sum(yh * yh, axis=-1, keepdims=True)
            out.append((yh * lax.rsqrt(ss * c + RMS_EPS * c)).astype(o_ref.dtype))
        return jnp.concatenate(out, axis=1)

    def finish_v(r0, rows):
        return conv_silu(r0, rows).astype(o_ref.dtype)

    def finish_z(r0, rows):
        return _silu(raw_ref[pl.ds(r0, rows), :]).astype(o_ref.dtype)

    def run(finish):
        done = []
        for c in range(n_chunks + 1):
            if c < n_chunks:
                rows = pl.ds(*chunks[c])
                lhs = xn_ref[rows, :]
                if c >= 2:
                    lhs = _ordered_after(lhs, done[c - 2])
                raw_ref[rows, :] = jnp.dot(lhs, w_ref[...], preferred_element_type=F32)
            if c >= 1:
                out = finish(*chunks[c - 1])
                o_ref[pl.ds(*chunks[c - 1]), :] = out
                done.append(out)

    @pl.when(j < 2 * qk_blocks)
    def _():
        run(finish_qk)
        halo_ref[j] = raw_ref[pl.ds(tm - CONV_HALO_ROWS, CONV_HALO_ROWS), :]

    @pl.when((j >= 2 * qk_blocks) & (j < 2 * qk_blocks + v_blocks))
    def _():
        run(finish_v)
        halo_ref[j] = raw_ref[pl.ds(tm - CONV_HALO_ROWS, CONV_HALO_ROWS), :]

    @pl.when(j >= 2 * qk_blocks + v_blocks)
    def _():
        run(finish_z)


def _gdn_in_proj(x, gain, w_in, conv_w, layer, w_ba, a_log, dt_bias, *, tm, tn, row_chunks):
    s, d = x.shape
    n = GDN_CONV_W + GDN_V_W
    conv_blocks = GDN_CONV_W // tn
    gate_w, alog, dtb = _gdn_gate_params(w_ba, a_log, dt_bias)
    lane_row = pl.BlockSpec((1, LANES), lambda i, j: (0, 0))
    return pl.pallas_call(
        functools.partial(_gdn_in_proj_kernel, qk_blocks=GDN_QK_W // tn, v_blocks=GDN_V_W // tn,
                          row_chunks=row_chunks),
        grid=(s // tm, n // tn),
        in_specs=[pl.BlockSpec((tm, d), lambda i, j: (i, 0)),
                  pl.BlockSpec((1, d), lambda i, j: (0, 0)),
                  pl.BlockSpec((None, d, tn), lambda i, j: (layer, 0, j)),
                  pl.BlockSpec((None, GDN_CONV, tn), lambda i, j: (layer, 0, jnp.minimum(j, conv_blocks - 1))),
                  pl.BlockSpec((d, LANES), lambda i, j: (0, 0)), lane_row, lane_row],
        out_specs=[pl.BlockSpec((tm, tn), lambda i, j: (i, j)),
                   pl.BlockSpec((tm, LANES), lambda i, j: (i, 0))],
        out_shape=[jax.ShapeDtypeStruct((s, n), BF16), jax.ShapeDtypeStruct((s, LANES), F32)],
        scratch_shapes=[pltpu.VMEM((tm, d), BF16),
                        pltpu.VMEM((conv_blocks, CONV_HALO_ROWS, tn), F32),
                        pltpu.VMEM((tm, tn), F32)],
        compiler_params=_compiler_params(("arbitrary", "arbitrary")),
        name="gdn_in_proj",
    )(x, gain.reshape(1, d), w_in, conv_w, gate_w, alog, dtb)


def _gdn_gate_values(p, alog, dtb):
    lane = lax.broadcasted_iota(jnp.int32, p.shape, 1)
    beta = _sigmoid(p)
    a = p + dtb
    softplus = jnp.maximum(a, 0.0) + jnp.log(1.0 + jnp.exp(-jnp.abs(a)))
    g = -jnp.exp(alog) * softplus
    return jnp.where(lane < GDN_HEADS, beta, jnp.where(lane < 2 * GDN_HEADS, g, 0.0))


def _gdn_gate_params(w_ba, a_log, dt_bias):
    pad = LANES - 2 * GDN_HEADS
    w = jnp.pad(w_ba, ((0, 0), (0, pad))).astype(BF16)
    zeros_h = jnp.zeros((GDN_HEADS,), F32)
    alog = jnp.pad(jnp.concatenate([zeros_h, a_log.astype(F32)]), (0, pad)).reshape(1, LANES)
    dtb = jnp.pad(jnp.concatenate([zeros_h, dt_bias.astype(F32)]), (0, pad)).reshape(1, LANES)
    return w, alog, dtb


def _unit_lower_inverse(lows, row, col):
    c = lows[0].shape[0]
    eye = (row == col).astype(F32)
    blk = GDN_INV_BASE
    diag_mask = (row // blk) == (col // blk)
    powers = [jnp.where(diag_mask, low, 0.0) for low in lows]
    invs = [eye - p for p in powers]
    span = 2
    while span < blk:
        powers = [_bdot(p, p) for p in powers]
        invs = [x + _bdot(x, p) for x, p in zip(invs, powers)]
        span *= 2
    while blk < c:
        off_mask = ((row // (2 * blk)) == (col // (2 * blk))) & ((row // blk) != (col // blk))
        offs = [_bdot(jnp.where(off_mask, low, 0.0), x) for low, x in zip(lows, invs)]
        invs = [x - _bdot(x, t) for x, t in zip(invs, offs)]
        blk *= 2
    return invs


def _gdn_kernel(q_ref, k_ref, v_ref, z_ref, gb_ref, on_ref, o_ref, state_ref):
    c_len = q_ref.shape[0]
    step = pl.program_id(1)

    @pl.when(step == 0)
    def _():
        state_ref[...] = jnp.zeros_like(state_ref)

    row = lax.broadcasted_iota(jnp.int32, (c_len, c_len), 0)
    col = lax.broadcasted_iota(jnp.int32, (c_len, c_len), 1)
    causal = row >= col
    strict = row > col

    gb = gb_ref[...]
    tri = causal.astype(BF16)
    gb_hi = gb.astype(BF16)
    gb_lo = (gb - gb_hi.astype(F32)).astype(BF16)
    gcum = (jnp.dot(tri, gb_hi, preferred_element_type=F32)
            + jnp.dot(tri, gb_lo, preferred_element_type=F32))
    gcum_t = gcum.T

    heads = range(GDN_HEAD_BLOCK)

    def head_slices(ref):
        return [ref[:, i * GDN_DK:(i + 1) * GDN_DK].astype(F32) for i in heads]

    qn, kn, v_all = head_slices(q_ref), head_slices(k_ref), head_slices(v_ref)
    beta = [gb[:, i:i + 1] for i in heads]
    gcol = [gcum[:, GDN_HEADS + i:GDN_HEADS + i + 1] for i in heads]
    grow = [gcum_t[GDN_HEADS + i:GDN_HEADS + i + 1, :] for i in heads]
    decay = [jnp.exp(jnp.where(causal, gc - gr, MASK_VALUE)) for gc, gr in zip(gcol, grow)]
    e_col = [jnp.exp(gc) for gc in gcol]
    g_last = [gc[c_len - 1:c_len, :] for gc in gcol]
    kb = [k * b for k, b in zip(kn, beta)]
    kk_qk = [_bdot_nt(jnp.concatenate([kb[i], qn[i]], axis=0), kn[i]) for i in heads]
    low = [jnp.where(strict, kk_qk[i][:c_len] * decay[i], 0.0) for i in heads]
    intra = [kk_qk[i][c_len:] * decay[i] for i in heads]
    inv = _unit_lower_inverse(low, row, col)
    sol = [_bdot(inv[i], jnp.concatenate([v_all[i] * beta[i], kb[i] * e_col[i]], axis=1)) for i in heads]
    state = [state_ref[i] for i in heads]
    ws_qs = [_bdot(jnp.concatenate([sol[i][:, GDN_DV:], qn[i] * e_col[i]], axis=0), state[i]) for i in heads]
    v_new = [sol[i][:, :GDN_DV] - ws_qs[i][:c_len] for i in heads]
    o = [ws_qs[i][c_len:] + _bdot(intra[i], v_new[i]) for i in heads]
    for i in heads:
        k_dec = kn[i] * jnp.exp(g_last[i] - gcol[i])
        state_ref[i] = state[i] * jnp.exp(g_last[i]) + _bdot_tn(k_dec, v_new[i])
    out_gain = on_ref[...]
    for i in heads:
        hs = slice(i * GDN_DV, (i + 1) * GDN_DV)
        o_ref[:, hs] = (_rms(o[i], out_gain) * z_ref[:, hs].astype(F32)).astype(o_ref.dtype)


def _gdn(proj, gb, out_norm):
    s = proj.shape[0]
    c = GDN_CHUNK
    hb = GDN_HEAD_BLOCK
    ng = GDN_HEADS // hb
    bw = hb * GDN_DK

    def cur(sec):
        return pl.BlockSpec((c, bw), lambda g, t, sec=sec: (t, sec * ng + g))

    return pl.pallas_call(
        _gdn_kernel,
        grid=(ng, s // c),
        in_specs=[cur(0), cur(1), cur(2), cur(3),
                  pl.BlockSpec((c, LANES), lambda g, t: (t, g)),
                  pl.BlockSpec((1, GDN_DV), lambda g, t: (0, 0))],
        out_specs=pl.BlockSpec((c, bw), lambda g, t: (t, g)),
        out_shape=jax.ShapeDtypeStruct((s, GDN_V_W), BF16),
        scratch_shapes=[pltpu.VMEM((hb, GDN_DK, GDN_DV), F32)],
        compiler_params=_compiler_params(("parallel", "arbitrary")),
        name="gdn_delta_rule",
    )(proj, proj, proj, proj, gb, out_norm.reshape(1, GDN_DV))


def _overlapped_row_chunks(n_chunks, lhs_rows, w_ref, finish_rows):
    done = []
    prev = None
    for c in range(n_chunks + 1):
        cur = None
        if c < n_chunks:
            lhs = lhs_rows(c)
            if c >= 2:
                lhs = _ordered_after(lhs, done[c - 2])
            cur = jnp.dot(lhs, w_ref[...], preferred_element_type=F32)
        if c >= 1:
            done.append(finish_rows(c - 1, prev))
        prev = cur


def _matmul_norm_res_kernel(a_ref, w_ref, g_ref, x_ref, o_ref, *, row_chunk):
    def rows(c):
        return pl.ds(c * row_chunk, row_chunk)

    def finish(c, m):
        out = x_ref[rows(c), :] + _rms(m, g_ref[...])
        o_ref[rows(c), :] = out
        return out

    _overlapped_row_chunks(a_ref.shape[0] // row_chunk, lambda c: a_ref[rows(c), :], w_ref, finish)


def _matmul_norm_res(a, w, gain, x, layer, *, tm, row_chunk):
    s, k = a.shape
    d = w.shape[2]
    return pl.pallas_call(
        functools.partial(_matmul_norm_res_kernel, row_chunk=row_chunk),
        grid=(s // tm,),
        in_specs=[pl.BlockSpec((tm, k), lambda i: (i, 0)),
                  pl.BlockSpec((None, k, d), lambda i: (layer, 0, 0)),
                  pl.BlockSpec((1, d), lambda i: (0, 0)),
                  pl.BlockSpec((tm, d), lambda i: (i, 0))],
        out_specs=pl.BlockSpec((tm, d), lambda i: (i, 0)),
        out_shape=jax.ShapeDtypeStruct((s, d), F32),
        compiler_params=_compiler_params(("parallel",)),
        name="matmul_norm_res",
    )(a, w, gain.reshape(1, d), x)


def _ffn_kernel(x_ref, gpre_ref, wg_ref, wu_ref, wd_ref, gpost_ref, o_ref, xn_ref):
    j = pl.program_id(1)

    @pl.when(j == 0)
    def _():
        xn_ref[...] = _rms(x_ref[...], gpre_ref[...]).astype(BF16)
        o_ref[...] = jnp.zeros_like(o_ref)

    xn = xn_ref[...]
    gate = jnp.dot(xn, wg_ref[...], preferred_element_type=F32)
    up = jnp.dot(xn, wu_ref[...], preferred_element_type=F32)
    act = (_silu(gate) * up).astype(BF16)
    o_ref[...] += jnp.dot(act, wd_ref[...], preferred_element_type=F32)

    @pl.when(j == pl.num_programs(1) - 1)
    def _():
        o_ref[...] = x_ref[...] + _rms(o_ref[...], gpost_ref[...])


def _ffn(x, g_pre, w_gate, w_up, w_down, g_post, layer, *, tm, tf):
    s, d = x.shape
    d_ff = w_down.shape[1]
    nf = d_ff // tf
    return pl.pallas_call(
        _ffn_kernel,
        grid=(s // tm, nf),
        in_specs=[pl.BlockSpec((tm, d), lambda i, j: (i, 0)),
                  pl.BlockSpec((1, d), lambda i, j: (0, 0)),
                  pl.BlockSpec((None, d, tf), lambda i, j: (layer, 0, j)),
                  pl.BlockSpec((None, d, tf), lambda i, j: (layer, 0, j)),
                  pl.BlockSpec((None, tf, d), lambda i, j: (layer, j, 0)),
                  pl.BlockSpec((1, d), lambda i, j: (0, 0))],
        out_specs=pl.BlockSpec((tm, d), lambda i, j: (i, 0)),
        out_shape=jax.ShapeDtypeStruct((s, d), F32),
        scratch_shapes=[pltpu.VMEM((tm, d), BF16)],
        compiler_params=_compiler_params(("parallel", "arbitrary"), FFN_VMEM_LIMIT_BYTES),
        name="swiglu_ffn",
    )(x, g_pre.reshape(1, d), w_gate, w_up, w_down, g_post.reshape(1, d))


def _t5_bucket(dist):
    max_exact = NUM_BUCKETS // 2
    d_f = jnp.maximum(dist, 1).astype(F32)
    large = max_exact + (jnp.log(d_f / max_exact) / math.log(MAX_DISTANCE / max_exact)
                         * (NUM_BUCKETS - max_exact)).astype(jnp.int32)
    large = jnp.minimum(large, NUM_BUCKETS - 1)
    return jnp.where(dist < max_exact, dist, large)


def _tile_offsets(group):
    w = DSA_SPAN
    classes = DSA_RESIDUES // DSA_GROUPS[group][1]
    rows = w // classes
    i = jnp.arange(w)
    return (i % rows) * classes + i // rows


def _band_buckets(group):
    w = DSA_SPAN
    off = _tile_offsets(group)
    rel = off[:, None] + w - jnp.concatenate([off, off + w])[None, :]
    band = (rel >= 0) & (rel <= w)
    return jnp.where(band, _t5_bucket(jnp.clip(rel, 0, w) * DSA_GROUPS[group][1]), -1).astype(jnp.int32)


def _dsa_min_tiles(group):
    classes = DSA_RESIDUES // DSA_GROUPS[group][1]
    return max(1, BF16_SUBLANES * classes // DSA_SPAN)


def _dsa_kernel(tab_ref, bucket_ref, q_ref, kp_ref, kc_ref, vp_ref, vc_ref, o_ref, lse_ref, bias_ref, *, group):
    w = DSA_SPAN
    classes = q_ref.shape[0]
    rows = w // classes
    n_tiles = q_ref.shape[1] // rows
    step = pl.program_id(1)

    @pl.when((pl.program_id(0) == 0) & (step == 0))
    def _():
        bucket = bucket_ref[...]
        key_col = lax.broadcasted_iota(jnp.int32, (w, 2 * w), 1)

        def per_head(h, carry):
            b = jnp.full(bucket.shape, MASK_VALUE, F32)
            for t in range(NUM_BUCKETS):
                b = jnp.where(bucket == t, tab_ref[t, group * DSA_HEADS + h] * LOG2_E, b)
            bias_ref[0, h] = b
            bias_ref[1, h] = jnp.where(key_col < w, MASK_VALUE, b)
            return carry

        lax.fori_loop(0, DSA_HEADS, per_head, 0)

    def tiles(ref, cols):
        blk = ref[:, :, cols]
        if rows % BF16_SUBLANES:
            blk = blk.astype(F32)
        return [blk[:, t * rows:(t + 1) * rows, :].reshape(w, blk.shape[-1]).astype(ref.dtype)
                for t in range(ref.shape[1] // rows)]

    def store(ref, cols, parts):
        parts = [p.reshape(classes, rows, p.shape[-1]) for p in parts]
        ref[:, :, cols] = jnp.concatenate(parts, axis=1).astype(ref.dtype)

    first = jnp.where(step == 0, 1, 0)
    lane = lax.broadcasted_iota(jnp.int32, (w, LANES), 1)
    lse_all = [jnp.zeros((w, LANES), F32) for _ in range(n_tiles)]
    for h in range(DSA_HEADS):
        hs = slice(h * DSA_DH, (h + 1) * DSA_DH)
        q_t = tiles(q_ref, hs)
        k_t = tiles(kc_ref, hs)
        v_t = tiles(vc_ref, hs)
        k_prev = [tiles(kp_ref, hs)[-1]] + k_t[:-1]
        v_prev = [tiles(vp_ref, hs)[-1]] + v_t[:-1]
        outs = []
        for t in range(n_tiles):
            k = jnp.concatenate([k_prev[t], k_t[t]], axis=0)
            v = jnp.concatenate([v_prev[t], v_t[t]], axis=0)
            s = _bdot_nt(q_t[t], k) + bias_ref[first if t == 0 else 0, h]
            m = jnp.max(s, axis=-1, keepdims=True)
            p = jnp.exp2(s - m)
            l = jnp.sum(p, axis=-1, keepdims=True)
            outs.append(_bdot(p, v) / l)
            lse_all[t] = jnp.where(lane == h, (m + jnp.log2(l)) * LN_2, lse_all[t])
        store(o_ref, hs, outs)
    store(lse_ref, slice(None), lse_all)


def _dsa_group(q_all, kv, rel_bias, group, *, tiles_per_step):
    s = q_all.shape[0]
    dilation = DSA_GROUPS[group][1]
    w = DSA_SPAN
    gw = DSA_GROUP_W
    classes = DSA_RESIDUES // dilation
    class_len = s // DSA_RESIDUES
    rows = w // classes
    prev_tiles = _dsa_min_tiles(group)
    n_tiles = max(prev_tiles, min(tiles_per_step, class_len // rows))
    assert n_tiles % prev_tiles == 0 and class_len % (n_tiles * rows) == 0

    def view(a):
        return a.reshape(classes, dilation, class_len, a.shape[-1])

    def cur(width, col):
        return pl.BlockSpec((classes, None, n_tiles * rows, width), lambda r, n: (0, r, n, col))

    def prev(width, col):
        ratio = n_tiles // prev_tiles
        return pl.BlockSpec((classes, None, prev_tiles * rows, width),
                            lambda r, n: (0, r, jnp.maximum(n * ratio - 1, 0), col))

    o, lse = pl.pallas_call(
        functools.partial(_dsa_kernel, group=group),
        grid=(dilation, class_len // (n_tiles * rows)),
        in_specs=[pl.BlockSpec(memory_space=pltpu.SMEM),
                  pl.BlockSpec((w, 2 * w), lambda r, n: (0, 0)),
                  cur(gw, group),
                  prev(gw, group), cur(gw, group),
                  prev(gw, N_GROUPS + group), cur(gw, N_GROUPS + group)],
        out_specs=[cur(gw, 0), cur(LANES, 0)],
        out_shape=[jax.ShapeDtypeStruct((classes, dilation, class_len, gw), BF16),
                   jax.ShapeDtypeStruct((classes, dilation, class_len, LANES), F32)],
        scratch_shapes=[pltpu.VMEM((2, DSA_HEADS, w, 2 * w), F32)],
        compiler_params=_compiler_params(("arbitrary", "arbitrary")),
        name=f"dsa_attention_g{group}",
    )(rel_bias, _band_buckets(group), view(q_all), view(kv), view(kv), view(kv), view(kv))
    return o.reshape(s, gw), lse.reshape(s, LANES)


def _to_residue_major_kernel(x_ref, o_ref):
    rows = o_ref.shape[1]
    x = x_ref[...].reshape(rows, DSA_RESIDUES, x_ref.shape[-1])
    o_ref[...] = pltpu.einshape("lrd->rld", x)


def _from_residue_major_kernel(x_ref, o_ref):
    rows = x_ref.shape[1]
    o_ref[...] = pltpu.einshape("rld->lrd", x_ref[...]).reshape(rows * DSA_RESIDUES, x_ref.shape[-1])


def _residue_major(x, *, rows, inverse):
    s, d = x.shape
    class_len = s // DSA_RESIDUES
    natural = pl.BlockSpec((rows * DSA_RESIDUES, d), lambda i: (i, 0))
    major = pl.BlockSpec((DSA_RESIDUES, rows, d), lambda i: (0, i, 0))
    if inverse:
        body, specs, arg, shape = (_from_residue_major_kernel, (major, natural),
                                   x.reshape(DSA_RESIDUES, class_len, d), (s, d))
    else:
        body, specs, arg, shape = _to_residue_major_kernel, (natural, major), x, (DSA_RESIDUES, class_len, d)
    out = pl.pallas_call(
        body,
        grid=(class_len // rows,),
        in_specs=[specs[0]],
        out_specs=specs[1],
        out_shape=jax.ShapeDtypeStruct(shape, x.dtype),
        compiler_params=_compiler_params(("parallel",)),
        name="from_residue_major" if inverse else "to_residue_major",
    )(arg)
    return out.reshape(s, d)


def _dsa_out_kernel(o0_ref, o1_ref, o2_ref, l0_ref, l1_ref, l2_ref, w_ref, g_ref, x_ref, out_ref, *, row_chunk):
    o_refs = [o0_ref, o1_ref, o2_ref]
    l_refs = [l0_ref, l1_ref, l2_ref]

    def rows(c):
        return pl.ds(c * row_chunk, row_chunk)

    def merged_rows(c):
        lses = [l[rows(c), :] for l in l_refs]
        top = jnp.maximum(jnp.maximum(lses[0], lses[1]), lses[2])
        es = [jnp.exp(l - top) for l in lses]
        inv_den = 1.0 / (es[0] + es[1] + es[2])
        wts = [e * inv_den for e in es]
        heads = []
        for h in range(DSA_HEADS):
            hs = slice(h * DSA_DH, (h + 1) * DSA_DH)
            merged = wts[0][:, h:h + 1] * o_refs[0][rows(c), hs].astype(F32)
            for g in range(1, N_GROUPS):
                merged = merged + wts[g][:, h:h + 1] * o_refs[g][rows(c), hs].astype(F32)
            heads.append(merged.astype(BF16))
        return jnp.concatenate(heads, axis=1)

    def finish(c, m):
        out = x_ref[rows(c), :] + _rms(m, g_ref[...])
        out_ref[rows(c), :] = out
        return out

    _overlapped_row_chunks(x_ref.shape[0] // row_chunk, merged_rows, w_ref, finish)


def _dsa_out(os_, lses, w, gain, x, layer, *, tm, row_chunk):
    s, k = os_[0].shape
    d = w.shape[2]
    o_spec = pl.BlockSpec((tm, k), lambda i: (i, 0))
    l_spec = pl.BlockSpec((tm, LANES), lambda i: (i, 0))
    return pl.pallas_call(
        functools.partial(_dsa_out_kernel, row_chunk=row_chunk),
        grid=(s // tm,),
        in_specs=[o_spec, o_spec, o_spec, l_spec, l_spec, l_spec,
                  pl.BlockSpec((None, k, d), lambda i: (layer, 0, 0)),
                  pl.BlockSpec((1, d), lambda i: (0, 0)),
                  pl.BlockSpec((tm, d), lambda i: (i, 0))],
        out_specs=pl.BlockSpec((tm, d), lambda i: (i, 0)),
        out_shape=jax.ShapeDtypeStruct((s, d), F32),
        compiler_params=_compiler_params(("parallel",)),
        name="dsa_merge_out",
    )(*os_, *lses, w, gain.reshape(1, d), x)


def _tiles(s):
    tm_proj = min(1024, s)
    eighth = tm_proj // 8
    return dict(tm_proj=tm_proj, tn_proj=2048, tn_gdn=1024, tm_out=min(512, s), out_chunk=128, tm_ffn=min(1024, s), tf_ffn=512, dsa_tiles=4,
                perm_rows=32, gdn_proj_chunks=(2 * eighth, 2 * eighth, 2 * eighth, eighth, eighth))


def kernel(x, norm_gains, ffn_w_gate_up, ffn_w_down, gdn_w_in, gdn_conv_w, gdn_a_log, gdn_dt_bias,
           gdn_out_norm, gdn_w_out, kv_norm, kv_w, dsa_w_q, dsa_w_out, rel_bias):
    b, s, d = x.shape
    assert b == 1 and d == D_MODEL and s % (DSA_SPAN * DSA_GROUPS[-1][1]) == 0
    t = _tiles(s)
    depth = norm_gains.shape[0]
    n_a = gdn_w_in.shape[0]
    xs = x.reshape(s, d)
    kv = None
    main_w = GDN_CONV_W + GDN_V_W
    d_ff = ffn_w_down.shape[1]
    ffn_w_gate, ffn_w_up = ffn_w_gate_up[:, :, :d_ff].astype(BF16), ffn_w_gate_up[:, :, d_ff:].astype(BF16)
    ffn_w_down = ffn_w_down.astype(BF16)
    gdn_w_ba = gdn_w_in[:, :, main_w:]
    gdn_w_in_bf, gdn_w_out = gdn_w_in[:, :, :main_w].astype(BF16), gdn_w_out.astype(BF16)
    dsa_w_q, dsa_w_out = dsa_w_q.astype(BF16), dsa_w_out.astype(BF16)
    for layer in range(depth):
        gains = norm_gains[layer]
        if layer < n_a:
            proj, gb = _gdn_in_proj(xs, gains[0], gdn_w_in_bf, gdn_conv_w, layer,
                                    gdn_w_ba[layer], gdn_a_log[layer], gdn_dt_bias[layer],
                                    tm=t["tm_proj"], tn=t["tn_gdn"], row_chunks=t["gdn_proj_chunks"])
            o = _gdn(proj, gb, gdn_out_norm[layer])
            xs = _matmul_norm_res(o, gdn_w_out, gains[1], xs, layer, tm=t["tm_out"], row_chunk=t["out_chunk"])
        else:
            j = layer - n_a
            q_all = _norm_matmul(xs, gains[0], dsa_w_q, j, tm=t["tm_proj"], tn=t["tn_proj"], out_dtype=BF16,
                                 out_scale=DSA_DH ** -0.5 * LOG2_E)
            outs = [_dsa_group(q_all, kv, rel_bias, g, tiles_per_step=t["dsa_tiles"]) for g in range(N_GROUPS)]
            xs = _dsa_out([o for o, _ in outs], [l for _, l in outs], dsa_w_out, gains[1], xs, j,
                          tm=t["tm_out"], row_chunk=t["out_chunk"])
        xs = _ffn(xs, gains[2], ffn_w_gate, ffn_w_up, ffn_w_down, gains[3], layer, tm=t["tm_ffn"], tf=t["tf_ffn"])
        if layer == n_a - 1 and depth > n_a:
            xs = _residue_major(xs, rows=t["perm_rows"], inverse=False)
            kv = _norm_matmul(xs, kv_norm, kv_w.astype(BF16)[None], 0, tm=t["tm_proj"], tn=t["tn_proj"],
                              out_dtype=BF16)
    if depth > n_a:
        xs = _residue_major(xs, rows=t["perm_rows"], inverse=True)
    return xs.reshape(b, s, d)
```

```python
import functools
import math

import jax
import jax.numpy as jnp
from jax import lax
from jax.experimental import pallas as pl
from jax.experimental.pallas import tpu as pltpu

F32 = jnp.float32
BF16 = jnp.bfloat16

RMS_EPS = 1e-6
D_MODEL = 2048

GDN_HEADS = 16
GDN_DK = 128
GDN_DV = 128
GDN_CONV = 4
GDN_QK_W = GDN_HEADS * GDN_DK
GDN_V_W = GDN_HEADS * GDN_DV
GDN_CONV_W = 2 * GDN_QK_W + GDN_V_W
GDN_CHUNK = 128
GDN_HEAD_BLOCK = 16
GDN_INV_BASE = 8
CONV_HALO_ROWS = 8

DSA_GROUPS = ((128, 1), (512, 4), (2048, 16))
N_GROUPS = len(DSA_GROUPS)
DSA_HEADS = 16
DSA_DH = 128
DSA_SPAN = 128
DSA_GROUP_W = DSA_HEADS * DSA_DH
NUM_BUCKETS = 32
MAX_DISTANCE = 2048
MASK_VALUE = -1e30
LOG2_E = math.log2(math.e)
LN_2 = math.log(2.0)

DSA_RESIDUES = DSA_GROUPS[-1][1]

LANES = 128
BF16_SUBLANES = 16
VMEM_BYTES = 64 * 1024 * 1024
VMEM_LIMIT_BYTES = VMEM_BYTES * 7 // 8
FFN_VMEM_LIMIT_BYTES = VMEM_BYTES * 31 // 32


def _compiler_params(semantics, vmem_limit_bytes=VMEM_LIMIT_BYTES):
    return pltpu.CompilerParams(dimension_semantics=semantics, vmem_limit_bytes=vmem_limit_bytes)


def _rms(x, gain):
    ms = jnp.mean(x * x, axis=-1, keepdims=True)
    return x * lax.rsqrt(ms + RMS_EPS) * gain


def _sigmoid(x):
    return 1.0 / (1.0 + jnp.exp(-x))


def _silu(x):
    h = 0.5 * x
    return h + h * jnp.tanh(h)


def _bdot(a, b):
    return jnp.dot(a.astype(BF16), b.astype(BF16), preferred_element_type=F32)


def _bdot_nt(a, b):
    return lax.dot_general(a.astype(BF16), b.astype(BF16), (((1,), (1,)), ((), ())),
                           preferred_element_type=F32)


def _bdot_tn(a, b):
    return lax.dot_general(a.astype(BF16), b.astype(BF16), (((0,), (0,)), ((), ())),
                           preferred_element_type=F32)


def _norm_matmul_kernel(x_ref, g_ref, w_ref, o_ref, xn_ref, *, out_scale):
    @pl.when(pl.program_id(1) == 0)
    def _():
        xn_ref[...] = _rms(x_ref[...], g_ref[...]).astype(BF16)

    y = jnp.dot(xn_ref[...], w_ref[...], preferred_element_type=F32)
    if out_scale != 1.0:
        y = y * out_scale
    o_ref[...] = y.astype(o_ref.dtype)


def _norm_matmul(x, gain, w, layer, *, tm, tn, out_dtype, out_scale=1.0):
    s, d = x.shape
    n = w.shape[2]
    return pl.pallas_call(
        functools.partial(_norm_matmul_kernel, out_scale=out_scale),
        grid=(s // tm, n // tn),
        in_specs=[pl.BlockSpec((tm, d), lambda i, j: (i, 0)),
                  pl.BlockSpec((1, d), lambda i, j: (0, 0)),
                  pl.BlockSpec((None, d, tn), lambda i, j: (layer, 0, j))],
        out_specs=pl.BlockSpec((tm, tn), lambda i, j: (i, j)),
        out_shape=jax.ShapeDtypeStruct((s, n), out_dtype),
        scratch_shapes=[pltpu.VMEM((tm, d), BF16)],
        compiler_params=_compiler_params(("parallel", "arbitrary")),
        name="norm_matmul",
    )(x, gain.reshape(1, d), w)


def _ordered_after(x, anchor):
    rows, lanes = BF16_SUBLANES, LANES
    a = lax.bitcast_convert_type(jnp.abs(anchor[:rows, :lanes].astype(F32)), jnp.int32)
    zero = lax.shift_right_logical(a, 31)
    tile = lax.bitcast_convert_type(x[:rows, :lanes].astype(F32), jnp.int32) | zero
    tile = lax.bitcast_convert_type(tile, F32).astype(x.dtype)
    top = jnp.concatenate([tile, x[:rows, lanes:]], axis=1)
    return jnp.concatenate([top, x[rows:, :]], axis=0)


def _gdn_in_proj_kernel(x_ref, g_ref, w_ref, cw_ref, wba_ref, alog_ref, dtb_ref, o_ref, gb_ref,
                        xn_ref, halo_ref, raw_ref, *, qk_blocks, v_blocks, row_chunks):
    i = pl.program_id(0)
    j = pl.program_id(1)
    tm = x_ref.shape[0]
    tn = w_ref.shape[1]

    @pl.when(j == 0)
    def _():
        xn = _rms(x_ref[...], g_ref[...]).astype(BF16)
        xn_ref[...] = xn
        gb_ref[...] = _gdn_gate_values(jnp.dot(xn, wba_ref[...], preferred_element_type=F32),
                                       alog_ref[...], dtb_ref[...])

    @pl.when((i == 0) & (j == 0))
    def _():
        halo_ref[...] = jnp.zeros_like(halo_ref)

    chunks = []
    for rows in row_chunks:
        chunks.append((sum(r for _, r in chunks), rows))
    assert sum(row_chunks) == tm
    n_chunks = len(chunks)

    def conv_silu(r0, rows):
        raw = raw_ref[pl.ds(r0, rows), :]
        if r0 == 0:
            halo = jnp.where(i > 0, halo_ref[j], 0.0)
        else:
            halo = raw_ref[pl.ds(r0 - CONV_HALO_ROWS, CONV_HALO_ROWS), :]
        cw = cw_ref[...]
        groups = jnp.concatenate([halo, raw], axis=0).reshape(rows // CONV_HALO_ROWS + 1, CONV_HALO_ROWS, tn)
        in_group = lax.broadcasted_iota(jnp.int32, (rows // CONV_HALO_ROWS, CONV_HALO_ROWS, tn), 1)
        y = cw[GDN_CONV - 1:GDN_CONV, :] * raw
        for t in range(GDN_CONV - 1):
            back = GDN_CONV - 1 - t
            rot = pltpu.roll(groups, back, axis=1)
            shifted = jnp.where(in_group < back, rot[:-1], rot[1:]).reshape(rows, tn)
            y = y + cw[t:t + 1, :] * shifted
        return _silu(y)

    def finish_qk(r0, rows):
        y = conv_silu(r0, rows)
        c = jnp.where(j < qk_blocks, float(GDN_DK), 1.0).astype(F32)
        out = []
        for h in range(tn // GDN_DK):
            yh = y[:, h * GDN_DK:(h + 1) * GDN_DK]
            ss = jnp.sum(yh * yh, axis=-1, keepdims=True)
            out.append((yh * lax.rsqrt(ss * c + RMS_EPS * c)).astype(o_ref.dtype))
        return jnp.concatenate(out, axis=1)

    def finish_v(r0, rows):
        return conv_silu(r0, rows).astype(o_ref.dtype)

    def finish_z(r0, rows):
        return _silu(raw_ref[pl.ds(r0, rows), :]).astype(o_ref.dtype)

    def run(finish):
        done = []
        for c in range(n_chunks + 1):
            if c < n_chunks:
                rows = pl.ds(*chunks[c])
                lhs = xn_ref[rows, :]
                if c >= 2:
                    lhs = _ordered_after(lhs, done[c - 2])
                raw_ref[rows, :] = jnp.dot(lhs, w_ref[...], preferred_element_type=F32)
            if c >= 1:
                out = finish(*chunks[c - 1])
                o_ref[pl.ds(*chunks[c - 1]), :] = out
                done.append(out)

    @pl.when(j < 2 * qk_blocks)
    def _():
        run(finish_qk)
        halo_ref[j] = raw_ref[pl.ds(tm - CONV_HALO_ROWS, CONV_HALO_ROWS), :]

    @pl.when((j >= 2 * qk_blocks) & (j < 2 * qk_blocks + v_blocks))
    def _():
        run(finish_v)
        halo_ref[j] = raw_ref[pl.ds(tm - CONV_HALO_ROWS, CONV_HALO_ROWS), :]

    @pl.when(j >= 2 * qk_blocks + v_blocks)
    def _():
        run(finish_z)


def _gdn_in_proj(x, gain, w_in, conv_w, layer, a_log, dt_bias, *, tm, tn, row_chunks):
    s, d = x.shape
    n = GDN_CONV_W + GDN_V_W
    assert n % LANES == 0 and w_in.shape[2] - n == 2 * GDN_HEADS
    conv_blocks = GDN_CONV_W // tn
    alog, dtb = _gdn_gate_params(a_log, dt_bias)
    lane_row = pl.BlockSpec((1, LANES), lambda i, j: (0, 0))
    return pl.pallas_call(
        functools.partial(_gdn_in_proj_kernel, qk_blocks=GDN_QK_W // tn, v_blocks=GDN_V_W // tn,
                          row_chunks=row_chunks),
        grid=(s // tm, n // tn),
        in_specs=[pl.BlockSpec((tm, d), lambda i, j: (i, 0)),
                  pl.BlockSpec((1, d), lambda i, j: (0, 0)),
                  pl.BlockSpec((None, d, tn), lambda i, j: (layer, 0, j)),
                  pl.BlockSpec((None, GDN_CONV, tn), lambda i, j: (layer, 0, jnp.minimum(j, conv_blocks - 1))),
                  pl.BlockSpec((None, d, LANES), lambda i, j: (layer, 0, n // LANES)), lane_row, lane_row],
        out_specs=[pl.BlockSpec((tm, tn), lambda i, j: (i, j)),
                   pl.BlockSpec((tm, LANES), lambda i, j: (i, 0))],
        out_shape=[jax.ShapeDtypeStruct((s, n), BF16), jax.ShapeDtypeStruct((s, LANES), F32)],
        scratch_shapes=[pltpu.VMEM((tm, d), BF16),
                        pltpu.VMEM((conv_blocks, CONV_HALO_ROWS, tn), F32),
                        pltpu.VMEM((tm, tn), F32)],
        compiler_params=_compiler_params(("arbitrary", "arbitrary")),
        name="gdn_in_proj",
    )(x, gain.reshape(1, d), w_in, conv_w, w_in, alog, dtb)


def _gdn_gate_values(p, alog, dtb):
    lane = lax.broadcasted_iota(jnp.int32, p.shape, 1)
    beta = _sigmoid(p)
    a = p + dtb
    softplus = jnp.maximum(a, 0.0) + jnp.log(1.0 + jnp.exp(-jnp.abs(a)))
    g = -jnp.exp(alog) * softplus
    return jnp.where(lane < GDN_HEADS, beta, jnp.where(lane < 2 * GDN_HEADS, g, 0.0))


def _gdn_gate_params(a_log, dt_bias):
    pad = LANES - 2 * GDN_HEADS
    zeros_h = jnp.zeros((GDN_HEADS,), F32)
    alog = jnp.pad(jnp.concatenate([zeros_h, a_log.astype(F32)]), (0, pad)).reshape(1, LANES)
    dtb = jnp.pad(jnp.concatenate([zeros_h, dt_bias.astype(F32)]), (0, pad)).reshape(1, LANES)
    return alog, dtb


def _unit_lower_inverse(lows, row, col):
    c = lows[0].shape[0]
    eye = (row == col).astype(F32)
    blk = GDN_INV_BASE
    diag_mask = (row // blk) == (col // blk)
    powers = [jnp.where(diag_mask, low, 0.0) for low in lows]
    invs = [eye - p for p in powers]
    span = 2
    while span < blk:
        powers = [_bdot(p, p) for p in powers]
        invs = [x + _bdot(x, p) for x, p in zip(invs, powers)]
        span *= 2
    while blk < c:
        off_mask = ((row // (2 * blk)) == (col // (2 * blk))) & ((row // blk) != (col // blk))
        offs = [_bdot(jnp.where(off_mask, low, 0.0), x) for low, x in zip(lows, invs)]
        invs = [x - _bdot(x, t) for x, t in zip(invs, offs)]
        blk *= 2
    return invs


def _gdn_kernel(q_ref, k_ref, v_ref, z_ref, gb_ref, on_ref, o_ref, state_ref):
    c_len = q_ref.shape[0]
    step = pl.program_id(1)

    @pl.when(step == 0)
    def _():
        state_ref[...] = jnp.zeros_like(state_ref)

    row = lax.broadcasted_iota(jnp.int32, (c_len, c_len), 0)
    col = lax.broadcasted_iota(jnp.int32, (c_len, c_len), 1)
    causal = row >= col
    strict = row > col

    gb = gb_ref[...]
    tri = causal.astype(BF16)
    gb_hi = gb.astype(BF16)
    gb_lo = (gb - gb_hi.astype(F32)).astype(BF16)
    gcum = (jnp.dot(tri, gb_hi, preferred_element_type=F32)
            + jnp.dot(tri, gb_lo, preferred_element_type=F32))
    gcum_t = gcum.T

    heads = range(GDN_HEAD_BLOCK)

    def head_slices(ref):
        return [ref[:, i * GDN_DK:(i + 1) * GDN_DK].astype(F32) for i in heads]

    qn, kn, v_all = head_slices(q_ref), head_slices(k_ref), head_slices(v_ref)
    beta = [gb[:, i:i + 1] for i in heads]
    gcol = [gcum[:, GDN_HEADS + i:GDN_HEADS + i + 1] for i in heads]
    grow = [gcum_t[GDN_HEADS + i:GDN_HEADS + i + 1, :] for i in heads]
    decay = [jnp.exp(jnp.where(causal, gc - gr, MASK_VALUE)) for gc, gr in zip(gcol, grow)]
    e_col = [jnp.exp(gc) for gc in gcol]
    g_last = [gc[c_len - 1:c_len, :] for gc in gcol]
    kb = [k * b for k, b in zip(kn, beta)]
    kk_qk = [_bdot_nt(jnp.concatenate([kb[i], qn[i]], axis=0), kn[i]) for i in heads]
    low = [jnp.where(strict, kk_qk[i][:c_len] * decay[i], 0.0) for i in heads]
    intra = [kk_qk[i][c_len:] * decay[i] for i in heads]
    inv = _unit_lower_inverse(low, row, col)
    sol = [_bdot(inv[i], jnp.concatenate([v_all[i] * beta[i], kb[i] * e_col[i]], axis=1)) for i in heads]
    state = [state_ref[i] for i in heads]
    ws_qs = [_bdot(jnp.concatenate([sol[i][:, GDN_DV:], qn[i] * e_col[i]], axis=0), state[i]) for i in heads]
    v_new = [sol[i][:, :GDN_DV] - ws_qs[i][:c_len] for i in heads]
    o = [ws_qs[i][c_len:] + _bdot(intra[i], v_new[i]) for i in heads]
    for i in heads:
        k_dec = kn[i] * jnp.exp(g_last[i] - gcol[i])
        state_ref[i] = state[i] * jnp.exp(g_last[i]) + _bdot_tn(k_dec, v_new[i])
    out_gain = on_ref[...]
    for i in heads:
        hs = slice(i * GDN_DV, (i + 1) * GDN_DV)
        o_ref[:, hs] = (_rms(o[i], out_gain) * z_ref[:, hs].astype(F32)).astype(o_ref.dtype)


def _gdn(proj, gb, out_norm):
    s = proj.shape[0]
    c = GDN_CHUNK
    hb = GDN_HEAD_BLOCK
    ng = GDN_HEADS // hb
    bw = hb * GDN_DK

    def cur(sec):
        return pl.BlockSpec((c, bw), lambda g, t, sec=sec: (t, sec * ng + g))

    return pl.pallas_call(
        _gdn_kernel,
        grid=(ng, s // c),
        in_specs=[cur(0), cur(1), cur(2), cur(3),
                  pl.BlockSpec((c, LANES), lambda g, t: (t, g)),
                  pl.BlockSpec((1, GDN_DV), lambda g, t: (0, 0))],
        out_specs=pl.BlockSpec((c, bw), lambda g, t: (t, g)),
        out_shape=jax.ShapeDtypeStruct((s, GDN_V_W), BF16),
        scratch_shapes=[pltpu.VMEM((hb, GDN_DK, GDN_DV), F32)],
        compiler_params=_compiler_params(("parallel", "arbitrary")),
        name="gdn_delta_rule",
    )(proj, proj, proj, proj, gb, out_norm.reshape(1, GDN_DV))


def _overlapped_row_chunks(n_chunks, lhs_rows, w_ref, finish_rows):
    done = []
    prev = None
    for c in range(n_chunks + 1):
        cur = None
        if c < n_chunks:
            lhs = lhs_rows(c)
            if c >= 2:
                lhs = _ordered_after(lhs, done[c - 2])
            cur = jnp.dot(lhs, w_ref[...], preferred_element_type=F32)
        if c >= 1:
            done.append(finish_rows(c - 1, prev))
        prev = cur


def _matmul_norm_res_kernel(a_ref, w_ref, g_ref, x_ref, o_ref, *, row_chunk):
    def rows(c):
        return pl.ds(c * row_chunk, row_chunk)

    def finish(c, m):
        out = x_ref[rows(c), :] + _rms(m, g_ref[...])
        o_ref[rows(c), :] = out
        return out

    _overlapped_row_chunks(a_ref.shape[0] // row_chunk, lambda c: a_ref[rows(c), :], w_ref, finish)


def _matmul_norm_res(a, w, gain, x, layer, *, tm, row_chunk):
    s, k = a.shape
    d = w.shape[2]
    return pl.pallas_call(
        functools.partial(_matmul_norm_res_kernel, row_chunk=row_chunk),
        grid=(s // tm,),
        in_specs=[pl.BlockSpec((tm, k), lambda i: (i, 0)),
                  pl.BlockSpec((None, k, d), lambda i: (layer, 0, 0)),
                  pl.BlockSpec((1, d), lambda i: (0, 0)),
                  pl.BlockSpec((tm, d), lambda i: (i, 0))],
        out_specs=pl.BlockSpec((tm, d), lambda i: (i, 0)),
        out_shape=jax.ShapeDtypeStruct((s, d), F32),
        compiler_params=_compiler_params(("parallel",)),
        name="matmul_norm_res",
    )(a, w, gain.reshape(1, d), x)


def _ffn_kernel(x_ref, gpre_ref, wg_ref, wu_ref, wd_ref, gpost_ref, o_ref, xn_ref):
    j = pl.program_id(1)

    @pl.when(j == 0)
    def _():
        xn_ref[...] = _rms(x_ref[...], gpre_ref[...]).astype(BF16)
        o_ref[...] = jnp.zeros_like(o_ref)

    xn = xn_ref[...]
    gate = jnp.dot(xn, wg_ref[...], preferred_element_type=F32)
    up = jnp.dot(xn, wu_ref[...], preferred_element_type=F32)
    act = (_silu(gate) * up).astype(BF16)
    o_ref[...] += jnp.dot(act, wd_ref[...], preferred_element_type=F32)

    @pl.when(j == pl.num_programs(1) - 1)
    def _():
        o_ref[...] = x_ref[...] + _rms(o_ref[...], gpost_ref[...])


def _ffn(x, g_pre, w_gate_up, w_down, g_post, layer, *, tm, tf):
    s, d = x.shape
    d_ff = w_down.shape[1]
    nf = d_ff // tf
    return pl.pallas_call(
        _ffn_kernel,
        grid=(s // tm, nf),
        in_specs=[pl.BlockSpec((tm, d), lambda i, j: (i, 0)),
                  pl.BlockSpec((1, d), lambda i, j: (0, 0)),
                  pl.BlockSpec((None, d, tf), lambda i, j: (layer, 0, j)),
                  pl.BlockSpec((None, d, tf), lambda i, j: (layer, 0, nf + j)),
                  pl.BlockSpec((None, tf, d), lambda i, j: (layer, j, 0)),
                  pl.BlockSpec((1, d), lambda i, j: (0, 0))],
        out_specs=pl.BlockSpec((tm, d), lambda i, j: (i, 0)),
        out_shape=jax.ShapeDtypeStruct((s, d), F32),
        scratch_shapes=[pltpu.VMEM((tm, d), BF16)],
        compiler_params=_compiler_params(("parallel", "arbitrary"), FFN_VMEM_LIMIT_BYTES),
        name="swiglu_ffn",
    )(x, g_pre.reshape(1, d), w_gate_up, w_gate_up, w_down, g_post.reshape(1, d))


def _t5_bucket(dist):
    max_exact = NUM_BUCKETS // 2
    d_f = jnp.maximum(dist, 1).astype(F32)
    large = max_exact + (jnp.log(d_f / max_exact) / math.log(MAX_DISTANCE / max_exact)
                         * (NUM_BUCKETS - max_exact)).astype(jnp.int32)
    large = jnp.minimum(large, NUM_BUCKETS - 1)
    return jnp.where(dist < max_exact, dist, large)


def _tile_offsets(group):
    w = DSA_SPAN
    classes = DSA_RESIDUES // DSA_GROUPS[group][1]
    rows = w // classes
    i = jnp.arange(w)
    return (i % rows) * classes + i // rows


def _band_buckets(group):
    w = DSA_SPAN
    off = _tile_offsets(group)
    rel = off[:, None] + w - jnp.concatenate([off, off + w])[None, :]
    band = (rel >= 0) & (rel <= w)
    return jnp.where(band, _t5_bucket(jnp.clip(rel, 0, w) * DSA_GROUPS[group][1]), -1).astype(jnp.int32)


def _dsa_min_tiles(group):
    classes = DSA_RESIDUES // DSA_GROUPS[group][1]
    return max(1, BF16_SUBLANES * classes // DSA_SPAN)


def _dsa_kernel(tab_ref, bucket_ref, q_ref, kp_ref, kc_ref, vp_ref, vc_ref, o_ref, lse_ref, bias_ref, *, group):
    w = DSA_SPAN
    classes = q_ref.shape[0]
    rows = w // classes
    n_tiles = q_ref.shape[1] // rows
    step = pl.program_id(1)

    @pl.when((pl.program_id(0) == 0) & (step == 0))
    def _():
        bucket = bucket_ref[...]
        key_col = lax.broadcasted_iota(jnp.int32, (w, 2 * w), 1)

        def per_head(h, carry):
            b = jnp.full(bucket.shape, MASK_VALUE, F32)
            for t in range(NUM_BUCKETS):
                b = jnp.where(bucket == t, tab_ref[t, group * DSA_HEADS + h] * LOG2_E, b)
            bias_ref[0, h] = b
            bias_ref[1, h] = jnp.where(key_col < w, MASK_VALUE, b)
            return carry

        lax.fori_loop(0, DSA_HEADS, per_head, 0)

    def tiles(ref, cols):
        blk = ref[:, :, cols]
        if rows % BF16_SUBLANES:
            blk = blk.astype(F32)
        return [blk[:, t * rows:(t + 1) * rows, :].reshape(w, blk.shape[-1]).astype(ref.dtype)
                for t in range(ref.shape[1] // rows)]

    def store(ref, cols, parts):
        parts = [p.reshape(classes, rows, p.shape[-1]) for p in parts]
        ref[:, :, cols] = jnp.concatenate(parts, axis=1).astype(ref.dtype)

    first = jnp.where(step == 0, 1, 0)
    lane = lax.broadcasted_iota(jnp.int32, (w, LANES), 1)
    lse_all = [jnp.zeros((w, LANES), F32) for _ in range(n_tiles)]
    for h in range(DSA_HEADS):
        hs = slice(h * DSA_DH, (h + 1) * DSA_DH)
        q_t = tiles(q_ref, hs)
        k_t = tiles(kc_ref, hs)
        v_t = tiles(vc_ref, hs)
        k_prev = [tiles(kp_ref, hs)[-1]] + k_t[:-1]
        v_prev = [tiles(vp_ref, hs)[-1]] + v_t[:-1]
        outs = []
        for t in range(n_tiles):
            k = jnp.concatenate([k_prev[t], k_t[t]], axis=0)
            v = jnp.concatenate([v_prev[t], v_t[t]], axis=0)
            s = _bdot_nt(q_t[t], k) + bias_ref[first if t == 0 else 0, h]
            m = jnp.max(s, axis=-1, keepdims=True)
            p = jnp.exp2(s - m)
            l = jnp.sum(p, axis=-1, keepdims=True)
            outs.append(_bdot(p, v) / l)
            lse_all[t] = jnp.where(lane == h, (m + jnp.log2(l)) * LN_2, lse_all[t])
        store(o_ref, hs, outs)
    store(lse_ref, slice(None), lse_all)


def _dsa_group(q_all, kv, rel_bias, group, *, tiles_per_step):
    s = q_all.shape[0]
    dilation = DSA_GROUPS[group][1]
    w = DSA_SPAN
    gw = DSA_GROUP_W
    classes = DSA_RESIDUES // dilation
    class_len = s // DSA_RESIDUES
    rows = w // classes
    prev_tiles = _dsa_min_tiles(group)
    n_tiles = max(prev_tiles, min(tiles_per_step, class_len // rows))
    assert n_tiles % prev_tiles == 0 and class_len % (n_tiles * rows) == 0

    def view(a):
        return a.reshape(classes, dilation, class_len, a.shape[-1])

    def cur(width, col):
        return pl.BlockSpec((classes, None, n_tiles * rows, width), lambda r, n: (0, r, n, col))

    def prev(width, col):
        ratio = n_tiles // prev_tiles
        return pl.BlockSpec((classes, None, prev_tiles * rows, width),
                            lambda r, n: (0, r, jnp.maximum(n * ratio - 1, 0), col))

    o, lse = pl.pallas_call(
        functools.partial(_dsa_kernel, group=group),
        grid=(dilation, class_len // (n_tiles * rows)),
        in_specs=[pl.BlockSpec(memory_space=pltpu.SMEM),
                  pl.BlockSpec((w, 2 * w), lambda r, n: (0, 0)),
                  cur(gw, group),
                  prev(gw, group), cur(gw, group),
                  prev(gw, N_GROUPS + group), cur(gw, N_GROUPS + group)],
        out_specs=[cur(gw, 0), cur(LANES, 0)],
        out_shape=[jax.ShapeDtypeStruct((classes, dilation, class_len, gw), BF16),
                   jax.ShapeDtypeStruct((classes, dilation, class_len, LANES), F32)],
        scratch_shapes=[pltpu.VMEM((2, DSA_HEADS, w, 2 * w), F32)],
        compiler_params=_compiler_params(("arbitrary", "arbitrary")),
        name=f"dsa_attention_g{group}",
    )(rel_bias, _band_buckets(group), view(q_all), view(kv), view(kv), view(kv), view(kv))
    return o.reshape(s, gw), lse.reshape(s, LANES)


def _to_residue_major_kernel(x_ref, o_ref):
    rows = o_ref.shape[1]
    x = x_ref[...].reshape(rows, DSA_RESIDUES, x_ref.shape[-1])
    o_ref[...] = pltpu.einshape("lrd->rld", x)


def _from_residue_major_kernel(x_ref, o_ref):
    rows = x_ref.shape[1]
    o_ref[...] = pltpu.einshape("rld->lrd", x_ref[...]).reshape(rows * DSA_RESIDUES, x_ref.shape[-1])


def _residue_major(x, *, rows, inverse):
    s, d = x.shape
    class_len = s // DSA_RESIDUES
    natural = pl.BlockSpec((rows * DSA_RESIDUES, d), lambda i: (i, 0))
    major = pl.BlockSpec((DSA_RESIDUES, rows, d), lambda i: (0, i, 0))
    if inverse:
        body, specs, arg, shape = (_from_residue_major_kernel, (major, natural),
                                   x.reshape(DSA_RESIDUES, class_len, d), (s, d))
    else:
        body, specs, arg, shape = _to_residue_major_kernel, (natural, major), x, (DSA_RESIDUES, class_len, d)
    out = pl.pallas_call(
        body,
        grid=(class_len // rows,),
        in_specs=[specs[0]],
        out_specs=specs[1],
        out_shape=jax.ShapeDtypeStruct(shape, x.dtype),
        compiler_params=_compiler_params(("parallel",)),
        name="from_residue_major" if inverse else "to_residue_major",
    )(arg)
    return out.reshape(s, d)


def _dsa_out_kernel(o0_ref, o1_ref, o2_ref, l0_ref, l1_ref, l2_ref, w_ref, g_ref, x_ref, out_ref, *, row_chunk):
    o_refs = [o0_ref, o1_ref, o2_ref]
    l_refs = [l0_ref, l1_ref, l2_ref]

    def rows(c):
        return pl.ds(c * row_chunk, row_chunk)

    def merged_rows(c):
        lses = [l[rows(c), :] for l in l_refs]
        top = jnp.maximum(jnp.maximum(lses[0], lses[1]), lses[2])
        es = [jnp.exp(l - top) for l in lses]
        inv_den = 1.0 / (es[0] + es[1] + es[2])
        wts = [e * inv_den for e in es]
        heads = []
        for h in range(DSA_HEADS):
            hs = slice(h * DSA_DH, (h + 1) * DSA_DH)
            merged = wts[0][:, h:h + 1] * o_refs[0][rows(c), hs].astype(F32)
            for g in range(1, N_GROUPS):
                merged = merged + wts[g][:, h:h + 1] * o_refs[g][rows(c), hs].astype(F32)
            heads.append(merged.astype(BF16))
        return jnp.concatenate(heads, axis=1)

    def finish(c, m):
        out = x_ref[rows(c), :] + _rms(m, g_ref[...])
        out_ref[rows(c), :] = out
        return out

    _overlapped_row_chunks(x_ref.shape[0] // row_chunk, merged_rows, w_ref, finish)


def _dsa_out(os_, lses, w, gain, x, layer, *, tm, row_chunk):
    s, k = os_[0].shape
    d = w.shape[2]
    o_spec = pl.BlockSpec((tm, k), lambda i: (i, 0))
    l_spec = pl.BlockSpec((tm, LANES), lambda i: (i, 0))
    return pl.pallas_call(
        functools.partial(_dsa_out_kernel, row_chunk=row_chunk),
        grid=(s // tm,),
        in_specs=[o_spec, o_spec, o_spec, l_spec, l_spec, l_spec,
                  pl.BlockSpec((None, k, d), lambda i: (layer, 0, 0)),
                  pl.BlockSpec((1, d), lambda i: (0, 0)),
                  pl.BlockSpec((tm, d), lambda i: (i, 0))],
        out_specs=pl.BlockSpec((tm, d), lambda i: (i, 0)),
        out_shape=jax.ShapeDtypeStruct((s, d), F32),
        compiler_params=_compiler_params(("parallel",)),
        name="dsa_merge_out",
    )(*os_, *lses, w, gain.reshape(1, d), x)


def _tiles(s):
    tm_proj = min(1024, s)
    eighth = tm_proj // 8
    return dict(tm_proj=tm_proj, tn_proj=2048, tn_gdn=1024, tm_out=min(512, s), out_chunk=128, tm_ffn=min(1024, s), tf_ffn=512, dsa_tiles=4,
                perm_rows=32, gdn_proj_chunks=(2 * eighth, 2 * eighth, 2 * eighth, eighth, eighth))


def kernel(x, norm_gains, ffn_w_gate_up, ffn_w_down, gdn_w_in, gdn_conv_w, gdn_a_log, gdn_dt_bias,
           gdn_out_norm, gdn_w_out, kv_norm, kv_w, dsa_w_q, dsa_w_out, rel_bias):
    b, s, d = x.shape
    assert b == 1 and d == D_MODEL and s % (DSA_SPAN * DSA_GROUPS[-1][1]) == 0
    t = _tiles(s)
    depth = norm_gains.shape[0]
    n_a = gdn_w_in.shape[0]
    xs = x.reshape(s, d)
    kv = None
    ffn_w_gate_up, ffn_w_down = ffn_w_gate_up.astype(BF16), ffn_w_down.astype(BF16)
    gdn_w_in, gdn_w_out = gdn_w_in.astype(BF16), gdn_w_out.astype(BF16)
    dsa_w_q, dsa_w_out = dsa_w_q.astype(BF16), dsa_w_out.astype(BF16)
    for layer in range(depth):
        gains = norm_gains[layer]
        if layer < n_a:
            proj, gb = _gdn_in_proj(xs, gains[0], gdn_w_in, gdn_conv_w, layer, gdn_a_log[layer], gdn_dt_bias[layer],
                                    tm=t["tm_proj"], tn=t["tn_gdn"], row_chunks=t["gdn_proj_chunks"])
            o = _gdn(proj, gb, gdn_out_norm[layer])
            xs = _matmul_norm_res(o, gdn_w_out, gains[1], xs, layer, tm=t["tm_out"], row_chunk=t["out_chunk"])
        else:
            j = layer - n_a
            q_all = _norm_matmul(xs, gains[0], dsa_w_q, j, tm=t["tm_proj"], tn=t["tn_proj"], out_dtype=BF16,
                                 out_scale=DSA_DH ** -0.5 * LOG2_E)
            outs = [_dsa_group(q_all, kv, rel_bias, g, tiles_per_step=t["dsa_tiles"]) for g in range(N_GROUPS)]
            xs = _dsa_out([o for o, _ in outs], [l for _, l in outs], dsa_w_out, gains[1], xs, j,
                          tm=t["tm_out"], row_chunk=t["out_chunk"])
        xs = _ffn(xs, gains[2], ffn_w_gate_up, ffn_w_down, gains[3], layer, tm=t["tm_ffn"], tf=t["tf_ffn"])
        if layer == n_a - 1 and depth > n_a:
            xs = _residue_major(xs, rows=t["perm_rows"], inverse=False)
            kv = _norm_matmul(xs, kv_norm, kv_w.astype(BF16)[None], 0, tm=t["tm_proj"], tn=t["tn_proj"],
                              out_dtype=BF16)
    if depth > n_a:
        xs = _residue_major(xs, rows=t["perm_rows"], inverse=True)
    return xs.reshape(b, s, d)
```

```python
import functools
import math

import jax
import jax.numpy as jnp
from jax import lax
from jax.experimental import pallas as pl
from jax.experimental.pallas import tpu as pltpu

F32 = jnp.float32
BF16 = jnp.bfloat16

RMS_EPS = 1e-6
D_MODEL = 2048

GDN_HEADS = 16
GDN_DK = 128
GDN_DV = 128
GDN_CONV = 4
GDN_QK_W = GDN_HEADS * GDN_DK
GDN_V_W = GDN_HEADS * GDN_DV
GDN_CONV_W = 2 * GDN_QK_W + GDN_V_W
GDN_CHUNK = 128
GDN_HEAD_BLOCK = 16
GDN_INV_BASE = 8
CONV_HALO_ROWS = 8

DSA_GROUPS = ((128, 1), (512, 4), (2048, 16))
N_GROUPS = len(DSA_GROUPS)
DSA_HEADS = 16
DSA_DH = 128
DSA_SPAN = 128
DSA_GROUP_W = DSA_HEADS * DSA_DH
NUM_BUCKETS = 32
MAX_DISTANCE = 2048
MASK_VALUE = -1e30
LOG2_E = math.log2(math.e)
LN_2 = math.log(2.0)

DSA_RESIDUES = DSA_GROUPS[-1][1]

LANES = 128
BF16_SUBLANES = 16
VMEM_BYTES = 64 * 1024 * 1024
VMEM_LIMIT_BYTES = VMEM_BYTES * 7 // 8
FFN_VMEM_LIMIT_BYTES = VMEM_BYTES * 31 // 32


def _compiler_params(semantics, vmem_limit_bytes=VMEM_LIMIT_BYTES):
    return pltpu.CompilerParams(dimension_semantics=semantics, vmem_limit_bytes=vmem_limit_bytes)


def _rms(x, gain):
    ms = jnp.mean(x * x, axis=-1, keepdims=True)
    return x * lax.rsqrt(ms + RMS_EPS) * gain


def _sigmoid(x):
    return 1.0 / (1.0 + jnp.exp(-x))


def _silu(x):
    h = 0.5 * x
    return h + h * jnp.tanh(h)


def _bdot(a, b):
    return jnp.dot(a.astype(BF16), b.astype(BF16), preferred_element_type=F32)


def _bdot_nt(a, b):
    return lax.dot_general(a.astype(BF16), b.astype(BF16), (((1,), (1,)), ((), ())),
                           preferred_element_type=F32)


def _bdot_tn(a, b):
    return lax.dot_general(a.astype(BF16), b.astype(BF16), (((0,), (0,)), ((), ())),
                           preferred_element_type=F32)


def _norm_matmul_kernel(x_ref, g_ref, w_ref, o_ref, xn_ref, *, out_scale):
    @pl.when(pl.program_id(1) == 0)
    def _():
        xn_ref[...] = _rms(x_ref[...], g_ref[...]).astype(BF16)

    y = jnp.dot(xn_ref[...], w_ref[...], preferred_element_type=F32)
    if out_scale != 1.0:
        y = y * out_scale
    o_ref[...] = y.astype(o_ref.dtype)


def _norm_matmul(x, gain, w, layer, *, tm, tn, out_dtype, out_scale=1.0):
    s, d = x.shape
    n = w.shape[2]
    return pl.pallas_call(
        functools.partial(_norm_matmul_kernel, out_scale=out_scale),
        grid=(s // tm, n // tn),
        in_specs=[pl.BlockSpec((tm, d), lambda i, j: (i, 0)),
                  pl.BlockSpec((1, d), lambda i, j: (0, 0)),
                  pl.BlockSpec((None, d, tn), lambda i, j: (layer, 0, j))],
        out_specs=pl.BlockSpec((tm, tn), lambda i, j: (i, j)),
        out_shape=jax.ShapeDtypeStruct((s, n), out_dtype),
        scratch_shapes=[pltpu.VMEM((tm, d), BF16)],
        compiler_params=_compiler_params(("parallel", "arbitrary")),
        name="norm_matmul",
    )(x, gain.reshape(1, d), w)


def _ordered_after(x, anchor):
    rows, lanes = BF16_SUBLANES, LANES
    a = lax.bitcast_convert_type(jnp.abs(anchor[:rows, :lanes].astype(F32)), jnp.int32)
    zero = lax.shift_right_logical(a, 31)
    tile = lax.bitcast_convert_type(x[:rows, :lanes].astype(F32), jnp.int32) | zero
    tile = lax.bitcast_convert_type(tile, F32).astype(x.dtype)
    top = jnp.concatenate([tile, x[:rows, lanes:]], axis=1)
    return jnp.concatenate([top, x[rows:, :]], axis=0)


def _gdn_in_proj_kernel(x_ref, g_ref, w_ref, cw_ref, wba_ref, alog_ref, dtb_ref, o_ref, gb_ref,
                        xn_ref, halo_ref, raw_ref, *, qk_blocks, v_blocks, row_chunks):
    i = pl.program_id(0)
    j = pl.program_id(1)
    tm = x_ref.shape[0]
    tn = w_ref.shape[1]

    @pl.when(j == 0)
    def _():
        xn = _rms(x_ref[...], g_ref[...]).astype(BF16)
        xn_ref[...] = xn
        gb_ref[...] = _gdn_gate_values(jnp.dot(xn, wba_ref[...], preferred_element_type=F32),
                                       alog_ref[...], dtb_ref[...])

    @pl.when((i == 0) & (j == 0))
    def _():
        halo_ref[...] = jnp.zeros_like(halo_ref)

    chunks = []
    for rows in row_chunks:
        chunks.append((sum(r for _, r in chunks), rows))
    assert sum(row_chunks) == tm
    n_chunks = len(chunks)

    def conv_silu(r0, rows):
        raw = raw_ref[pl.ds(r0, rows), :]
        if r0 == 0:
            halo = jnp.where(i > 0, halo_ref[j], 0.0)
        else:
            halo = raw_ref[pl.ds(r0 - CONV_HALO_ROWS, CONV_HALO_ROWS), :]
        cw = cw_ref[...]
        groups = jnp.concatenate([halo, raw], axis=0).reshape(rows // CONV_HALO_ROWS + 1, CONV_HALO_ROWS, tn)
        in_group = lax.broadcasted_iota(jnp.int32, (rows // CONV_HALO_ROWS, CONV_HALO_ROWS, tn), 1)
        y = cw[GDN_CONV - 1:GDN_CONV, :] * raw
        for t in range(GDN_CONV - 1):
            back = GDN_CONV - 1 - t
            rot = pltpu.roll(groups, back, axis=1)
            shifted = jnp.where(in_group < back, rot[:-1], rot[1:]).reshape(rows, tn)
            y = y + cw[t:t + 1, :] * shifted
        return _silu(y)

    def finish_qk(r0, rows):
        y = conv_silu(r0, rows)
        c = jnp.where(j < qk_blocks, float(GDN_DK), 1.0).astype(F32)
        out = []
        for h in range(tn // GDN_DK):
            yh = y[:, h * GDN_DK:(h + 1) * GDN_DK]
            ss = jnp.sum(yh * yh, axis=-1, keepdims=True)
            out.append((yh * lax.rsqrt(ss * c + RMS_EPS * c)).astype(o_ref.dtype))
        return jnp.concatenate(out, axis=1)

    def finish_v(r0, rows):
        return conv_silu(r0, rows).astype(o_ref.dtype)

    def finish_z(r0, rows):
        return _silu(raw_ref[pl.ds(r0, rows), :]).astype(o_ref.dtype)

    def run(finish):
        done = []
        for c in range(n_chunks + 1):
            if c < n_chunks:
                rows = pl.ds(*chunks[c])
                lhs = xn_ref[rows, :]
                if c >= 2:
                    lhs = _ordered_after(lhs, done[c - 2])
                raw_ref[rows, :] = jnp.dot(lhs, w_ref[...], preferred_element_type=F32)
            if c >= 1:
                out = finish(*chunks[c - 1])
                o_ref[pl.ds(*chunks[c - 1]), :] = out
                done.append(out)

    @pl.when(j < 2 * qk_blocks)
    def _():
        run(finish_qk)
        halo_ref[j] = raw_ref[pl.ds(tm - CONV_HALO_ROWS, CONV_HALO_ROWS), :]

    @pl.when((j >= 2 * qk_blocks) & (j < 2 * qk_blocks + v_blocks))
    def _():
        run(finish_v)
        halo_ref[j] = raw_ref[pl.ds(tm - CONV_HALO_ROWS, CONV_HALO_ROWS), :]

    @pl.when(j >= 2 * qk_blocks + v_blocks)
    def _():
        run(finish_z)


def _gdn_in_proj(x, gain, w_in, conv_w, layer, a_log, dt_bias, *, tm, tn, row_chunks):
    s, d = x.shape
    n = GDN_CONV_W + GDN_V_W
    assert n % LANES == 0 and w_in.shape[2] - n == 2 * GDN_HEADS
    conv_blocks = GDN_CONV_W // tn
    alog, dtb = _gdn_gate_params(a_log, dt_bias)
    lane_row = pl.BlockSpec((1, LANES), lambda i, j: (0, 0))
    return pl.pallas_call(
        functools.partial(_gdn_in_proj_kernel, qk_blocks=GDN_QK_W // tn, v_blocks=GDN_V_W // tn,
                          row_chunks=row_chunks),
        grid=(s // tm, n // tn),
        in_specs=[pl.BlockSpec((tm, d), lambda i, j: (i, 0)),
                  pl.BlockSpec((1, d), lambda i, j: (0, 0)),
                  pl.BlockSpec((None, d, tn), lambda i, j: (layer, 0, j)),
                  pl.BlockSpec((None, GDN_CONV, tn), lambda i, j: (layer, 0, jnp.minimum(j, conv_blocks - 1))),
                  pl.BlockSpec((None, d, LANES), lambda i, j: (layer, 0, n // LANES)), lane_row, lane_row],
        out_specs=[pl.BlockSpec((tm, tn), lambda i, j: (i, j)),
                   pl.BlockSpec((tm, LANES), lambda i, j: (i, 0))],
        out_shape=[jax.ShapeDtypeStruct((s, n), BF16), jax.ShapeDtypeStruct((s, LANES), F32)],
        scratch_shapes=[pltpu.VMEM((tm, d), BF16),
                        pltpu.VMEM((conv_blocks, CONV_HALO_ROWS, tn), F32),
                        pltpu.VMEM((tm, tn), F32)],
        compiler_params=_compiler_params(("arbitrary", "arbitrary")),
        name="gdn_in_proj",
    )(x, gain.reshape(1, d), w_in, conv_w, w_in, alog, dtb)


def _gdn_gate_values(p, alog, dtb):
    lane = lax.broadcasted_iota(jnp.int32, p.shape, 1)
    beta = _sigmoid(p)
    a = p + dtb
    softplus = jnp.maximum(a, 0.0) + jnp.log(1.0 + jnp.exp(-jnp.abs(a)))
    g = -jnp.exp(alog) * softplus
    return jnp.where(lane < GDN_HEADS, beta, jnp.where(lane < 2 * GDN_HEADS, g, 0.0))


def _gdn_gate_params(a_log, dt_bias):
    pad = LANES - 2 * GDN_HEADS
    zeros_h = jnp.zeros((GDN_HEADS,), F32)
    alog = jnp.pad(jnp.concatenate([zeros_h, a_log.astype(F32)]), (0, pad)).reshape(1, LANES)
    dtb = jnp.pad(jnp.concatenate([zeros_h, dt_bias.astype(F32)]), (0, pad)).reshape(1, LANES)
    return alog, dtb


def _unit_lower_inverse(lows, row, col):
    c = lows[0].shape[0]
    eye = (row == col).astype(F32)
    blk = GDN_INV_BASE
    diag_mask = (row // blk) == (col // blk)
    powers = [jnp.where(diag_mask, low, 0.0) for low in lows]
    invs = [eye - p for p in powers]
    span = 2
    while span < blk:
        powers = [_bdot(p, p) for p in powers]
        invs = [x + _bdot(x, p) for x, p in zip(invs, powers)]
        span *= 2
    while blk < c:
        off_mask = ((row // (2 * blk)) == (col // (2 * blk))) & ((row // blk) != (col // blk))
        offs = [_bdot(jnp.where(off_mask, low, 0.0), x) for low, x in zip(lows, invs)]
        invs = [x - _bdot(x, t) for x, t in zip(invs, offs)]
        blk *= 2
    return invs


def _gdn_kernel(q_ref, k_ref, v_ref, z_ref, gb_ref, on_ref, o_ref, state_ref):
    c_len = q_ref.shape[0]
    step = pl.program_id(1)

    @pl.when(step == 0)
    def _():
        state_ref[...] = jnp.zeros_like(state_ref)

    row = lax.broadcasted_iota(jnp.int32, (c_len, c_len), 0)
    col = lax.broadcasted_iota(jnp.int32, (c_len, c_len), 1)
    causal = row >= col
    strict = row > col

    gb = gb_ref[...]
    tri = causal.astype(BF16)
    gb_hi = gb.astype(BF16)
    gb_lo = (gb - gb_hi.astype(F32)).astype(BF16)
    gcum = (jnp.dot(tri, gb_hi, preferred_element_type=F32)
            + jnp.dot(tri, gb_lo, preferred_element_type=F32))
    gcum_t = gcum.T

    heads = range(GDN_HEAD_BLOCK)

    def head_slices(ref):
        return [ref[:, i * GDN_DK:(i + 1) * GDN_DK].astype(F32) for i in heads]

    qn, kn, v_all = head_slices(q_ref), head_slices(k_ref), head_slices(v_ref)
    beta = [gb[:, i:i + 1] for i in heads]
    gcol = [gcum[:, GDN_HEADS + i:GDN_HEADS + i + 1] for i in heads]
    grow = [gcum_t[GDN_HEADS + i:GDN_HEADS + i + 1, :] for i in heads]
    decay = [jnp.exp(jnp.where(causal, gc - gr, MASK_VALUE)) for gc, gr in zip(gcol, grow)]
    e_col = [jnp.exp(gc) for gc in gcol]
    g_last = [gc[c_len - 1:c_len, :] for gc in gcol]
    kb = [k * b for k, b in zip(kn, beta)]
    kk_qk = [_bdot_nt(jnp.concatenate([kb[i], qn[i]], axis=0), kn[i]) for i in heads]
    low = [jnp.where(strict, kk_qk[i][:c_len] * decay[i], 0.0) for i in heads]
    intra = [kk_qk[i][c_len:] * decay[i] for i in heads]
    inv = _unit_lower_inverse(low, row, col)
    sol = [_bdot(inv[i], jnp.concatenate([v_all[i] * beta[i], kb[i] * e_col[i]], axis=1)) for i in heads]
    state = [state_ref[i] for i in heads]
    ws_qs = [_bdot(jnp.concatenate([sol[i][:, GDN_DV:], qn[i] * e_col[i]], axis=0), state[i]) for i in heads]
    v_new = [sol[i][:, :GDN_DV] - ws_qs[i][:c_len] for i in heads]
    o = [ws_qs[i][c_len:] + _bdot(intra[i], v_new[i]) for i in heads]
    for i in heads:
        k_dec = kn[i] * jnp.exp(g_last[i] - gcol[i])
        state_ref[i] = state[i] * jnp.exp(g_last[i]) + _bdot_tn(k_dec, v_new[i])
    out_gain = on_ref[...]
    for i in heads:
        hs = slice(i * GDN_DV, (i + 1) * GDN_DV)
        o_ref[:, hs] = (_rms(o[i], out_gain) * z_ref[:, hs].astype(F32)).astype(o_ref.dtype)


def _gdn(proj, gb, out_norm):
    s = proj.shape[0]
    c = GDN_CHUNK
    hb = GDN_HEAD_BLOCK
    ng = GDN_HEADS // hb
    bw = hb * GDN_DK

    def cur(sec):
        return pl.BlockSpec((c, bw), lambda g, t, sec=sec: (t, sec * ng + g))

    return pl.pallas_call(
        _gdn_kernel,
        grid=(ng, s // c),
        in_specs=[cur(0), cur(1), cur(2), cur(3),
                  pl.BlockSpec((c, LANES), lambda g, t: (t, g)),
                  pl.BlockSpec((1, GDN_DV), lambda g, t: (0, 0))],
        out_specs=pl.BlockSpec((c, bw), lambda g, t: (t, g)),
        out_shape=jax.ShapeDtypeStruct((s, GDN_V_W), BF16),
        scratch_shapes=[pltpu.VMEM((hb, GDN_DK, GDN_DV), F32)],
        compiler_params=_compiler_params(("parallel", "arbitrary")),
        name="gdn_delta_rule",
    )(proj, proj, proj, proj, gb, out_norm.reshape(1, GDN_DV))


def _overlapped_row_chunks(n_chunks, lhs_rows, w_ref, finish_rows):
    done = []
    prev = None
    for c in range(n_chunks + 1):
        cur = None
        if c < n_chunks:
            lhs = lhs_rows(c)
            if c >= 2:
                lhs = _ordered_after(lhs, done[c - 2])
            cur = jnp.dot(lhs, w_ref[...], preferred_element_type=F32)
        if c >= 1:
            done.append(finish_rows(c - 1, prev))
        prev = cur


def _matmul_norm_res_kernel(a_ref, w_ref, g_ref, x_ref, o_ref, *, row_chunk):
    def rows(c):
        return pl.ds(c * row_chunk, row_chunk)

    def finish(c, m):
        out = x_ref[rows(c), :] + _rms(m, g_ref[...])
        o_ref[rows(c), :] = out
        return out

    _overlapped_row_chunks(a_ref.shape[0] // row_chunk, lambda c: a_ref[rows(c), :], w_ref, finish)


def _matmul_norm_res(a, w, gain, x, layer, *, tm, row_chunk):
    s, k = a.shape
    d = w.shape[2]
    return pl.pallas_call(
        functools.partial(_matmul_norm_res_kernel, row_chunk=row_chunk),
        grid=(s // tm,),
        in_specs=[pl.BlockSpec((tm, k), lambda i: (i, 0)),
                  pl.BlockSpec((None, k, d), lambda i: (layer, 0, 0)),
                  pl.BlockSpec((1, d), lambda i: (0, 0)),
                  pl.BlockSpec((tm, d), lambda i: (i, 0))],
        out_specs=pl.BlockSpec((tm, d), lambda i: (i, 0)),
        out_shape=jax.ShapeDtypeStruct((s, d), F32),
        compiler_params=_compiler_params(("parallel",)),
        name="matmul_norm_res",
    )(a, w, gain.reshape(1, d), x)


def _ffn_kernel(x_ref, gpre_ref, wg_ref, wu_ref, wd_ref, gpost_ref, o_ref, xn_ref):
    j = pl.program_id(1)

    @pl.when(j == 0)
    def _():
        xn_ref[...] = _rms(x_ref[...], gpre_ref[...]).astype(BF16)
        o_ref[...] = jnp.zeros_like(o_ref)

    xn = xn_ref[...]
    gate = jnp.dot(xn, wg_ref[...], preferred_element_type=F32)
    up = jnp.dot(xn, wu_ref[...], preferred_element_type=F32)
    act = (_silu(gate) * up).astype(BF16)
    o_ref[...] += jnp.dot(act, wd_ref[...], preferred_element_type=F32)

    @pl.when(j == pl.num_programs(1) - 1)
    def _():
        o_ref[...] = x_ref[...] + _rms(o_ref[...], gpost_ref[...])


def _ffn(x, g_pre, w_gate_up, w_down, g_post, layer, *, tm, tf):
    s, d = x.shape
    d_ff = w_down.shape[1]
    nf = d_ff // tf
    return pl.pallas_call(
        _ffn_kernel,
        grid=(s // tm, nf),
        in_specs=[pl.BlockSpec((tm, d), lambda i, j: (i, 0)),
                  pl.BlockSpec((1, d), lambda i, j: (0, 0)),
                  pl.BlockSpec((None, d, tf), lambda i, j: (layer, 0, j)),
                  pl.BlockSpec((None, d, tf), lambda i, j: (layer, 0, nf + j)),
                  pl.BlockSpec((None, tf, d), lambda i, j: (layer, j, 0)),
                  pl.BlockSpec((1, d), lambda i, j: (0, 0))],
        out_specs=pl.BlockSpec((tm, d), lambda i, j: (i, 0)),
        out_shape=jax.ShapeDtypeStruct((s, d), F32),
        scratch_shapes=[pltpu.VMEM((tm, d), BF16)],
        compiler_params=_compiler_params(("parallel", "arbitrary"), FFN_VMEM_LIMIT_BYTES),
        name="swiglu_ffn",
    )(x, g_pre.reshape(1, d), w_gate_up, w_gate_up, w_down, g_post.reshape(1, d))


def _t5_bucket(dist):
    max_exact = NUM_BUCKETS // 2
    d_f = jnp.maximum(dist, 1).astype(F32)
    large = max_exact + (jnp.log(d_f / max_exact) / math.log(MAX_DISTANCE / max_exact)
                         * (NUM_BUCKETS - max_exact)).astype(jnp.int32)
    large = jnp.minimum(large, NUM_BUCKETS - 1)
    return jnp.where(dist < max_exact, dist, large)


def _tile_offsets(group):
    w = DSA_SPAN
    classes = DSA_RESIDUES // DSA_GROUPS[group][1]
    rows = w // classes
    i = jnp.arange(w)
    return (i % rows) * classes + i // rows


def _band_buckets(group):
    w = DSA_SPAN
    off = _tile_offsets(group)
    rel = off[:, None] + w - jnp.concatenate([off, off + w])[None, :]
    band = (rel >= 0) & (rel <= w)
    return jnp.where(band, _t5_bucket(jnp.clip(rel, 0, w) * DSA_GROUPS[group][1]), -1).astype(jnp.int32)


def _dsa_min_tiles(group):
    classes = DSA_RESIDUES // DSA_GROUPS[group][1]
    return max(1, BF16_SUBLANES * classes // DSA_SPAN)


def _dsa_kernel(tab_ref, bucket_ref, q_ref, kp_ref, kc_ref, vp_ref, vc_ref, o_ref, lse_ref, bias_ref, *, group):
    w = DSA_SPAN
    classes = q_ref.shape[0]
    rows = w // classes
    n_tiles = q_ref.shape[1] // rows
    step = pl.program_id(1)

    @pl.when((pl.program_id(0) == 0) & (step == 0))
    def _():
        bucket = bucket_ref[...]
        key_col = lax.broadcasted_iota(jnp.int32, (w, 2 * w), 1)

        def per_head(h, carry):
            b = jnp.full(bucket.shape, MASK_VALUE, F32)
            for t in range(NUM_BUCKETS):
                b = jnp.where(bucket == t, tab_ref[t, group * DSA_HEADS + h] * LOG2_E, b)
            bias_ref[0, h] = b
            bias_ref[1, h] = jnp.where(key_col < w, MASK_VALUE, b)
            return carry

        lax.fori_loop(0, DSA_HEADS, per_head, 0)

    def tiles(ref, cols):
        blk = ref[:, :, cols]
        if rows % BF16_SUBLANES:
            blk = blk.astype(F32)
        return [blk[:, t * rows:(t + 1) * rows, :].reshape(w, blk.shape[-1]).astype(ref.dtype)
                for t in range(ref.shape[1] // rows)]

    def store(ref, cols, parts):
        parts = [p.reshape(classes, rows, p.shape[-1]) for p in parts]
        ref[:, :, cols] = jnp.concatenate(parts, axis=1).astype(ref.dtype)

    first = jnp.where(step == 0, 1, 0)
    lane = lax.broadcasted_iota(jnp.int32, (w, LANES), 1)
    lse_all = [jnp.zeros((w, LANES), F32) for _ in range(n_tiles)]
    ones = jnp.ones((2 * w, DSA_DH), BF16)
    for h in range(DSA_HEADS):
        hs = slice(h * DSA_DH, (h + 1) * DSA_DH)
        q_t = tiles(q_ref, hs)
        k_t = tiles(kc_ref, hs)
        v_t = tiles(vc_ref, hs)
        k_prev = [tiles(kp_ref, hs)[-1]] + k_t[:-1]
        v_prev = [tiles(vp_ref, hs)[-1]] + v_t[:-1]
        outs = []
        for t in range(n_tiles):
            k = jnp.concatenate([k_prev[t], k_t[t]], axis=0)
            v = jnp.concatenate([v_prev[t], v_t[t]], axis=0)
            s = _bdot_nt(q_t[t], k) + bias_ref[first if t == 0 else 0, h]
            m = jnp.max(s, axis=-1, keepdims=True)
            p = jnp.exp2(s - m)
            acc = _bdot(p, jnp.concatenate([v, ones], axis=1))
            l = acc[:, DSA_DH:]
            outs.append(acc[:, :DSA_DH] / l)
            lse_all[t] = jnp.where(lane == h, (m + jnp.log2(l)) * LN_2, lse_all[t])
        store(o_ref, hs, outs)
    store(lse_ref, slice(None), lse_all)


def _dsa_group(q_all, kv, rel_bias, group, *, tiles_per_step):
    s = q_all.shape[0]
    dilation = DSA_GROUPS[group][1]
    w = DSA_SPAN
    gw = DSA_GROUP_W
    classes = DSA_RESIDUES // dilation
    class_len = s // DSA_RESIDUES
    rows = w // classes
    prev_tiles = _dsa_min_tiles(group)
    n_tiles = max(prev_tiles, min(tiles_per_step, class_len // rows))
    assert n_tiles % prev_tiles == 0 and class_len % (n_tiles * rows) == 0

    def view(a):
        return a.reshape(classes, dilation, class_len, a.shape[-1])

    def cur(width, col):
        return pl.BlockSpec((classes, None, n_tiles * rows, width), lambda r, n: (0, r, n, col))

    def prev(width, col):
        ratio = n_tiles // prev_tiles
        return pl.BlockSpec((classes, None, prev_tiles * rows, width),
                            lambda r, n: (0, r, jnp.maximum(n * ratio - 1, 0), col))

    o, lse = pl.pallas_call(
        functools.partial(_dsa_kernel, group=group),
        grid=(dilation, class_len // (n_tiles * rows)),
        in_specs=[pl.BlockSpec(memory_space=pltpu.SMEM),
                  pl.BlockSpec((w, 2 * w), lambda r, n: (0, 0)),
                  cur(gw, group),
                  prev(gw, group), cur(gw, group),
                  prev(gw, N_GROUPS + group), cur(gw, N_GROUPS + group)],
        out_specs=[cur(gw, 0), cur(LANES, 0)],
        out_shape=[jax.ShapeDtypeStruct((classes, dilation, class_len, gw), BF16),
                   jax.ShapeDtypeStruct((classes, dilation, class_len, LANES), F32)],
        scratch_shapes=[pltpu.VMEM((2, DSA_HEADS, w, 2 * w), F32)],
        compiler_params=_compiler_params(("arbitrary", "arbitrary")),
        name=f"dsa_attention_g{group}",
    )(rel_bias, _band_buckets(group), view(q_all), view(kv), view(kv), view(kv), view(kv))
    return o.reshape(s, gw), lse.reshape(s, LANES)


def _to_residue_major_kernel(x_ref, o_ref):
    rows = o_ref.shape[1]
    x = x_ref[...].reshape(rows, DSA_RESIDUES, x_ref.shape[-1])
    o_ref[...] = pltpu.einshape("lrd->rld", x)


def _from_residue_major_kernel(x_ref, o_ref):
    rows = x_ref.shape[1]
    o_ref[...] = pltpu.einshape("rld->lrd", x_ref[...]).reshape(rows * DSA_RESIDUES, x_ref.shape[-1])


def _residue_major(x, *, rows, inverse):
    s, d = x.shape
    class_len = s // DSA_RESIDUES
    natural = pl.BlockSpec((rows * DSA_RESIDUES, d), lambda i: (i, 0))
    major = pl.BlockSpec((DSA_RESIDUES, rows, d), lambda i: (0, i, 0))
    if inverse:
        body, specs, arg, shape = (_from_residue_major_kernel, (major, natural),
                                   x.reshape(DSA_RESIDUES, class_len, d), (s, d))
    else:
        body, specs, arg, shape = _to_residue_major_kernel, (natural, major), x, (DSA_RESIDUES, class_len, d)
    out = pl.pallas_call(
        body,
        grid=(class_len // rows,),
        in_specs=[specs[0]],
        out_specs=specs[1],
        out_shape=jax.ShapeDtypeStruct(shape, x.dtype),
        compiler_params=_compiler_params(("parallel",)),
        name="from_residue_major" if inverse else "to_residue_major",
    )(arg)
    return out.reshape(s, d)


def _dsa_out_kernel(o0_ref, o1_ref, o2_ref, l0_ref, l1_ref, l2_ref, w_ref, g_ref, x_ref, out_ref, *, row_chunk):
    o_refs = [o0_ref, o1_ref, o2_ref]
    l_refs = [l0_ref, l1_ref, l2_ref]

    def rows(c):
        return pl.ds(c * row_chunk, row_chunk)

    def merged_rows(c):
        lses = [l[rows(c), :] for l in l_refs]
        top = jnp.maximum(jnp.maximum(lses[0], lses[1]), lses[2])
        es = [jnp.exp(l - top) for l in lses]
        inv_den = 1.0 / (es[0] + es[1] + es[2])
        wts = [e * inv_den for e in es]
        heads = []
        for h in range(DSA_HEADS):
            hs = slice(h * DSA_DH, (h + 1) * DSA_DH)
            merged = wts[0][:, h:h + 1] * o_refs[0][rows(c), hs].astype(F32)
            for g in range(1, N_GROUPS):
                merged = merged + wts[g][:, h:h + 1] * o_refs[g][rows(c), hs].astype(F32)
            heads.append(merged.astype(BF16))
        return jnp.concatenate(heads, axis=1)

    def finish(c, m):
        out = x_ref[rows(c), :] + _rms(m, g_ref[...])
        out_ref[rows(c), :] = out
        return out

    _overlapped_row_chunks(x_ref.shape[0] // row_chunk, merged_rows, w_ref, finish)


def _dsa_out(os_, lses, w, gain, x, layer, *, tm, row_chunk):
    s, k = os_[0].shape
    d = w.shape[2]
    o_spec = pl.BlockSpec((tm, k), lambda i: (i, 0))
    l_spec = pl.BlockSpec((tm, LANES), lambda i: (i, 0))
    return pl.pallas_call(
        functools.partial(_dsa_out_kernel, row_chunk=row_chunk),
        grid=(s // tm,),
        in_specs=[o_spec, o_spec, o_spec, l_spec, l_spec, l_spec,
                  pl.BlockSpec((None, k, d), lambda i: (layer, 0, 0)),
                  pl.BlockSpec((1, d), lambda i: (0, 0)),
                  pl.BlockSpec((tm, d), lambda i: (i, 0))],
        out_specs=pl.BlockSpec((tm, d), lambda i: (i, 0)),
        out_shape=jax.ShapeDtypeStruct((s, d), F32),
        compiler_params=_compiler_params(("parallel",)),
        name="dsa_merge_out",
    )(*os_, *lses, w, gain.reshape(1, d), x)


def _tiles(s):
    tm_proj = min(1024, s)
    eighth = tm_proj // 8
    return dict(tm_proj=tm_proj, tn_proj=2048, tn_gdn=1024, tm_out=min(512, s), out_chunk=128, tm_ffn=min(1024, s), tf_ffn=512, dsa_tiles=4,
                perm_rows=32, gdn_proj_chunks=(2 * eighth, 2 * eighth, 2 * eighth, eighth, eighth))


def kernel(x, norm_gains, ffn_w_gate_up, ffn_w_down, gdn_w_in, gdn_conv_w, gdn_a_log, gdn_dt_bias,
           gdn_out_norm, gdn_w_out, kv_norm, kv_w, dsa_w_q, dsa_w_out, rel_bias):
    b, s, d = x.shape
    assert b == 1 and d == D_MODEL and s % (DSA_SPAN * DSA_GROUPS[-1][1]) == 0
    t = _tiles(s)
    depth = norm_gains.shape[0]
    n_a = gdn_w_in.shape[0]
    xs = x.reshape(s, d)
    kv = None
    ffn_w_gate_up, ffn_w_down = ffn_w_gate_up.astype(BF16), ffn_w_down.astype(BF16)
    gdn_w_in, gdn_w_out = gdn_w_in.astype(BF16), gdn_w_out.astype(BF16)
    dsa_w_q, dsa_w_out = dsa_w_q.astype(BF16), dsa_w_out.astype(BF16)
    for layer in range(depth):
        gains = norm_gains[layer]
        if layer < n_a:
            proj, gb = _gdn_in_proj(xs, gains[0], gdn_w_in, gdn_conv_w, layer, gdn_a_log[layer], gdn_dt_bias[layer],
                                    tm=t["tm_proj"], tn=t["tn_gdn"], row_chunks=t["gdn_proj_chunks"])
            o = _gdn(proj, gb, gdn_out_norm[layer])
            xs = _matmul_norm_res(o, gdn_w_out, gains[1], xs, layer, tm=t["tm_out"], row_chunk=t["out_chunk"])
        else:
            j = layer - n_a
            q_all = _norm_matmul(xs, gains[0], dsa_w_q, j, tm=t["tm_proj"], tn=t["tn_proj"], out_dtype=BF16,
                                 out_scale=DSA_DH ** -0.5 * LOG2_E)
            outs = [_dsa_group(q_all, kv, rel_bias, g, tiles_per_step=t["dsa_tiles"]) for g in range(N_GROUPS)]
            xs = _dsa_out([o for o, _ in outs], [l for _, l in outs], dsa_w_out, gains[1], xs, j,
                          tm=t["tm_out"], row_chunk=t["out_chunk"])
        xs = _ffn(xs, gains[2], ffn_w_gate_up, ffn_w_down, gains[3], layer, tm=t["tm_ffn"], tf=t["tf_ffn"])
        if layer == n_a - 1 and depth > n_a:
            xs = _residue_major(xs, rows=t["perm_rows"], inverse=False)
            kv = _norm_matmul(xs, kv_norm, kv_w.astype(BF16)[None], 0, tm=t["tm_proj"], tn=t["tn_proj"],
                              out_dtype=BF16)
    if depth > n_a:
        xs = _residue_major(xs, rows=t["perm_rows"], inverse=True)
    return xs.reshape(b, s, d)
```

```python
import functools
import math

import jax
import jax.numpy as jnp
from jax import lax
from jax.experimental import pallas as pl
from jax.experimental.pallas import tpu as pltpu

F32 = jnp.float32
BF16 = jnp.bfloat16

RMS_EPS = 1e-6
D_MODEL = 2048

GDN_HEADS = 16
GDN_DK = 128
GDN_DV = 128
GDN_CONV = 4
GDN_QK_W = GDN_HEADS * GDN_DK
GDN_V_W = GDN_HEADS * GDN_DV
GDN_CONV_W = 2 * GDN_QK_W + GDN_V_W
GDN_CHUNK = 128
GDN_HEAD_BLOCK = 16
GDN_INV_BASE = 8
CONV_HALO_ROWS = 8

DSA_GROUPS = ((128, 1), (512, 4), (2048, 16))
N_GROUPS = len(DSA_GROUPS)
DSA_HEADS = 16
DSA_DH = 128
DSA_SPAN = 128
DSA_GROUP_W = DSA_HEADS * DSA_DH
NUM_BUCKETS = 32
MAX_DISTANCE = 2048
MASK_VALUE = -1e30
LOG2_E = math.log2(math.e)
LN_2 = math.log(2.0)

DSA_RESIDUES = DSA_GROUPS[-1][1]

LANES = 128
BF16_SUBLANES = 16
VMEM_BYTES = 64 * 1024 * 1024
VMEM_LIMIT_BYTES = VMEM_BYTES * 7 // 8
FFN_VMEM_LIMIT_BYTES = VMEM_BYTES * 31 // 32


def _compiler_params(semantics, vmem_limit_bytes=VMEM_LIMIT_BYTES):
    return pltpu.CompilerParams(dimension_semantics=semantics, vmem_limit_bytes=vmem_limit_bytes)


def _rms(x, gain):
    ms = jnp.mean(x * x, axis=-1, keepdims=True)
    return x * lax.rsqrt(ms + RMS_EPS) * gain


def _sigmoid(x):
    return 1.0 / (1.0 + jnp.exp(-x))


def _silu(x):
    h = 0.5 * x
    return h + h * jnp.tanh(h)


def _bdot(a, b):
    return jnp.dot(a.astype(BF16), b.astype(BF16), preferred_element_type=F32)


def _bdot_nt(a, b):
    return lax.dot_general(a.astype(BF16), b.astype(BF16), (((1,), (1,)), ((), ())),
                           preferred_element_type=F32)


def _bdot_tn(a, b):
    return lax.dot_general(a.astype(BF16), b.astype(BF16), (((0,), (0,)), ((), ())),
                           preferred_element_type=F32)


def _norm_matmul_kernel(x_ref, g_ref, w_ref, o_ref, xn_ref, *, out_scale):
    @pl.when(pl.program_id(1) == 0)
    def _():
        xn_ref[...] = _rms(x_ref[...], g_ref[...]).astype(BF16)

    y = jnp.dot(xn_ref[...], w_ref[...], preferred_element_type=F32)
    if out_scale != 1.0:
        y = y * out_scale
    o_ref[...] = y.astype(o_ref.dtype)


def _norm_matmul(x, gain, w, layer, *, tm, tn, out_dtype, out_scale=1.0):
    s, d = x.shape
    n = w.shape[2]
    return pl.pallas_call(
        functools.partial(_norm_matmul_kernel, out_scale=out_scale),
        grid=(s // tm, n // tn),
        in_specs=[pl.BlockSpec((tm, d), lambda i, j: (i, 0)),
                  pl.BlockSpec((1, d), lambda i, j: (0, 0)),
                  pl.BlockSpec((None, d, tn), lambda i, j: (layer, 0, j))],
        out_specs=pl.BlockSpec((tm, tn), lambda i, j: (i, j)),
        out_shape=jax.ShapeDtypeStruct((s, n), out_dtype),
        scratch_shapes=[pltpu.VMEM((tm, d), BF16)],
        compiler_params=_compiler_params(("parallel", "arbitrary")),
        name="norm_matmul",
    )(x, gain.reshape(1, d), w)


def _ordered_after(x, anchor):
    rows, lanes = BF16_SUBLANES, LANES
    a = lax.bitcast_convert_type(jnp.abs(anchor[:rows, :lanes].astype(F32)), jnp.int32)
    zero = lax.shift_right_logical(a, 31)
    tile = lax.bitcast_convert_type(x[:rows, :lanes].astype(F32), jnp.int32) | zero
    tile = lax.bitcast_convert_type(tile, F32).astype(x.dtype)
    top = jnp.concatenate([tile, x[:rows, lanes:]], axis=1)
    return jnp.concatenate([top, x[rows:, :]], axis=0)


def _gdn_in_proj_kernel(x_ref, g_ref, w_ref, cw_ref, wba_ref, alog_ref, dtb_ref, o_ref, gb_ref,
                        xn_ref, halo_ref, raw_ref, *, qk_blocks, v_blocks, row_chunks):
    i = pl.program_id(0)
    j = pl.program_id(1)
    tm = x_ref.shape[0]
    tn = w_ref.shape[1]

    @pl.when(j == 0)
    def _():
        xn = _rms(x_ref[...], g_ref[...]).astype(BF16)
        xn_ref[...] = xn
        gb_ref[...] = _gdn_gate_values(jnp.dot(xn, wba_ref[...], preferred_element_type=F32),
                                       alog_ref[...], dtb_ref[...])

    @pl.when((i == 0) & (j == 0))
    def _():
        halo_ref[...] = jnp.zeros_like(halo_ref)

    chunks = []
    for rows in row_chunks:
        chunks.append((sum(r for _, r in chunks), rows))
    assert sum(row_chunks) == tm
    n_chunks = len(chunks)

    def conv_silu(r0, rows):
        raw = raw_ref[pl.ds(r0, rows), :]
        if r0 == 0:
            halo = jnp.where(i > 0, halo_ref[j], 0.0)
        else:
            halo = raw_ref[pl.ds(r0 - CONV_HALO_ROWS, CONV_HALO_ROWS), :]
        cw = cw_ref[...]
        groups = jnp.concatenate([halo, raw], axis=0).reshape(rows // CONV_HALO_ROWS + 1, CONV_HALO_ROWS, tn)
        in_group = lax.broadcasted_iota(jnp.int32, (rows // CONV_HALO_ROWS, CONV_HALO_ROWS, tn), 1)
        y = cw[GDN_CONV - 1:GDN_CONV, :] * raw
        for t in range(GDN_CONV - 1):
            back = GDN_CONV - 1 - t
            rot = pltpu.roll(groups, back, axis=1)
            shifted = jnp.where(in_group < back, rot[:-1], rot[1:]).reshape(rows, tn)
            y = y + cw[t:t + 1, :] * shifted
        return _silu(y)

    def finish_qk(r0, rows):
        y = conv_silu(r0, rows)
        c = jnp.where(j < qk_blocks, float(GDN_DK), 1.0).astype(F32)
        out = []
        for h in range(tn // GDN_DK):
            yh = y[:, h * GDN_DK:(h + 1) * GDN_DK]
            ss = jnp.sum(yh * yh, axis=-1, keepdims=True)
            out.append((yh * lax.rsqrt(ss * c + RMS_EPS * c)).astype(o_ref.dtype))
        return jnp.concatenate(out, axis=1)

    def finish_v(r0, rows):
        return conv_silu(r0, rows).astype(o_ref.dtype)

    def finish_z(r0, rows):
        return _silu(raw_ref[pl.ds(r0, rows), :]).astype(o_ref.dtype)

    def run(finish):
        done = []
        for c in range(n_chunks + 1):
            if c < n_chunks:
                rows = pl.ds(*chunks[c])
                lhs = xn_ref[rows, :]
                if c >= 2:
                    lhs = _ordered_after(lhs, done[c - 2])
                raw_ref[rows, :] = jnp.dot(lhs, w_ref[...], preferred_element_type=F32)
            if c >= 1:
                out = finish(*chunks[c - 1])
                o_ref[pl.ds(*chunks[c - 1]), :] = out
                done.append(out)

    @pl.when(j < 2 * qk_blocks)
    def _():
        run(finish_qk)
        halo_ref[j] = raw_ref[pl.ds(tm - CONV_HALO_ROWS, CONV_HALO_ROWS), :]

    @pl.when((j >= 2 * qk_blocks) & (j < 2 * qk_blocks + v_blocks))
    def _():
        run(finish_v)
        halo_ref[j] = raw_ref[pl.ds(tm - CONV_HALO_ROWS, CONV_HALO_ROWS), :]

    @pl.when(j >= 2 * qk_blocks + v_blocks)
    def _():
        run(finish_z)


def _gdn_in_proj(x, gain, w_in, conv_w, layer, a_log, dt_bias, *, tm, tn, row_chunks):
    s, d = x.shape
    n = GDN_CONV_W + GDN_V_W
    assert n % LANES == 0 and w_in.shape[2] - n == 2 * GDN_HEADS
    conv_blocks = GDN_CONV_W // tn
    alog, dtb = _gdn_gate_params(a_log, dt_bias)
    lane_row = pl.BlockSpec((1, LANES), lambda i, j: (0, 0))
    return pl.pallas_call(
        functools.partial(_gdn_in_proj_kernel, qk_blocks=GDN_QK_W // tn, v_blocks=GDN_V_W // tn,
                          row_chunks=row_chunks),
        grid=(s // tm, n // tn),
        in_specs=[pl.BlockSpec((tm, d), lambda i, j: (i, 0)),
                  pl.BlockSpec((1, d), lambda i, j: (0, 0)),
                  pl.BlockSpec((None, d, tn), lambda i, j: (layer, 0, j)),
                  pl.BlockSpec((None, GDN_CONV, tn), lambda i, j: (layer, 0, jnp.minimum(j, conv_blocks - 1))),
                  pl.BlockSpec((None, d, LANES), lambda i, j: (layer, 0, n // LANES)), lane_row, lane_row],
        out_specs=[pl.BlockSpec((tm, tn), lambda i, j: (i, j)),
                   pl.BlockSpec((tm, LANES), lambda i, j: (i, 0))],
        out_shape=[jax.ShapeDtypeStruct((s, n), BF16), jax.ShapeDtypeStruct((s, LANES), F32)],
        scratch_shapes=[pltpu.VMEM((tm, d), BF16),
                        pltpu.VMEM((conv_blocks, CONV_HALO_ROWS, tn), F32),
                        pltpu.VMEM((tm, tn), F32)],
        compiler_params=_compiler_params(("arbitrary", "arbitrary")),
        name="gdn_in_proj",
    )(x, gain.reshape(1, d), w_in, conv_w, w_in, alog, dtb)


def _gdn_gate_values(p, alog, dtb):
    lane = lax.broadcasted_iota(jnp.int32, p.shape, 1)
    beta = _sigmoid(p)
    a = p + dtb
    softplus = jnp.maximum(a, 0.0) + jnp.log(1.0 + jnp.exp(-jnp.abs(a)))
    g = -jnp.exp(alog) * softplus
    return jnp.where(lane < GDN_HEADS, beta, jnp.where(lane < 2 * GDN_HEADS, g, 0.0))


def _gdn_gate_params(a_log, dt_bias):
    pad = LANES - 2 * GDN_HEADS
    zeros_h = jnp.zeros((GDN_HEADS,), F32)
    alog = jnp.pad(jnp.concatenate([zeros_h, a_log.astype(F32)]), (0, pad)).reshape(1, LANES)
    dtb = jnp.pad(jnp.concatenate([zeros_h, dt_bias.astype(F32)]), (0, pad)).reshape(1, LANES)
    return alog, dtb


def _unit_lower_inverse(lows, row, col):
    c = lows[0].shape[0]
    eye = (row == col).astype(F32)
    blk = GDN_INV_BASE
    diag_mask = (row // blk) == (col // blk)
    powers = [jnp.where(diag_mask, low, 0.0) for low in lows]
    invs = [eye - p for p in powers]
    span = 2
    while span < blk:
        powers = [_bdot(p, p) for p in powers]
        invs = [x + _bdot(x, p) for x, p in zip(invs, powers)]
        span *= 2
    while blk < c:
        off_mask = ((row // (2 * blk)) == (col // (2 * blk))) & ((row // blk) != (col // blk))
        offs = [_bdot(jnp.where(off_mask, low, 0.0), x) for low, x in zip(lows, invs)]
        invs = [x - _bdot(x, t) for x, t in zip(invs, offs)]
        blk *= 2
    return invs


def _gdn_kernel(q_ref, k_ref, v_ref, z_ref, gb_ref, on_ref, o_ref, state_ref):
    c_len = q_ref.shape[0]
    step = pl.program_id(1)

    @pl.when(step == 0)
    def _():
        state_ref[...] = jnp.zeros_like(state_ref)

    row = lax.broadcasted_iota(jnp.int32, (c_len, c_len), 0)
    col = lax.broadcasted_iota(jnp.int32, (c_len, c_len), 1)
    causal = row >= col
    strict = row > col

    gb = gb_ref[...]
    tri = causal.astype(BF16)
    gb_hi = gb.astype(BF16)
    gb_lo = (gb - gb_hi.astype(F32)).astype(BF16)
    gcum = (jnp.dot(tri, gb_hi, preferred_element_type=F32)
            + jnp.dot(tri, gb_lo, preferred_element_type=F32))
    gcum_t = gcum.T

    heads = range(GDN_HEAD_BLOCK)

    def head_slices(ref):
        return [ref[:, i * GDN_DK:(i + 1) * GDN_DK].astype(F32) for i in heads]

    qn, kn, v_all = head_slices(q_ref), head_slices(k_ref), head_slices(v_ref)
    beta = [gb[:, i:i + 1] for i in heads]
    gcol = [gcum[:, GDN_HEADS + i:GDN_HEADS + i + 1] for i in heads]
    grow = [gcum_t[GDN_HEADS + i:GDN_HEADS + i + 1, :] for i in heads]
    decay = [jnp.exp(jnp.where(causal, gc - gr, MASK_VALUE)) for gc, gr in zip(gcol, grow)]
    e_col = [jnp.exp(gc) for gc in gcol]
    g_last = [gc[c_len - 1:c_len, :] for gc in gcol]
    kb = [k * b for k, b in zip(kn, beta)]
    kk_qk = [_bdot_nt(jnp.concatenate([kb[i], qn[i]], axis=0), kn[i]) for i in heads]
    low = [jnp.where(strict, kk_qk[i][:c_len] * decay[i], 0.0) for i in heads]
    intra = [kk_qk[i][c_len:] * decay[i] for i in heads]
    inv = _unit_lower_inverse(low, row, col)
    sol = [_bdot(inv[i], jnp.concatenate([v_all[i] * beta[i], kb[i] * e_col[i]], axis=1)) for i in heads]
    state = [state_ref[i] for i in heads]
    ws_qs = [_bdot(jnp.concatenate([sol[i][:, GDN_DV:], qn[i] * e_col[i]], axis=0), state[i]) for i in heads]
    v_new = [sol[i][:, :GDN_DV] - ws_qs[i][:c_len] for i in heads]
    o = [ws_qs[i][c_len:] + _bdot(intra[i], v_new[i]) for i in heads]
    for i in heads:
        k_dec = kn[i] * jnp.exp(g_last[i] - gcol[i])
        state_ref[i] = state[i] * jnp.exp(g_last[i]) + _bdot_tn(k_dec, v_new[i])
    out_gain = on_ref[...]
    for i in heads:
        hs = slice(i * GDN_DV, (i + 1) * GDN_DV)
        o_ref[:, hs] = (_rms(o[i], out_gain) * z_ref[:, hs].astype(F32)).astype(o_ref.dtype)


def _gdn(proj, gb, out_norm):
    s = proj.shape[0]
    c = GDN_CHUNK
    hb = GDN_HEAD_BLOCK
    ng = GDN_HEADS // hb
    bw = hb * GDN_DK

    def cur(sec):
        return pl.BlockSpec((c, bw), lambda g, t, sec=sec: (t, sec * ng + g))

    return pl.pallas_call(
        _gdn_kernel,
        grid=(ng, s // c),
        in_specs=[cur(0), cur(1), cur(2), cur(3),
                  pl.BlockSpec((c, LANES), lambda g, t: (t, g)),
                  pl.BlockSpec((1, GDN_DV), lambda g, t: (0, 0))],
        out_specs=pl.BlockSpec((c, bw), lambda g, t: (t, g)),
        out_shape=jax.ShapeDtypeStruct((s, GDN_V_W), BF16),
        scratch_shapes=[pltpu.VMEM((hb, GDN_DK, GDN_DV), F32)],
        compiler_params=_compiler_params(("parallel", "arbitrary")),
        name="gdn_delta_rule",
    )(proj, proj, proj, proj, gb, out_norm.reshape(1, GDN_DV))


def _overlapped_row_chunks(n_chunks, lhs_rows, w_ref, finish_rows):
    done = []
    prev = None
    for c in range(n_chunks + 1):
        cur = None
        if c < n_chunks:
            lhs = lhs_rows(c)
            if c >= 2:
                lhs = _ordered_after(lhs, done[c - 2])
            cur = jnp.dot(lhs, w_ref[...], preferred_element_type=F32)
        if c >= 1:
            done.append(finish_rows(c - 1, prev))
        prev = cur


def _matmul_norm_res_kernel(a_ref, w_ref, g_ref, x_ref, o_ref, *, row_chunk):
    def rows(c):
        return pl.ds(c * row_chunk, row_chunk)

    def finish(c, m):
        out = x_ref[rows(c), :] + _rms(m, g_ref[...])
        o_ref[rows(c), :] = out
        return out

    _overlapped_row_chunks(a_ref.shape[0] // row_chunk, lambda c: a_ref[rows(c), :], w_ref, finish)


def _matmul_norm_res(a, w, gain, x, layer, *, tm, row_chunk):
    s, k = a.shape
    d = w.shape[2]
    return pl.pallas_call(
        functools.partial(_matmul_norm_res_kernel, row_chunk=row_chunk),
        grid=(s // tm,),
        in_specs=[pl.BlockSpec((tm, k), lambda i: (i, 0)),
                  pl.BlockSpec((None, k, d), lambda i: (layer, 0, 0)),
                  pl.BlockSpec((1, d), lambda i: (0, 0)),
                  pl.BlockSpec((tm, d), lambda i: (i, 0))],
        out_specs=pl.BlockSpec((tm, d), lambda i: (i, 0)),
        out_shape=jax.ShapeDtypeStruct((s, d), F32),
        compiler_params=_compiler_params(("parallel",)),
        name="matmul_norm_res",
    )(a, w, gain.reshape(1, d), x)


def _ffn_kernel(x_ref, gpre_ref, wg_ref, wu_ref, wd_ref, gpost_ref, o_ref, xn_ref, *, row_chunk):
    j = pl.program_id(1)
    last = pl.num_programs(1) - 1

    @pl.when(j == 0)
    def _():
        xn_ref[...] = _rms(x_ref[...], gpre_ref[...]).astype(BF16)
        o_ref[...] = jnp.zeros_like(o_ref)

    def activations():
        xn = xn_ref[...]
        gate = jnp.dot(xn, wg_ref[...], preferred_element_type=F32)
        up = jnp.dot(xn, wu_ref[...], preferred_element_type=F32)
        return (_silu(gate) * up).astype(BF16)

    @pl.when(j < last)
    def _():
        o_ref[...] += jnp.dot(activations(), wd_ref[...], preferred_element_type=F32)

    @pl.when(j == last)
    def _():
        act = activations()

        def rows(c):
            return pl.ds(c * row_chunk, row_chunk)

        def finish(c, m):
            out = x_ref[rows(c), :] + _rms(o_ref[rows(c), :] + m, gpost_ref[...])
            o_ref[rows(c), :] = out
            return out

        _overlapped_row_chunks(o_ref.shape[0] // row_chunk,
                               lambda c: act[c * row_chunk:(c + 1) * row_chunk, :], wd_ref, finish)


def _ffn(x, g_pre, w_gate_up, w_down, g_post, layer, *, tm, tf, row_chunk):
    s, d = x.shape
    d_ff = w_down.shape[1]
    nf = d_ff // tf
    return pl.pallas_call(
        functools.partial(_ffn_kernel, row_chunk=row_chunk),
        grid=(s // tm, nf),
        in_specs=[pl.BlockSpec((tm, d), lambda i, j: (i, 0)),
                  pl.BlockSpec((1, d), lambda i, j: (0, 0)),
                  pl.BlockSpec((None, d, tf), lambda i, j: (layer, 0, j)),
                  pl.BlockSpec((None, d, tf), lambda i, j: (layer, 0, nf + j)),
                  pl.BlockSpec((None, tf, d), lambda i, j: (layer, j, 0)),
                  pl.BlockSpec((1, d), lambda i, j: (0, 0))],
        out_specs=pl.BlockSpec((tm, d), lambda i, j: (i, 0)),
        out_shape=jax.ShapeDtypeStruct((s, d), F32),
        scratch_shapes=[pltpu.VMEM((tm, d), BF16)],
        compiler_params=_compiler_params(("parallel", "arbitrary"), FFN_VMEM_LIMIT_BYTES),
        name="swiglu_ffn",
    )(x, g_pre.reshape(1, d), w_gate_up, w_gate_up, w_down, g_post.reshape(1, d))


def _t5_bucket(dist):
    max_exact = NUM_BUCKETS // 2
    d_f = jnp.maximum(dist, 1).astype(F32)
    large = max_exact + (jnp.log(d_f / max_exact) / math.log(MAX_DISTANCE / max_exact)
                         * (NUM_BUCKETS - max_exact)).astype(jnp.int32)
    large = jnp.minimum(large, NUM_BUCKETS - 1)
    return jnp.where(dist < max_exact, dist, large)


def _tile_offsets(group):
    w = DSA_SPAN
    classes = DSA_RESIDUES // DSA_GROUPS[group][1]
    rows = w // classes
    i = jnp.arange(w)
    return (i % rows) * classes + i // rows


def _band_buckets(group):
    w = DSA_SPAN
    off = _tile_offsets(group)
    rel = off[:, None] + w - jnp.concatenate([off, off + w])[None, :]
    band = (rel >= 0) & (rel <= w)
    return jnp.where(band, _t5_bucket(jnp.clip(rel, 0, w) * DSA_GROUPS[group][1]), -1).astype(jnp.int32)


def _dsa_min_tiles(group):
    classes = DSA_RESIDUES // DSA_GROUPS[group][1]
    return max(1, BF16_SUBLANES * classes // DSA_SPAN)


def _dsa_kernel(tab_ref, bucket_ref, q_ref, kp_ref, kc_ref, vp_ref, vc_ref, o_ref, lse_ref, bias_ref, *, group):
    w = DSA_SPAN
    classes = q_ref.shape[0]
    rows = w // classes
    n_tiles = q_ref.shape[1] // rows
    step = pl.program_id(1)

    @pl.when((pl.program_id(0) == 0) & (step == 0))
    def _():
        bucket = bucket_ref[...]
        key_col = lax.broadcasted_iota(jnp.int32, (w, 2 * w), 1)

        def per_head(h, carry):
            b = jnp.full(bucket.shape, MASK_VALUE, F32)
            for t in range(NUM_BUCKETS):
                b = jnp.where(bucket == t, tab_ref[t, group * DSA_HEADS + h] * LOG2_E, b)
            bias_ref[0, h] = b
            bias_ref[1, h] = jnp.where(key_col < w, MASK_VALUE, b)
            return carry

        lax.fori_loop(0, DSA_HEADS, per_head, 0)

    def tiles(ref, cols):
        blk = ref[:, :, cols]
        if rows % BF16_SUBLANES:
            blk = blk.astype(F32)
        return [blk[:, t * rows:(t + 1) * rows, :].reshape(w, blk.shape[-1]).astype(ref.dtype)
                for t in range(ref.shape[1] // rows)]

    def store(ref, cols, parts):
        parts = [p.reshape(classes, rows, p.shape[-1]) for p in parts]
        ref[:, :, cols] = jnp.concatenate(parts, axis=1).astype(ref.dtype)

    first = jnp.where(step == 0, 1, 0)
    lane = lax.broadcasted_iota(jnp.int32, (w, LANES), 1)
    lse_all = [jnp.zeros((w, LANES), F32) for _ in range(n_tiles)]
    ones = jnp.ones((2 * w, DSA_DH), BF16)
    for h in range(DSA_HEADS):
        hs = slice(h * DSA_DH, (h + 1) * DSA_DH)
        q_t = tiles(q_ref, hs)
        k_t = tiles(kc_ref, hs)
        v_t = tiles(vc_ref, hs)
        k_prev = [tiles(kp_ref, hs)[-1]] + k_t[:-1]
        v_prev = [tiles(vp_ref, hs)[-1]] + v_t[:-1]
        outs = []
        for t in range(n_tiles):
            k = jnp.concatenate([k_prev[t], k_t[t]], axis=0)
            v = jnp.concatenate([v_prev[t], v_t[t]], axis=0)
            s = _bdot_nt(q_t[t], k) + bias_ref[first if t == 0 else 0, h]
            m = jnp.max(s, axis=-1, keepdims=True)
            p = jnp.exp2(s - m)
            acc = _bdot(p, jnp.concatenate([v, ones], axis=1))
            l = acc[:, DSA_DH:]
            outs.append(acc[:, :DSA_DH] / l)
            lse_all[t] = jnp.where(lane == h, (m + jnp.log2(l)) * LN_2, lse_all[t])
        store(o_ref, hs, outs)
    store(lse_ref, slice(None), lse_all)


def _dsa_group(q_all, kv, rel_bias, group, *, tiles_per_step):
    s = q_all.shape[0]
    dilation = DSA_GROUPS[group][1]
    w = DSA_SPAN
    gw = DSA_GROUP_W
    classes = DSA_RESIDUES // dilation
    class_len = s // DSA_RESIDUES
    rows = w // classes
    prev_tiles = _dsa_min_tiles(group)
    n_tiles = max(prev_tiles, min(tiles_per_step, class_len // rows))
    assert n_tiles % prev_tiles == 0 and class_len % (n_tiles * rows) == 0

    def view(a):
        return a.reshape(classes, dilation, class_len, a.shape[-1])

    def cur(width, col):
        return pl.BlockSpec((classes, None, n_tiles * rows, width), lambda r, n: (0, r, n, col))

    def prev(width, col):
        ratio = n_tiles // prev_tiles
        return pl.BlockSpec((classes, None, prev_tiles * rows, width),
                            lambda r, n: (0, r, jnp.maximum(n * ratio - 1, 0), col))

    o, lse = pl.pallas_call(
        functools.partial(_dsa_kernel, group=group),
        grid=(dilation, class_len // (n_tiles * rows)),
        in_specs=[pl.BlockSpec(memory_space=pltpu.SMEM),
                  pl.BlockSpec((w, 2 * w), lambda r, n: (0, 0)),
                  cur(gw, group),
                  prev(gw, group), cur(gw, group),
                  prev(gw, N_GROUPS + group), cur(gw, N_GROUPS + group)],
        out_specs=[cur(gw, 0), cur(LANES, 0)],
        out_shape=[jax.ShapeDtypeStruct((classes, dilation, class_len, gw), BF16),
                   jax.ShapeDtypeStruct((classes, dilation, class_len, LANES), F32)],
        scratch_shapes=[pltpu.VMEM((2, DSA_HEADS, w, 2 * w), F32)],
        compiler_params=_compiler_params(("arbitrary", "arbitrary")),
        name=f"dsa_attention_g{group}",
    )(rel_bias, _band_buckets(group), view(q_all), view(kv), view(kv), view(kv), view(kv))
    return o.reshape(s, gw), lse.reshape(s, LANES)


def _to_residue_major_kernel(x_ref, o_ref):
    rows = o_ref.shape[1]
    x = x_ref[...].reshape(rows, DSA_RESIDUES, x_ref.shape[-1])
    o_ref[...] = pltpu.einshape("lrd->rld", x)


def _from_residue_major_kernel(x_ref, o_ref):
    rows = x_ref.shape[1]
    o_ref[...] = pltpu.einshape("rld->lrd", x_ref[...]).reshape(rows * DSA_RESIDUES, x_ref.shape[-1])


def _residue_major(x, *, rows, inverse):
    s, d = x.shape
    class_len = s // DSA_RESIDUES
    natural = pl.BlockSpec((rows * DSA_RESIDUES, d), lambda i: (i, 0))
    major = pl.BlockSpec((DSA_RESIDUES, rows, d), lambda i: (0, i, 0))
    if inverse:
        body, specs, arg, shape = (_from_residue_major_kernel, (major, natural),
                                   x.reshape(DSA_RESIDUES, class_len, d), (s, d))
    else:
        body, specs, arg, shape = _to_residue_major_kernel, (natural, major), x, (DSA_RESIDUES, class_len, d)
    out = pl.pallas_call(
        body,
        grid=(class_len // rows,),
        in_specs=[specs[0]],
        out_specs=specs[1],
        out_shape=jax.ShapeDtypeStruct(shape, x.dtype),
        compiler_params=_compiler_params(("parallel",)),
        name="from_residue_major" if inverse else "to_residue_major",
    )(arg)
    return out.reshape(s, d)


def _dsa_out_kernel(o0_ref, o1_ref, o2_ref, l0_ref, l1_ref, l2_ref, w_ref, g_ref, x_ref, out_ref, *, row_chunk):
    o_refs = [o0_ref, o1_ref, o2_ref]
    l_refs = [l0_ref, l1_ref, l2_ref]

    def rows(c):
        return pl.ds(c * row_chunk, row_chunk)

    def merged_rows(c):
        lses = [l[rows(c), :] for l in l_refs]
        top = jnp.maximum(jnp.maximum(lses[0], lses[1]), lses[2])
        es = [jnp.exp(l - top) for l in lses]
        inv_den = 1.0 / (es[0] + es[1] + es[2])
        wts = [e * inv_den for e in es]
        heads = []
        for h in range(DSA_HEADS):
            hs = slice(h * DSA_DH, (h + 1) * DSA_DH)
            merged = wts[0][:, h:h + 1] * o_refs[0][rows(c), hs].astype(F32)
            for g in range(1, N_GROUPS):
                merged = merged + wts[g][:, h:h + 1] * o_refs[g][rows(c), hs].astype(F32)
            heads.append(merged.astype(BF16))
        return jnp.concatenate(heads, axis=1)

    def finish(c, m):
        out = x_ref[rows(c), :] + _rms(m, g_ref[...])
        out_ref[rows(c), :] = out
        return out

    _overlapped_row_chunks(x_ref.shape[0] // row_chunk, merged_rows, w_ref, finish)


def _dsa_out(os_, lses, w, gain, x, layer, *, tm, row_chunk):
    s, k = os_[0].shape
    d = w.shape[2]
    o_spec = pl.BlockSpec((tm, k), lambda i: (i, 0))
    l_spec = pl.BlockSpec((tm, LANES), lambda i: (i, 0))
    return pl.pallas_call(
        functools.partial(_dsa_out_kernel, row_chunk=row_chunk),
        grid=(s // tm,),
        in_specs=[o_spec, o_spec, o_spec, l_spec, l_spec, l_spec,
                  pl.BlockSpec((None, k, d), lambda i: (layer, 0, 0)),
                  pl.BlockSpec((1, d), lambda i: (0, 0)),
                  pl.BlockSpec((tm, d), lambda i: (i, 0))],
        out_specs=pl.BlockSpec((tm, d), lambda i: (i, 0)),
        out_shape=jax.ShapeDtypeStruct((s, d), F32),
        compiler_params=_compiler_params(("parallel",)),
        name="dsa_merge_out",
    )(*os_, *lses, w, gain.reshape(1, d), x)


def _tiles(s):
    tm_proj = min(1024, s)
    eighth = tm_proj // 8
    return dict(tm_proj=tm_proj, tn_proj=2048, tn_gdn=1024, tm_out=min(512, s), out_chunk=128, tm_ffn=min(1024, s), tf_ffn=512, ffn_chunk=min(256, s), dsa_tiles=4,
                perm_rows=32, gdn_proj_chunks=(2 * eighth, 2 * eighth, 2 * eighth, eighth, eighth))


def kernel(x, norm_gains, ffn_w_gate_up, ffn_w_down, gdn_w_in, gdn_conv_w, gdn_a_log, gdn_dt_bias,
           gdn_out_norm, gdn_w_out, kv_norm, kv_w, dsa_w_q, dsa_w_out, rel_bias):
    b, s, d = x.shape
    assert b == 1 and d == D_MODEL and s % (DSA_SPAN * DSA_GROUPS[-1][1]) == 0
    t = _tiles(s)
    depth = norm_gains.shape[0]
    n_a = gdn_w_in.shape[0]
    xs = x.reshape(s, d)
    kv = None
    ffn_w_gate_up, ffn_w_down = ffn_w_gate_up.astype(BF16), ffn_w_down.astype(BF16)
    gdn_w_in, gdn_w_out = gdn_w_in.astype(BF16), gdn_w_out.astype(BF16)
    dsa_w_q, dsa_w_out = dsa_w_q.astype(BF16), dsa_w_out.astype(BF16)
    for layer in range(depth):
        gains = norm_gains[layer]
        if layer < n_a:
            proj, gb = _gdn_in_proj(xs, gains[0], gdn_w_in, gdn_conv_w, layer, gdn_a_log[layer], gdn_dt_bias[layer],
                                    tm=t["tm_proj"], tn=t["tn_gdn"], row_chunks=t["gdn_proj_chunks"])
            o = _gdn(proj, gb, gdn_out_norm[layer])
            xs = _matmul_norm_res(o, gdn_w_out, gains[1], xs, layer, tm=t["tm_out"], row_chunk=t["out_chunk"])
        else:
            j = layer - n_a
            q_all = _norm_matmul(xs, gains[0], dsa_w_q, j, tm=t["tm_proj"], tn=t["tn_proj"], out_dtype=BF16,
                                 out_scale=DSA_DH ** -0.5 * LOG2_E)
            outs = [_dsa_group(q_all, kv, rel_bias, g, tiles_per_step=t["dsa_tiles"]) for g in range(N_GROUPS)]
            xs = _dsa_out([o for o, _ in outs], [l for _, l in outs], dsa_w_out, gains[1], xs, j,
                          tm=t["tm_out"], row_chunk=t["out_chunk"])
        xs = _ffn(xs, gains[2], ffn_w_gate_up, ffn_w_down, gains[3], layer, tm=t["tm_ffn"], tf=t["tf_ffn"],
                  row_chunk=t["ffn_chunk"])
        if layer == n_a - 1 and depth > n_a:
            xs = _residue_major(xs, rows=t["perm_rows"], inverse=False)
            kv = _norm_matmul(xs, kv_norm, kv_w.astype(BF16)[None], 0, tm=t["tm_proj"], tn=t["tn_proj"],
                              out_dtype=BF16)
    if depth > n_a:
        xs = _residue_major(xs, rows=t["perm_rows"], inverse=True)
    return xs.reshape(b, s, d)
```

```python
import functools
import math

import jax
import jax.numpy as jnp
from jax import lax
from jax.experimental import pallas as pl
from jax.experimental.pallas import tpu as pltpu

F32 = jnp.float32
BF16 = jnp.bfloat16

RMS_EPS = 1e-6
D_MODEL = 2048

GDN_HEADS = 16
GDN_DK = 128
GDN_DV = 128
GDN_CONV = 4
GDN_QK_W = GDN_HEADS * GDN_DK
GDN_V_W = GDN_HEADS * GDN_DV
GDN_CONV_W = 2 * GDN_QK_W + GDN_V_W
GDN_CHUNK = 128
GDN_INV_BASE = 8
CONV_HALO_ROWS = 8

DSA_GROUPS = ((128, 1), (512, 4), (2048, 16))
N_GROUPS = len(DSA_GROUPS)
DSA_HEADS = 16
DSA_DH = 128
DSA_SPAN = 128
DSA_GROUP_W = DSA_HEADS * DSA_DH
NUM_BUCKETS = 32
MAX_DISTANCE = 2048
MASK_VALUE = -1e30
LOG2_E = math.log2(math.e)
LN_2 = math.log(2.0)

DSA_RESIDUES = DSA_GROUPS[-1][1]

LANES = 128
BF16_SUBLANES = 16
VMEM_BYTES = 64 * 1024 * 1024
VMEM_LIMIT_BYTES = VMEM_BYTES * 7 // 8
FFN_VMEM_LIMIT_BYTES = VMEM_BYTES * 31 // 32


def _compiler_params(semantics, vmem_limit_bytes=VMEM_LIMIT_BYTES):
    return pltpu.CompilerParams(dimension_semantics=semantics, vmem_limit_bytes=vmem_limit_bytes)


def _rms(x, gain):
    ms = jnp.mean(x * x, axis=-1, keepdims=True)
    return x * lax.rsqrt(ms + RMS_EPS) * gain


def _sigmoid(x):
    return 1.0 / (1.0 + jnp.exp(-x))


def _silu(x):
    h = 0.5 * x
    return h + h * jnp.tanh(h)


def _bdot(a, b):
    return jnp.dot(a.astype(BF16), b.astype(BF16), preferred_element_type=F32)


def _bdot_nt(a, b):
    return lax.dot_general(a.astype(BF16), b.astype(BF16), (((1,), (1,)), ((), ())),
                           preferred_element_type=F32)


def _bdot_tn(a, b):
    return lax.dot_general(a.astype(BF16), b.astype(BF16), (((0,), (0,)), ((), ())),
                           preferred_element_type=F32)


def _norm_matmul_kernel(x_ref, g_ref, w_ref, o_ref, xn_ref, *, out_scale):
    @pl.when(pl.program_id(1) == 0)
    def _():
        xn_ref[...] = _rms(x_ref[...], g_ref[...]).astype(BF16)

    y = jnp.dot(xn_ref[...], w_ref[...], preferred_element_type=F32)
    if out_scale != 1.0:
        y = y * out_scale
    o_ref[...] = y.astype(o_ref.dtype)


def _norm_matmul(x, gain, w, layer, *, tm, tn, out_dtype, out_scale=1.0):
    s, d = x.shape
    n = w.shape[2]
    return pl.pallas_call(
        functools.partial(_norm_matmul_kernel, out_scale=out_scale),
        grid=(s // tm, n // tn),
        in_specs=[pl.BlockSpec((tm, d), lambda i, j: (i, 0)),
                  pl.BlockSpec((1, d), lambda i, j: (0, 0)),
                  pl.BlockSpec((None, d, tn), lambda i, j: (layer, 0, j))],
        out_specs=pl.BlockSpec((tm, tn), lambda i, j: (i, j)),
        out_shape=jax.ShapeDtypeStruct((s, n), out_dtype),
        scratch_shapes=[pltpu.VMEM((tm, d), BF16)],
        compiler_params=_compiler_params(("parallel", "arbitrary")),
        name="norm_matmul",
    )(x, gain.reshape(1, d), w)


def _ordered_after(x, anchor):
    rows, lanes = BF16_SUBLANES, LANES
    a = lax.bitcast_convert_type(jnp.abs(anchor[:rows, :lanes].astype(F32)), jnp.int32)
    zero = lax.shift_right_logical(a, 31)
    tile = lax.bitcast_convert_type(x[:rows, :lanes].astype(F32), jnp.int32) | zero
    tile = lax.bitcast_convert_type(tile, F32).astype(x.dtype)
    top = jnp.concatenate([tile, x[:rows, lanes:]], axis=1)
    return jnp.concatenate([top, x[rows:, :]], axis=0)


def _gdn_in_proj_kernel(x_ref, g_ref, w_ref, cw_ref, wba_ref, alog_ref, dtb_ref, o_ref, gb_ref,
                        xn_ref, halo_ref, raw_ref, *, qk_blocks, v_blocks, row_chunks):
    i = pl.program_id(0)
    j = pl.program_id(1)
    tm = x_ref.shape[0]
    tn = w_ref.shape[1]

    @pl.when(j == 0)
    def _():
        xn = _rms(x_ref[...], g_ref[...]).astype(BF16)
        xn_ref[...] = xn
        gb_ref[...] = _gdn_gate_values(jnp.dot(xn, wba_ref[...], preferred_element_type=F32),
                                       alog_ref[...], dtb_ref[...])

    @pl.when((i == 0) & (j == 0))
    def _():
        halo_ref[...] = jnp.zeros_like(halo_ref)

    chunks = []
    for rows in row_chunks:
        chunks.append((sum(r for _, r in chunks), rows))
    assert sum(row_chunks) == tm
    n_chunks = len(chunks)

    def conv_silu(r0, rows):
        raw = raw_ref[pl.ds(r0, rows), :]
        if r0 == 0:
            halo = jnp.where(i > 0, halo_ref[j], 0.0)
        else:
            halo = raw_ref[pl.ds(r0 - CONV_HALO_ROWS, CONV_HALO_ROWS), :]
        cw = cw_ref[...]
        groups = jnp.concatenate([halo, raw], axis=0).reshape(rows // CONV_HALO_ROWS + 1, CONV_HALO_ROWS, tn)
        in_group = lax.broadcasted_iota(jnp.int32, (rows // CONV_HALO_ROWS, CONV_HALO_ROWS, tn), 1)
        y = cw[GDN_CONV - 1:GDN_CONV, :] * raw
        for t in range(GDN_CONV - 1):
            back = GDN_CONV - 1 - t
            rot = pltpu.roll(groups, back, axis=1)
            shifted = jnp.where(in_group < back, rot[:-1], rot[1:]).reshape(rows, tn)
            y = y + cw[t:t + 1, :] * shifted
        return _silu(y)

    def finish_qk(r0, rows):
        y = conv_silu(r0, rows)
        c = jnp.where(j < qk_blocks, float(GDN_DK), 1.0).astype(F32)
        out = []
        for h in range(tn // GDN_DK):
            yh = y[:, h * GDN_DK:(h + 1) * GDN_DK]
            ss = jnp.sum(yh * yh, axis=-1, keepdims=True)
            out.append((yh * lax.rsqrt(ss * c + RMS_EPS * c)).astype(o_ref.dtype))
        return jnp.concatenate(out, axis=1)

    def finish_v(r0, rows):
        return conv_silu(r0, rows).astype(o_ref.dtype)

    def finish_z(r0, rows):
        return _silu(raw_ref[pl.ds(r0, rows), :]).astype(o_ref.dtype)

    def run(finish):
        done = []
        for c in range(n_chunks + 1):
            if c < n_chunks:
                rows = pl.ds(*chunks[c])
                lhs = xn_ref[rows, :]
                if c >= 2:
                    lhs = _ordered_after(lhs, done[c - 2])
                raw_ref[rows, :] = jnp.dot(lhs, w_ref[...], preferred_element_type=F32)
            if c >= 1:
                out = finish(*chunks[c - 1])
                o_ref[pl.ds(*chunks[c - 1]), :] = out
                done.append(out)

    @pl.when(j < 2 * qk_blocks)
    def _():
        run(finish_qk)
        halo_ref[j] = raw_ref[pl.ds(tm - CONV_HALO_ROWS, CONV_HALO_ROWS), :]

    @pl.when((j >= 2 * qk_blocks) & (j < 2 * qk_blocks + v_blocks))
    def _():
        run(finish_v)
        halo_ref[j] = raw_ref[pl.ds(tm - CONV_HALO_ROWS, CONV_HALO_ROWS), :]

    @pl.when(j >= 2 * qk_blocks + v_blocks)
    def _():
        run(finish_z)


def _gdn_in_proj(x, gain, w_in, conv_w, layer, a_log, dt_bias, *, tm, tn, row_chunks):
    s, d = x.shape
    n = GDN_CONV_W + GDN_V_W
    assert n % LANES == 0 and w_in.shape[2] - n == 2 * GDN_HEADS
    conv_blocks = GDN_CONV_W // tn
    alog, dtb = _gdn_gate_params(a_log, dt_bias)
    lane_row = pl.BlockSpec((1, LANES), lambda i, j: (0, 0))
    return pl.pallas_call(
        functools.partial(_gdn_in_proj_kernel, qk_blocks=GDN_QK_W // tn, v_blocks=GDN_V_W // tn,
                          row_chunks=row_chunks),
        grid=(s // tm, n // tn),
        in_specs=[pl.BlockSpec((tm, d), lambda i, j: (i, 0)),
                  pl.BlockSpec((1, d), lambda i, j: (0, 0)),
                  pl.BlockSpec((None, d, tn), lambda i, j: (layer, 0, j)),
                  pl.BlockSpec((None, GDN_CONV, tn), lambda i, j: (layer, 0, jnp.minimum(j, conv_blocks - 1))),
                  pl.BlockSpec((None, d, LANES), lambda i, j: (layer, 0, n // LANES)), lane_row, lane_row],
        out_specs=[pl.BlockSpec((tm, tn), lambda i, j: (i, j)),
                   pl.BlockSpec((tm, LANES), lambda i, j: (i, 0))],
        out_shape=[jax.ShapeDtypeStruct((s, n), BF16), jax.ShapeDtypeStruct((s, LANES), F32)],
        scratch_shapes=[pltpu.VMEM((tm, d), BF16),
                        pltpu.VMEM((conv_blocks, CONV_HALO_ROWS, tn), F32),
                        pltpu.VMEM((tm, tn), F32)],
        compiler_params=_compiler_params(("arbitrary", "arbitrary")),
        name="gdn_in_proj",
    )(x, gain.reshape(1, d), w_in, conv_w, w_in, alog, dtb)


def _gdn_gate_values(p, alog, dtb):
    lane = lax.broadcasted_iota(jnp.int32, p.shape, 1)
    beta = _sigmoid(p)
    a = p + dtb
    softplus = jnp.maximum(a, 0.0) + jnp.log(1.0 + jnp.exp(-jnp.abs(a)))
    g = -jnp.exp(alog) * softplus
    return jnp.where(lane < GDN_HEADS, beta, jnp.where(lane < 2 * GDN_HEADS, g, 0.0))


def _gdn_gate_params(a_log, dt_bias):
    pad = LANES - 2 * GDN_HEADS
    zeros_h = jnp.zeros((GDN_HEADS,), F32)
    alog = jnp.pad(jnp.concatenate([zeros_h, a_log.astype(F32)]), (0, pad)).reshape(1, LANES)
    dtb = jnp.pad(jnp.concatenate([zeros_h, dt_bias.astype(F32)]), (0, pad)).reshape(1, LANES)
    return alog, dtb


def _unit_lower_inverse(lows, row, col):
    c = lows[0].shape[0]
    eye = (row == col).astype(F32)
    blk = GDN_INV_BASE
    diag_mask = (row // blk) == (col // blk)
    powers = [jnp.where(diag_mask, low, 0.0) for low in lows]
    invs = [eye - p for p in powers]
    span = 2
    while span < blk:
        powers = [_bdot(p, p) for p in powers]
        invs = [x + _bdot(x, p) for x, p in zip(invs, powers)]
        span *= 2
    while blk < c:
        off_mask = ((row // (2 * blk)) == (col // (2 * blk))) & ((row // blk) != (col // blk))
        offs = [_bdot(jnp.where(off_mask, low, 0.0), x) for low, x in zip(lows, invs)]
        invs = [x - _bdot(x, t) for x, t in zip(invs, offs)]
        blk *= 2
    return invs


def _gdn_kernel(q_ref, k_ref, v_ref, z_ref, gb_ref, on_ref, o_ref, state_ref):
    c_len = q_ref.shape[0]
    step = pl.program_id(1)

    @pl.when(step == 0)
    def _():
        state_ref[...] = jnp.zeros_like(state_ref)

    row = lax.broadcasted_iota(jnp.int32, (c_len, c_len), 0)
    col = lax.broadcasted_iota(jnp.int32, (c_len, c_len), 1)
    causal = row >= col
    strict = row > col

    gb = gb_ref[...]
    tri = causal.astype(BF16)
    gb_hi = gb.astype(BF16)
    gb_lo = (gb - gb_hi.astype(F32)).astype(BF16)
    gcum = (jnp.dot(tri, gb_hi, preferred_element_type=F32)
            + jnp.dot(tri, gb_lo, preferred_element_type=F32))
    gcum_t = gcum.T

    heads = range(GDN_HEADS)

    def head_slices(ref):
        return [ref[:, i * GDN_DK:(i + 1) * GDN_DK].astype(F32) for i in heads]

    qn, kn, v_all = head_slices(q_ref), head_slices(k_ref), head_slices(v_ref)
    beta = [gb[:, i:i + 1] for i in heads]
    gcol = [gcum[:, GDN_HEADS + i:GDN_HEADS + i + 1] for i in heads]
    grow = [gcum_t[GDN_HEADS + i:GDN_HEADS + i + 1, :] for i in heads]
    decay = [jnp.exp(jnp.where(causal, gc - gr, MASK_VALUE)) for gc, gr in zip(gcol, grow)]
    e_col = [jnp.exp(gc) for gc in gcol]
    g_last = [gc[c_len - 1:c_len, :] for gc in gcol]
    kb = [k * b for k, b in zip(kn, beta)]
    kk_qk = [_bdot_nt(jnp.concatenate([kb[i], qn[i]], axis=0), kn[i]) for i in heads]
    low = [jnp.where(strict, kk_qk[i][:c_len] * decay[i], 0.0) for i in heads]
    intra = [kk_qk[i][c_len:] * decay[i] for i in heads]
    inv = _unit_lower_inverse(low, row, col)
    sol = [_bdot(inv[i], jnp.concatenate([v_all[i] * beta[i], kb[i] * e_col[i]], axis=1)) for i in heads]
    state = [state_ref[i] for i in heads]
    ws_qs = [_bdot(jnp.concatenate([sol[i][:, GDN_DV:], qn[i] * e_col[i]], axis=0), state[i]) for i in heads]
    v_new = [sol[i][:, :GDN_DV] - ws_qs[i][:c_len] for i in heads]
    o = [ws_qs[i][c_len:] + _bdot(intra[i], v_new[i]) for i in heads]
    for i in heads:
        k_dec = kn[i] * jnp.exp(g_last[i] - gcol[i])
        state_ref[i] = state[i] * jnp.exp(g_last[i]) + _bdot_tn(k_dec, v_new[i])
    out_gain = on_ref[...]
    for i in heads:
        hs = slice(i * GDN_DV, (i + 1) * GDN_DV)
        o_ref[:, hs] = (_rms(o[i], out_gain) * z_ref[:, hs].astype(F32)).astype(o_ref.dtype)


def _gdn(proj, gb, out_norm):
    s = proj.shape[0]
    c = GDN_CHUNK
    hb = GDN_HEADS
    ng = GDN_HEADS // hb
    bw = hb * GDN_DK

    def cur(sec):
        return pl.BlockSpec((c, bw), lambda g, t, sec=sec: (t, sec * ng + g))

    return pl.pallas_call(
        _gdn_kernel,
        grid=(ng, s // c),
        in_specs=[cur(0), cur(1), cur(2), cur(3),
                  pl.BlockSpec((c, LANES), lambda g, t: (t, g)),
                  pl.BlockSpec((1, GDN_DV), lambda g, t: (0, 0))],
        out_specs=pl.BlockSpec((c, bw), lambda g, t: (t, g)),
        out_shape=jax.ShapeDtypeStruct((s, GDN_V_W), BF16),
        scratch_shapes=[pltpu.VMEM((hb, GDN_DK, GDN_DV), F32)],
        compiler_params=_compiler_params(("parallel", "arbitrary")),
        name="gdn_delta_rule",
    )(proj, proj, proj, proj, gb, out_norm.reshape(1, GDN_DV))


def _overlapped_row_chunks(n_chunks, lhs_rows, w_ref, finish_rows):
    done = []
    prev = None
    for c in range(n_chunks + 1):
        cur = None
        if c < n_chunks:
            lhs = lhs_rows(c)
            if c >= 2:
                lhs = _ordered_after(lhs, done[c - 2])
            cur = jnp.dot(lhs, w_ref[...], preferred_element_type=F32)
        if c >= 1:
            done.append(finish_rows(c - 1, prev))
        prev = cur


def _matmul_norm_res_kernel(a_ref, w_ref, g_ref, x_ref, o_ref, *, row_chunk):
    def rows(c):
        return pl.ds(c * row_chunk, row_chunk)

    def finish(c, m):
        out = x_ref[rows(c), :] + _rms(m, g_ref[...])
        o_ref[rows(c), :] = out
        return out

    _overlapped_row_chunks(a_ref.shape[0] // row_chunk, lambda c: a_ref[rows(c), :], w_ref, finish)


def _matmul_norm_res(a, w, gain, x, layer, *, tm, row_chunk):
    s, k = a.shape
    d = w.shape[2]
    return pl.pallas_call(
        functools.partial(_matmul_norm_res_kernel, row_chunk=row_chunk),
        grid=(s // tm,),
        in_specs=[pl.BlockSpec((tm, k), lambda i: (i, 0)),
                  pl.BlockSpec((None, k, d), lambda i: (layer, 0, 0)),
                  pl.BlockSpec((1, d), lambda i: (0, 0)),
                  pl.BlockSpec((tm, d), lambda i: (i, 0))],
        out_specs=pl.BlockSpec((tm, d), lambda i: (i, 0)),
        out_shape=jax.ShapeDtypeStruct((s, d), F32),
        compiler_params=_compiler_params(("parallel",)),
        name="matmul_norm_res",
    )(a, w, gain.reshape(1, d), x)


def _ffn_kernel(x_ref, gpre_ref, wg_ref, wu_ref, wd_ref, gpost_ref, o_ref, xn_ref, *, row_chunk):
    j = pl.program_id(1)
    last = pl.num_programs(1) - 1

    @pl.when(j == 0)
    def _():
        xn_ref[...] = _rms(x_ref[...], gpre_ref[...]).astype(BF16)
        o_ref[...] = jnp.zeros_like(o_ref)

    def activations():
        xn = xn_ref[...]
        gate = jnp.dot(xn, wg_ref[...], preferred_element_type=F32)
        up = jnp.dot(xn, wu_ref[...], preferred_element_type=F32)
        return (_silu(gate) * up).astype(BF16)

    @pl.when(j < last)
    def _():
        o_ref[...] += jnp.dot(activations(), wd_ref[...], preferred_element_type=F32)

    @pl.when(j == last)
    def _():
        act = activations()

        def rows(c):
            return pl.ds(c * row_chunk, row_chunk)

        def finish(c, m):
            out = x_ref[rows(c), :] + _rms(o_ref[rows(c), :] + m, gpost_ref[...])
            o_ref[rows(c), :] = out
            return out

        _overlapped_row_chunks(o_ref.shape[0] // row_chunk,
                               lambda c: act[c * row_chunk:(c + 1) * row_chunk, :], wd_ref, finish)


def _ffn(x, g_pre, w_gate_up, w_down, g_post, layer, *, tm, tf, row_chunk):
    s, d = x.shape
    d_ff = w_down.shape[1]
    nf = d_ff // tf
    return pl.pallas_call(
        functools.partial(_ffn_kernel, row_chunk=row_chunk),
        grid=(s // tm, nf),
        in_specs=[pl.BlockSpec((tm, d), lambda i, j: (i, 0)),
                  pl.BlockSpec((1, d), lambda i, j: (0, 0)),
                  pl.BlockSpec((None, d, tf), lambda i, j: (layer, 0, j)),
                  pl.BlockSpec((None, d, tf), lambda i, j: (layer, 0, nf + j)),
                  pl.BlockSpec((None, tf, d), lambda i, j: (layer, j, 0)),
                  pl.BlockSpec((1, d), lambda i, j: (0, 0))],
        out_specs=pl.BlockSpec((tm, d), lambda i, j: (i, 0)),
        out_shape=jax.ShapeDtypeStruct((s, d), F32),
        scratch_shapes=[pltpu.VMEM((tm, d), BF16)],
        compiler_params=_compiler_params(("parallel", "arbitrary"), FFN_VMEM_LIMIT_BYTES),
        name="swiglu_ffn",
    )(x, g_pre.reshape(1, d), w_gate_up, w_gate_up, w_down, g_post.reshape(1, d))


def _t5_bucket(dist):
    max_exact = NUM_BUCKETS // 2
    d_f = jnp.maximum(dist, 1).astype(F32)
    large = max_exact + (jnp.log(d_f / max_exact) / math.log(MAX_DISTANCE / max_exact)
                         * (NUM_BUCKETS - max_exact)).astype(jnp.int32)
    large = jnp.minimum(large, NUM_BUCKETS - 1)
    return jnp.where(dist < max_exact, dist, large)


def _tile_offsets(group):
    w = DSA_SPAN
    classes = DSA_RESIDUES // DSA_GROUPS[group][1]
    rows = w // classes
    i = jnp.arange(w)
    return (i % rows) * classes + i // rows


def _band_buckets(group):
    w = DSA_SPAN
    off = _tile_offsets(group)
    rel = off[:, None] + w - jnp.concatenate([off, off + w])[None, :]
    band = (rel >= 0) & (rel <= w)
    return jnp.where(band, _t5_bucket(jnp.clip(rel, 0, w) * DSA_GROUPS[group][1]), -1).astype(jnp.int32)


def _dsa_min_tiles(group):
    classes = DSA_RESIDUES // DSA_GROUPS[group][1]
    return max(1, BF16_SUBLANES * classes // DSA_SPAN)


def _dsa_kernel(tab_ref, bucket_ref, q_ref, kp_ref, kc_ref, vp_ref, vc_ref, o_ref, lse_ref, bias_ref, *, group):
    w = DSA_SPAN
    classes = q_ref.shape[0]
    rows = w // classes
    n_tiles = q_ref.shape[1] // rows
    step = pl.program_id(1)

    @pl.when((pl.program_id(0) == 0) & (step == 0))
    def _():
        bucket = bucket_ref[...]
        key_col = lax.broadcasted_iota(jnp.int32, (w, 2 * w), 1)

        def per_head(h, carry):
            b = jnp.full(bucket.shape, MASK_VALUE, F32)
            for t in range(NUM_BUCKETS):
                b = jnp.where(bucket == t, tab_ref[t, group * DSA_HEADS + h] * LOG2_E, b)
            bias_ref[0, h] = b
            bias_ref[1, h] = jnp.where(key_col < w, MASK_VALUE, b)
            return carry

        lax.fori_loop(0, DSA_HEADS, per_head, 0)

    def tiles(ref, cols):
        blk = ref[:, :, cols]
        if rows % BF16_SUBLANES:
            blk = blk.astype(F32)
        return [blk[:, t * rows:(t + 1) * rows, :].reshape(w, blk.shape[-1]).astype(ref.dtype)
                for t in range(ref.shape[1] // rows)]

    def store(ref, cols, parts):
        parts = [p.reshape(classes, rows, p.shape[-1]) for p in parts]
        ref[:, :, cols] = jnp.concatenate(parts, axis=1).astype(ref.dtype)

    first = jnp.where(step == 0, 1, 0)
    lane = lax.broadcasted_iota(jnp.int32, (w, LANES), 1)
    lse_all = [jnp.zeros((w, LANES), F32) for _ in range(n_tiles)]
    ones = jnp.ones((2 * w, DSA_DH), BF16)
    for h in range(DSA_HEADS):
        hs = slice(h * DSA_DH, (h + 1) * DSA_DH)
        q_t = tiles(q_ref, hs)
        k_t = tiles(kc_ref, hs)
        v_t = tiles(vc_ref, hs)
        k_prev = [tiles(kp_ref, hs)[-1]] + k_t[:-1]
        v_prev = [tiles(vp_ref, hs)[-1]] + v_t[:-1]
        outs = []
        for t in range(n_tiles):
            k = jnp.concatenate([k_prev[t], k_t[t]], axis=0)
            v = jnp.concatenate([v_prev[t], v_t[t]], axis=0)
            s = _bdot_nt(q_t[t], k) + bias_ref[first if t == 0 else 0, h]
            m = jnp.max(s, axis=-1, keepdims=True)
            p = jnp.exp2(s - m)
            acc = _bdot(p, jnp.concatenate([v, ones], axis=1))
            l = acc[:, DSA_DH:]
            outs.append(acc[:, :DSA_DH] / l)
            lse_all[t] = jnp.where(lane == h, (m + jnp.log2(l)) * LN_2, lse_all[t])
        store(o_ref, hs, outs)
    store(lse_ref, slice(None), lse_all)


def _dsa_group(q_all, kv, rel_bias, group, *, tiles_per_step):
    s = q_all.shape[0]
    dilation = DSA_GROUPS[group][1]
    w = DSA_SPAN
    gw = DSA_GROUP_W
    classes = DSA_RESIDUES // dilation
    class_len = s // DSA_RESIDUES
    rows = w // classes
    prev_tiles = _dsa_min_tiles(group)
    n_tiles = max(prev_tiles, min(tiles_per_step, class_len // rows))
    assert n_tiles % prev_tiles == 0 and class_len % (n_tiles * rows) == 0

    def view(a):
        return a.reshape(classes, dilation, class_len, a.shape[-1])

    def cur(width, col):
        return pl.BlockSpec((classes, None, n_tiles * rows, width), lambda r, n: (0, r, n, col))

    def prev(width, col):
        ratio = n_tiles // prev_tiles
        return pl.BlockSpec((classes, None, prev_tiles * rows, width),
                            lambda r, n: (0, r, jnp.maximum(n * ratio - 1, 0), col))

    o, lse = pl.pallas_call(
        functools.partial(_dsa_kernel, group=group),
        grid=(dilation, class_len // (n_tiles * rows)),
        in_specs=[pl.BlockSpec(memory_space=pltpu.SMEM),
                  pl.BlockSpec((w, 2 * w), lambda r, n: (0, 0)),
                  cur(gw, group),
                  prev(gw, group), cur(gw, group),
                  prev(gw, N_GROUPS + group), cur(gw, N_GROUPS + group)],
        out_specs=[cur(gw, 0), cur(LANES, 0)],
        out_shape=[jax.ShapeDtypeStruct((classes, dilation, class_len, gw), BF16),
                   jax.ShapeDtypeStruct((classes, dilation, class_len, LANES), F32)],
        scratch_shapes=[pltpu.VMEM((2, DSA_HEADS, w, 2 * w), F32)],
        compiler_params=_compiler_params(("arbitrary", "arbitrary")),
        name=f"dsa_attention_g{group}",
    )(rel_bias, _band_buckets(group), view(q_all), view(kv), view(kv), view(kv), view(kv))
    return o.reshape(s, gw), lse.reshape(s, LANES)


def _to_residue_major_kernel(x_ref, o_ref):
    rows = o_ref.shape[1]
    x = x_ref[...].reshape(rows, DSA_RESIDUES, x_ref.shape[-1])
    o_ref[...] = pltpu.einshape("lrd->rld", x)


def _from_residue_major_kernel(x_ref, o_ref):
    rows = x_ref.shape[1]
    o_ref[...] = pltpu.einshape("rld->lrd", x_ref[...]).reshape(rows * DSA_RESIDUES, x_ref.shape[-1])


def _residue_major(x, *, rows, inverse):
    s, d = x.shape
    class_len = s // DSA_RESIDUES
    natural = pl.BlockSpec((rows * DSA_RESIDUES, d), lambda i: (i, 0))
    major = pl.BlockSpec((DSA_RESIDUES, rows, d), lambda i: (0, i, 0))
    if inverse:
        body, specs, arg, shape = (_from_residue_major_kernel, (major, natural),
                                   x.reshape(DSA_RESIDUES, class_len, d), (s, d))
    else:
        body, specs, arg, shape = _to_residue_major_kernel, (natural, major), x, (DSA_RESIDUES, class_len, d)
    out = pl.pallas_call(
        body,
        grid=(class_len // rows,),
        in_specs=[specs[0]],
        out_specs=specs[1],
        out_shape=jax.ShapeDtypeStruct(shape, x.dtype),
        compiler_params=_compiler_params(("parallel",)),
        name="from_residue_major" if inverse else "to_residue_major",
    )(arg)
    return out.reshape(s, d)


def _dsa_out_kernel(o0_ref, o1_ref, o2_ref, l0_ref, l1_ref, l2_ref, w_ref, g_ref, x_ref, out_ref, *, row_chunk):
    o_refs = [o0_ref, o1_ref, o2_ref]
    l_refs = [l0_ref, l1_ref, l2_ref]

    def rows(c):
        return pl.ds(c * row_chunk, row_chunk)

    def merged_rows(c):
        lses = [l[rows(c), :] for l in l_refs]
        top = jnp.maximum(jnp.maximum(lses[0], lses[1]), lses[2])
        es = [jnp.exp(l - top) for l in lses]
        inv_den = 1.0 / (es[0] + es[1] + es[2])
        wts = [e * inv_den for e in es]
        heads = []
        for h in range(DSA_HEADS):
            hs = slice(h * DSA_DH, (h + 1) * DSA_DH)
            merged = wts[0][:, h:h + 1] * o_refs[0][rows(c), hs].astype(F32)
            for g in range(1, N_GROUPS):
                merged = merged + wts[g][:, h:h + 1] * o_refs[g][rows(c), hs].astype(F32)
            heads.append(merged.astype(BF16))
        return jnp.concatenate(heads, axis=1)

    def finish(c, m):
        out = x_ref[rows(c), :] + _rms(m, g_ref[...])
        out_ref[rows(c), :] = out
        return out

    _overlapped_row_chunks(x_ref.shape[0] // row_chunk, merged_rows, w_ref, finish)


def _dsa_out(os_, lses, w, gain, x, layer, *, tm, row_chunk):
    s, k = os_[0].shape
    d = w.shape[2]
    o_spec = pl.BlockSpec((tm, k), lambda i: (i, 0))
    l_spec = pl.BlockSpec((tm, LANES), lambda i: (i, 0))
    return pl.pallas_call(
        functools.partial(_dsa_out_kernel, row_chunk=row_chunk),
        grid=(s // tm,),
        in_specs=[o_spec, o_spec, o_spec, l_spec, l_spec, l_spec,
                  pl.BlockSpec((None, k, d), lambda i: (layer, 0, 0)),
                  pl.BlockSpec((1, d), lambda i: (0, 0)),
                  pl.BlockSpec((tm, d), lambda i: (i, 0))],
        out_specs=pl.BlockSpec((tm, d), lambda i: (i, 0)),
        out_shape=jax.ShapeDtypeStruct((s, d), F32),
        compiler_params=_compiler_params(("parallel",)),
        name="dsa_merge_out",
    )(*os_, *lses, w, gain.reshape(1, d), x)


def _tiles(s):
    tm_proj = min(1024, s)
    eighth = tm_proj // 8
    return dict(tm_proj=tm_proj, tn_proj=2048, tn_gdn=1024, tm_out=min(512, s), out_chunk=128, tm_ffn=min(1024, s), tf_ffn=512, ffn_chunk=min(256, s), dsa_tiles=4,
                perm_rows=32, gdn_proj_chunks=(2 * eighth, 2 * eighth, 2 * eighth, eighth, eighth))


def kernel(x, norm_gains, ffn_w_gate_up, ffn_w_down, gdn_w_in, gdn_conv_w, gdn_a_log, gdn_dt_bias,
           gdn_out_norm, gdn_w_out, kv_norm, kv_w, dsa_w_q, dsa_w_out, rel_bias):
    b, s, d = x.shape
    assert b == 1 and d == D_MODEL and s % (DSA_SPAN * DSA_GROUPS[-1][1]) == 0
    t = _tiles(s)
    depth = norm_gains.shape[0]
    n_a = gdn_w_in.shape[0]
    xs = x.reshape(s, d)
    kv = None
    ffn_w_gate_up, ffn_w_down = ffn_w_gate_up.astype(BF16), ffn_w_down.astype(BF16)
    gdn_w_in, gdn_w_out = gdn_w_in.astype(BF16), gdn_w_out.astype(BF16)
    dsa_w_q, dsa_w_out = dsa_w_q.astype(BF16), dsa_w_out.astype(BF16)
    for layer in range(depth):
        gains = norm_gains[layer]
        if layer < n_a:
            proj, gb = _gdn_in_proj(xs, gains[0], gdn_w_in, gdn_conv_w, layer, gdn_a_log[layer], gdn_dt_bias[layer],
                                    tm=t["tm_proj"], tn=t["tn_gdn"], row_chunks=t["gdn_proj_chunks"])
            o = _gdn(proj, gb, gdn_out_norm[layer])
            xs = _matmul_norm_res(o, gdn_w_out, gains[1], xs, layer, tm=t["tm_out"], row_chunk=t["out_chunk"])
        else:
            j = layer - n_a
            q_all = _norm_matmul(xs, gains[0], dsa_w_q, j, tm=t["tm_proj"], tn=t["tn_proj"], out_dtype=BF16,
                                 out_scale=DSA_DH ** -0.5 * LOG2_E)
            outs = [_dsa_group(q_all, kv, rel_bias, g, tiles_per_step=t["dsa_tiles"]) for g in range(N_GROUPS)]
            xs = _dsa_out([o for o, _ in outs], [l for _, l in outs], dsa_w_out, gains[1], xs, j,
                          tm=t["tm_out"], row_chunk=t["out_chunk"])
        xs = _ffn(xs, gains[2], ffn_w_gate_up, ffn_w_down, gains[3], layer, tm=t["tm_ffn"], tf=t["tf_ffn"],
                  row_chunk=t["ffn_chunk"])
        if layer == n_a - 1 and depth > n_a:
            xs = _residue_major(xs, rows=t["perm_rows"], inverse=False)
            kv = _norm_matmul(xs, kv_norm, kv_w.astype(BF16)[None], 0, tm=t["tm_proj"], tn=t["tn_proj"],
                              out_dtype=BF16)
    if depth > n_a:
        xs = _residue_major(xs, rows=t["perm_rows"], inverse=True)
    return xs.reshape(b, s, d)
```

```python
import functools
import math

import jax
import jax.numpy as jnp
from jax import lax
from jax.experimental import pallas as pl
from jax.experimental.pallas import tpu as pltpu

F32 = jnp.float32
BF16 = jnp.bfloat16

RMS_EPS = 1e-6
D_MODEL = 2048

GDN_HEADS = 16
GDN_DK = 128
GDN_DV = 128
GDN_CONV = 4
GDN_QK_W = GDN_HEADS * GDN_DK
GDN_V_W = GDN_HEADS * GDN_DV
GDN_CONV_W = 2 * GDN_QK_W + GDN_V_W
GDN_CHUNK = 128
GDN_INV_BASE = 8
CONV_HALO_ROWS = 8

DSA_GROUPS = ((128, 1), (512, 4), (2048, 16))
N_GROUPS = len(DSA_GROUPS)
DSA_HEADS = 16
DSA_DH = 128
DSA_SPAN = 128
DSA_GROUP_W = DSA_HEADS * DSA_DH
NUM_BUCKETS = 32
MAX_DISTANCE = 2048
MASK_VALUE = -1e30
LOG2_E = math.log2(math.e)
LN_2 = math.log(2.0)

DSA_RESIDUES = DSA_GROUPS[-1][1]

LANES = 128
BF16_SUBLANES = 16
VMEM_BYTES = 64 * 1024 * 1024
VMEM_LIMIT_BYTES = VMEM_BYTES * 7 // 8
FFN_VMEM_LIMIT_BYTES = VMEM_BYTES * 31 // 32


def _compiler_params(semantics, vmem_limit_bytes=VMEM_LIMIT_BYTES):
    return pltpu.CompilerParams(dimension_semantics=semantics, vmem_limit_bytes=vmem_limit_bytes)


def _rms(x, gain):
    ms = jnp.mean(x * x, axis=-1, keepdims=True)
    return x * lax.rsqrt(ms + RMS_EPS) * gain


def _sigmoid(x):
    return 1.0 / (1.0 + jnp.exp(-x))


def _silu(x):
    h = 0.5 * x
    return h + h * jnp.tanh(h)


def _bdot(a, b):
    return jnp.dot(a.astype(BF16), b.astype(BF16), preferred_element_type=F32)


def _bdot_nt(a, b):
    return lax.dot_general(a.astype(BF16), b.astype(BF16), (((1,), (1,)), ((), ())),
                           preferred_element_type=F32)


def _bdot_tn(a, b):
    return lax.dot_general(a.astype(BF16), b.astype(BF16), (((0,), (0,)), ((), ())),
                           preferred_element_type=F32)


def _norm_matmul_kernel(x_ref, g_ref, w_ref, o_ref, xn_ref, *, out_scale):
    @pl.when(pl.program_id(1) == 0)
    def _():
        xn_ref[...] = _rms(x_ref[...], g_ref[...]).astype(BF16)

    y = jnp.dot(xn_ref[...], w_ref[...], preferred_element_type=F32)
    if out_scale != 1.0:
        y = y * out_scale
    o_ref[...] = y.astype(o_ref.dtype)


def _norm_matmul(x, gain, w, layer, *, tm, tn, out_dtype, out_scale=1.0):
    s, d = x.shape
    n = w.shape[2]
    return pl.pallas_call(
        functools.partial(_norm_matmul_kernel, out_scale=out_scale),
        grid=(s // tm, n // tn),
        in_specs=[pl.BlockSpec((tm, d), lambda i, j: (i, 0)),
                  pl.BlockSpec((1, d), lambda i, j: (0, 0)),
                  pl.BlockSpec((None, d, tn), lambda i, j: (layer, 0, j))],
        out_specs=pl.BlockSpec((tm, tn), lambda i, j: (i, j)),
        out_shape=jax.ShapeDtypeStruct((s, n), out_dtype),
        scratch_shapes=[pltpu.VMEM((tm, d), BF16)],
        compiler_params=_compiler_params(("parallel", "arbitrary")),
        name="norm_matmul",
    )(x, gain.reshape(1, d), w)


def _ordered_after(x, anchor):
    rows, lanes = BF16_SUBLANES, LANES
    a = lax.bitcast_convert_type(jnp.abs(anchor[:rows, :lanes].astype(F32)), jnp.int32)
    zero = lax.shift_right_logical(a, 31)
    tile = lax.bitcast_convert_type(x[:rows, :lanes].astype(F32), jnp.int32) | zero
    tile = lax.bitcast_convert_type(tile, F32).astype(x.dtype)
    top = jnp.concatenate([tile, x[:rows, lanes:]], axis=1)
    return jnp.concatenate([top, x[rows:, :]], axis=0)


def _gdn_in_proj_kernel(x_ref, g_ref, w_ref, cw_ref, wba_ref, alog_ref, dtb_ref, o_ref, gb_ref,
                        xn_ref, halo_ref, *, qk_blocks, v_blocks, row_chunks):
    i = pl.program_id(0)
    j = pl.program_id(1)
    tm = x_ref.shape[0]
    tn = w_ref.shape[1]

    @pl.when(j == 0)
    def _():
        xn = _rms(x_ref[...], g_ref[...]).astype(BF16)
        xn_ref[...] = xn
        gb_ref[...] = _gdn_gate_values(jnp.dot(xn, wba_ref[...], preferred_element_type=F32),
                                       alog_ref[...], dtb_ref[...])

    @pl.when((i == 0) & (j == 0))
    def _():
        halo_ref[...] = jnp.zeros_like(halo_ref)

    chunks = []
    for rows in row_chunks:
        chunks.append((sum(r for _, r in chunks), rows))
    assert sum(row_chunks) == tm
    n_chunks = len(chunks)

    def conv_silu(raw, halo):
        rows = raw.shape[0]
        cw = cw_ref[...]
        groups = jnp.concatenate([halo, raw], axis=0).reshape(rows // CONV_HALO_ROWS + 1, CONV_HALO_ROWS, tn)
        in_group = lax.broadcasted_iota(jnp.int32, (rows // CONV_HALO_ROWS, CONV_HALO_ROWS, tn), 1)
        y = cw[GDN_CONV - 1:GDN_CONV, :] * raw
        for t in range(GDN_CONV - 1):
            back = GDN_CONV - 1 - t
            rot = pltpu.roll(groups, back, axis=1)
            shifted = jnp.where(in_group < back, rot[:-1], rot[1:]).reshape(rows, tn)
            y = y + cw[t:t + 1, :] * shifted
        return _silu(y)

    def finish_qk(raw, halo):
        y = conv_silu(raw, halo)
        c = jnp.where(j < qk_blocks, float(GDN_DK), 1.0).astype(F32)
        out = []
        for h in range(tn // GDN_DK):
            yh = y[:, h * GDN_DK:(h + 1) * GDN_DK]
            ss = jnp.sum(yh * yh, axis=-1, keepdims=True)
            out.append((yh * lax.rsqrt(ss * c + RMS_EPS * c)).astype(o_ref.dtype))
        return jnp.concatenate(out, axis=1)

    def finish_v(raw, halo):
        return conv_silu(raw, halo).astype(o_ref.dtype)

    def finish_z(raw, halo):
        return _silu(raw).astype(o_ref.dtype)

    def run(finish, conv):
        done = []
        prev = None
        halo = jnp.where(i > 0, halo_ref[j], 0.0) if conv else None
        for c in range(n_chunks + 1):
            cur = None
            if c < n_chunks:
                lhs = xn_ref[pl.ds(*chunks[c]), :]
                if c >= 2:
                    lhs = _ordered_after(lhs, done[c - 2])
                cur = jnp.dot(lhs, w_ref[...], preferred_element_type=F32)
            if c >= 1:
                out = finish(prev, halo)
                o_ref[pl.ds(*chunks[c - 1]), :] = out
                done.append(out)
                halo = prev[prev.shape[0] - CONV_HALO_ROWS:, :]
            prev = cur
        if conv:
            halo_ref[j] = halo

    @pl.when(j < 2 * qk_blocks)
    def _():
        run(finish_qk, True)

    @pl.when((j >= 2 * qk_blocks) & (j < 2 * qk_blocks + v_blocks))
    def _():
        run(finish_v, True)

    @pl.when(j >= 2 * qk_blocks + v_blocks)
    def _():
        run(finish_z, False)


def _gdn_in_proj(x, gain, w_in, conv_w, layer, a_log, dt_bias, *, tm, tn, row_chunks):
    s, d = x.shape
    n = GDN_CONV_W + GDN_V_W
    assert n % LANES == 0 and w_in.shape[2] - n == 2 * GDN_HEADS
    conv_blocks = GDN_CONV_W // tn
    alog, dtb = _gdn_gate_params(a_log, dt_bias)
    lane_row = pl.BlockSpec((1, LANES), lambda i, j: (0, 0))
    return pl.pallas_call(
        functools.partial(_gdn_in_proj_kernel, qk_blocks=GDN_QK_W // tn, v_blocks=GDN_V_W // tn,
                          row_chunks=row_chunks),
        grid=(s // tm, n // tn),
        in_specs=[pl.BlockSpec((tm, d), lambda i, j: (i, 0)),
                  pl.BlockSpec((1, d), lambda i, j: (0, 0)),
                  pl.BlockSpec((None, d, tn), lambda i, j: (layer, 0, j)),
                  pl.BlockSpec((None, GDN_CONV, tn), lambda i, j: (layer, 0, jnp.minimum(j, conv_blocks - 1))),
                  pl.BlockSpec((None, d, LANES), lambda i, j: (layer, 0, n // LANES)), lane_row, lane_row],
        out_specs=[pl.BlockSpec((tm, tn), lambda i, j: (i, j)),
                   pl.BlockSpec((tm, LANES), lambda i, j: (i, 0))],
        out_shape=[jax.ShapeDtypeStruct((s, n), BF16), jax.ShapeDtypeStruct((s, LANES), F32)],
        scratch_shapes=[pltpu.VMEM((tm, d), BF16),
                        pltpu.VMEM((conv_blocks, CONV_HALO_ROWS, tn), F32)],
        compiler_params=_compiler_params(("arbitrary", "arbitrary")),
        name="gdn_in_proj",
    )(x, gain.reshape(1, d), w_in, conv_w, w_in, alog, dtb)


def _gdn_gate_values(p, alog, dtb):
    lane = lax.broadcasted_iota(jnp.int32, p.shape, 1)
    beta = _sigmoid(p)
    a = p + dtb
    softplus = jnp.maximum(a, 0.0) + jnp.log(1.0 + jnp.exp(-jnp.abs(a)))
    g = -jnp.exp(alog) * softplus
    return jnp.where(lane < GDN_HEADS, beta, jnp.where(lane < 2 * GDN_HEADS, g, 0.0))


def _gdn_gate_params(a_log, dt_bias):
    pad = LANES - 2 * GDN_HEADS
    zeros_h = jnp.zeros((GDN_HEADS,), F32)
    alog = jnp.pad(jnp.concatenate([zeros_h, a_log.astype(F32)]), (0, pad)).reshape(1, LANES)
    dtb = jnp.pad(jnp.concatenate([zeros_h, dt_bias.astype(F32)]), (0, pad)).reshape(1, LANES)
    return alog, dtb


def _unit_lower_inverse(lows, row, col):
    c = lows[0].shape[0]
    eye = (row == col).astype(F32)
    blk = GDN_INV_BASE
    diag_mask = (row // blk) == (col // blk)
    powers = [jnp.where(diag_mask, low, 0.0) for low in lows]
    invs = [eye - p for p in powers]
    span = 2
    while span < blk:
        powers = [_bdot(p, p) for p in powers]
        invs = [x + _bdot(x, p) for x, p in zip(invs, powers)]
        span *= 2
    while blk < c:
        off_mask = ((row // (2 * blk)) == (col // (2 * blk))) & ((row // blk) != (col // blk))
        offs = [_bdot(jnp.where(off_mask, low, 0.0), x) for low, x in zip(lows, invs)]
        invs = [x - _bdot(x, t) for x, t in zip(invs, offs)]
        blk *= 2
    return invs


def _gdn_kernel(q_ref, k_ref, v_ref, z_ref, gb_ref, on_ref, o_ref, state_ref):
    c_len = q_ref.shape[0]
    step = pl.program_id(1)

    @pl.when(step == 0)
    def _():
        state_ref[...] = jnp.zeros_like(state_ref)

    row = lax.broadcasted_iota(jnp.int32, (c_len, c_len), 0)
    col = lax.broadcasted_iota(jnp.int32, (c_len, c_len), 1)
    causal = row >= col
    strict = row > col

    gb = gb_ref[...]
    tri = causal.astype(BF16)
    gb_hi = gb.astype(BF16)
    gb_lo = (gb - gb_hi.astype(F32)).astype(BF16)
    gcum = (jnp.dot(tri, gb_hi, preferred_element_type=F32)
            + jnp.dot(tri, gb_lo, preferred_element_type=F32))
    gcum_t = gcum.T

    heads = range(GDN_HEADS)

    def head_slices(ref):
        return [ref[:, i * GDN_DK:(i + 1) * GDN_DK].astype(F32) for i in heads]

    qn, kn, v_all = head_slices(q_ref), head_slices(k_ref), head_slices(v_ref)
    beta = [gb[:, i:i + 1] for i in heads]
    gcol = [gcum[:, GDN_HEADS + i:GDN_HEADS + i + 1] for i in heads]
    grow = [gcum_t[GDN_HEADS + i:GDN_HEADS + i + 1, :] for i in heads]
    decay = [jnp.exp(jnp.where(causal, gc - gr, MASK_VALUE)) for gc, gr in zip(gcol, grow)]
    e_col = [jnp.exp(gc) for gc in gcol]
    g_last = [gc[c_len - 1:c_len, :] for gc in gcol]
    kb = [k * b for k, b in zip(kn, beta)]
    kk_qk = [_bdot_nt(jnp.concatenate([kb[i], qn[i]], axis=0), kn[i]) for i in heads]
    low = [jnp.where(strict, kk_qk[i][:c_len] * decay[i], 0.0) for i in heads]
    intra = [kk_qk[i][c_len:] * decay[i] for i in heads]
    inv = _unit_lower_inverse(low, row, col)
    sol = [_bdot(inv[i], jnp.concatenate([v_all[i] * beta[i], kb[i] * e_col[i]], axis=1)) for i in heads]
    state = [state_ref[i] for i in heads]
    ws_qs = [_bdot(jnp.concatenate([sol[i][:, GDN_DV:], qn[i] * e_col[i]], axis=0), state[i]) for i in heads]
    v_new = [sol[i][:, :GDN_DV] - ws_qs[i][:c_len] for i in heads]
    o = [ws_qs[i][c_len:] + _bdot(intra[i], v_new[i]) for i in heads]
    for i in heads:
        k_dec = kn[i] * jnp.exp(g_last[i] - gcol[i])
        state_ref[i] = state[i] * jnp.exp(g_last[i]) + _bdot_tn(k_dec, v_new[i])
    out_gain = on_ref[...]
    for i in heads:
        hs = slice(i * GDN_DV, (i + 1) * GDN_DV)
        o_ref[:, hs] = (_rms(o[i], out_gain) * z_ref[:, hs].astype(F32)).astype(o_ref.dtype)


def _gdn(proj, gb, out_norm):
    s = proj.shape[0]
    c = GDN_CHUNK
    hb = GDN_HEADS
    ng = GDN_HEADS // hb
    bw = hb * GDN_DK

    def cur(sec):
        return pl.BlockSpec((c, bw), lambda g, t, sec=sec: (t, sec * ng + g))

    return pl.pallas_call(
        _gdn_kernel,
        grid=(ng, s // c),
        in_specs=[cur(0), cur(1), cur(2), cur(3),
                  pl.BlockSpec((c, LANES), lambda g, t: (t, g)),
                  pl.BlockSpec((1, GDN_DV), lambda g, t: (0, 0))],
        out_specs=pl.BlockSpec((c, bw), lambda g, t: (t, g)),
        out_shape=jax.ShapeDtypeStruct((s, GDN_V_W), BF16),
        scratch_shapes=[pltpu.VMEM((hb, GDN_DK, GDN_DV), F32)],
        compiler_params=_compiler_params(("parallel", "arbitrary")),
        name="gdn_delta_rule",
    )(proj, proj, proj, proj, gb, out_norm.reshape(1, GDN_DV))


def _overlapped_row_chunks(n_chunks, lhs_rows, w_ref, finish_rows):
    done = []
    prev = None
    for c in range(n_chunks + 1):
        cur = None
        if c < n_chunks:
            lhs = lhs_rows(c)
            if c >= 2:
                lhs = _ordered_after(lhs, done[c - 2])
            cur = jnp.dot(lhs, w_ref[...], preferred_element_type=F32)
        if c >= 1:
            done.append(finish_rows(c - 1, prev))
        prev = cur


def _matmul_norm_res_kernel(a_ref, w_ref, g_ref, x_ref, o_ref, *, row_chunk):
    def rows(c):
        return pl.ds(c * row_chunk, row_chunk)

    def finish(c, m):
        out = x_ref[rows(c), :] + _rms(m, g_ref[...])
        o_ref[rows(c), :] = out
        return out

    _overlapped_row_chunks(a_ref.shape[0] // row_chunk, lambda c: a_ref[rows(c), :], w_ref, finish)


def _matmul_norm_res(a, w, gain, x, layer, *, tm, row_chunk):
    s, k = a.shape
    d = w.shape[2]
    return pl.pallas_call(
        functools.partial(_matmul_norm_res_kernel, row_chunk=row_chunk),
        grid=(s // tm,),
        in_specs=[pl.BlockSpec((tm, k), lambda i: (i, 0)),
                  pl.BlockSpec((None, k, d), lambda i: (layer, 0, 0)),
                  pl.BlockSpec((1, d), lambda i: (0, 0)),
                  pl.BlockSpec((tm, d), lambda i: (i, 0))],
        out_specs=pl.BlockSpec((tm, d), lambda i: (i, 0)),
        out_shape=jax.ShapeDtypeStruct((s, d), F32),
        compiler_params=_compiler_params(("parallel",)),
        name="matmul_norm_res",
    )(a, w, gain.reshape(1, d), x)


def _ffn_kernel(x_ref, gpre_ref, wg_ref, wu_ref, wd_ref, gpost_ref, o_ref, xn_ref, *, row_chunk):
    j = pl.program_id(1)
    last = pl.num_programs(1) - 1

    @pl.when(j == 0)
    def _():
        xn_ref[...] = _rms(x_ref[...], gpre_ref[...]).astype(BF16)
        o_ref[...] = jnp.zeros_like(o_ref)

    def activations():
        xn = xn_ref[...]
        gate = jnp.dot(xn, wg_ref[...], preferred_element_type=F32)
        up = jnp.dot(xn, wu_ref[...], preferred_element_type=F32)
        return (_silu(gate) * up).astype(BF16)

    @pl.when(j < last)
    def _():
        o_ref[...] += jnp.dot(activations(), wd_ref[...], preferred_element_type=F32)

    @pl.when(j == last)
    def _():
        act = activations()

        def rows(c):
            return pl.ds(c * row_chunk, row_chunk)

        def finish(c, m):
            out = x_ref[rows(c), :] + _rms(o_ref[rows(c), :] + m, gpost_ref[...])
            o_ref[rows(c), :] = out
            return out

        _overlapped_row_chunks(o_ref.shape[0] // row_chunk,
                               lambda c: act[c * row_chunk:(c + 1) * row_chunk, :], wd_ref, finish)


def _ffn(x, g_pre, w_gate_up, w_down, g_post, layer, *, tm, tf, row_chunk):
    s, d = x.shape
    d_ff = w_down.shape[1]
    nf = d_ff // tf
    return pl.pallas_call(
        functools.partial(_ffn_kernel, row_chunk=row_chunk),
        grid=(s // tm, nf),
        in_specs=[pl.BlockSpec((tm, d), lambda i, j: (i, 0)),
                  pl.BlockSpec((1, d), lambda i, j: (0, 0)),
                  pl.BlockSpec((None, d, tf), lambda i, j: (layer, 0, j)),
                  pl.BlockSpec((None, d, tf), lambda i, j: (layer, 0, nf + j)),
                  pl.BlockSpec((None, tf, d), lambda i, j: (layer, j, 0)),
                  pl.BlockSpec((1, d), lambda i, j: (0, 0))],
        out_specs=pl.BlockSpec((tm, d), lambda i, j: (i, 0)),
        out_shape=jax.ShapeDtypeStruct((s, d), F32),
        scratch_shapes=[pltpu.VMEM((tm, d), BF16)],
        compiler_params=_compiler_params(("parallel", "arbitrary"), FFN_VMEM_LIMIT_BYTES),
        name="swiglu_ffn",
    )(x, g_pre.reshape(1, d), w_gate_up, w_gate_up, w_down, g_post.reshape(1, d))


def _t5_bucket(dist):
    max_exact = NUM_BUCKETS // 2
    d_f = jnp.maximum(dist, 1).astype(F32)
    large = max_exact + (jnp.log(d_f / max_exact) / math.log(MAX_DISTANCE / max_exact)
                         * (NUM_BUCKETS - max_exact)).astype(jnp.int32)
    large = jnp.minimum(large, NUM_BUCKETS - 1)
    return jnp.where(dist < max_exact, dist, large)


def _tile_offsets(group):
    w = DSA_SPAN
    classes = DSA_RESIDUES // DSA_GROUPS[group][1]
    rows = w // classes
    i = jnp.arange(w)
    return (i % rows) * classes + i // rows


def _band_buckets(group):
    w = DSA_SPAN
    off = _tile_offsets(group)
    rel = off[:, None] + w - jnp.concatenate([off, off + w])[None, :]
    band = (rel >= 0) & (rel <= w)
    return jnp.where(band, _t5_bucket(jnp.clip(rel, 0, w) * DSA_GROUPS[group][1]), -1).astype(jnp.int32)


def _dsa_min_tiles(group):
    classes = DSA_RESIDUES // DSA_GROUPS[group][1]
    return max(1, BF16_SUBLANES * classes // DSA_SPAN)


def _dsa_kernel(tab_ref, bucket_ref, q_ref, kp_ref, kc_ref, vp_ref, vc_ref, o_ref, lse_ref, bias_ref, *, group):
    w = DSA_SPAN
    classes = q_ref.shape[0]
    rows = w // classes
    n_tiles = q_ref.shape[1] // rows
    step = pl.program_id(1)

    @pl.when((pl.program_id(0) == 0) & (step == 0))
    def _():
        bucket = bucket_ref[...]
        key_col = lax.broadcasted_iota(jnp.int32, (w, 2 * w), 1)

        def per_head(h, carry):
            b = jnp.full(bucket.shape, MASK_VALUE, F32)
            for t in range(NUM_BUCKETS):
                b = jnp.where(bucket == t, tab_ref[t, group * DSA_HEADS + h] * LOG2_E, b)
            bias_ref[0, h] = b
            bias_ref[1, h] = jnp.where(key_col < w, MASK_VALUE, b)
            return carry

        lax.fori_loop(0, DSA_HEADS, per_head, 0)

    def tiles(ref, cols):
        blk = ref[:, :, cols]
        if rows % BF16_SUBLANES:
            blk = blk.astype(F32)
        return [blk[:, t * rows:(t + 1) * rows, :].reshape(w, blk.shape[-1]).astype(ref.dtype)
                for t in range(ref.shape[1] // rows)]

    def store(ref, cols, parts):
        parts = [p.reshape(classes, rows, p.shape[-1]) for p in parts]
        ref[:, :, cols] = jnp.concatenate(parts, axis=1).astype(ref.dtype)

    first = jnp.where(step == 0, 1, 0)
    lane = lax.broadcasted_iota(jnp.int32, (w, LANES), 1)
    lse_all = [jnp.zeros((w, LANES), F32) for _ in range(n_tiles)]
    ones = jnp.ones((2 * w, DSA_DH), BF16)
    for h in range(DSA_HEADS):
        hs = slice(h * DSA_DH, (h + 1) * DSA_DH)
        q_t = tiles(q_ref, hs)
        k_t = tiles(kc_ref, hs)
        v_t = tiles(vc_ref, hs)
        k_prev = [tiles(kp_ref, hs)[-1]] + k_t[:-1]
        v_prev = [tiles(vp_ref, hs)[-1]] + v_t[:-1]
        outs = []
        for t in range(n_tiles):
            k = jnp.concatenate([k_prev[t], k_t[t]], axis=0)
            v = jnp.concatenate([v_prev[t], v_t[t]], axis=0)
            s = _bdot_nt(q_t[t], k) + bias_ref[first if t == 0 else 0, h]
            m = jnp.max(s, axis=-1, keepdims=True)
            p = jnp.exp2(s - m)
            acc = _bdot(p, jnp.concatenate([v, ones], axis=1))
            l = acc[:, DSA_DH:]
            outs.append(acc[:, :DSA_DH] / l)
            lse_all[t] = jnp.where(lane == h, (m + jnp.log2(l)) * LN_2, lse_all[t])
        store(o_ref, hs, outs)
    store(lse_ref, slice(None), lse_all)


def _dsa_group(q_all, kv, rel_bias, group, *, tiles_per_step):
    s = q_all.shape[0]
    dilation = DSA_GROUPS[group][1]
    w = DSA_SPAN
    gw = DSA_GROUP_W
    classes = DSA_RESIDUES // dilation
    class_len = s // DSA_RESIDUES
    rows = w // classes
    prev_tiles = _dsa_min_tiles(group)
    n_tiles = max(prev_tiles, min(tiles_per_step, class_len // rows))
    assert n_tiles % prev_tiles == 0 and class_len % (n_tiles * rows) == 0

    def view(a):
        return a.reshape(classes, dilation, class_len, a.shape[-1])

    def cur(width, col):
        return pl.BlockSpec((classes, None, n_tiles * rows, width), lambda r, n: (0, r, n, col))

    def prev(width, col):
        ratio = n_tiles // prev_tiles
        return pl.BlockSpec((classes, None, prev_tiles * rows, width),
                            lambda r, n: (0, r, jnp.maximum(n * ratio - 1, 0), col))

    o, lse = pl.pallas_call(
        functools.partial(_dsa_kernel, group=group),
        grid=(dilation, class_len // (n_tiles * rows)),
        in_specs=[pl.BlockSpec(memory_space=pltpu.SMEM),
                  pl.BlockSpec((w, 2 * w), lambda r, n: (0, 0)),
                  cur(gw, group),
                  prev(gw, group), cur(gw, group),
                  prev(gw, N_GROUPS + group), cur(gw, N_GROUPS + group)],
        out_specs=[cur(gw, 0), cur(LANES, 0)],
        out_shape=[jax.ShapeDtypeStruct((classes, dilation, class_len, gw), BF16),
                   jax.ShapeDtypeStruct((classes, dilation, class_len, LANES), F32)],
        scratch_shapes=[pltpu.VMEM((2, DSA_HEADS, w, 2 * w), F32)],
        compiler_params=_compiler_params(("arbitrary", "arbitrary")),
        name=f"dsa_attention_g{group}",
    )(rel_bias, _band_buckets(group), view(q_all), view(kv), view(kv), view(kv), view(kv))
    return o.reshape(s, gw), lse.reshape(s, LANES)


def _to_residue_major_kernel(x_ref, o_ref):
    rows = o_ref.shape[1]
    x = x_ref[...].reshape(rows, DSA_RESIDUES, x_ref.shape[-1])
    o_ref[...] = pltpu.einshape("lrd->rld", x)


def _from_residue_major_kernel(x_ref, o_ref):
    rows = x_ref.shape[1]
    o_ref[...] = pltpu.einshape("rld->lrd", x_ref[...]).reshape(rows * DSA_RESIDUES, x_ref.shape[-1])


def _residue_major(x, *, rows, inverse):
    s, d = x.shape
    class_len = s // DSA_RESIDUES
    natural = pl.BlockSpec((rows * DSA_RESIDUES, d), lambda i: (i, 0))
    major = pl.BlockSpec((DSA_RESIDUES, rows, d), lambda i: (0, i, 0))
    if inverse:
        body, specs, arg, shape = (_from_residue_major_kernel, (major, natural),
                                   x.reshape(DSA_RESIDUES, class_len, d), (s, d))
    else:
        body, specs, arg, shape = _to_residue_major_kernel, (natural, major), x, (DSA_RESIDUES, class_len, d)
    out = pl.pallas_call(
        body,
        grid=(class_len // rows,),
        in_specs=[specs[0]],
        out_specs=specs[1],
        out_shape=jax.ShapeDtypeStruct(shape, x.dtype),
        compiler_params=_compiler_params(("parallel",)),
        name="from_residue_major" if inverse else "to_residue_major",
    )(arg)
    return out.reshape(s, d)


def _dsa_out_kernel(o0_ref, o1_ref, o2_ref, l0_ref, l1_ref, l2_ref, w_ref, g_ref, x_ref, out_ref, *, row_chunk):
    o_refs = [o0_ref, o1_ref, o2_ref]
    l_refs = [l0_ref, l1_ref, l2_ref]

    def rows(c):
        return pl.ds(c * row_chunk, row_chunk)

    def merged_rows(c):
        lses = [l[rows(c), :] for l in l_refs]
        top = jnp.maximum(jnp.maximum(lses[0], lses[1]), lses[2])
        es = [jnp.exp(l - top) for l in lses]
        inv_den = 1.0 / (es[0] + es[1] + es[2])
        wts = [e * inv_den for e in es]
        heads = []
        for h in range(DSA_HEADS):
            hs = slice(h * DSA_DH, (h + 1) * DSA_DH)
            merged = wts[0][:, h:h + 1] * o_refs[0][rows(c), hs].astype(F32)
            for g in range(1, N_GROUPS):
                merged = merged + wts[g][:, h:h + 1] * o_refs[g][rows(c), hs].astype(F32)
            heads.append(merged.astype(BF16))
        return jnp.concatenate(heads, axis=1)

    def finish(c, m):
        out = x_ref[rows(c), :] + _rms(m, g_ref[...])
        out_ref[rows(c), :] = out
        return out

    _overlapped_row_chunks(x_ref.shape[0] // row_chunk, merged_rows, w_ref, finish)


def _dsa_out(os_, lses, w, gain, x, layer, *, tm, row_chunk):
    s, k = os_[0].shape
    d = w.shape[2]
    o_spec = pl.BlockSpec((tm, k), lambda i: (i, 0))
    l_spec = pl.BlockSpec((tm, LANES), lambda i: (i, 0))
    return pl.pallas_call(
        functools.partial(_dsa_out_kernel, row_chunk=row_chunk),
        grid=(s // tm,),
        in_specs=[o_spec, o_spec, o_spec, l_spec, l_spec, l_spec,
                  pl.BlockSpec((None, k, d), lambda i: (layer, 0, 0)),
                  pl.BlockSpec((1, d), lambda i: (0, 0)),
                  pl.BlockSpec((tm, d), lambda i: (i, 0))],
        out_specs=pl.BlockSpec((tm, d), lambda i: (i, 0)),
        out_shape=jax.ShapeDtypeStruct((s, d), F32),
        compiler_params=_compiler_params(("parallel",)),
        name="dsa_merge_out",
    )(*os_, *lses, w, gain.reshape(1, d), x)


def _tiles(s):
    tm_proj = min(1024, s)
    eighth = tm_proj // 8
    return dict(tm_proj=tm_proj, tn_proj=2048, tn_gdn=1024, tm_out=min(512, s), out_chunk=128, tm_ffn=min(1024, s), tf_ffn=512, ffn_chunk=min(256, s), dsa_tiles=4,
                perm_rows=32, gdn_proj_chunks=(2 * eighth, 2 * eighth, 2 * eighth, eighth, eighth))


def kernel(x, norm_gains, ffn_w_gate_up, ffn_w_down, gdn_w_in, gdn_conv_w, gdn_a_log, gdn_dt_bias,
           gdn_out_norm, gdn_w_out, kv_norm, kv_w, dsa_w_q, dsa_w_out, rel_bias):
    b, s, d = x.shape
    assert b == 1 and d == D_MODEL and s % (DSA_SPAN * DSA_GROUPS[-1][1]) == 0
    t = _tiles(s)
    depth = norm_gains.shape[0]
    n_a = gdn_w_in.shape[0]
    xs = x.reshape(s, d)
    kv = None
    ffn_w_gate_up, ffn_w_down = ffn_w_gate_up.astype(BF16), ffn_w_down.astype(BF16)
    gdn_w_in, gdn_w_out = gdn_w_in.astype(BF16), gdn_w_out.astype(BF16)
    dsa_w_q, dsa_w_out = dsa_w_q.astype(BF16), dsa_w_out.astype(BF16)
    for layer in range(depth):
        gains = norm_gains[layer]
        if layer < n_a:
            proj, gb = _gdn_in_proj(xs, gains[0], gdn_w_in, gdn_conv_w, layer, gdn_a_log[layer], gdn_dt_bias[layer],
                                    tm=t["tm_proj"], tn=t["tn_gdn"], row_chunks=t["gdn_proj_chunks"])
            o = _gdn(proj, gb, gdn_out_norm[layer])
            xs = _matmul_norm_res(o, gdn_w_out, gains[1], xs, layer, tm=t["tm_out"], row_chunk=t["out_chunk"])
        else:
            j = layer - n_a
            q_all = _norm_matmul(xs, gains[0], dsa_w_q, j, tm=t["tm_proj"], tn=t["tn_proj"], out_dtype=BF16,
                                 out_scale=DSA_DH ** -0.5 * LOG2_E)
            outs = [_dsa_group(q_all, kv, rel_bias, g, tiles_per_step=t["dsa_tiles"]) for g in range(N_GROUPS)]
            xs = _dsa_out([o for o, _ in outs], [l for _, l in outs], dsa_w_out, gains[1], xs, j,
                          tm=t["tm_out"], row_chunk=t["out_chunk"])
        xs = _ffn(xs, gains[2], ffn_w_gate_up, ffn_w_down, gains[3], layer, tm=t["tm_ffn"], tf=t["tf_ffn"],
                  row_chunk=t["ffn_chunk"])
        if layer == n_a - 1 and depth > n_a:
            xs = _residue_major(xs, rows=t["perm_rows"], inverse=False)
            kv = _norm_matmul(xs, kv_norm, kv_w.astype(BF16)[None], 0, tm=t["tm_proj"], tn=t["tn_proj"],
                              out_dtype=BF16)
    if depth > n_a:
        xs = _residue_major(xs, rows=t["perm_rows"], inverse=True)
    return xs.reshape(b, s, d)
```

```python
import functools
import math

import jax
import jax.numpy as jnp
from jax import lax
from jax.experimental import pallas as pl
from jax.experimental.pallas import tpu as pltpu

F32 = jnp.float32
BF16 = jnp.bfloat16

RMS_EPS = 1e-6
D_MODEL = 2048

GDN_HEADS = 16
GDN_DK = 128
GDN_DV = 128
GDN_CONV = 4
GDN_QK_W = GDN_HEADS * GDN_DK
GDN_V_W = GDN_HEADS * GDN_DV
GDN_CONV_W = 2 * GDN_QK_W + GDN_V_W
GDN_CHUNK = 128
GDN_INV_BASE = 8
CONV_HALO_ROWS = 8

DSA_GROUPS = ((128, 1), (512, 4), (2048, 16))
N_GROUPS = len(DSA_GROUPS)
DSA_HEADS = 16
DSA_DH = 128
DSA_SPAN = 128
DSA_GROUP_W = DSA_HEADS * DSA_DH
NUM_BUCKETS = 32
MAX_DISTANCE = 2048
MASK_VALUE = -1e30
LOG2_E = math.log2(math.e)
LN_2 = math.log(2.0)

DSA_RESIDUES = DSA_GROUPS[-1][1]

LANES = 128
BF16_SUBLANES = 16
VMEM_BYTES = 64 * 1024 * 1024
VMEM_LIMIT_BYTES = VMEM_BYTES * 7 // 8
FFN_VMEM_LIMIT_BYTES = VMEM_BYTES * 31 // 32


def _compiler_params(semantics, vmem_limit_bytes=VMEM_LIMIT_BYTES):
    return pltpu.CompilerParams(dimension_semantics=semantics, vmem_limit_bytes=vmem_limit_bytes)


def _rms(x, gain):
    ms = jnp.mean(x * x, axis=-1, keepdims=True)
    return x * lax.rsqrt(ms + RMS_EPS) * gain


def _sigmoid(x):
    return 1.0 / (1.0 + jnp.exp(-x))


def _silu(x):
    h = 0.5 * x
    return h + h * jnp.tanh(h)


def _bdot(a, b):
    return jnp.dot(a.astype(BF16), b.astype(BF16), preferred_element_type=F32)


def _bdot_nt(a, b):
    return lax.dot_general(a.astype(BF16), b.astype(BF16), (((1,), (1,)), ((), ())),
                           preferred_element_type=F32)


def _bdot_tn(a, b):
    return lax.dot_general(a.astype(BF16), b.astype(BF16), (((0,), (0,)), ((), ())),
                           preferred_element_type=F32)


def _norm_matmul_kernel(x_ref, g_ref, w_ref, o_ref, xn_ref, *, out_scale):
    @pl.when(pl.program_id(1) == 0)
    def _():
        xn_ref[...] = _rms(x_ref[...], g_ref[...]).astype(BF16)

    y = jnp.dot(xn_ref[...], w_ref[...], preferred_element_type=F32)
    if out_scale != 1.0:
        y = y * out_scale
    o_ref[...] = y.astype(o_ref.dtype)


def _norm_matmul(x, gain, w, layer, *, tm, tn, out_dtype, out_scale=1.0):
    s, d = x.shape
    n = w.shape[2]
    return pl.pallas_call(
        functools.partial(_norm_matmul_kernel, out_scale=out_scale),
        grid=(s // tm, n // tn),
        in_specs=[pl.BlockSpec((tm, d), lambda i, j: (i, 0)),
                  pl.BlockSpec((1, d), lambda i, j: (0, 0)),
                  pl.BlockSpec((None, d, tn), lambda i, j: (layer, 0, j))],
        out_specs=pl.BlockSpec((tm, tn), lambda i, j: (i, j)),
        out_shape=jax.ShapeDtypeStruct((s, n), out_dtype),
        scratch_shapes=[pltpu.VMEM((tm, d), BF16)],
        compiler_params=_compiler_params(("parallel", "arbitrary")),
        name="norm_matmul",
    )(x, gain.reshape(1, d), w)


def _ordered_after(x, anchor):
    rows, lanes = BF16_SUBLANES, LANES
    a = lax.bitcast_convert_type(jnp.abs(anchor[:rows, :lanes].astype(F32)), jnp.int32)
    zero = lax.shift_right_logical(a, 31)
    tile = lax.bitcast_convert_type(x[:rows, :lanes].astype(F32), jnp.int32) | zero
    tile = lax.bitcast_convert_type(tile, F32).astype(x.dtype)
    top = jnp.concatenate([tile, x[:rows, lanes:]], axis=1)
    return jnp.concatenate([top, x[rows:, :]], axis=0)


def _gdn_in_proj_kernel(x_ref, g_ref, w_ref, cw_ref, wba_ref, alog_ref, dtb_ref, o_ref, gb_ref,
                        xn_ref, halo_ref, *, qk_blocks, v_blocks, row_chunks):
    i = pl.program_id(0)
    j = pl.program_id(1)
    tm = x_ref.shape[0]
    tn = w_ref.shape[1]

    @pl.when(j == 0)
    def _():
        xn = _rms(x_ref[...], g_ref[...]).astype(BF16)
        xn_ref[...] = xn
        gb_ref[...] = _gdn_gate_values(jnp.dot(xn, wba_ref[...], preferred_element_type=F32),
                                       alog_ref[...], dtb_ref[...])

    @pl.when((i == 0) & (j == 0))
    def _():
        halo_ref[...] = jnp.zeros_like(halo_ref)

    chunks = []
    for rows in row_chunks:
        chunks.append((sum(r for _, r in chunks), rows))
    assert sum(row_chunks) == tm
    n_chunks = len(chunks)

    def conv_silu(raw, halo):
        rows = raw.shape[0]
        cw = cw_ref[...]
        groups = jnp.concatenate([halo, raw], axis=0).reshape(rows // CONV_HALO_ROWS + 1, CONV_HALO_ROWS, tn)
        in_group = lax.broadcasted_iota(jnp.int32, (rows // CONV_HALO_ROWS, CONV_HALO_ROWS, tn), 1)
        y = cw[GDN_CONV - 1:GDN_CONV, :] * raw
        for t in range(GDN_CONV - 1):
            back = GDN_CONV - 1 - t
            rot = pltpu.roll(groups, back, axis=1)
            shifted = jnp.where(in_group < back, rot[:-1], rot[1:]).reshape(rows, tn)
            y = y + cw[t:t + 1, :] * shifted
        return _silu(y)

    def finish_qk(raw, halo):
        y = conv_silu(raw, halo)
        c = jnp.where(j < qk_blocks, float(GDN_DK), 1.0).astype(F32)
        out = []
        for h in range(tn // GDN_DK):
            yh = y[:, h * GDN_DK:(h + 1) * GDN_DK]
            ss = jnp.sum(yh * yh, axis=-1, keepdims=True)
            out.append((yh * lax.rsqrt(ss * c + RMS_EPS * c)).astype(o_ref.dtype))
        return jnp.concatenate(out, axis=1)

    def finish_v(raw, halo):
        return conv_silu(raw, halo).astype(o_ref.dtype)

    def finish_z(raw, halo):
        return _silu(raw).astype(o_ref.dtype)

    def run(finish, conv):
        done = []
        prev = None
        halo = jnp.where(i > 0, halo_ref[j], 0.0) if conv else None
        for c in range(n_chunks + 1):
            cur = None
            if c < n_chunks:
                lhs = xn_ref[pl.ds(*chunks[c]), :]
                if c >= 2:
                    lhs = _ordered_after(lhs, done[c - 2])
                cur = jnp.dot(lhs, w_ref[...], preferred_element_type=F32)
            if c >= 1:
                out = finish(prev, halo)
                o_ref[pl.ds(*chunks[c - 1]), :] = out
                done.append(out)
                halo = prev[prev.shape[0] - CONV_HALO_ROWS:, :]
            prev = cur
        if conv:
            halo_ref[j] = halo

    @pl.when(j < 2 * qk_blocks)
    def _():
        run(finish_qk, True)

    @pl.when((j >= 2 * qk_blocks) & (j < 2 * qk_blocks + v_blocks))
    def _():
        run(finish_v, True)

    @pl.when(j >= 2 * qk_blocks + v_blocks)
    def _():
        run(finish_z, False)


def _gdn_in_proj(x, gain, w_in, conv_w, layer, a_log, dt_bias, *, tm, tn, row_chunks):
    s, d = x.shape
    n = GDN_CONV_W + GDN_V_W
    assert n % LANES == 0 and w_in.shape[2] - n == 2 * GDN_HEADS
    conv_blocks = GDN_CONV_W // tn
    alog, dtb = _gdn_gate_params(a_log, dt_bias)
    lane_row = pl.BlockSpec((1, LANES), lambda i, j: (0, 0))
    return pl.pallas_call(
        functools.partial(_gdn_in_proj_kernel, qk_blocks=GDN_QK_W // tn, v_blocks=GDN_V_W // tn,
                          row_chunks=row_chunks),
        grid=(s // tm, n // tn),
        in_specs=[pl.BlockSpec((tm, d), lambda i, j: (i, 0)),
                  pl.BlockSpec((1, d), lambda i, j: (0, 0)),
                  pl.BlockSpec((None, d, tn), lambda i, j: (layer, 0, j)),
                  pl.BlockSpec((None, GDN_CONV, tn), lambda i, j: (layer, 0, jnp.minimum(j, conv_blocks - 1))),
                  pl.BlockSpec((None, d, LANES), lambda i, j: (layer, 0, n // LANES)), lane_row, lane_row],
        out_specs=[pl.BlockSpec((tm, tn), lambda i, j: (i, j)),
                   pl.BlockSpec((tm, LANES), lambda i, j: (i, 0))],
        out_shape=[jax.ShapeDtypeStruct((s, n), BF16), jax.ShapeDtypeStruct((s, LANES), F32)],
        scratch_shapes=[pltpu.VMEM((tm, d), BF16),
                        pltpu.VMEM((conv_blocks, CONV_HALO_ROWS, tn), F32)],
        compiler_params=_compiler_params(("arbitrary", "arbitrary")),
        name="gdn_in_proj",
    )(x, gain.reshape(1, d), w_in, conv_w, w_in, alog, dtb)


def _gdn_gate_values(p, alog, dtb):
    lane = lax.broadcasted_iota(jnp.int32, p.shape, 1)
    beta = _sigmoid(p)
    a = p + dtb
    softplus = jnp.maximum(a, 0.0) + jnp.log(1.0 + jnp.exp(-jnp.abs(a)))
    g = -jnp.exp(alog) * softplus
    return jnp.where(lane < GDN_HEADS, beta, jnp.where(lane < 2 * GDN_HEADS, g, 0.0))


def _gdn_gate_params(a_log, dt_bias):
    pad = LANES - 2 * GDN_HEADS
    zeros_h = jnp.zeros((GDN_HEADS,), F32)
    alog = jnp.pad(jnp.concatenate([zeros_h, a_log.astype(F32)]), (0, pad)).reshape(1, LANES)
    dtb = jnp.pad(jnp.concatenate([zeros_h, dt_bias.astype(F32)]), (0, pad)).reshape(1, LANES)
    return alog, dtb


def _unit_lower_inverse(lows, row, col):
    c = lows[0].shape[0]
    eye = (row == col).astype(F32)
    blk = GDN_INV_BASE
    diag_mask = (row // blk) == (col // blk)
    powers = [jnp.where(diag_mask, low, 0.0) for low in lows]
    invs = [eye - p for p in powers]
    span = 2
    while span < blk:
        powers = [_bdot(p, p) for p in powers]
        invs = [x + _bdot(x, p) for x, p in zip(invs, powers)]
        span *= 2
    while blk < c:
        off_mask = ((row // (2 * blk)) == (col // (2 * blk))) & ((row // blk) != (col // blk))
        offs = [_bdot(jnp.where(off_mask, low, 0.0), x) for low, x in zip(lows, invs)]
        invs = [x - _bdot(x, t) for x, t in zip(invs, offs)]
        blk *= 2
    return invs


def _gdn_kernel(q_ref, k_ref, v_ref, z_ref, gb_ref, on_ref, o_ref, state_ref):
    c_len = q_ref.shape[0]
    step = pl.program_id(1)

    @pl.when(step == 0)
    def _():
        state_ref[...] = jnp.zeros_like(state_ref)

    row = lax.broadcasted_iota(jnp.int32, (c_len, c_len), 0)
    col = lax.broadcasted_iota(jnp.int32, (c_len, c_len), 1)
    causal = row >= col
    strict = row > col

    gb = gb_ref[...]
    tri = causal.astype(BF16)
    gb_hi = gb.astype(BF16)
    gb_lo = (gb - gb_hi.astype(F32)).astype(BF16)
    gcum = (jnp.dot(tri, gb_hi, preferred_element_type=F32)
            + jnp.dot(tri, gb_lo, preferred_element_type=F32))
    gcum_t = gcum.T

    heads = range(GDN_HEADS)

    def head_slices(ref):
        return [ref[:, i * GDN_DK:(i + 1) * GDN_DK].astype(F32) for i in heads]

    qn, kn, v_all = head_slices(q_ref), head_slices(k_ref), head_slices(v_ref)
    beta = [gb[:, i:i + 1] for i in heads]
    gcol = [gcum[:, GDN_HEADS + i:GDN_HEADS + i + 1] for i in heads]
    grow = [gcum_t[GDN_HEADS + i:GDN_HEADS + i + 1, :] for i in heads]
    decay = [jnp.exp(jnp.where(causal, gc - gr, MASK_VALUE)) for gc, gr in zip(gcol, grow)]
    e_col = [jnp.exp(gc) for gc in gcol]
    g_last = [gc[c_len - 1:c_len, :] for gc in gcol]
    kb = [k * b for k, b in zip(kn, beta)]
    kk_qk = [_bdot_nt(jnp.concatenate([kb[i], qn[i]], axis=0), kn[i]) for i in heads]
    low = [jnp.where(strict, kk_qk[i][:c_len] * decay[i], 0.0) for i in heads]
    intra = [kk_qk[i][c_len:] * decay[i] for i in heads]
    inv = _unit_lower_inverse(low, row, col)
    sol = [_bdot(inv[i], jnp.concatenate([v_all[i] * beta[i], kb[i] * e_col[i]], axis=1)) for i in heads]
    state = [state_ref[i] for i in heads]
    ws_qs = [_bdot(jnp.concatenate([sol[i][:, GDN_DV:], qn[i] * e_col[i]], axis=0), state[i]) for i in heads]
    v_new = [sol[i][:, :GDN_DV] - ws_qs[i][:c_len] for i in heads]
    o = [ws_qs[i][c_len:] + _bdot(intra[i], v_new[i]) for i in heads]
    for i in heads:
        k_dec = kn[i] * jnp.exp(g_last[i] - gcol[i])
        state_ref[i] = state[i] * jnp.exp(g_last[i]) + _bdot_tn(k_dec, v_new[i])
    out_gain = on_ref[...]
    for i in heads:
        hs = slice(i * GDN_DV, (i + 1) * GDN_DV)
        o_ref[:, hs] = (_rms(o[i], out_gain) * z_ref[:, hs].astype(F32)).astype(o_ref.dtype)


def _gdn(proj, gb, out_norm):
    s = proj.shape[0]
    c = GDN_CHUNK
    hb = GDN_HEADS
    ng = GDN_HEADS // hb
    bw = hb * GDN_DK

    def cur(sec):
        return pl.BlockSpec((c, bw), lambda g, t, sec=sec: (t, sec * ng + g))

    return pl.pallas_call(
        _gdn_kernel,
        grid=(ng, s // c),
        in_specs=[cur(0), cur(1), cur(2), cur(3),
                  pl.BlockSpec((c, LANES), lambda g, t: (t, g)),
                  pl.BlockSpec((1, GDN_DV), lambda g, t: (0, 0))],
        out_specs=pl.BlockSpec((c, bw), lambda g, t: (t, g)),
        out_shape=jax.ShapeDtypeStruct((s, GDN_V_W), BF16),
        scratch_shapes=[pltpu.VMEM((hb, GDN_DK, GDN_DV), F32)],
        compiler_params=_compiler_params(("parallel", "arbitrary")),
        name="gdn_delta_rule",
    )(proj, proj, proj, proj, gb, out_norm.reshape(1, GDN_DV))


def _overlapped_row_chunks(n_chunks, lhs_rows, w_ref, finish_rows):
    done = []
    prev = None
    for c in range(n_chunks + 1):
        cur = None
        if c < n_chunks:
            lhs = lhs_rows(c)
            if c >= 2:
                lhs = _ordered_after(lhs, done[c - 2])
            cur = jnp.dot(lhs, w_ref[...], preferred_element_type=F32)
        if c >= 1:
            done.append(finish_rows(c - 1, prev))
        prev = cur


def _matmul_norm_res_kernel(a_ref, w_ref, g_ref, x_ref, o_ref, *, row_chunk):
    def rows(c):
        return pl.ds(c * row_chunk, row_chunk)

    def finish(c, m):
        out = x_ref[rows(c), :] + _rms(m, g_ref[...])
        o_ref[rows(c), :] = out
        return out

    _overlapped_row_chunks(a_ref.shape[0] // row_chunk, lambda c: a_ref[rows(c), :], w_ref, finish)


def _matmul_norm_res(a, w, gain, x, layer, *, tm, row_chunk):
    s, k = a.shape
    d = w.shape[2]
    return pl.pallas_call(
        functools.partial(_matmul_norm_res_kernel, row_chunk=row_chunk),
        grid=(s // tm,),
        in_specs=[pl.BlockSpec((tm, k), lambda i: (i, 0)),
                  pl.BlockSpec((None, k, d), lambda i: (layer, 0, 0)),
                  pl.BlockSpec((1, d), lambda i: (0, 0)),
                  pl.BlockSpec((tm, d), lambda i: (i, 0))],
        out_specs=pl.BlockSpec((tm, d), lambda i: (i, 0)),
        out_shape=jax.ShapeDtypeStruct((s, d), F32),
        compiler_params=_compiler_params(("parallel",)),
        name="matmul_norm_res",
    )(a, w, gain.reshape(1, d), x)


def _ffn_kernel(x_ref, gpre_ref, wg_ref, wu_ref, wd_ref, gpost_ref, o_ref, xn_ref, *, row_chunk):
    j = pl.program_id(1)
    last = pl.num_programs(1) - 1

    @pl.when(j == 0)
    def _():
        xn_ref[...] = _rms(x_ref[...], gpre_ref[...]).astype(BF16)
        o_ref[...] = jnp.zeros_like(o_ref)

    def activations():
        xn = xn_ref[...]
        gate = jnp.dot(xn, wg_ref[...], preferred_element_type=F32)
        up = jnp.dot(xn, wu_ref[...], preferred_element_type=F32)
        return (_silu(gate) * up).astype(BF16)

    @pl.when(j < last)
    def _():
        o_ref[...] += jnp.dot(activations(), wd_ref[...], preferred_element_type=F32)

    @pl.when(j == last)
    def _():
        act = activations()

        def rows(c):
            return pl.ds(c * row_chunk, row_chunk)

        def finish(c, m):
            out = x_ref[rows(c), :] + _rms(o_ref[rows(c), :] + m, gpost_ref[...])
            o_ref[rows(c), :] = out
            return out

        _overlapped_row_chunks(o_ref.shape[0] // row_chunk,
                               lambda c: act[c * row_chunk:(c + 1) * row_chunk, :], wd_ref, finish)


def _ffn(x, g_pre, w_gate_up, w_down, g_post, layer, *, tm, tf, row_chunk):
    s, d = x.shape
    d_ff = w_down.shape[1]
    nf = d_ff // tf
    return pl.pallas_call(
        functools.partial(_ffn_kernel, row_chunk=row_chunk),
        grid=(s // tm, nf),
        in_specs=[pl.BlockSpec((tm, d), lambda i, j: (i, 0)),
                  pl.BlockSpec((1, d), lambda i, j: (0, 0)),
                  pl.BlockSpec((None, d, tf), lambda i, j: (layer, 0, j)),
                  pl.BlockSpec((None, d, tf), lambda i, j: (layer, 0, nf + j)),
                  pl.BlockSpec((None, tf, d), lambda i, j: (layer, j, 0)),
                  pl.BlockSpec((1, d), lambda i, j: (0, 0))],
        out_specs=pl.BlockSpec((tm, d), lambda i, j: (i, 0)),
        out_shape=jax.ShapeDtypeStruct((s, d), F32),
        scratch_shapes=[pltpu.VMEM((tm, d), BF16)],
        compiler_params=_compiler_params(("parallel", "arbitrary"), FFN_VMEM_LIMIT_BYTES),
        name="swiglu_ffn",
    )(x, g_pre.reshape(1, d), w_gate_up, w_gate_up, w_down, g_post.reshape(1, d))


def _t5_bucket(dist):
    max_exact = NUM_BUCKETS // 2
    d_f = jnp.maximum(dist, 1).astype(F32)
    large = max_exact + (jnp.log(d_f / max_exact) / math.log(MAX_DISTANCE / max_exact)
                         * (NUM_BUCKETS - max_exact)).astype(jnp.int32)
    large = jnp.minimum(large, NUM_BUCKETS - 1)
    return jnp.where(dist < max_exact, dist, large)


def _tile_offsets(group):
    w = DSA_SPAN
    classes = DSA_RESIDUES // DSA_GROUPS[group][1]
    rows = w // classes
    i = jnp.arange(w)
    return (i % rows) * classes + i // rows


def _band_buckets(group):
    w = DSA_SPAN
    off = _tile_offsets(group)
    rel = off[:, None] + w - jnp.concatenate([off, off + w])[None, :]
    band = (rel >= 0) & (rel <= w)
    return jnp.where(band, _t5_bucket(jnp.clip(rel, 0, w) * DSA_GROUPS[group][1]), -1).astype(jnp.int32)


def _dsa_min_tiles(group):
    classes = DSA_RESIDUES // DSA_GROUPS[group][1]
    return max(1, BF16_SUBLANES * classes // DSA_SPAN)


def _dsa_kernel(tab_ref, bucket_ref, q_ref, kp_ref, kc_ref, vp_ref, vc_ref, o_ref, lse_ref, bias_ref, *, group):
    w = DSA_SPAN
    classes = q_ref.shape[0]
    rows = w // classes
    n_tiles = q_ref.shape[1] // rows
    step = pl.program_id(1)

    @pl.when((pl.program_id(0) == 0) & (step == 0))
    def _():
        bucket = bucket_ref[...]
        key_col = lax.broadcasted_iota(jnp.int32, (w, 2 * w), 1)

        def per_head(h, carry):
            b = jnp.full(bucket.shape, MASK_VALUE, F32)
            for t in range(NUM_BUCKETS):
                b = jnp.where(bucket == t, tab_ref[t, group * DSA_HEADS + h] * LOG2_E, b)
            bias_ref[0, h] = b
            bias_ref[1, h] = jnp.where(key_col < w, MASK_VALUE, b)
            return carry

        lax.fori_loop(0, DSA_HEADS, per_head, 0)

    def tiles(ref, cols):
        blk = ref[:, :, cols]
        if rows % BF16_SUBLANES:
            blk = blk.astype(F32)
        return [blk[:, t * rows:(t + 1) * rows, :].reshape(w, blk.shape[-1]).astype(ref.dtype)
                for t in range(ref.shape[1] // rows)]

    def store(ref, cols, parts):
        parts = [p.reshape(classes, rows, p.shape[-1]) for p in parts]
        ref[:, :, cols] = jnp.concatenate(parts, axis=1).astype(ref.dtype)

    first = jnp.where(step == 0, 1, 0)
    lane = lax.broadcasted_iota(jnp.int32, (w, LANES), 1)
    lse_all = [jnp.zeros((w, LANES), F32) for _ in range(n_tiles)]
    ones = jnp.ones((2 * w, DSA_DH), BF16)
    for h in range(DSA_HEADS):
        hs = slice(h * DSA_DH, (h + 1) * DSA_DH)
        q_t = tiles(q_ref, hs)
        k_t = tiles(kc_ref, hs)
        v_t = tiles(vc_ref, hs)
        k_prev = [tiles(kp_ref, hs)[-1]] + k_t[:-1]
        v_prev = [tiles(vp_ref, hs)[-1]] + v_t[:-1]
        outs = []
        for t in range(n_tiles):
            k = jnp.concatenate([k_prev[t], k_t[t]], axis=0)
            v = jnp.concatenate([v_prev[t], v_t[t]], axis=0)
            s = _bdot_nt(q_t[t], k) + bias_ref[first if t == 0 else 0, h]
            m = jnp.max(s, axis=-1, keepdims=True)
            p = jnp.exp2(s - m)
            acc = _bdot(p, jnp.concatenate([v, ones], axis=1))
            l = acc[:, DSA_DH:]
            outs.append(acc[:, :DSA_DH] / l)
            lse_all[t] = jnp.where(lane == h, (m + jnp.log2(l)) * LN_2, lse_all[t])
        store(o_ref, hs, outs)
    store(lse_ref, slice(None), lse_all)


def _dsa_group(q_all, kv, rel_bias, group, *, tiles_per_step):
    s = q_all.shape[0]
    dilation = DSA_GROUPS[group][1]
    w = DSA_SPAN
    gw = DSA_GROUP_W
    classes = DSA_RESIDUES // dilation
    class_len = s // DSA_RESIDUES
    rows = w // classes
    prev_tiles = _dsa_min_tiles(group)
    n_tiles = max(prev_tiles, min(tiles_per_step, class_len // rows))
    assert n_tiles % prev_tiles == 0 and class_len % (n_tiles * rows) == 0

    def view(a):
        return a.reshape(classes, dilation, class_len, a.shape[-1])

    def cur(width, col):
        return pl.BlockSpec((classes, None, n_tiles * rows, width), lambda r, n: (0, r, n, col))

    def prev(width, col):
        ratio = n_tiles // prev_tiles
        return pl.BlockSpec((classes, None, prev_tiles * rows, width),
                            lambda r, n: (0, r, jnp.maximum(n * ratio - 1, 0), col))

    o, lse = pl.pallas_call(
        functools.partial(_dsa_kernel, group=group),
        grid=(dilation, class_len // (n_tiles * rows)),
        in_specs=[pl.BlockSpec(memory_space=pltpu.SMEM),
                  pl.BlockSpec((w, 2 * w), lambda r, n: (0, 0)),
                  cur(gw, group),
                  prev(gw, group), cur(gw, group),
                  prev(gw, N_GROUPS + group), cur(gw, N_GROUPS + group)],
        out_specs=[cur(gw, 0), cur(LANES, 0)],
        out_shape=[jax.ShapeDtypeStruct((classes, dilation, class_len, gw), BF16),
                   jax.ShapeDtypeStruct((classes, dilation, class_len, LANES), F32)],
        scratch_shapes=[pltpu.VMEM((2, DSA_HEADS, w, 2 * w), F32)],
        compiler_params=_compiler_params(("arbitrary", "arbitrary")),
        name=f"dsa_attention_g{group}",
    )(rel_bias, _band_buckets(group), view(q_all), view(kv), view(kv), view(kv), view(kv))
    return o.reshape(s, gw), lse.reshape(s, LANES)


def _to_residue_major_kernel(x_ref, o_ref):
    rows = o_ref.shape[1]
    x = x_ref[...].reshape(rows, DSA_RESIDUES, x_ref.shape[-1])
    o_ref[...] = pltpu.einshape("lrd->rld", x)


def _from_residue_major_kernel(x_ref, o_ref):
    rows = x_ref.shape[1]
    o_ref[...] = pltpu.einshape("rld->lrd", x_ref[...]).reshape(rows * DSA_RESIDUES, x_ref.shape[-1])


def _residue_major(x, *, rows, inverse):
    s, d = x.shape
    class_len = s // DSA_RESIDUES
    natural = pl.BlockSpec((rows * DSA_RESIDUES, d), lambda i: (i, 0))
    major = pl.BlockSpec((DSA_RESIDUES, rows, d), lambda i: (0, i, 0))
    if inverse:
        body, specs, arg, shape = (_from_residue_major_kernel, (major, natural),
                                   x.reshape(DSA_RESIDUES, class_len, d), (s, d))
    else:
        body, specs, arg, shape = _to_residue_major_kernel, (natural, major), x, (DSA_RESIDUES, class_len, d)
    out = pl.pallas_call(
        body,
        grid=(class_len // rows,),
        in_specs=[specs[0]],
        out_specs=specs[1],
        out_shape=jax.ShapeDtypeStruct(shape, x.dtype),
        compiler_params=_compiler_params(("parallel",)),
        name="from_residue_major" if inverse else "to_residue_major",
    )(arg)
    return out.reshape(s, d)


def _dsa_out_kernel(o0_ref, o1_ref, o2_ref, l0_ref, l1_ref, l2_ref, w_ref, g_ref, x_ref, out_ref, *, row_chunk):
    o_refs = [o0_ref, o1_ref, o2_ref]
    l_refs = [l0_ref, l1_ref, l2_ref]

    def rows(c):
        return pl.ds(c * row_chunk, row_chunk)

    def merged_rows(c):
        lses = [l[rows(c), :] for l in l_refs]
        top = jnp.maximum(jnp.maximum(lses[0], lses[1]), lses[2])
        es = [jnp.exp(l - top) for l in lses]
        inv_den = 1.0 / (es[0] + es[1] + es[2])
        wts = [(e * inv_den).astype(BF16) for e in es]
        heads = []
        for h in range(DSA_HEADS):
            hs = slice(h * DSA_DH, (h + 1) * DSA_DH)
            merged = wts[0][:, h:h + 1] * o_refs[0][rows(c), hs]
            for g in range(1, N_GROUPS):
                merged = merged + wts[g][:, h:h + 1] * o_refs[g][rows(c), hs]
            heads.append(merged)
        return jnp.concatenate(heads, axis=1)

    def finish(c, m):
        out = x_ref[rows(c), :] + _rms(m, g_ref[...])
        out_ref[rows(c), :] = out
        return out

    _overlapped_row_chunks(x_ref.shape[0] // row_chunk, merged_rows, w_ref, finish)


def _dsa_out(os_, lses, w, gain, x, layer, *, tm, row_chunk):
    s, k = os_[0].shape
    d = w.shape[2]
    o_spec = pl.BlockSpec((tm, k), lambda i: (i, 0))
    l_spec = pl.BlockSpec((tm, LANES), lambda i: (i, 0))
    return pl.pallas_call(
        functools.partial(_dsa_out_kernel, row_chunk=row_chunk),
        grid=(s // tm,),
        in_specs=[o_spec, o_spec, o_spec, l_spec, l_spec, l_spec,
                  pl.BlockSpec((None, k, d), lambda i: (layer, 0, 0)),
                  pl.BlockSpec((1, d), lambda i: (0, 0)),
                  pl.BlockSpec((tm, d), lambda i: (i, 0))],
        out_specs=pl.BlockSpec((tm, d), lambda i: (i, 0)),
        out_shape=jax.ShapeDtypeStruct((s, d), F32),
        compiler_params=_compiler_params(("parallel",)),
        name="dsa_merge_out",
    )(*os_, *lses, w, gain.reshape(1, d), x)


def _tiles(s):
    tm_proj = min(1024, s)
    eighth = tm_proj // 8
    return dict(tm_proj=tm_proj, tn_proj=2048, tn_gdn=1024, tm_out=min(512, s), out_chunk=128, tm_ffn=min(1024, s), tf_ffn=512, ffn_chunk=min(256, s), dsa_tiles=4,
                perm_rows=32, gdn_proj_chunks=(2 * eighth, 2 * eighth, 2 * eighth, eighth, eighth))


def kernel(x, norm_gains, ffn_w_gate_up, ffn_w_down, gdn_w_in, gdn_conv_w, gdn_a_log, gdn_dt_bias,
           gdn_out_norm, gdn_w_out, kv_norm, kv_w, dsa_w_q, dsa_w_out, rel_bias):
    b, s, d = x.shape
    assert b == 1 and d == D_MODEL and s % (DSA_SPAN * DSA_GROUPS[-1][1]) == 0
    t = _tiles(s)
    depth = norm_gains.shape[0]
    n_a = gdn_w_in.shape[0]
    xs = x.reshape(s, d)
    kv = None
    ffn_w_gate_up, ffn_w_down = ffn_w_gate_up.astype(BF16), ffn_w_down.astype(BF16)
    gdn_w_in, gdn_w_out = gdn_w_in.astype(BF16), gdn_w_out.astype(BF16)
    dsa_w_q, dsa_w_out = dsa_w_q.astype(BF16), dsa_w_out.astype(BF16)
    for layer in range(depth):
        gains = norm_gains[layer]
        if layer < n_a:
            proj, gb = _gdn_in_proj(xs, gains[0], gdn_w_in, gdn_conv_w, layer, gdn_a_log[layer], gdn_dt_bias[layer],
                                    tm=t["tm_proj"], tn=t["tn_gdn"], row_chunks=t["gdn_proj_chunks"])
            o = _gdn(proj, gb, gdn_out_norm[layer])
            xs = _matmul_norm_res(o, gdn_w_out, gains[1], xs, layer, tm=t["tm_out"], row_chunk=t["out_chunk"])
        else:
            j = layer - n_a
            q_all = _norm_matmul(xs, gains[0], dsa_w_q, j, tm=t["tm_proj"], tn=t["tn_proj"], out_dtype=BF16,
                                 out_scale=DSA_DH ** -0.5 * LOG2_E)
            outs = [_dsa_group(q_all, kv, rel_bias, g, tiles_per_step=t["dsa_tiles"]) for g in range(N_GROUPS)]
            xs = _dsa_out([o for o, _ in outs], [l for _, l in outs], dsa_w_out, gains[1], xs, j,
                          tm=t["tm_out"], row_chunk=t["out_chunk"])
        xs = _ffn(xs, gains[2], ffn_w_gate_up, ffn_w_down, gains[3], layer, tm=t["tm_ffn"], tf=t["tf_ffn"],
                  row_chunk=t["ffn_chunk"])
        if layer == n_a - 1 and depth > n_a:
            xs = _residue_major(xs, rows=t["perm_rows"], inverse=False)
            kv = _norm_matmul(xs, kv_norm, kv_w.astype(BF16)[None], 0, tm=t["tm_proj"], tn=t["tn_proj"],
                              out_dtype=BF16)
    if depth > n_a:
        xs = _residue_major(xs, rows=t["perm_rows"], inverse=True)
    return xs.reshape(b, s, d)
```

```python
import functools
import math

import jax
import jax.numpy as jnp
from jax import lax
from jax.experimental import pallas as pl
from jax.experimental.pallas import tpu as pltpu

F32 = jnp.float32
BF16 = jnp.bfloat16

RMS_EPS = 1e-6
D_MODEL = 2048

GDN_HEADS = 16
GDN_DK = 128
GDN_DV = 128
GDN_CONV = 4
GDN_QK_W = GDN_HEADS * GDN_DK
GDN_V_W = GDN_HEADS * GDN_DV
GDN_CONV_W = 2 * GDN_QK_W + GDN_V_W
GDN_CHUNK = 128
GDN_INV_BASE = 8
CONV_HALO_ROWS = 8

DSA_GROUPS = ((128, 1), (512, 4), (2048, 16))
N_GROUPS = len(DSA_GROUPS)
DSA_HEADS = 16
DSA_DH = 128
DSA_SPAN = 128
DSA_GROUP_W = DSA_HEADS * DSA_DH
NUM_BUCKETS = 32
MAX_DISTANCE = 2048
MASK_VALUE = -1e30
LOG2_E = math.log2(math.e)
LN_2 = math.log(2.0)

DSA_RESIDUES = DSA_GROUPS[-1][1]

LANES = 128
BF16_SUBLANES = 16
VMEM_BYTES = 64 * 1024 * 1024
VMEM_LIMIT_BYTES = VMEM_BYTES * 7 // 8
FFN_VMEM_LIMIT_BYTES = VMEM_BYTES * 31 // 32


def _compiler_params(semantics, vmem_limit_bytes=VMEM_LIMIT_BYTES):
    return pltpu.CompilerParams(dimension_semantics=semantics, vmem_limit_bytes=vmem_limit_bytes)


def _rms(x, gain):
    ms = jnp.mean(x * x, axis=-1, keepdims=True)
    return x * lax.rsqrt(ms + RMS_EPS) * gain


def _sigmoid(x):
    return 1.0 / (1.0 + jnp.exp(-x))


def _silu(x):
    h = 0.5 * x
    return h + h * jnp.tanh(h)


def _bdot(a, b):
    return jnp.dot(a.astype(BF16), b.astype(BF16), preferred_element_type=F32)


def _bdot_nt(a, b):
    return lax.dot_general(a.astype(BF16), b.astype(BF16), (((1,), (1,)), ((), ())),
                           preferred_element_type=F32)


def _bdot_tn(a, b):
    return lax.dot_general(a.astype(BF16), b.astype(BF16), (((0,), (0,)), ((), ())),
                           preferred_element_type=F32)


def _norm_matmul_kernel(x_ref, g_ref, w_ref, o_ref, xn_ref, *, out_scale):
    @pl.when(pl.program_id(1) == 0)
    def _():
        xn_ref[...] = _rms(x_ref[...], g_ref[...]).astype(BF16)

    y = jnp.dot(xn_ref[...], w_ref[...], preferred_element_type=F32)
    if out_scale != 1.0:
        y = y * out_scale
    o_ref[...] = y.astype(o_ref.dtype)


def _norm_matmul(x, gain, w, layer, *, tm, tn, out_dtype, out_scale=1.0):
    s, d = x.shape
    n = w.shape[2]
    return pl.pallas_call(
        functools.partial(_norm_matmul_kernel, out_scale=out_scale),
        grid=(s // tm, n // tn),
        in_specs=[pl.BlockSpec((tm, d), lambda i, j: (i, 0)),
                  pl.BlockSpec((1, d), lambda i, j: (0, 0)),
                  pl.BlockSpec((None, d, tn), lambda i, j: (layer, 0, j))],
        out_specs=pl.BlockSpec((tm, tn), lambda i, j: (i, j)),
        out_shape=jax.ShapeDtypeStruct((s, n), out_dtype),
        scratch_shapes=[pltpu.VMEM((tm, d), BF16)],
        compiler_params=_compiler_params(("parallel", "arbitrary")),
        name="norm_matmul",
    )(x, gain.reshape(1, d), w)


def _ordered_after(x, anchor):
    rows, lanes = BF16_SUBLANES, LANES
    a = lax.bitcast_convert_type(jnp.abs(anchor[:rows, :lanes].astype(F32)), jnp.int32)
    zero = lax.shift_right_logical(a, 31)
    tile = lax.bitcast_convert_type(x[:rows, :lanes].astype(F32), jnp.int32) | zero
    tile = lax.bitcast_convert_type(tile, F32).astype(x.dtype)
    top = jnp.concatenate([tile, x[:rows, lanes:]], axis=1)
    return jnp.concatenate([top, x[rows:, :]], axis=0)


def _gdn_in_proj_kernel(x_ref, g_ref, w_ref, cw_ref, wba_ref, alog_ref, dtb_ref, o_ref, gb_ref,
                        xn_ref, halo_ref, *, qk_blocks, v_blocks, row_chunks):
    i = pl.program_id(0)
    j = pl.program_id(1)
    tm = x_ref.shape[0]
    tn = w_ref.shape[1]

    @pl.when(j == 0)
    def _():
        xn = _rms(x_ref[...], g_ref[...]).astype(BF16)
        xn_ref[...] = xn
        gb_ref[...] = _gdn_gate_values(jnp.dot(xn, wba_ref[...], preferred_element_type=F32),
                                       alog_ref[...], dtb_ref[...])

    @pl.when((i == 0) & (j == 0))
    def _():
        halo_ref[...] = jnp.zeros_like(halo_ref)

    chunks = []
    for rows in row_chunks:
        chunks.append((sum(r for _, r in chunks), rows))
    assert sum(row_chunks) == tm
    n_chunks = len(chunks)

    def conv_silu(raw, halo):
        rows = raw.shape[0]
        cw = cw_ref[...]
        groups = jnp.concatenate([halo, raw], axis=0).reshape(rows // CONV_HALO_ROWS + 1, CONV_HALO_ROWS, tn)
        in_group = lax.broadcasted_iota(jnp.int32, (rows // CONV_HALO_ROWS, CONV_HALO_ROWS, tn), 1)
        y = cw[GDN_CONV - 1:GDN_CONV, :] * raw
        for t in range(GDN_CONV - 1):
            back = GDN_CONV - 1 - t
            rot = pltpu.roll(groups, back, axis=1)
            shifted = jnp.where(in_group < back, rot[:-1], rot[1:]).reshape(rows, tn)
            y = y + cw[t:t + 1, :] * shifted
        return _silu(y)

    def finish_qk(raw, halo):
        y = conv_silu(raw, halo)
        c = jnp.where(j < qk_blocks, float(GDN_DK), 1.0).astype(F32)
        out = []
        for h in range(tn // GDN_DK):
            yh = y[:, h * GDN_DK:(h + 1) * GDN_DK]
            ss = jnp.sum(yh * yh, axis=-1, keepdims=True)
            out.append((yh * lax.rsqrt(ss * c + RMS_EPS * c)).astype(o_ref.dtype))
        return jnp.concatenate(out, axis=1)

    def finish_v(raw, halo):
        return conv_silu(raw, halo).astype(o_ref.dtype)

    def finish_z(raw, halo):
        return _silu(raw).astype(o_ref.dtype)

    def run(finish, conv):
        done = []
        prev = None
        halo = jnp.where(i > 0, halo_ref[j], 0.0) if conv else None
        for c in range(n_chunks + 1):
            cur = None
            if c < n_chunks:
                lhs = xn_ref[pl.ds(*chunks[c]), :]
                if c >= 2:
                    lhs = _ordered_after(lhs, done[c - 2])
                cur = jnp.dot(lhs, w_ref[...], preferred_element_type=F32)
            if c >= 1:
                out = finish(prev, halo)
                o_ref[pl.ds(*chunks[c - 1]), :] = out
                done.append(out)
                halo = prev[prev.shape[0] - CONV_HALO_ROWS:, :]
            prev = cur
        if conv:
            halo_ref[j] = halo

    @pl.when(j < 2 * qk_blocks)
    def _():
        run(finish_qk, True)

    @pl.when((j >= 2 * qk_blocks) & (j < 2 * qk_blocks + v_blocks))
    def _():
        run(finish_v, True)

    @pl.when(j >= 2 * qk_blocks + v_blocks)
    def _():
        run(finish_z, False)


def _gdn_in_proj(x, gain, w_in, conv_w, layer, a_log, dt_bias, *, tm, tn, row_chunks):
    s, d = x.shape
    n = GDN_CONV_W + GDN_V_W
    assert n % LANES == 0 and w_in.shape[2] - n == 2 * GDN_HEADS
    conv_blocks = GDN_CONV_W // tn
    alog, dtb = _gdn_gate_params(a_log, dt_bias)
    lane_row = pl.BlockSpec((1, LANES), lambda i, j: (0, 0))
    return pl.pallas_call(
        functools.partial(_gdn_in_proj_kernel, qk_blocks=GDN_QK_W // tn, v_blocks=GDN_V_W // tn,
                          row_chunks=row_chunks),
        grid=(s // tm, n // tn),
        in_specs=[pl.BlockSpec((tm, d), lambda i, j: (i, 0)),
                  pl.BlockSpec((1, d), lambda i, j: (0, 0)),
                  pl.BlockSpec((None, d, tn), lambda i, j: (layer, 0, j)),
                  pl.BlockSpec((None, GDN_CONV, tn), lambda i, j: (layer, 0, jnp.minimum(j, conv_blocks - 1))),
                  pl.BlockSpec((None, d, LANES), lambda i, j: (layer, 0, n // LANES)), lane_row, lane_row],
        out_specs=[pl.BlockSpec((tm, tn), lambda i, j: (i, j)),
                   pl.BlockSpec((tm, LANES), lambda i, j: (i, 0))],
        out_shape=[jax.ShapeDtypeStruct((s, n), BF16), jax.ShapeDtypeStruct((s, LANES), F32)],
        scratch_shapes=[pltpu.VMEM((tm, d), BF16),
                        pltpu.VMEM((conv_blocks, CONV_HALO_ROWS, tn), F32)],
        compiler_params=_compiler_params(("arbitrary", "arbitrary")),
        name="gdn_in_proj",
    )(x, gain.reshape(1, d), w_in, conv_w, w_in, alog, dtb)


def _gdn_gate_values(p, alog, dtb):
    lane = lax.broadcasted_iota(jnp.int32, p.shape, 1)
    beta = _sigmoid(p)
    a = p + dtb
    softplus = jnp.maximum(a, 0.0) + jnp.log(1.0 + jnp.exp(-jnp.abs(a)))
    g = -jnp.exp(alog) * softplus
    return jnp.where(lane < GDN_HEADS, beta, jnp.where(lane < 2 * GDN_HEADS, g, 0.0))


def _gdn_gate_params(a_log, dt_bias):
    pad = LANES - 2 * GDN_HEADS
    zeros_h = jnp.zeros((GDN_HEADS,), F32)
    alog = jnp.pad(jnp.concatenate([zeros_h, a_log.astype(F32)]), (0, pad)).reshape(1, LANES)
    dtb = jnp.pad(jnp.concatenate([zeros_h, dt_bias.astype(F32)]), (0, pad)).reshape(1, LANES)
    return alog, dtb


def _unit_lower_inverse(lows, row, col):
    c = lows[0].shape[0]
    eye = (row == col).astype(F32)
    blk = GDN_INV_BASE
    diag_mask = (row // blk) == (col // blk)
    powers = [jnp.where(diag_mask, low, 0.0) for low in lows]
    invs = [eye - p for p in powers]
    span = 2
    while span < blk:
        powers = [_bdot(p, p) for p in powers]
        invs = [x + _bdot(x, p) for x, p in zip(invs, powers)]
        span *= 2
    while blk < c:
        off_mask = ((row // (2 * blk)) == (col // (2 * blk))) & ((row // blk) != (col // blk))
        offs = [_bdot(jnp.where(off_mask, low, 0.0), x) for low, x in zip(lows, invs)]
        invs = [x - _bdot(x, t) for x, t in zip(invs, offs)]
        blk *= 2
    return invs


def _gdn_kernel(q_ref, k_ref, v_ref, z_ref, gb_ref, on_ref, o_ref, state_ref):
    c_len = q_ref.shape[0]
    step = pl.program_id(1)

    @pl.when(step == 0)
    def _():
        state_ref[...] = jnp.zeros_like(state_ref)

    row = lax.broadcasted_iota(jnp.int32, (c_len, c_len), 0)
    col = lax.broadcasted_iota(jnp.int32, (c_len, c_len), 1)
    causal = row >= col
    strict = row > col

    gb = gb_ref[...]
    tri = causal.astype(BF16)
    gb_hi = gb.astype(BF16)
    gb_lo = (gb - gb_hi.astype(F32)).astype(BF16)
    gcum = (jnp.dot(tri, gb_hi, preferred_element_type=F32)
            + jnp.dot(tri, gb_lo, preferred_element_type=F32))
    gcum_t = gcum.T

    heads = range(GDN_HEADS)

    def head_slices(ref):
        return [ref[:, i * GDN_DK:(i + 1) * GDN_DK].astype(F32) for i in heads]

    qn, kn, v_all = head_slices(q_ref), head_slices(k_ref), head_slices(v_ref)
    beta = [gb[:, i:i + 1] for i in heads]
    gcol = [gcum[:, GDN_HEADS + i:GDN_HEADS + i + 1] for i in heads]
    grow = [gcum_t[GDN_HEADS + i:GDN_HEADS + i + 1, :] for i in heads]
    decay = [jnp.exp(jnp.where(causal, gc - gr, MASK_VALUE)) for gc, gr in zip(gcol, grow)]
    e_col = [jnp.exp(gc) for gc in gcol]
    g_last = [gc[c_len - 1:c_len, :] for gc in gcol]
    kb = [k * b for k, b in zip(kn, beta)]
    kk_qk = [_bdot_nt(jnp.concatenate([kb[i], qn[i]], axis=0), kn[i]) for i in heads]
    low = [jnp.where(strict, kk_qk[i][:c_len] * decay[i], 0.0) for i in heads]
    intra = [kk_qk[i][c_len:] * decay[i] for i in heads]
    inv = _unit_lower_inverse(low, row, col)
    sol = [_bdot(inv[i], jnp.concatenate([v_all[i] * beta[i], kb[i] * e_col[i]], axis=1)) for i in heads]
    state = [state_ref[i] for i in heads]
    ws_qs = [_bdot(jnp.concatenate([sol[i][:, GDN_DV:], qn[i] * e_col[i]], axis=0), state[i]) for i in heads]
    v_new = [sol[i][:, :GDN_DV] - ws_qs[i][:c_len] for i in heads]
    o = [ws_qs[i][c_len:] + _bdot(intra[i], v_new[i]) for i in heads]
    for i in heads:
        k_dec = kn[i] * jnp.exp(g_last[i] - gcol[i])
        state_ref[i] = state[i] * jnp.exp(g_last[i]) + _bdot_tn(k_dec, v_new[i])
    out_gain = on_ref[...]
    for i in heads:
        hs = slice(i * GDN_DV, (i + 1) * GDN_DV)
        o_ref[:, hs] = (_rms(o[i], out_gain) * z_ref[:, hs].astype(F32)).astype(o_ref.dtype)


def _gdn(proj, gb, out_norm):
    s = proj.shape[0]
    c = GDN_CHUNK
    hb = GDN_HEADS
    ng = GDN_HEADS // hb
    bw = hb * GDN_DK

    def cur(sec):
        return pl.BlockSpec((c, bw), lambda g, t, sec=sec: (t, sec * ng + g))

    return pl.pallas_call(
        _gdn_kernel,
        grid=(ng, s // c),
        in_specs=[cur(0), cur(1), cur(2), cur(3),
                  pl.BlockSpec((c, LANES), lambda g, t: (t, g)),
                  pl.BlockSpec((1, GDN_DV), lambda g, t: (0, 0))],
        out_specs=pl.BlockSpec((c, bw), lambda g, t: (t, g)),
        out_shape=jax.ShapeDtypeStruct((s, GDN_V_W), BF16),
        scratch_shapes=[pltpu.VMEM((hb, GDN_DK, GDN_DV), F32)],
        compiler_params=_compiler_params(("parallel", "arbitrary")),
        name="gdn_delta_rule",
    )(proj, proj, proj, proj, gb, out_norm.reshape(1, GDN_DV))


def _overlapped_row_chunks(n_chunks, lhs_rows, w_ref, finish_rows):
    done = []
    prev = None
    for c in range(n_chunks + 1):
        cur = None
        if c < n_chunks:
            lhs = lhs_rows(c)
            if c >= 2:
                lhs = _ordered_after(lhs, done[c - 2])
            cur = jnp.dot(lhs, w_ref[...], preferred_element_type=F32)
        if c >= 1:
            done.append(finish_rows(c - 1, prev))
        prev = cur


def _matmul_norm_res_kernel(a_ref, w_ref, g_ref, x_ref, o_ref, *, row_chunk):
    def rows(c):
        return pl.ds(c * row_chunk, row_chunk)

    def finish(c, m):
        out = x_ref[rows(c), :] + _rms(m, g_ref[...])
        o_ref[rows(c), :] = out
        return out

    _overlapped_row_chunks(a_ref.shape[0] // row_chunk, lambda c: a_ref[rows(c), :], w_ref, finish)


def _matmul_norm_res(a, w, gain, x, layer, *, tm, row_chunk):
    s, k = a.shape
    d = w.shape[2]
    return pl.pallas_call(
        functools.partial(_matmul_norm_res_kernel, row_chunk=row_chunk),
        grid=(s // tm,),
        in_specs=[pl.BlockSpec((tm, k), lambda i: (i, 0)),
                  pl.BlockSpec((None, k, d), lambda i: (layer, 0, 0)),
                  pl.BlockSpec((1, d), lambda i: (0, 0)),
                  pl.BlockSpec((tm, d), lambda i: (i, 0))],
        out_specs=pl.BlockSpec((tm, d), lambda i: (i, 0)),
        out_shape=jax.ShapeDtypeStruct((s, d), F32),
        compiler_params=_compiler_params(("parallel",)),
        name="matmul_norm_res",
    )(a, w, gain.reshape(1, d), x)


def _ffn_kernel(x_ref, gpre_ref, wg_ref, wu_ref, wd_ref, gpost_ref, o_ref, xn_ref, *, row_chunk):
    j = pl.program_id(1)
    last = pl.num_programs(1) - 1

    @pl.when(j == 0)
    def _():
        xn_ref[...] = _rms(x_ref[...], gpre_ref[...]).astype(BF16)
        o_ref[...] = jnp.zeros_like(o_ref)

    def activations():
        xn = xn_ref[...]
        gate = jnp.dot(xn, wg_ref[...], preferred_element_type=F32)
        up = jnp.dot(xn, wu_ref[...], preferred_element_type=F32)
        return (_silu(gate) * up).astype(BF16)

    @pl.when(j < last)
    def _():
        o_ref[...] += jnp.dot(activations(), wd_ref[...], preferred_element_type=F32)

    @pl.when(j == last)
    def _():
        act = activations()

        def rows(c):
            return pl.ds(c * row_chunk, row_chunk)

        def finish(c, m):
            out = x_ref[rows(c), :] + _rms(o_ref[rows(c), :] + m, gpost_ref[...])
            o_ref[rows(c), :] = out
            return out

        _overlapped_row_chunks(o_ref.shape[0] // row_chunk,
                               lambda c: act[c * row_chunk:(c + 1) * row_chunk, :], wd_ref, finish)


def _ffn(x, g_pre, w_gate_up, w_down, g_post, layer, *, tm, tf, row_chunk):
    s, d = x.shape
    d_ff = w_down.shape[1]
    nf = d_ff // tf
    return pl.pallas_call(
        functools.partial(_ffn_kernel, row_chunk=row_chunk),
        grid=(s // tm, nf),
        in_specs=[pl.BlockSpec((tm, d), lambda i, j: (i, 0)),
                  pl.BlockSpec((1, d), lambda i, j: (0, 0)),
                  pl.BlockSpec((None, d, tf), lambda i, j: (layer, 0, j)),
                  pl.BlockSpec((None, d, tf), lambda i, j: (layer, 0, nf + j)),
                  pl.BlockSpec((None, tf, d), lambda i, j: (layer, j, 0)),
                  pl.BlockSpec((1, d), lambda i, j: (0, 0))],
        out_specs=pl.BlockSpec((tm, d), lambda i, j: (i, 0)),
        out_shape=jax.ShapeDtypeStruct((s, d), F32),
        scratch_shapes=[pltpu.VMEM((tm, d), BF16)],
        compiler_params=_compiler_params(("parallel", "arbitrary"), FFN_VMEM_LIMIT_BYTES),
        name="swiglu_ffn",
    )(x, g_pre.reshape(1, d), w_gate_up, w_gate_up, w_down, g_post.reshape(1, d))


def _t5_bucket(dist):
    max_exact = NUM_BUCKETS // 2
    d_f = jnp.maximum(dist, 1).astype(F32)
    large = max_exact + (jnp.log(d_f / max_exact) / math.log(MAX_DISTANCE / max_exact)
                         * (NUM_BUCKETS - max_exact)).astype(jnp.int32)
    large = jnp.minimum(large, NUM_BUCKETS - 1)
    return jnp.where(dist < max_exact, dist, large)


def _tile_offsets(group):
    w = DSA_SPAN
    classes = DSA_RESIDUES // DSA_GROUPS[group][1]
    rows = w // classes
    i = jnp.arange(w)
    return (i % rows) * classes + i // rows


def _band_buckets(group):
    w = DSA_SPAN
    off = _tile_offsets(group)
    rel = off[:, None] + w - jnp.concatenate([off, off + w])[None, :]
    band = (rel >= 0) & (rel <= w)
    return jnp.where(band, _t5_bucket(jnp.clip(rel, 0, w) * DSA_GROUPS[group][1]), -1).astype(jnp.int32)


def _dsa_min_tiles(group):
    classes = DSA_RESIDUES // DSA_GROUPS[group][1]
    return max(1, BF16_SUBLANES * classes // DSA_SPAN)


def _dsa_kernel(tab_ref, bucket_ref, q_ref, kp_ref, kc_ref, vp_ref, vc_ref, o_ref, lse_ref, bias_ref, *, group):
    w = DSA_SPAN
    classes = q_ref.shape[0]
    rows = w // classes
    n_tiles = q_ref.shape[1] // rows
    step = pl.program_id(1)

    @pl.when((pl.program_id(0) == 0) & (step == 0))
    def _():
        bucket = bucket_ref[...]
        key_col = lax.broadcasted_iota(jnp.int32, (w, 2 * w), 1)

        def per_head(h, carry):
            b = jnp.full(bucket.shape, MASK_VALUE, F32)
            for t in range(NUM_BUCKETS):
                b = jnp.where(bucket == t, tab_ref[t, group * DSA_HEADS + h] * LOG2_E, b)
            bias_ref[0, h] = b
            bias_ref[1, h] = jnp.where(key_col < w, MASK_VALUE, b)
            return carry

        lax.fori_loop(0, DSA_HEADS, per_head, 0)

    def tiles(ref, cols):
        blk = ref[:, :, cols]
        if rows % BF16_SUBLANES:
            blk = blk.astype(F32)
        return [blk[:, t * rows:(t + 1) * rows, :].reshape(w, blk.shape[-1]).astype(ref.dtype)
                for t in range(ref.shape[1] // rows)]

    def store(ref, cols, parts):
        parts = [p.reshape(classes, rows, p.shape[-1]) for p in parts]
        ref[:, :, cols] = jnp.concatenate(parts, axis=1).astype(ref.dtype)

    first = jnp.where(step == 0, 1, 0)
    lane = lax.broadcasted_iota(jnp.int32, (w, LANES), 1)
    lse_all = [jnp.zeros((w, LANES), F32) for _ in range(n_tiles)]
    ones = jnp.ones((2 * w, DSA_DH), BF16)
    for h in range(DSA_HEADS):
        hs = slice(h * DSA_DH, (h + 1) * DSA_DH)
        q_t = tiles(q_ref, hs)
        k_t = tiles(kc_ref, hs)
        v_t = tiles(vc_ref, hs)
        k_prev = [tiles(kp_ref, hs)[-1]] + k_t[:-1]
        v_prev = [tiles(vp_ref, hs)[-1]] + v_t[:-1]
        outs = []
        for t in range(n_tiles):
            k = jnp.concatenate([k_prev[t], k_t[t]], axis=0)
            v = jnp.concatenate([v_prev[t], v_t[t]], axis=0)
            s = _bdot_nt(q_t[t], k) + bias_ref[first if t == 0 else 0, h]
            m = jnp.max(s, axis=-1, keepdims=True)
            p = jnp.exp2(s - m)
            acc = _bdot(p, jnp.concatenate([v, ones], axis=1))
            l = acc[:, DSA_DH:]
            outs.append(acc[:, :DSA_DH] / l)
            lse_all[t] = jnp.where(lane == h, (m + jnp.log2(l)) * LN_2, lse_all[t])
        store(o_ref, hs, outs)
    store(lse_ref, slice(None), lse_all)


def _dsa_group(q_all, kv, rel_bias, group, *, tiles_per_step):
    s = q_all.shape[0]
    dilation = DSA_GROUPS[group][1]
    w = DSA_SPAN
    gw = DSA_GROUP_W
    classes = DSA_RESIDUES // dilation
    class_len = s // DSA_RESIDUES
    rows = w // classes
    prev_tiles = _dsa_min_tiles(group)
    n_tiles = max(prev_tiles, min(tiles_per_step, class_len // rows))
    assert n_tiles % prev_tiles == 0 and class_len % (n_tiles * rows) == 0

    def view(a):
        return a.reshape(classes, dilation, class_len, a.shape[-1])

    def cur(width, col):
        return pl.BlockSpec((classes, None, n_tiles * rows, width), lambda r, n: (0, r, n, col))

    def prev(width, col):
        ratio = n_tiles // prev_tiles
        return pl.BlockSpec((classes, None, prev_tiles * rows, width),
                            lambda r, n: (0, r, jnp.maximum(n * ratio - 1, 0), col))

    o, lse = pl.pallas_call(
        functools.partial(_dsa_kernel, group=group),
        grid=(dilation, class_len // (n_tiles * rows)),
        in_specs=[pl.BlockSpec(memory_space=pltpu.SMEM),
                  pl.BlockSpec((w, 2 * w), lambda r, n: (0, 0)),
                  cur(gw, group),
                  prev(gw, group), cur(gw, group),
                  prev(gw, N_GROUPS + group), cur(gw, N_GROUPS + group)],
        out_specs=[cur(gw, 0), cur(LANES, 0)],
        out_shape=[jax.ShapeDtypeStruct((classes, dilation, class_len, gw), BF16),
                   jax.ShapeDtypeStruct((classes, dilation, class_len, LANES), F32)],
        scratch_shapes=[pltpu.VMEM((2, DSA_HEADS, w, 2 * w), F32)],
        compiler_params=_compiler_params(("arbitrary", "arbitrary")),
        name=f"dsa_attention_g{group}",
    )(rel_bias, _band_buckets(group), view(q_all), view(kv), view(kv), view(kv), view(kv))
    return o.reshape(s, gw), lse.reshape(s, LANES)


def _to_residue_major_kernel(x_ref, o_ref):
    rows = o_ref.shape[1]
    x = x_ref[...].reshape(rows, DSA_RESIDUES, x_ref.shape[-1])
    o_ref[...] = pltpu.einshape("lrd->rld", x)


def _from_residue_major_kernel(x_ref, o_ref):
    rows = x_ref.shape[1]
    o_ref[...] = pltpu.einshape("rld->lrd", x_ref[...]).reshape(rows * DSA_RESIDUES, x_ref.shape[-1])


def _residue_major(x, *, rows, inverse):
    s, d = x.shape
    class_len = s // DSA_RESIDUES
    natural = pl.BlockSpec((rows * DSA_RESIDUES, d), lambda i: (i, 0))
    major = pl.BlockSpec((DSA_RESIDUES, rows, d), lambda i: (0, i, 0))
    if inverse:
        body, specs, arg, shape = (_from_residue_major_kernel, (major, natural),
                                   x.reshape(DSA_RESIDUES, class_len, d), (s, d))
    else:
        body, specs, arg, shape = _to_residue_major_kernel, (natural, major), x, (DSA_RESIDUES, class_len, d)
    out = pl.pallas_call(
        body,
        grid=(class_len // rows,),
        in_specs=[specs[0]],
        out_specs=specs[1],
        out_shape=jax.ShapeDtypeStruct(shape, x.dtype),
        compiler_params=_compiler_params(("parallel",)),
        name="from_residue_major" if inverse else "to_residue_major",
    )(arg)
    return out.reshape(s, d)


def _dsa_out_kernel(o0_ref, o1_ref, o2_ref, l0_ref, l1_ref, l2_ref, w_ref, g_ref, x_ref, out_ref, *, row_chunk):
    o_refs = [o0_ref, o1_ref, o2_ref]
    l_refs = [l0_ref, l1_ref, l2_ref]

    def rows(c):
        return pl.ds(c * row_chunk, row_chunk)

    def merged_rows(c):
        lses = [l[rows(c), :] for l in l_refs]
        top = jnp.maximum(jnp.maximum(lses[0], lses[1]), lses[2])
        es = [jnp.exp(l - top) for l in lses]
        inv_den = 1.0 / (es[0] + es[1] + es[2])
        wts = [(e * inv_den).astype(BF16) for e in es]
        heads = []
        for h in range(DSA_HEADS):
            hs = slice(h * DSA_DH, (h + 1) * DSA_DH)
            merged = wts[0][:, h:h + 1] * o_refs[0][rows(c), hs]
            for g in range(1, N_GROUPS):
                merged = merged + wts[g][:, h:h + 1] * o_refs[g][rows(c), hs]
            heads.append(merged)
        return jnp.concatenate(heads, axis=1)

    def finish(c, m):
        out = x_ref[rows(c), :] + _rms(m, g_ref[...])
        out_ref[rows(c), :] = out
        return out

    _overlapped_row_chunks(x_ref.shape[0] // row_chunk, merged_rows, w_ref, finish)


def _dsa_out(os_, lses, w, gain, x, layer, *, tm, row_chunk):
    s, k = os_[0].shape
    d = w.shape[2]
    o_spec = pl.BlockSpec((tm, k), lambda i: (i, 0))
    l_spec = pl.BlockSpec((tm, LANES), lambda i: (i, 0))
    return pl.pallas_call(
        functools.partial(_dsa_out_kernel, row_chunk=row_chunk),
        grid=(s // tm,),
        in_specs=[o_spec, o_spec, o_spec, l_spec, l_spec, l_spec,
                  pl.BlockSpec((None, k, d), lambda i: (layer, 0, 0)),
                  pl.BlockSpec((1, d), lambda i: (0, 0)),
                  pl.BlockSpec((tm, d), lambda i: (i, 0))],
        out_specs=pl.BlockSpec((tm, d), lambda i: (i, 0)),
        out_shape=jax.ShapeDtypeStruct((s, d), F32),
        compiler_params=_compiler_params(("parallel",)),
        name="dsa_merge_out",
    )(*os_, *lses, w, gain.reshape(1, d), x)


def _tiles(s):
    tm_proj = min(1024, s)
    eighth = tm_proj // 8
    return dict(tm_proj=tm_proj, tn_proj=2048, tn_gdn=1024, tm_out=min(512, s), out_chunk=128, tm_ffn=min(1024, s), tf_ffn=512, ffn_chunk=min(256, s), dsa_tiles=8,
                perm_rows=32, gdn_proj_chunks=(2 * eighth, 2 * eighth, 2 * eighth, eighth, eighth))


def kernel(x, norm_gains, ffn_w_gate_up, ffn_w_down, gdn_w_in, gdn_conv_w, gdn_a_log, gdn_dt_bias,
           gdn_out_norm, gdn_w_out, kv_norm, kv_w, dsa_w_q, dsa_w_out, rel_bias):
    b, s, d = x.shape
    assert b == 1 and d == D_MODEL and s % (DSA_SPAN * DSA_GROUPS[-1][1]) == 0
    t = _tiles(s)
    depth = norm_gains.shape[0]
    n_a = gdn_w_in.shape[0]
    xs = x.reshape(s, d)
    kv = None
    ffn_w_gate_up, ffn_w_down = ffn_w_gate_up.astype(BF16), ffn_w_down.astype(BF16)
    gdn_w_in, gdn_w_out = gdn_w_in.astype(BF16), gdn_w_out.astype(BF16)
    dsa_w_q, dsa_w_out = dsa_w_q.astype(BF16), dsa_w_out.astype(BF16)
    for layer in range(depth):
        gains = norm_gains[layer]
        if layer < n_a:
            proj, gb = _gdn_in_proj(xs, gains[0], gdn_w_in, gdn_conv_w, layer, gdn_a_log[layer], gdn_dt_bias[layer],
                                    tm=t["tm_proj"], tn=t["tn_gdn"], row_chunks=t["gdn_proj_chunks"])
            o = _gdn(proj, gb, gdn_out_norm[layer])
            xs = _matmul_norm_res(o, gdn_w_out, gains[1], xs, layer, tm=t["tm_out"], row_chunk=t["out_chunk"])
        else:
            j = layer - n_a
            q_all = _norm_matmul(xs, gains[0], dsa_w_q, j, tm=t["tm_proj"], tn=t["tn_proj"], out_dtype=BF16,
                                 out_scale=DSA_DH ** -0.5 * LOG2_E)
            outs = [_dsa_group(q_all, kv, rel_bias, g, tiles_per_step=t["dsa_tiles"]) for g in range(N_GROUPS)]
            xs = _dsa_out([o for o, _ in outs], [l for _, l in outs], dsa_w_out, gains[1], xs, j,
                          tm=t["tm_out"], row_chunk=t["out_chunk"])
        xs = _ffn(xs, gains[2], ffn_w_gate_up, ffn_w_down, gains[3], layer, tm=t["tm_ffn"], tf=t["tf_ffn"],
                  row_chunk=t["ffn_chunk"])
        if layer == n_a - 1 and depth > n_a:
            xs = _residue_major(xs, rows=t["perm_rows"], inverse=False)
            kv = _norm_matmul(xs, kv_norm, kv_w.astype(BF16)[None], 0, tm=t["tm_proj"], tn=t["tn_proj"],
                              out_dtype=BF16)
    if depth > n_a:
        xs = _residue_major(xs, rows=t["perm_rows"], inverse=True)
    return xs.reshape(b, s, d)
```

```python
import functools
import math

import jax
import jax.numpy as jnp
from jax import lax
from jax.experimental import pallas as pl
from jax.experimental.pallas import tpu as pltpu

F32 = jnp.float32
BF16 = jnp.bfloat16

RMS_EPS = 1e-6
D_MODEL = 2048

GDN_HEADS = 16
GDN_DK = 128
GDN_DV = 128
GDN_CONV = 4
GDN_QK_W = GDN_HEADS * GDN_DK
GDN_V_W = GDN_HEADS * GDN_DV
GDN_CONV_W = 2 * GDN_QK_W + GDN_V_W
GDN_CHUNK = 128
GDN_INV_BASE = 8
CONV_HALO_ROWS = 8

DSA_GROUPS = ((128, 1), (512, 4), (2048, 16))
N_GROUPS = len(DSA_GROUPS)
DSA_HEADS = 16
DSA_DH = 128
DSA_SPAN = 128
DSA_GROUP_W = DSA_HEADS * DSA_DH
NUM_BUCKETS = 32
MAX_DISTANCE = 2048
MASK_VALUE = -1e30
LOG2_E = math.log2(math.e)
LN_2 = math.log(2.0)

DSA_RESIDUES = DSA_GROUPS[-1][1]

LANES = 128
BF16_SUBLANES = 16
VMEM_BYTES = 64 * 1024 * 1024
VMEM_LIMIT_BYTES = VMEM_BYTES * 7 // 8
FFN_VMEM_LIMIT_BYTES = VMEM_BYTES * 31 // 32


def _compiler_params(semantics, vmem_limit_bytes=VMEM_LIMIT_BYTES):
    return pltpu.CompilerParams(dimension_semantics=semantics, vmem_limit_bytes=vmem_limit_bytes)


def _rms(x, gain):
    ms = jnp.mean(x * x, axis=-1, keepdims=True)
    return x * lax.rsqrt(ms + RMS_EPS) * gain


def _sigmoid(x):
    return 1.0 / (1.0 + jnp.exp(-x))


def _silu(x):
    h = 0.5 * x
    return h + h * jnp.tanh(h)


def _bdot(a, b):
    return jnp.dot(a.astype(BF16), b.astype(BF16), preferred_element_type=F32)


def _bdot_nt(a, b):
    return lax.dot_general(a.astype(BF16), b.astype(BF16), (((1,), (1,)), ((), ())),
                           preferred_element_type=F32)


def _bdot_tn(a, b):
    return lax.dot_general(a.astype(BF16), b.astype(BF16), (((0,), (0,)), ((), ())),
                           preferred_element_type=F32)


def _norm_matmul_kernel(x_ref, g_ref, w_ref, o_ref, xn_ref, *, out_scale):
    @pl.when(pl.program_id(1) == 0)
    def _():
        xn_ref[...] = _rms(x_ref[...], g_ref[...]).astype(BF16)

    y = jnp.dot(xn_ref[...], w_ref[...], preferred_element_type=F32)
    if out_scale != 1.0:
        y = y * out_scale
    o_ref[...] = y.astype(o_ref.dtype)


def _norm_matmul(x, gain, w, layer, *, tm, tn, out_dtype, out_scale=1.0):
    s, d = x.shape
    n = w.shape[2]
    return pl.pallas_call(
        functools.partial(_norm_matmul_kernel, out_scale=out_scale),
        grid=(s // tm, n // tn),
        in_specs=[pl.BlockSpec((tm, d), lambda i, j: (i, 0)),
                  pl.BlockSpec((1, d), lambda i, j: (0, 0)),
                  pl.BlockSpec((None, d, tn), lambda i, j: (layer, 0, j))],
        out_specs=pl.BlockSpec((tm, tn), lambda i, j: (i, j)),
        out_shape=jax.ShapeDtypeStruct((s, n), out_dtype),
        scratch_shapes=[pltpu.VMEM((tm, d), BF16)],
        compiler_params=_compiler_params(("parallel", "arbitrary")),
        name="norm_matmul",
    )(x, gain.reshape(1, d), w)


def _ordered_after(x, anchor):
    rows, lanes = BF16_SUBLANES, LANES
    a = lax.bitcast_convert_type(jnp.abs(anchor[:rows, :lanes].astype(F32)), jnp.int32)
    zero = lax.shift_right_logical(a, 31)
    tile = lax.bitcast_convert_type(x[:rows, :lanes].astype(F32), jnp.int32) | zero
    tile = lax.bitcast_convert_type(tile, F32).astype(x.dtype)
    top = jnp.concatenate([tile, x[:rows, lanes:]], axis=1)
    return jnp.concatenate([top, x[rows:, :]], axis=0)


def _gdn_in_proj_kernel(x_ref, g_ref, w_ref, cw_ref, wba_ref, alog_ref, dtb_ref, o_ref, gb_ref,
                        xn_ref, halo_ref, *, qk_blocks, v_blocks, row_chunks):
    i = pl.program_id(0)
    j = pl.program_id(1)
    tm = x_ref.shape[0]
    tn = w_ref.shape[1]

    @pl.when(j == 0)
    def _():
        xn = _rms(x_ref[...], g_ref[...]).astype(BF16)
        xn_ref[...] = xn
        gb_ref[...] = _gdn_gate_values(jnp.dot(xn, wba_ref[...], preferred_element_type=F32),
                                       alog_ref[...], dtb_ref[...])

    @pl.when((i == 0) & (j == 0))
    def _():
        halo_ref[...] = jnp.zeros_like(halo_ref)

    chunks = []
    for rows in row_chunks:
        chunks.append((sum(r for _, r in chunks), rows))
    assert sum(row_chunks) == tm
    n_chunks = len(chunks)

    def conv_silu(raw, halo):
        rows = raw.shape[0]
        cw = cw_ref[...]
        groups = jnp.concatenate([halo, raw], axis=0).reshape(rows // CONV_HALO_ROWS + 1, CONV_HALO_ROWS, tn)
        in_group = lax.broadcasted_iota(jnp.int32, (rows // CONV_HALO_ROWS, CONV_HALO_ROWS, tn), 1)
        y = cw[GDN_CONV - 1:GDN_CONV, :] * raw
        for t in range(GDN_CONV - 1):
            back = GDN_CONV - 1 - t
            rot = pltpu.roll(groups, back, axis=1)
            shifted = jnp.where(in_group < back, rot[:-1], rot[1:]).reshape(rows, tn)
            y = y + cw[t:t + 1, :] * shifted
        return _silu(y)

    def finish_qk(raw, halo):
        y = conv_silu(raw, halo)
        c = jnp.where(j < qk_blocks, float(GDN_DK), 1.0).astype(F32)
        out = []
        for h in range(tn // GDN_DK):
            yh = y[:, h * GDN_DK:(h + 1) * GDN_DK]
            ss = jnp.sum(yh * yh, axis=-1, keepdims=True)
            out.append((yh * lax.rsqrt(ss * c + RMS_EPS * c)).astype(o_ref.dtype))
        return jnp.concatenate(out, axis=1)

    def finish_v(raw, halo):
        return conv_silu(raw, halo).astype(o_ref.dtype)

    def finish_z(raw, halo):
        return _silu(raw).astype(o_ref.dtype)

    def run(finish, conv):
        done = []
        prev = None
        halo = jnp.where(i > 0, halo_ref[j], 0.0) if conv else None
        for c in range(n_chunks + 1):
            cur = None
            if c < n_chunks:
                lhs = xn_ref[pl.ds(*chunks[c]), :]
                if c >= 2:
                    lhs = _ordered_after(lhs, done[c - 2])
                cur = jnp.dot(lhs, w_ref[...], preferred_element_type=F32)
            if c >= 1:
                out = finish(prev, halo)
                o_ref[pl.ds(*chunks[c - 1]), :] = out
                done.append(out)
                halo = prev[prev.shape[0] - CONV_HALO_ROWS:, :]
            prev = cur
        if conv:
            halo_ref[j] = halo

    @pl.when(j < 2 * qk_blocks)
    def _():
        run(finish_qk, True)

    @pl.when((j >= 2 * qk_blocks) & (j < 2 * qk_blocks + v_blocks))
    def _():
        run(finish_v, True)

    @pl.when(j >= 2 * qk_blocks + v_blocks)
    def _():
        run(finish_z, False)


def _gdn_in_proj(x, gain, w_in, conv_w, layer, a_log, dt_bias, *, tm, tn, row_chunks):
    s, d = x.shape
    n = GDN_CONV_W + GDN_V_W
    assert n % LANES == 0 and w_in.shape[2] - n == 2 * GDN_HEADS
    conv_blocks = GDN_CONV_W // tn
    alog, dtb = _gdn_gate_params(a_log, dt_bias)
    lane_row = pl.BlockSpec((1, LANES), lambda i, j: (0, 0))
    return pl.pallas_call(
        functools.partial(_gdn_in_proj_kernel, qk_blocks=GDN_QK_W // tn, v_blocks=GDN_V_W // tn,
                          row_chunks=row_chunks),
        grid=(s // tm, n // tn),
        in_specs=[pl.BlockSpec((tm, d), lambda i, j: (i, 0)),
                  pl.BlockSpec((1, d), lambda i, j: (0, 0)),
                  pl.BlockSpec((None, d, tn), lambda i, j: (layer, 0, j)),
                  pl.BlockSpec((None, GDN_CONV, tn), lambda i, j: (layer, 0, jnp.minimum(j, conv_blocks - 1))),
                  pl.BlockSpec((None, d, LANES), lambda i, j: (layer, 0, n // LANES)), lane_row, lane_row],
        out_specs=[pl.BlockSpec((tm, tn), lambda i, j: (i, j)),
                   pl.BlockSpec((tm, LANES), lambda i, j: (i, 0))],
        out_shape=[jax.ShapeDtypeStruct((s, n), BF16), jax.ShapeDtypeStruct((s, LANES), F32)],
        scratch_shapes=[pltpu.VMEM((tm, d), BF16),
                        pltpu.VMEM((conv_blocks, CONV_HALO_ROWS, tn), F32)],
        compiler_params=_compiler_params(("arbitrary", "arbitrary")),
        name="gdn_in_proj",
    )(x, gain.reshape(1, d), w_in, conv_w, w_in, alog, dtb)


def _gdn_gate_values(p, alog, dtb):
    lane = lax.broadcasted_iota(jnp.int32, p.shape, 1)
    beta = _sigmoid(p)
    a = p + dtb
    softplus = jnp.maximum(a, 0.0) + jnp.log(1.0 + jnp.exp(-jnp.abs(a)))
    g = -jnp.exp(alog) * softplus
    return jnp.where(lane < GDN_HEADS, beta, jnp.where(lane < 2 * GDN_HEADS, g, 0.0))


def _gdn_gate_params(a_log, dt_bias):
    pad = LANES - 2 * GDN_HEADS
    zeros_h = jnp.zeros((GDN_HEADS,), F32)
    alog = jnp.pad(jnp.concatenate([zeros_h, a_log.astype(F32)]), (0, pad)).reshape(1, LANES)
    dtb = jnp.pad(jnp.concatenate([zeros_h, dt_bias.astype(F32)]), (0, pad)).reshape(1, LANES)
    return alog, dtb


def _unit_lower_inverse(lows, row, col):
    c = lows[0].shape[0]
    eye = (row == col).astype(F32)
    blk = GDN_INV_BASE
    diag_mask = (row // blk) == (col // blk)
    powers = [jnp.where(diag_mask, low, 0.0) for low in lows]
    invs = [eye - p for p in powers]
    span = 2
    while span < blk:
        powers = [_bdot(p, p) for p in powers]
        invs = [x + _bdot(x, p) for x, p in zip(invs, powers)]
        span *= 2
    while blk < c:
        off_mask = ((row // (2 * blk)) == (col // (2 * blk))) & ((row // blk) != (col // blk))
        offs = [_bdot(jnp.where(off_mask, low, 0.0), x) for low, x in zip(lows, invs)]
        invs = [x - _bdot(x, t) for x, t in zip(invs, offs)]
        blk *= 2
    return invs


def _gdn_kernel(q_ref, k_ref, v_ref, z_ref, gb_ref, on_ref, o_ref, state_ref):
    c_len = GDN_CHUNK
    step = pl.program_id(1)

    @pl.when(step == 0)
    def _():
        state_ref[...] = jnp.zeros_like(state_ref)

    row = lax.broadcasted_iota(jnp.int32, (c_len, c_len), 0)
    col = lax.broadcasted_iota(jnp.int32, (c_len, c_len), 1)
    causal = row >= col
    strict = row > col

    tri = causal.astype(BF16)
    heads = range(GDN_HEADS)
    out_gain = on_ref[...]
    for r0 in range(0, q_ref.shape[0], c_len):
        _gdn_chunk(pl.ds(r0, c_len), q_ref, k_ref, v_ref, z_ref, gb_ref, o_ref, state_ref,
                   row, col, causal, strict, tri, heads, out_gain)


def _gdn_chunk(rs, q_ref, k_ref, v_ref, z_ref, gb_ref, o_ref, state_ref, row, col, causal, strict, tri, heads,
               out_gain):
    c_len = GDN_CHUNK
    gb = gb_ref[rs, :]
    gb_hi = gb.astype(BF16)
    gb_lo = (gb - gb_hi.astype(F32)).astype(BF16)
    gcum = (jnp.dot(tri, gb_hi, preferred_element_type=F32)
            + jnp.dot(tri, gb_lo, preferred_element_type=F32))
    gcum_t = gcum.T

    def head_slices(ref):
        return [ref[rs, i * GDN_DK:(i + 1) * GDN_DK].astype(F32) for i in heads]

    qn, kn, v_all = head_slices(q_ref), head_slices(k_ref), head_slices(v_ref)
    beta = [gb[:, i:i + 1] for i in heads]
    gcol = [gcum[:, GDN_HEADS + i:GDN_HEADS + i + 1] for i in heads]
    grow = [gcum_t[GDN_HEADS + i:GDN_HEADS + i + 1, :] for i in heads]
    decay = [jnp.exp(jnp.where(causal, gc - gr, MASK_VALUE)) for gc, gr in zip(gcol, grow)]
    e_col = [jnp.exp(gc) for gc in gcol]
    g_last = [gc[c_len - 1:c_len, :] for gc in gcol]
    kb = [k * b for k, b in zip(kn, beta)]
    kk_qk = [_bdot_nt(jnp.concatenate([kb[i], qn[i]], axis=0), kn[i]) for i in heads]
    low = [jnp.where(strict, kk_qk[i][:c_len] * decay[i], 0.0) for i in heads]
    intra = [kk_qk[i][c_len:] * decay[i] for i in heads]
    inv = _unit_lower_inverse(low, row, col)
    sol = [_bdot(inv[i], jnp.concatenate([v_all[i] * beta[i], kb[i] * e_col[i]], axis=1)) for i in heads]
    state = [state_ref[i] for i in heads]
    ws_qs = [_bdot(jnp.concatenate([sol[i][:, GDN_DV:], qn[i] * e_col[i]], axis=0), state[i]) for i in heads]
    v_new = [sol[i][:, :GDN_DV] - ws_qs[i][:c_len] for i in heads]
    o = [ws_qs[i][c_len:] + _bdot(intra[i], v_new[i]) for i in heads]
    for i in heads:
        k_dec = kn[i] * jnp.exp(g_last[i] - gcol[i])
        state_ref[i] = state[i] * jnp.exp(g_last[i]) + _bdot_tn(k_dec, v_new[i])
    for i in heads:
        hs = slice(i * GDN_DV, (i + 1) * GDN_DV)
        o_ref[rs, hs] = (_rms(o[i], out_gain) * z_ref[rs, hs].astype(F32)).astype(o_ref.dtype)


def _gdn(proj, gb, out_norm, *, rows):
    s = proj.shape[0]
    assert rows % GDN_CHUNK == 0
    c = rows
    hb = GDN_HEADS
    ng = GDN_HEADS // hb
    bw = hb * GDN_DK

    def cur(sec):
        return pl.BlockSpec((c, bw), lambda g, t, sec=sec: (t, sec * ng + g))

    return pl.pallas_call(
        _gdn_kernel,
        grid=(ng, s // c),
        in_specs=[cur(0), cur(1), cur(2), cur(3),
                  pl.BlockSpec((c, LANES), lambda g, t: (t, g)),
                  pl.BlockSpec((1, GDN_DV), lambda g, t: (0, 0))],
        out_specs=pl.BlockSpec((c, bw), lambda g, t: (t, g)),
        out_shape=jax.ShapeDtypeStruct((s, GDN_V_W), BF16),
        scratch_shapes=[pltpu.VMEM((hb, GDN_DK, GDN_DV), F32)],
        compiler_params=_compiler_params(("parallel", "arbitrary")),
        name="gdn_delta_rule",
    )(proj, proj, proj, proj, gb, out_norm.reshape(1, GDN_DV))


def _overlapped_row_chunks(n_chunks, lhs_rows, w_ref, finish_rows):
    done = []
    prev = None
    for c in range(n_chunks + 1):
        cur = None
        if c < n_chunks:
            lhs = lhs_rows(c)
            if c >= 2:
                lhs = _ordered_after(lhs, done[c - 2])
            cur = jnp.dot(lhs, w_ref[...], preferred_element_type=F32)
        if c >= 1:
            done.append(finish_rows(c - 1, prev))
        prev = cur


def _matmul_norm_res_kernel(a_ref, w_ref, g_ref, x_ref, o_ref, *, row_chunk):
    def rows(c):
        return pl.ds(c * row_chunk, row_chunk)

    def finish(c, m):
        out = x_ref[rows(c), :] + _rms(m, g_ref[...])
        o_ref[rows(c), :] = out
        return out

    _overlapped_row_chunks(a_ref.shape[0] // row_chunk, lambda c: a_ref[rows(c), :], w_ref, finish)


def _matmul_norm_res(a, w, gain, x, layer, *, tm, row_chunk):
    s, k = a.shape
    d = w.shape[2]
    return pl.pallas_call(
        functools.partial(_matmul_norm_res_kernel, row_chunk=row_chunk),
        grid=(s // tm,),
        in_specs=[pl.BlockSpec((tm, k), lambda i: (i, 0)),
                  pl.BlockSpec((None, k, d), lambda i: (layer, 0, 0)),
                  pl.BlockSpec((1, d), lambda i: (0, 0)),
                  pl.BlockSpec((tm, d), lambda i: (i, 0))],
        out_specs=pl.BlockSpec((tm, d), lambda i: (i, 0)),
        out_shape=jax.ShapeDtypeStruct((s, d), F32),
        compiler_params=_compiler_params(("parallel",)),
        name="matmul_norm_res",
    )(a, w, gain.reshape(1, d), x)


def _ffn_kernel(x_ref, gpre_ref, wg_ref, wu_ref, wd_ref, gpost_ref, o_ref, xn_ref, *, row_chunk):
    j = pl.program_id(1)
    last = pl.num_programs(1) - 1

    @pl.when(j == 0)
    def _():
        xn_ref[...] = _rms(x_ref[...], gpre_ref[...]).astype(BF16)
        o_ref[...] = jnp.zeros_like(o_ref)

    def activations():
        xn = xn_ref[...]
        gate = jnp.dot(xn, wg_ref[...], preferred_element_type=F32)
        up = jnp.dot(xn, wu_ref[...], preferred_element_type=F32)
        return (_silu(gate) * up).astype(BF16)

    @pl.when(j < last)
    def _():
        o_ref[...] += jnp.dot(activations(), wd_ref[...], preferred_element_type=F32)

    @pl.when(j == last)
    def _():
        act = activations()

        def rows(c):
            return pl.ds(c * row_chunk, row_chunk)

        def finish(c, m):
            out = x_ref[rows(c), :] + _rms(o_ref[rows(c), :] + m, gpost_ref[...])
            o_ref[rows(c), :] = out
            return out

        _overlapped_row_chunks(o_ref.shape[0] // row_chunk,
                               lambda c: act[c * row_chunk:(c + 1) * row_chunk, :], wd_ref, finish)


def _ffn(x, g_pre, w_gate_up, w_down, g_post, layer, *, tm, tf, row_chunk):
    s, d = x.shape
    d_ff = w_down.shape[1]
    nf = d_ff // tf
    return pl.pallas_call(
        functools.partial(_ffn_kernel, row_chunk=row_chunk),
        grid=(s // tm, nf),
        in_specs=[pl.BlockSpec((tm, d), lambda i, j: (i, 0)),
                  pl.BlockSpec((1, d), lambda i, j: (0, 0)),
                  pl.BlockSpec((None, d, tf), lambda i, j: (layer, 0, j)),
                  pl.BlockSpec((None, d, tf), lambda i, j: (layer, 0, nf + j)),
                  pl.BlockSpec((None, tf, d), lambda i, j: (layer, j, 0)),
                  pl.BlockSpec((1, d), lambda i, j: (0, 0))],
        out_specs=pl.BlockSpec((tm, d), lambda i, j: (i, 0)),
        out_shape=jax.ShapeDtypeStruct((s, d), F32),
        scratch_shapes=[pltpu.VMEM((tm, d), BF16)],
        compiler_params=_compiler_params(("parallel", "arbitrary"), FFN_VMEM_LIMIT_BYTES),
        name="swiglu_ffn",
    )(x, g_pre.reshape(1, d), w_gate_up, w_gate_up, w_down, g_post.reshape(1, d))


def _t5_bucket(dist):
    max_exact = NUM_BUCKETS // 2
    d_f = jnp.maximum(dist, 1).astype(F32)
    large = max_exact + (jnp.log(d_f / max_exact) / math.log(MAX_DISTANCE / max_exact)
                         * (NUM_BUCKETS - max_exact)).astype(jnp.int32)
    large = jnp.minimum(large, NUM_BUCKETS - 1)
    return jnp.where(dist < max_exact, dist, large)


def _tile_offsets(group):
    w = DSA_SPAN
    classes = DSA_RESIDUES // DSA_GROUPS[group][1]
    rows = w // classes
    i = jnp.arange(w)
    return (i % rows) * classes + i // rows


def _band_buckets(group):
    w = DSA_SPAN
    off = _tile_offsets(group)
    rel = off[:, None] + w - jnp.concatenate([off, off + w])[None, :]
    band = (rel >= 0) & (rel <= w)
    return jnp.where(band, _t5_bucket(jnp.clip(rel, 0, w) * DSA_GROUPS[group][1]), -1).astype(jnp.int32)


def _dsa_min_tiles(group):
    classes = DSA_RESIDUES // DSA_GROUPS[group][1]
    return max(1, BF16_SUBLANES * classes // DSA_SPAN)


def _dsa_kernel(tab_ref, bucket_ref, q_ref, kp_ref, kc_ref, vp_ref, vc_ref, o_ref, lse_ref, bias_ref, *, group):
    w = DSA_SPAN
    classes = q_ref.shape[0]
    rows = w // classes
    n_tiles = q_ref.shape[1] // rows
    step = pl.program_id(1)

    @pl.when((pl.program_id(0) == 0) & (step == 0))
    def _():
        bucket = bucket_ref[...]
        key_col = lax.broadcasted_iota(jnp.int32, (w, 2 * w), 1)

        def per_head(h, carry):
            b = jnp.full(bucket.shape, MASK_VALUE, F32)
            for t in range(NUM_BUCKETS):
                b = jnp.where(bucket == t, tab_ref[t, group * DSA_HEADS + h] * LOG2_E, b)
            bias_ref[0, h] = b
            bias_ref[1, h] = jnp.where(key_col < w, MASK_VALUE, b)
            return carry

        lax.fori_loop(0, DSA_HEADS, per_head, 0)

    def tiles(ref, cols):
        blk = ref[:, :, cols]
        if rows % BF16_SUBLANES:
            blk = blk.astype(F32)
        return [blk[:, t * rows:(t + 1) * rows, :].reshape(w, blk.shape[-1]).astype(ref.dtype)
                for t in range(ref.shape[1] // rows)]

    def store(ref, cols, parts):
        parts = [p.reshape(classes, rows, p.shape[-1]) for p in parts]
        ref[:, :, cols] = jnp.concatenate(parts, axis=1).astype(ref.dtype)

    first = jnp.where(step == 0, 1, 0)
    lane = lax.broadcasted_iota(jnp.int32, (w, LANES), 1)
    lse_all = [jnp.zeros((w, LANES), F32) for _ in range(n_tiles)]
    ones = jnp.ones((2 * w, DSA_DH), BF16)
    for h in range(DSA_HEADS):
        hs = slice(h * DSA_DH, (h + 1) * DSA_DH)
        q_t = tiles(q_ref, hs)
        k_t = tiles(kc_ref, hs)
        v_t = tiles(vc_ref, hs)
        k_prev = [tiles(kp_ref, hs)[-1]] + k_t[:-1]
        v_prev = [tiles(vp_ref, hs)[-1]] + v_t[:-1]
        outs = []
        for t in range(n_tiles):
            k = jnp.concatenate([k_prev[t], k_t[t]], axis=0)
            v = jnp.concatenate([v_prev[t], v_t[t]], axis=0)
            s = _bdot_nt(q_t[t], k) + bias_ref[first if t == 0 else 0, h]
            m = jnp.max(s, axis=-1, keepdims=True)
            p = jnp.exp2(s - m)
            acc = _bdot(p, jnp.concatenate([v, ones], axis=1))
            l = acc[:, DSA_DH:]
            outs.append(acc[:, :DSA_DH] / l)
            lse_all[t] = jnp.where(lane == h, (m + jnp.log2(l)) * LN_2, lse_all[t])
        store(o_ref, hs, outs)
    store(lse_ref, slice(None), lse_all)


def _dsa_group(q_all, kv, rel_bias, group, *, tiles_per_step):
    s = q_all.shape[0]
    dilation = DSA_GROUPS[group][1]
    w = DSA_SPAN
    gw = DSA_GROUP_W
    classes = DSA_RESIDUES // dilation
    class_len = s // DSA_RESIDUES
    rows = w // classes
    prev_tiles = _dsa_min_tiles(group)
    n_tiles = max(prev_tiles, min(tiles_per_step, class_len // rows))
    assert n_tiles % prev_tiles == 0 and class_len % (n_tiles * rows) == 0

    def view(a):
        return a.reshape(classes, dilation, class_len, a.shape[-1])

    def cur(width, col):
        return pl.BlockSpec((classes, None, n_tiles * rows, width), lambda r, n: (0, r, n, col))

    def prev(width, col):
        ratio = n_tiles // prev_tiles
        return pl.BlockSpec((classes, None, prev_tiles * rows, width),
                            lambda r, n: (0, r, jnp.maximum(n * ratio - 1, 0), col))

    o, lse = pl.pallas_call(
        functools.partial(_dsa_kernel, group=group),
        grid=(dilation, class_len // (n_tiles * rows)),
        in_specs=[pl.BlockSpec(memory_space=pltpu.SMEM),
                  pl.BlockSpec((w, 2 * w), lambda r, n: (0, 0)),
                  cur(gw, group),
                  prev(gw, group), cur(gw, group),
                  prev(gw, N_GROUPS + group), cur(gw, N_GROUPS + group)],
        out_specs=[cur(gw, 0), cur(LANES, 0)],
        out_shape=[jax.ShapeDtypeStruct((classes, dilation, class_len, gw), BF16),
                   jax.ShapeDtypeStruct((classes, dilation, class_len, LANES), F32)],
        scratch_shapes=[pltpu.VMEM((2, DSA_HEADS, w, 2 * w), F32)],
        compiler_params=_compiler_params(("arbitrary", "arbitrary")),
        name=f"dsa_attention_g{group}",
    )(rel_bias, _band_buckets(group), view(q_all), view(kv), view(kv), view(kv), view(kv))
    return o.reshape(s, gw), lse.reshape(s, LANES)


def _to_residue_major_kernel(x_ref, o_ref):
    rows = o_ref.shape[1]
    x = x_ref[...].reshape(rows, DSA_RESIDUES, x_ref.shape[-1])
    o_ref[...] = pltpu.einshape("lrd->rld", x)


def _from_residue_major_kernel(x_ref, o_ref):
    rows = x_ref.shape[1]
    o_ref[...] = pltpu.einshape("rld->lrd", x_ref[...]).reshape(rows * DSA_RESIDUES, x_ref.shape[-1])


def _residue_major(x, *, rows, inverse):
    s, d = x.shape
    class_len = s // DSA_RESIDUES
    natural = pl.BlockSpec((rows * DSA_RESIDUES, d), lambda i: (i, 0))
    major = pl.BlockSpec((DSA_RESIDUES, rows, d), lambda i: (0, i, 0))
    if inverse:
        body, specs, arg, shape = (_from_residue_major_kernel, (major, natural),
                                   x.reshape(DSA_RESIDUES, class_len, d), (s, d))
    else:
        body, specs, arg, shape = _to_residue_major_kernel, (natural, major), x, (DSA_RESIDUES, class_len, d)
    out = pl.pallas_call(
        body,
        grid=(class_len // rows,),
        in_specs=[specs[0]],
        out_specs=specs[1],
        out_shape=jax.ShapeDtypeStruct(shape, x.dtype),
        compiler_params=_compiler_params(("parallel",)),
        name="from_residue_major" if inverse else "to_residue_major",
    )(arg)
    return out.reshape(s, d)


def _dsa_out_kernel(o0_ref, o1_ref, o2_ref, l0_ref, l1_ref, l2_ref, w_ref, g_ref, x_ref, out_ref, *, row_chunk):
    o_refs = [o0_ref, o1_ref, o2_ref]
    l_refs = [l0_ref, l1_ref, l2_ref]

    def rows(c):
        return pl.ds(c * row_chunk, row_chunk)

    def merged_rows(c):
        lses = [l[rows(c), :] for l in l_refs]
        top = jnp.maximum(jnp.maximum(lses[0], lses[1]), lses[2])
        es = [jnp.exp(l - top) for l in lses]
        inv_den = 1.0 / (es[0] + es[1] + es[2])
        wts = [(e * inv_den).astype(BF16) for e in es]
        heads = []
        for h in range(DSA_HEADS):
            hs = slice(h * DSA_DH, (h + 1) * DSA_DH)
            merged = wts[0][:, h:h + 1] * o_refs[0][rows(c), hs]
            for g in range(1, N_GROUPS):
                merged = merged + wts[g][:, h:h + 1] * o_refs[g][rows(c), hs]
            heads.append(merged)
        return jnp.concatenate(heads, axis=1)

    def finish(c, m):
        out = x_ref[rows(c), :] + _rms(m, g_ref[...])
        out_ref[rows(c), :] = out
        return out

    _overlapped_row_chunks(x_ref.shape[0] // row_chunk, merged_rows, w_ref, finish)


def _dsa_out(os_, lses, w, gain, x, layer, *, tm, row_chunk):
    s, k = os_[0].shape
    d = w.shape[2]
    o_spec = pl.BlockSpec((tm, k), lambda i: (i, 0))
    l_spec = pl.BlockSpec((tm, LANES), lambda i: (i, 0))
    return pl.pallas_call(
        functools.partial(_dsa_out_kernel, row_chunk=row_chunk),
        grid=(s // tm,),
        in_specs=[o_spec, o_spec, o_spec, l_spec, l_spec, l_spec,
                  pl.BlockSpec((None, k, d), lambda i: (layer, 0, 0)),
                  pl.BlockSpec((1, d), lambda i: (0, 0)),
                  pl.BlockSpec((tm, d), lambda i: (i, 0))],
        out_specs=pl.BlockSpec((tm, d), lambda i: (i, 0)),
        out_shape=jax.ShapeDtypeStruct((s, d), F32),
        compiler_params=_compiler_params(("parallel",)),
        name="dsa_merge_out",
    )(*os_, *lses, w, gain.reshape(1, d), x)


def _tiles(s):
    tm_proj = min(1024, s)
    eighth = tm_proj // 8
    return dict(tm_proj=tm_proj, tn_proj=2048, tn_gdn=1024, tm_out=min(512, s), out_chunk=128, tm_ffn=min(1024, s), tf_ffn=512, ffn_chunk=min(256, s), dsa_tiles=8, gdn_rows=min(2 * GDN_CHUNK, s),
                perm_rows=32, gdn_proj_chunks=(2 * eighth, 2 * eighth, 2 * eighth, eighth, eighth))


def kernel(x, norm_gains, ffn_w_gate_up, ffn_w_down, gdn_w_in, gdn_conv_w, gdn_a_log, gdn_dt_bias,
           gdn_out_norm, gdn_w_out, kv_norm, kv_w, dsa_w_q, dsa_w_out, rel_bias):
    b, s, d = x.shape
    assert b == 1 and d == D_MODEL and s % (DSA_SPAN * DSA_GROUPS[-1][1]) == 0
    t = _tiles(s)
    depth = norm_gains.shape[0]
    n_a = gdn_w_in.shape[0]
    xs = x.reshape(s, d)
    kv = None
    ffn_w_gate_up, ffn_w_down = ffn_w_gate_up.astype(BF16), ffn_w_down.astype(BF16)
    gdn_w_in, gdn_w_out = gdn_w_in.astype(BF16), gdn_w_out.astype(BF16)
    dsa_w_q, dsa_w_out = dsa_w_q.astype(BF16), dsa_w_out.astype(BF16)
    for layer in range(depth):
        gains = norm_gains[layer]
        if layer < n_a:
            proj, gb = _gdn_in_proj(xs, gains[0], gdn_w_in, gdn_conv_w, layer, gdn_a_log[layer], gdn_dt_bias[layer],
                                    tm=t["tm_proj"], tn=t["tn_gdn"], row_chunks=t["gdn_proj_chunks"])
            o = _gdn(proj, gb, gdn_out_norm[layer], rows=t["gdn_rows"])
            xs = _matmul_norm_res(o, gdn_w_out, gains[1], xs, layer, tm=t["tm_out"], row_chunk=t["out_chunk"])
        else:
            j = layer - n_a
            q_all = _norm_matmul(xs, gains[0], dsa_w_q, j, tm=t["tm_proj"], tn=t["tn_proj"], out_dtype=BF16,
                                 out_scale=DSA_DH ** -0.5 * LOG2_E)
            outs = [_dsa_group(q_all, kv, rel_bias, g, tiles_per_step=t["dsa_tiles"]) for g in range(N_GROUPS)]
            xs = _dsa_out([o for o, _ in outs], [l for _, l in outs], dsa_w_out, gains[1], xs, j,
                          tm=t["tm_out"], row_chunk=t["out_chunk"])
        xs = _ffn(xs, gains[2], ffn_w_gate_up, ffn_w_down, gains[3], layer, tm=t["tm_ffn"], tf=t["tf_ffn"],
                  row_chunk=t["ffn_chunk"])
        if layer == n_a - 1 and depth > n_a:
            xs = _residue_major(xs, rows=t["perm_rows"], inverse=False)
            kv = _norm_matmul(xs, kv_norm, kv_w.astype(BF16)[None], 0, tm=t["tm_proj"], tn=t["tn_proj"],
                              out_dtype=BF16)
    if depth > n_a:
        xs = _residue_major(xs, rows=t["perm_rows"], inverse=True)
    return xs.reshape(b, s, d)
```

```python
import functools
import math

import jax
import jax.numpy as jnp
from jax import lax
from jax.experimental import pallas as pl
from jax.experimental.pallas import tpu as pltpu

F32 = jnp.float32
BF16 = jnp.bfloat16

RMS_EPS = 1e-6
D_MODEL = 2048

GDN_HEADS = 16
GDN_DK = 128
GDN_DV = 128
GDN_CONV = 4
GDN_QK_W = GDN_HEADS * GDN_DK
GDN_V_W = GDN_HEADS * GDN_DV
GDN_CONV_W = 2 * GDN_QK_W + GDN_V_W
GDN_CHUNK = 128
GDN_INV_BASE = 8
CONV_HALO_ROWS = 8

DSA_GROUPS = ((128, 1), (512, 4), (2048, 16))
N_GROUPS = len(DSA_GROUPS)
DSA_HEADS = 16
DSA_DH = 128
DSA_SPAN = 128
DSA_GROUP_W = DSA_HEADS * DSA_DH
NUM_BUCKETS = 32
MAX_DISTANCE = 2048
MASK_VALUE = -1e30
LOG2_E = math.log2(math.e)
LN_2 = math.log(2.0)

DSA_RESIDUES = DSA_GROUPS[-1][1]

LANES = 128
BF16_SUBLANES = 16
VMEM_BYTES = 64 * 1024 * 1024
VMEM_LIMIT_BYTES = VMEM_BYTES * 7 // 8
FFN_VMEM_LIMIT_BYTES = VMEM_BYTES * 31 // 32


def _compiler_params(semantics, vmem_limit_bytes=VMEM_LIMIT_BYTES):
    return pltpu.CompilerParams(dimension_semantics=semantics, vmem_limit_bytes=vmem_limit_bytes)


def _rms(x, gain):
    ms = jnp.mean(x * x, axis=-1, keepdims=True)
    return x * lax.rsqrt(ms + RMS_EPS) * gain


def _sigmoid(x):
    return 1.0 / (1.0 + jnp.exp(-x))


def _silu(x):
    h = 0.5 * x
    return h + h * jnp.tanh(h)


def _bdot(a, b):
    return jnp.dot(a.astype(BF16), b.astype(BF16), preferred_element_type=F32)


def _bdot_nt(a, b):
    return lax.dot_general(a.astype(BF16), b.astype(BF16), (((1,), (1,)), ((), ())),
                           preferred_element_type=F32)


def _bdot_tn(a, b):
    return lax.dot_general(a.astype(BF16), b.astype(BF16), (((0,), (0,)), ((), ())),
                           preferred_element_type=F32)


def _norm_matmul_kernel(x_ref, g_ref, w_ref, o_ref, xn_ref, *, out_scale):
    @pl.when(pl.program_id(1) == 0)
    def _():
        xn_ref[...] = _rms(x_ref[...], g_ref[...]).astype(BF16)

    y = jnp.dot(xn_ref[...], w_ref[...], preferred_element_type=F32)
    if out_scale != 1.0:
        y = y * out_scale
    o_ref[...] = y.astype(o_ref.dtype)


def _norm_matmul(x, gain, w, layer, *, tm, tn, out_dtype, out_scale=1.0):
    s, d = x.shape
    n = w.shape[2]
    return pl.pallas_call(
        functools.partial(_norm_matmul_kernel, out_scale=out_scale),
        grid=(s // tm, n // tn),
        in_specs=[pl.BlockSpec((tm, d), lambda i, j: (i, 0)),
                  pl.BlockSpec((1, d), lambda i, j: (0, 0)),
                  pl.BlockSpec((None, d, tn), lambda i, j: (layer, 0, j))],
        out_specs=pl.BlockSpec((tm, tn), lambda i, j: (i, j)),
        out_shape=jax.ShapeDtypeStruct((s, n), out_dtype),
        scratch_shapes=[pltpu.VMEM((tm, d), BF16)],
        compiler_params=_compiler_params(("parallel", "arbitrary")),
        name="norm_matmul",
    )(x, gain.reshape(1, d), w)


def _ordered_after(x, anchor):
    rows, lanes = BF16_SUBLANES, LANES
    a = lax.bitcast_convert_type(jnp.abs(anchor[:rows, :lanes].astype(F32)), jnp.int32)
    zero = lax.shift_right_logical(a, 31)
    tile = lax.bitcast_convert_type(x[:rows, :lanes].astype(F32), jnp.int32) | zero
    tile = lax.bitcast_convert_type(tile, F32).astype(x.dtype)
    top = jnp.concatenate([tile, x[:rows, lanes:]], axis=1)
    return jnp.concatenate([top, x[rows:, :]], axis=0)


def _gdn_in_proj_kernel(x_ref, g_ref, w_ref, cw_ref, wba_ref, alog_ref, dtb_ref, o_ref, gb_ref,
                        xn_ref, halo_ref, *, qk_blocks, v_blocks, row_chunks):
    i = pl.program_id(0)
    j = pl.program_id(1)
    tm = x_ref.shape[0]
    tn = w_ref.shape[1]

    @pl.when(j == 0)
    def _():
        xn = _rms(x_ref[...], g_ref[...]).astype(BF16)
        xn_ref[...] = xn
        gb_ref[...] = _gdn_gate_values(jnp.dot(xn, wba_ref[...], preferred_element_type=F32),
                                       alog_ref[...], dtb_ref[...])

    @pl.when((i == 0) & (j == 0))
    def _():
        halo_ref[...] = jnp.zeros_like(halo_ref)

    chunks = []
    for rows in row_chunks:
        chunks.append((sum(r for _, r in chunks), rows))
    assert sum(row_chunks) == tm
    n_chunks = len(chunks)

    def conv_silu(raw, halo):
        rows = raw.shape[0]
        cw = cw_ref[...]
        groups = jnp.concatenate([halo, raw], axis=0).reshape(rows // CONV_HALO_ROWS + 1, CONV_HALO_ROWS, tn)
        in_group = lax.broadcasted_iota(jnp.int32, (rows // CONV_HALO_ROWS, CONV_HALO_ROWS, tn), 1)
        y = cw[GDN_CONV - 1:GDN_CONV, :] * raw
        for t in range(GDN_CONV - 1):
            back = GDN_CONV - 1 - t
            rot = pltpu.roll(groups, back, axis=1)
            shifted = jnp.where(in_group < back, rot[:-1], rot[1:]).reshape(rows, tn)
            y = y + cw[t:t + 1, :] * shifted
        return _silu(y)

    def finish_qk(raw, halo):
        y = conv_silu(raw, halo)
        c = jnp.where(j < qk_blocks, float(GDN_DK), 1.0).astype(F32)
        out = []
        for h in range(tn // GDN_DK):
            yh = y[:, h * GDN_DK:(h + 1) * GDN_DK]
            ss = jnp.sum(yh * yh, axis=-1, keepdims=True)
            out.append((yh * lax.rsqrt(ss * c + RMS_EPS * c)).astype(o_ref.dtype))
        return jnp.concatenate(out, axis=1)

    def finish_v(raw, halo):
        return conv_silu(raw, halo).astype(o_ref.dtype)

    def finish_z(raw, halo):
        return _silu(raw).astype(o_ref.dtype)

    def run(finish, conv):
        done = []
        prev = None
        halo = jnp.where(i > 0, halo_ref[j], 0.0) if conv else None
        for c in range(n_chunks + 1):
            cur = None
            if c < n_chunks:
                lhs = xn_ref[pl.ds(*chunks[c]), :]
                if c >= 2:
                    lhs = _ordered_after(lhs, done[c - 2])
                cur = jnp.dot(lhs, w_ref[...], preferred_element_type=F32)
            if c >= 1:
                out = finish(prev, halo)
                o_ref[pl.ds(*chunks[c - 1]), :] = out
                done.append(out)
                halo = prev[prev.shape[0] - CONV_HALO_ROWS:, :]
            prev = cur
        if conv:
            halo_ref[j] = halo

    @pl.when(j < 2 * qk_blocks)
    def _():
        run(finish_qk, True)

    @pl.when((j >= 2 * qk_blocks) & (j < 2 * qk_blocks + v_blocks))
    def _():
        run(finish_v, True)

    @pl.when(j >= 2 * qk_blocks + v_blocks)
    def _():
        run(finish_z, False)


def _gdn_in_proj(x, gain, w_in, conv_w, layer, a_log, dt_bias, *, tm, tn, row_chunks):
    s, d = x.shape
    n = GDN_CONV_W + GDN_V_W
    assert n % LANES == 0 and w_in.shape[2] - n == 2 * GDN_HEADS
    conv_blocks = GDN_CONV_W // tn
    alog, dtb = _gdn_gate_params(a_log, dt_bias)
    lane_row = pl.BlockSpec((1, LANES), lambda i, j: (0, 0))
    return pl.pallas_call(
        functools.partial(_gdn_in_proj_kernel, qk_blocks=GDN_QK_W // tn, v_blocks=GDN_V_W // tn,
                          row_chunks=row_chunks),
        grid=(s // tm, n // tn),
        in_specs=[pl.BlockSpec((tm, d), lambda i, j: (i, 0)),
                  pl.BlockSpec((1, d), lambda i, j: (0, 0)),
                  pl.BlockSpec((None, d, tn), lambda i, j: (layer, 0, j)),
                  pl.BlockSpec((None, GDN_CONV, tn), lambda i, j: (layer, 0, jnp.minimum(j, conv_blocks - 1))),
                  pl.BlockSpec((None, d, LANES), lambda i, j: (layer, 0, n // LANES)), lane_row, lane_row],
        out_specs=[pl.BlockSpec((tm, tn), lambda i, j: (i, j)),
                   pl.BlockSpec((tm, LANES), lambda i, j: (i, 0))],
        out_shape=[jax.ShapeDtypeStruct((s, n), BF16), jax.ShapeDtypeStruct((s, LANES), F32)],
        scratch_shapes=[pltpu.VMEM((tm, d), BF16),
                        pltpu.VMEM((conv_blocks, CONV_HALO_ROWS, tn), F32)],
        compiler_params=_compiler_params(("arbitrary", "arbitrary")),
        name="gdn_in_proj",
    )(x, gain.reshape(1, d), w_in, conv_w, w_in, alog, dtb)


def _gdn_gate_values(p, alog, dtb):
    lane = lax.broadcasted_iota(jnp.int32, p.shape, 1)
    beta = _sigmoid(p)
    a = p + dtb
    softplus = jnp.maximum(a, 0.0) + jnp.log(1.0 + jnp.exp(-jnp.abs(a)))
    g = -jnp.exp(alog) * softplus
    return jnp.where(lane < GDN_HEADS, beta, jnp.where(lane < 2 * GDN_HEADS, g, 0.0))


def _gdn_gate_params(a_log, dt_bias):
    pad = LANES - 2 * GDN_HEADS
    zeros_h = jnp.zeros((GDN_HEADS,), F32)
    alog = jnp.pad(jnp.concatenate([zeros_h, a_log.astype(F32)]), (0, pad)).reshape(1, LANES)
    dtb = jnp.pad(jnp.concatenate([zeros_h, dt_bias.astype(F32)]), (0, pad)).reshape(1, LANES)
    return alog, dtb


def _unit_lower_inverse(lows, row, col):
    c = lows[0].shape[0]
    eye = (row == col).astype(F32)
    blk = GDN_INV_BASE
    diag_mask = (row // blk) == (col // blk)
    powers = [jnp.where(diag_mask, low, 0.0) for low in lows]
    invs = [eye - p for p in powers]
    span = 2
    while span < blk:
        powers = [_bdot(p, p) for p in powers]
        invs = [x + _bdot(x, p) for x, p in zip(invs, powers)]
        span *= 2
    while blk < c:
        off_mask = ((row // (2 * blk)) == (col // (2 * blk))) & ((row // blk) != (col // blk))
        offs = [_bdot(jnp.where(off_mask, low, 0.0), x) for low, x in zip(lows, invs)]
        invs = [x - _bdot(x, t) for x, t in zip(invs, offs)]
        blk *= 2
    return invs


def _gdn_kernel(q_ref, k_ref, v_ref, z_ref, gb_ref, on_ref, o_ref, state_ref):
    c_len = GDN_CHUNK
    step = pl.program_id(1)

    @pl.when(step == 0)
    def _():
        state_ref[...] = jnp.zeros_like(state_ref)

    row = lax.broadcasted_iota(jnp.int32, (c_len, c_len), 0)
    col = lax.broadcasted_iota(jnp.int32, (c_len, c_len), 1)
    causal = row >= col
    strict = row > col

    tri = causal.astype(BF16)
    heads = range(GDN_HEADS)
    out_gain = on_ref[...]
    for r0 in range(0, q_ref.shape[0], c_len):
        _gdn_chunk(pl.ds(r0, c_len), q_ref, k_ref, v_ref, z_ref, gb_ref, o_ref, state_ref,
                   row, col, causal, strict, tri, heads, out_gain)


def _gdn_chunk(rs, q_ref, k_ref, v_ref, z_ref, gb_ref, o_ref, state_ref, row, col, causal, strict, tri, heads,
               out_gain):
    c_len = GDN_CHUNK
    gb = gb_ref[rs, :]
    gb_hi = gb.astype(BF16)
    gb_lo = (gb - gb_hi.astype(F32)).astype(BF16)
    gcum = (jnp.dot(tri, gb_hi, preferred_element_type=F32)
            + jnp.dot(tri, gb_lo, preferred_element_type=F32))
    gcum_t = gcum.T

    def head_slices(ref):
        return [ref[rs, i * GDN_DK:(i + 1) * GDN_DK].astype(F32) for i in heads]

    qn, kn, v_all = head_slices(q_ref), head_slices(k_ref), head_slices(v_ref)
    beta = [gb[:, i:i + 1] for i in heads]
    gcol = [gcum[:, GDN_HEADS + i:GDN_HEADS + i + 1] for i in heads]
    grow = [gcum_t[GDN_HEADS + i:GDN_HEADS + i + 1, :] for i in heads]
    decay = [jnp.exp(jnp.where(causal, gc - gr, MASK_VALUE)) for gc, gr in zip(gcol, grow)]
    e_col = [jnp.exp(gc) for gc in gcol]
    g_last = [gc[c_len - 1:c_len, :] for gc in gcol]
    kb = [k * b for k, b in zip(kn, beta)]
    kk_qk = [_bdot_nt(jnp.concatenate([kb[i], qn[i]], axis=0), kn[i]) for i in heads]
    low = [jnp.where(strict, kk_qk[i][:c_len] * decay[i], 0.0) for i in heads]
    intra = [kk_qk[i][c_len:] * decay[i] for i in heads]
    inv = _unit_lower_inverse(low, row, col)
    sol = [_bdot(inv[i], jnp.concatenate([v_all[i] * beta[i], kb[i] * e_col[i]], axis=1)) for i in heads]
    state = [state_ref[i] for i in heads]
    ws_qs = [_bdot(jnp.concatenate([sol[i][:, GDN_DV:], qn[i] * e_col[i]], axis=0), state[i]) for i in heads]
    v_new = [sol[i][:, :GDN_DV] - ws_qs[i][:c_len] for i in heads]
    o = [ws_qs[i][c_len:] + _bdot(intra[i], v_new[i]) for i in heads]
    for i in heads:
        k_dec = kn[i] * jnp.exp(g_last[i] - gcol[i])
        state_ref[i] = state[i] * jnp.exp(g_last[i]) + _bdot_tn(k_dec, v_new[i])
    for i in heads:
        hs = slice(i * GDN_DV, (i + 1) * GDN_DV)
        o_ref[rs, hs] = (_rms(o[i], out_gain) * z_ref[rs, hs].astype(F32)).astype(o_ref.dtype)


def _gdn(proj, gb, out_norm, *, rows):
    s = proj.shape[0]
    assert rows % GDN_CHUNK == 0
    c = rows
    hb = GDN_HEADS
    ng = GDN_HEADS // hb
    bw = hb * GDN_DK

    def cur(sec):
        return pl.BlockSpec((c, bw), lambda g, t, sec=sec: (t, sec * ng + g))

    return pl.pallas_call(
        _gdn_kernel,
        grid=(ng, s // c),
        in_specs=[cur(0), cur(1), cur(2), cur(3),
                  pl.BlockSpec((c, LANES), lambda g, t: (t, g)),
                  pl.BlockSpec((1, GDN_DV), lambda g, t: (0, 0))],
        out_specs=pl.BlockSpec((c, bw), lambda g, t: (t, g)),
        out_shape=jax.ShapeDtypeStruct((s, GDN_V_W), BF16),
        scratch_shapes=[pltpu.VMEM((hb, GDN_DK, GDN_DV), F32)],
        compiler_params=_compiler_params(("parallel", "arbitrary")),
        name="gdn_delta_rule",
    )(proj, proj, proj, proj, gb, out_norm.reshape(1, GDN_DV))


def _overlapped_row_chunks(n_chunks, lhs_rows, w_ref, finish_rows):
    done = []
    prev = None
    for c in range(n_chunks + 1):
        cur = None
        if c < n_chunks:
            lhs = lhs_rows(c)
            if c >= 2:
                lhs = _ordered_after(lhs, done[c - 2])
            cur = jnp.dot(lhs, w_ref[...], preferred_element_type=F32)
        if c >= 1:
            done.append(finish_rows(c - 1, prev))
        prev = cur


def _matmul_norm_res_kernel(a_ref, w_ref, g_ref, x_ref, o_ref, *, row_chunk):
    def rows(c):
        return pl.ds(c * row_chunk, row_chunk)

    def finish(c, m):
        out = x_ref[rows(c), :] + _rms(m, g_ref[...])
        o_ref[rows(c), :] = out
        return out

    _overlapped_row_chunks(a_ref.shape[0] // row_chunk, lambda c: a_ref[rows(c), :], w_ref, finish)


def _matmul_norm_res(a, w, gain, x, layer, *, tm, row_chunk):
    s, k = a.shape
    d = w.shape[2]
    return pl.pallas_call(
        functools.partial(_matmul_norm_res_kernel, row_chunk=row_chunk),
        grid=(s // tm,),
        in_specs=[pl.BlockSpec((tm, k), lambda i: (i, 0)),
                  pl.BlockSpec((None, k, d), lambda i: (layer, 0, 0)),
                  pl.BlockSpec((1, d), lambda i: (0, 0)),
                  pl.BlockSpec((tm, d), lambda i: (i, 0))],
        out_specs=pl.BlockSpec((tm, d), lambda i: (i, 0)),
        out_shape=jax.ShapeDtypeStruct((s, d), F32),
        compiler_params=_compiler_params(("parallel",)),
        name="matmul_norm_res",
    )(a, w, gain.reshape(1, d), x)


def _ffn_kernel(x_ref, gpre_ref, wg_ref, wu_ref, wd_ref, gpost_ref, o_ref, xn_ref, *, row_chunk):
    j = pl.program_id(1)
    last = pl.num_programs(1) - 1

    @pl.when(j == 0)
    def _():
        xn_ref[...] = _rms(x_ref[...], gpre_ref[...]).astype(BF16)
        o_ref[...] = jnp.zeros_like(o_ref)

    def activations():
        xn = xn_ref[...]
        gate = jnp.dot(xn, wg_ref[...], preferred_element_type=F32)
        up = jnp.dot(xn, wu_ref[...], preferred_element_type=F32)
        return (_silu(gate) * up).astype(BF16)

    @pl.when(j < last)
    def _():
        o_ref[...] += jnp.dot(activations(), wd_ref[...], preferred_element_type=F32)

    @pl.when(j == last)
    def _():
        act = activations()

        def rows(c):
            return pl.ds(c * row_chunk, row_chunk)

        def finish(c, m):
            out = x_ref[rows(c), :] + _rms(o_ref[rows(c), :] + m, gpost_ref[...])
            o_ref[rows(c), :] = out
            return out

        _overlapped_row_chunks(o_ref.shape[0] // row_chunk,
                               lambda c: act[c * row_chunk:(c + 1) * row_chunk, :], wd_ref, finish)


def _ffn(x, g_pre, w_gate_up, w_down, g_post, layer, *, tm, tf, row_chunk):
    s, d = x.shape
    d_ff = w_down.shape[1]
    nf = d_ff // tf
    return pl.pallas_call(
        functools.partial(_ffn_kernel, row_chunk=row_chunk),
        grid=(s // tm, nf),
        in_specs=[pl.BlockSpec((tm, d), lambda i, j: (i, 0)),
                  pl.BlockSpec((1, d), lambda i, j: (0, 0)),
                  pl.BlockSpec((None, d, tf), lambda i, j: (layer, 0, j)),
                  pl.BlockSpec((None, d, tf), lambda i, j: (layer, 0, nf + j)),
                  pl.BlockSpec((None, tf, d), lambda i, j: (layer, j, 0)),
                  pl.BlockSpec((1, d), lambda i, j: (0, 0))],
        out_specs=pl.BlockSpec((tm, d), lambda i, j: (i, 0)),
        out_shape=jax.ShapeDtypeStruct((s, d), F32),
        scratch_shapes=[pltpu.VMEM((tm, d), BF16)],
        compiler_params=_compiler_params(("parallel", "arbitrary"), FFN_VMEM_LIMIT_BYTES),
        name="swiglu_ffn",
    )(x, g_pre.reshape(1, d), w_gate_up, w_gate_up, w_down, g_post.reshape(1, d))


def _t5_bucket(dist):
    max_exact = NUM_BUCKETS // 2
    d_f = jnp.maximum(dist, 1).astype(F32)
    large = max_exact + (jnp.log(d_f / max_exact) / math.log(MAX_DISTANCE / max_exact)
                         * (NUM_BUCKETS - max_exact)).astype(jnp.int32)
    large = jnp.minimum(large, NUM_BUCKETS - 1)
    return jnp.where(dist < max_exact, dist, large)


def _tile_offsets(group):
    w = DSA_SPAN
    classes = DSA_RESIDUES // DSA_GROUPS[group][1]
    rows = w // classes
    i = jnp.arange(w)
    return (i % rows) * classes + i // rows


def _band_buckets(group):
    w = DSA_SPAN
    off = _tile_offsets(group)
    rel = off[:, None] + w - jnp.concatenate([off, off + w])[None, :]
    band = (rel >= 0) & (rel <= w)
    return jnp.where(band, _t5_bucket(jnp.clip(rel, 0, w) * DSA_GROUPS[group][1]), -1).astype(jnp.int32)


def _dsa_min_tiles(group):
    classes = DSA_RESIDUES // DSA_GROUPS[group][1]
    return max(1, BF16_SUBLANES * classes // DSA_SPAN)


def _dsa_kernel(tab_ref, bucket_ref, q_ref, kp_ref, kc_ref, vp_ref, vc_ref, o_ref, lse_ref, bias_ref, *, group):
    w = DSA_SPAN
    classes = q_ref.shape[0]
    rows = w // classes
    n_tiles = q_ref.shape[1] // rows
    step = pl.program_id(1)

    @pl.when((pl.program_id(0) == 0) & (step == 0))
    def _():
        bucket = bucket_ref[...]
        key_col = lax.broadcasted_iota(jnp.int32, (w, 2 * w), 1)

        def per_head(h, carry):
            b = jnp.full(bucket.shape, MASK_VALUE, F32)
            for t in range(NUM_BUCKETS):
                b = jnp.where(bucket == t, tab_ref[t, group * DSA_HEADS + h] * LOG2_E, b)
            bias_ref[0, h] = b
            bias_ref[1, h] = jnp.where(key_col < w, MASK_VALUE, b)
            return carry

        lax.fori_loop(0, DSA_HEADS, per_head, 0)

    def tiles(ref, cols):
        blk = ref[:, :, cols]
        if rows % BF16_SUBLANES:
            blk = blk.astype(F32)
        return [blk[:, t * rows:(t + 1) * rows, :].reshape(w, blk.shape[-1]).astype(ref.dtype)
                for t in range(ref.shape[1] // rows)]

    def store(ref, cols, parts):
        parts = [p.reshape(classes, rows, p.shape[-1]) for p in parts]
        ref[:, :, cols] = jnp.concatenate(parts, axis=1).astype(ref.dtype)

    first = jnp.where(step == 0, 1, 0)
    lane = lax.broadcasted_iota(jnp.int32, (w, LANES), 1)
    lse_all = [jnp.zeros((w, LANES), F32) for _ in range(n_tiles)]
    ones = jnp.ones((2 * w, DSA_DH), BF16)
    for h in range(DSA_HEADS):
        hs = slice(h * DSA_DH, (h + 1) * DSA_DH)
        q_t = tiles(q_ref, hs)
        k_t = tiles(kc_ref, hs)
        v_t = tiles(vc_ref, hs)
        k_prev = [tiles(kp_ref, hs)[-1]] + k_t[:-1]
        v_prev = [tiles(vp_ref, hs)[-1]] + v_t[:-1]
        outs = []
        for t in range(n_tiles):
            k = jnp.concatenate([k_prev[t], k_t[t]], axis=0)
            v = jnp.concatenate([v_prev[t], v_t[t]], axis=0)
            s = _bdot_nt(q_t[t], k) + bias_ref[first if t == 0 else 0, h]
            m = jnp.max(s, axis=-1, keepdims=True)
            p = jnp.exp2(s - m)
            acc = _bdot(p, jnp.concatenate([v, ones], axis=1))
            l = acc[:, DSA_DH:]
            outs.append(acc[:, :DSA_DH] / l)
            lse_all[t] = jnp.where(lane == h, (m + jnp.log2(l)) * LN_2, lse_all[t])
        store(o_ref, hs, outs)
    store(lse_ref, slice(None), lse_all)


def _dsa_group(q_all, kv, rel_bias, group, *, tiles_per_step):
    s = q_all.shape[0]
    dilation = DSA_GROUPS[group][1]
    w = DSA_SPAN
    gw = DSA_GROUP_W
    classes = DSA_RESIDUES // dilation
    class_len = s // DSA_RESIDUES
    rows = w // classes
    prev_tiles = _dsa_min_tiles(group)
    n_tiles = max(prev_tiles, min(tiles_per_step, class_len // rows))
    assert n_tiles % prev_tiles == 0 and class_len % (n_tiles * rows) == 0

    def view(a):
        return a.reshape(classes, dilation, class_len, a.shape[-1])

    def cur(width, col):
        return pl.BlockSpec((classes, None, n_tiles * rows, width), lambda r, n: (0, r, n, col))

    def prev(width, col):
        ratio = n_tiles // prev_tiles
        return pl.BlockSpec((classes, None, prev_tiles * rows, width),
                            lambda r, n: (0, r, jnp.maximum(n * ratio - 1, 0), col))

    o, lse = pl.pallas_call(
        functools.partial(_dsa_kernel, group=group),
        grid=(dilation, class_len // (n_tiles * rows)),
        in_specs=[pl.BlockSpec(memory_space=pltpu.SMEM),
                  pl.BlockSpec((w, 2 * w), lambda r, n: (0, 0)),
                  cur(gw, group),
                  prev(gw, group), cur(gw, group),
                  prev(gw, N_GROUPS + group), cur(gw, N_GROUPS + group)],
        out_specs=[cur(gw, 0), cur(LANES, 0)],
        out_shape=[jax.ShapeDtypeStruct((classes, dilation, class_len, gw), BF16),
                   jax.ShapeDtypeStruct((classes, dilation, class_len, LANES), F32)],
        scratch_shapes=[pltpu.VMEM((2, DSA_HEADS, w, 2 * w), F32)],
        compiler_params=_compiler_params(("arbitrary", "arbitrary")),
        name=f"dsa_attention_g{group}",
    )(rel_bias, _band_buckets(group), view(q_all), view(kv), view(kv), view(kv), view(kv))
    return o.reshape(s, gw), lse.reshape(s, LANES)


def _to_residue_major_kernel(x_ref, o_ref):
    rows = o_ref.shape[1]
    x = x_ref[...].reshape(rows, DSA_RESIDUES, x_ref.shape[-1])
    o_ref[...] = pltpu.einshape("lrd->rld", x)


def _from_residue_major_kernel(x_ref, o_ref):
    rows = x_ref.shape[1]
    o_ref[...] = pltpu.einshape("rld->lrd", x_ref[...]).reshape(rows * DSA_RESIDUES, x_ref.shape[-1])


def _residue_major(x, *, rows, inverse):
    s, d = x.shape
    class_len = s // DSA_RESIDUES
    natural = pl.BlockSpec((rows * DSA_RESIDUES, d), lambda i: (i, 0))
    major = pl.BlockSpec((DSA_RESIDUES, rows, d), lambda i: (0, i, 0))
    if inverse:
        body, specs, arg, shape = (_from_residue_major_kernel, (major, natural),
                                   x.reshape(DSA_RESIDUES, class_len, d), (s, d))
    else:
        body, specs, arg, shape = _to_residue_major_kernel, (natural, major), x, (DSA_RESIDUES, class_len, d)
    out = pl.pallas_call(
        body,
        grid=(class_len // rows,),
        in_specs=[specs[0]],
        out_specs=specs[1],
        out_shape=jax.ShapeDtypeStruct(shape, x.dtype),
        compiler_params=_compiler_params(("parallel",)),
        name="from_residue_major" if inverse else "to_residue_major",
    )(arg)
    return out.reshape(s, d)


def _dsa_out_kernel(o0_ref, o1_ref, o2_ref, l0_ref, l1_ref, l2_ref, w_ref, g_ref, x_ref, out_ref, *, row_chunk):
    o_refs = [o0_ref, o1_ref, o2_ref]
    l_refs = [l0_ref, l1_ref, l2_ref]

    def rows(c):
        return pl.ds(c * row_chunk, row_chunk)

    def merged_rows(c):
        lses = [l[rows(c), :] for l in l_refs]
        top = jnp.maximum(jnp.maximum(lses[0], lses[1]), lses[2])
        es = [jnp.exp(l - top) for l in lses]
        inv_den = 1.0 / (es[0] + es[1] + es[2])
        wts = [(e * inv_den).astype(BF16) for e in es]
        heads = []
        for h in range(DSA_HEADS):
            hs = slice(h * DSA_DH, (h + 1) * DSA_DH)
            merged = wts[0][:, h:h + 1] * o_refs[0][rows(c), hs]
            for g in range(1, N_GROUPS):
                merged = merged + wts[g][:, h:h + 1] * o_refs[g][rows(c), hs]
            heads.append(merged)
        return jnp.concatenate(heads, axis=1)

    def finish(c, m):
        out = x_ref[rows(c), :] + _rms(m, g_ref[...])
        out_ref[rows(c), :] = out
        return out

    _overlapped_row_chunks(x_ref.shape[0] // row_chunk, merged_rows, w_ref, finish)


def _dsa_out(os_, lses, w, gain, x, layer, *, tm, row_chunk):
    s, k = os_[0].shape
    d = w.shape[2]
    o_spec = pl.BlockSpec((tm, k), lambda i: (i, 0))
    l_spec = pl.BlockSpec((tm, LANES), lambda i: (i, 0))
    return pl.pallas_call(
        functools.partial(_dsa_out_kernel, row_chunk=row_chunk),
        grid=(s // tm,),
        in_specs=[o_spec, o_spec, o_spec, l_spec, l_spec, l_spec,
                  pl.BlockSpec((None, k, d), lambda i: (layer, 0, 0)),
                  pl.BlockSpec((1, d), lambda i: (0, 0)),
                  pl.BlockSpec((tm, d), lambda i: (i, 0))],
        out_specs=pl.BlockSpec((tm, d), lambda i: (i, 0)),
        out_shape=jax.ShapeDtypeStruct((s, d), F32),
        compiler_params=_compiler_params(("parallel",)),
        name="dsa_merge_out",
    )(*os_, *lses, w, gain.reshape(1, d), x)


def _tiles(s):
    tm_proj = min(1024, s)
    eighth = tm_proj // 8
    return dict(tm_proj=tm_proj, tn_proj=2048, tn_gdn=1024, tm_out=min(512, s), out_chunk=128, tm_ffn=min(1024, s), tf_ffn=512, ffn_chunk=min(256, s), dsa_tiles=8, gdn_rows=min(4 * GDN_CHUNK, s),
                perm_rows=32, gdn_proj_chunks=(2 * eighth, 2 * eighth, 2 * eighth, eighth, eighth))


def kernel(x, norm_gains, ffn_w_gate_up, ffn_w_down, gdn_w_in, gdn_conv_w, gdn_a_log, gdn_dt_bias,
           gdn_out_norm, gdn_w_out, kv_norm, kv_w, dsa_w_q, dsa_w_out, rel_bias):
    b, s, d = x.shape
    assert b == 1 and d == D_MODEL and s % (DSA_SPAN * DSA_GROUPS[-1][1]) == 0
    t = _tiles(s)
    depth = norm_gains.shape[0]
    n_a = gdn_w_in.shape[0]
    xs = x.reshape(s, d)
    kv = None
    ffn_w_gate_up, ffn_w_down = ffn_w_gate_up.astype(BF16), ffn_w_down.astype(BF16)
    gdn_w_in, gdn_w_out = gdn_w_in.astype(BF16), gdn_w_out.astype(BF16)
    dsa_w_q, dsa_w_out = dsa_w_q.astype(BF16), dsa_w_out.astype(BF16)
    for layer in range(depth):
        gains = norm_gains[layer]
        if layer < n_a:
            proj, gb = _gdn_in_proj(xs, gains[0], gdn_w_in, gdn_conv_w, layer, gdn_a_log[layer], gdn_dt_bias[layer],
                                    tm=t["tm_proj"], tn=t["tn_gdn"], row_chunks=t["gdn_proj_chunks"])
            o = _gdn(proj, gb, gdn_out_norm[layer], rows=t["gdn_rows"])
            xs = _matmul_norm_res(o, gdn_w_out, gains[1], xs, layer, tm=t["tm_out"], row_chunk=t["out_chunk"])
        else:
            j = layer - n_a
            q_all = _norm_matmul(xs, gains[0], dsa_w_q, j, tm=t["tm_proj"], tn=t["tn_proj"], out_dtype=BF16,
                                 out_scale=DSA_DH ** -0.5 * LOG2_E)
            outs = [_dsa_group(q_all, kv, rel_bias, g, tiles_per_step=t["dsa_tiles"]) for g in range(N_GROUPS)]
            xs = _dsa_out([o for o, _ in outs], [l for _, l in outs], dsa_w_out, gains[1], xs, j,
                          tm=t["tm_out"], row_chunk=t["out_chunk"])
        xs = _ffn(xs, gains[2], ffn_w_gate_up, ffn_w_down, gains[3], layer, tm=t["tm_ffn"], tf=t["tf_ffn"],
                  row_chunk=t["ffn_chunk"])
        if layer == n_a - 1 and depth > n_a:
            xs = _residue_major(xs, rows=t["perm_rows"], inverse=False)
            kv = _norm_matmul(xs, kv_norm, kv_w.astype(BF16)[None], 0, tm=t["tm_proj"], tn=t["tn_proj"],
                              out_dtype=BF16)
    if depth > n_a:
        xs = _residue_major(xs, rows=t["perm_rows"], inverse=True)
    return xs.reshape(b, s, d)
```

```python
import functools
import math

import jax
import jax.numpy as jnp
from jax import lax
from jax.experimental import pallas as pl
from jax.experimental.pallas import tpu as pltpu

F32 = jnp.float32
BF16 = jnp.bfloat16

RMS_EPS = 1e-6
D_MODEL = 2048

GDN_HEADS = 16
GDN_DK = 128
GDN_DV = 128
GDN_CONV = 4
GDN_QK_W = GDN_HEADS * GDN_DK
GDN_V_W = GDN_HEADS * GDN_DV
GDN_CONV_W = 2 * GDN_QK_W + GDN_V_W
GDN_CHUNK = 128
GDN_INV_BASE = 8
CONV_HALO_ROWS = 8

DSA_GROUPS = ((128, 1), (512, 4), (2048, 16))
N_GROUPS = len(DSA_GROUPS)
DSA_HEADS = 16
DSA_DH = 128
DSA_SPAN = 128
DSA_GROUP_W = DSA_HEADS * DSA_DH
NUM_BUCKETS = 32
MAX_DISTANCE = 2048
MASK_VALUE = -1e30
LOG2_E = math.log2(math.e)

DSA_RESIDUES = DSA_GROUPS[-1][1]

LANES = 128
BF16_SUBLANES = 16
VMEM_BYTES = 64 * 1024 * 1024
VMEM_LIMIT_BYTES = VMEM_BYTES * 7 // 8
FFN_VMEM_LIMIT_BYTES = VMEM_BYTES * 31 // 32


def _compiler_params(semantics, vmem_limit_bytes=VMEM_LIMIT_BYTES):
    return pltpu.CompilerParams(dimension_semantics=semantics, vmem_limit_bytes=vmem_limit_bytes)


def _rms(x, gain):
    ms = jnp.mean(x * x, axis=-1, keepdims=True)
    return x * lax.rsqrt(ms + RMS_EPS) * gain


def _sigmoid(x):
    return 1.0 / (1.0 + jnp.exp(-x))


def _silu(x):
    h = 0.5 * x
    return h + h * jnp.tanh(h)


def _bdot(a, b):
    return jnp.dot(a.astype(BF16), b.astype(BF16), preferred_element_type=F32)


def _bdot_nt(a, b):
    return lax.dot_general(a.astype(BF16), b.astype(BF16), (((1,), (1,)), ((), ())),
                           preferred_element_type=F32)


def _bdot_tn(a, b):
    return lax.dot_general(a.astype(BF16), b.astype(BF16), (((0,), (0,)), ((), ())),
                           preferred_element_type=F32)


def _norm_matmul_kernel(x_ref, g_ref, w_ref, o_ref, xn_ref, *, out_scale):
    @pl.when(pl.program_id(1) == 0)
    def _():
        xn_ref[...] = _rms(x_ref[...], g_ref[...]).astype(BF16)

    y = jnp.dot(xn_ref[...], w_ref[...], preferred_element_type=F32)
    if out_scale != 1.0:
        y = y * out_scale
    o_ref[...] = y.astype(o_ref.dtype)


def _norm_matmul(x, gain, w, layer, *, tm, tn, out_dtype, out_scale=1.0):
    s, d = x.shape
    n = w.shape[2]
    return pl.pallas_call(
        functools.partial(_norm_matmul_kernel, out_scale=out_scale),
        grid=(s // tm, n // tn),
        in_specs=[pl.BlockSpec((tm, d), lambda i, j: (i, 0)),
                  pl.BlockSpec((1, d), lambda i, j: (0, 0)),
                  pl.BlockSpec((None, d, tn), lambda i, j: (layer, 0, j))],
        out_specs=pl.BlockSpec((tm, tn), lambda i, j: (i, j)),
        out_shape=jax.ShapeDtypeStruct((s, n), out_dtype),
        scratch_shapes=[pltpu.VMEM((tm, d), BF16)],
        compiler_params=_compiler_params(("parallel", "arbitrary")),
        name="norm_matmul",
    )(x, gain.reshape(1, d), w)


def _ordered_after(x, anchor):
    rows, lanes = BF16_SUBLANES, LANES
    a = lax.bitcast_convert_type(jnp.abs(anchor[:rows, :lanes].astype(F32)), jnp.int32)
    zero = lax.shift_right_logical(a, 31)
    tile = lax.bitcast_convert_type(x[:rows, :lanes].astype(F32), jnp.int32) | zero
    tile = lax.bitcast_convert_type(tile, F32).astype(x.dtype)
    top = jnp.concatenate([tile, x[:rows, lanes:]], axis=1)
    return jnp.concatenate([top, x[rows:, :]], axis=0)


def _gdn_in_proj_kernel(x_ref, g_ref, w_ref, cw_ref, wba_ref, alog_ref, dtb_ref, o_ref, gb_ref,
                        xn_ref, halo_ref, *, qk_blocks, v_blocks, row_chunks):
    i = pl.program_id(0)
    j = pl.program_id(1)
    tm = x_ref.shape[0]
    tn = w_ref.shape[1]

    @pl.when(j == 0)
    def _():
        xn = _rms(x_ref[...], g_ref[...]).astype(BF16)
        xn_ref[...] = xn
        gb_ref[...] = _gdn_gate_values(jnp.dot(xn, wba_ref[...], preferred_element_type=F32),
                                       alog_ref[...], dtb_ref[...])

    @pl.when((i == 0) & (j == 0))
    def _():
        halo_ref[...] = jnp.zeros_like(halo_ref)

    chunks = []
    for rows in row_chunks:
        chunks.append((sum(r for _, r in chunks), rows))
    assert sum(row_chunks) == tm
    n_chunks = len(chunks)

    def conv_silu(raw, halo):
        rows = raw.shape[0]
        cw = cw_ref[...]
        groups = jnp.concatenate([halo, raw], axis=0).reshape(rows // CONV_HALO_ROWS + 1, CONV_HALO_ROWS, tn)
        in_group = lax.broadcasted_iota(jnp.int32, (rows // CONV_HALO_ROWS, CONV_HALO_ROWS, tn), 1)
        y = cw[GDN_CONV - 1:GDN_CONV, :] * raw
        for t in range(GDN_CONV - 1):
            back = GDN_CONV - 1 - t
            rot = pltpu.roll(groups, back, axis=1)
            shifted = jnp.where(in_group < back, rot[:-1], rot[1:]).reshape(rows, tn)
            y = y + cw[t:t + 1, :] * shifted
        return _silu(y)

    def finish_qk(raw, halo):
        y = conv_silu(raw, halo)
        c = jnp.where(j < qk_blocks, float(GDN_DK), 1.0).astype(F32)
        out = []
        for h in range(tn // GDN_DK):
            yh = y[:, h * GDN_DK:(h + 1) * GDN_DK]
            ss = jnp.sum(yh * yh, axis=-1, keepdims=True)
            out.append((yh * lax.rsqrt(ss * c + RMS_EPS * c)).astype(o_ref.dtype))
        return jnp.concatenate(out, axis=1)

    def finish_v(raw, halo):
        return conv_silu(raw, halo).astype(o_ref.dtype)

    def finish_z(raw, halo):
        return _silu(raw).astype(o_ref.dtype)

    def run(finish, conv):
        done = []
        prev = None
        halo = jnp.where(i > 0, halo_ref[j], 0.0) if conv else None
        for c in range(n_chunks + 1):
            cur = None
            if c < n_chunks:
                lhs = xn_ref[pl.ds(*chunks[c]), :]
                if c >= 2:
                    lhs = _ordered_after(lhs, done[c - 2])
                cur = jnp.dot(lhs, w_ref[...], preferred_element_type=F32)
            if c >= 1:
                out = finish(prev, halo)
                o_ref[pl.ds(*chunks[c - 1]), :] = out
                done.append(out)
                halo = prev[prev.shape[0] - CONV_HALO_ROWS:, :]
            prev = cur
        if conv:
            halo_ref[j] = halo

    @pl.when(j < 2 * qk_blocks)
    def _():
        run(finish_qk, True)

    @pl.when((j >= 2 * qk_blocks) & (j < 2 * qk_blocks + v_blocks))
    def _():
        run(finish_v, True)

    @pl.when(j >= 2 * qk_blocks + v_blocks)
    def _():
        run(finish_z, False)


def _gdn_in_proj(x, gain, w_in, conv_w, layer, a_log, dt_bias, *, tm, tn, row_chunks):
    s, d = x.shape
    n = GDN_CONV_W + GDN_V_W
    assert n % LANES == 0 and w_in.shape[2] - n == 2 * GDN_HEADS
    conv_blocks = GDN_CONV_W // tn
    alog, dtb = _gdn_gate_params(a_log, dt_bias)
    lane_row = pl.BlockSpec((1, LANES), lambda i, j: (0, 0))
    return pl.pallas_call(
        functools.partial(_gdn_in_proj_kernel, qk_blocks=GDN_QK_W // tn, v_blocks=GDN_V_W // tn,
                          row_chunks=row_chunks),
        grid=(s // tm, n // tn),
        in_specs=[pl.BlockSpec((tm, d), lambda i, j: (i, 0)),
                  pl.BlockSpec((1, d), lambda i, j: (0, 0)),
                  pl.BlockSpec((None, d, tn), lambda i, j: (layer, 0, j)),
                  pl.BlockSpec((None, GDN_CONV, tn), lambda i, j: (layer, 0, jnp.minimum(j, conv_blocks - 1))),
                  pl.BlockSpec((None, d, LANES), lambda i, j: (layer, 0, n // LANES)), lane_row, lane_row],
        out_specs=[pl.BlockSpec((tm, tn), lambda i, j: (i, j)),
                   pl.BlockSpec((tm, LANES), lambda i, j: (i, 0))],
        out_shape=[jax.ShapeDtypeStruct((s, n), BF16), jax.ShapeDtypeStruct((s, LANES), F32)],
        scratch_shapes=[pltpu.VMEM((tm, d), BF16),
                        pltpu.VMEM((conv_blocks, CONV_HALO_ROWS, tn), F32)],
        compiler_params=_compiler_params(("arbitrary", "arbitrary")),
        name="gdn_in_proj",
    )(x, gain.reshape(1, d), w_in, conv_w, w_in, alog, dtb)


def _gdn_gate_values(p, alog, dtb):
    lane = lax.broadcasted_iota(jnp.int32, p.shape, 1)
    beta = _sigmoid(p)
    a = p + dtb
    softplus = jnp.maximum(a, 0.0) + jnp.log(1.0 + jnp.exp(-jnp.abs(a)))
    g = -jnp.exp(alog) * softplus
    return jnp.where(lane < GDN_HEADS, beta, jnp.where(lane < 2 * GDN_HEADS, g, 0.0))


def _gdn_gate_params(a_log, dt_bias):
    pad = LANES - 2 * GDN_HEADS
    zeros_h = jnp.zeros((GDN_HEADS,), F32)
    alog = jnp.pad(jnp.concatenate([zeros_h, a_log.astype(F32)]), (0, pad)).reshape(1, LANES)
    dtb = jnp.pad(jnp.concatenate([zeros_h, dt_bias.astype(F32)]), (0, pad)).reshape(1, LANES)
    return alog, dtb


def _unit_lower_inverse(lows, row, col):
    c = lows[0].shape[0]
    eye = (row == col).astype(F32)
    blk = GDN_INV_BASE
    diag_mask = (row // blk) == (col // blk)
    powers = [jnp.where(diag_mask, low, 0.0) for low in lows]
    invs = [eye - p for p in powers]
    span = 2
    while span < blk:
        powers = [_bdot(p, p) for p in powers]
        invs = [x + _bdot(x, p) for x, p in zip(invs, powers)]
        span *= 2
    while blk < c:
        off_mask = ((row // (2 * blk)) == (col // (2 * blk))) & ((row // blk) != (col // blk))
        offs = [_bdot(jnp.where(off_mask, low, 0.0), x) for low, x in zip(lows, invs)]
        invs = [x - _bdot(x, t) for x, t in zip(invs, offs)]
        blk *= 2
    return invs


def _gdn_kernel(q_ref, k_ref, v_ref, z_ref, gb_ref, on_ref, o_ref, state_ref):
    c_len = GDN_CHUNK
    step = pl.program_id(1)

    @pl.when(step == 0)
    def _():
        state_ref[...] = jnp.zeros_like(state_ref)

    row = lax.broadcasted_iota(jnp.int32, (c_len, c_len), 0)
    col = lax.broadcasted_iota(jnp.int32, (c_len, c_len), 1)
    causal = row >= col
    strict = row > col

    tri = causal.astype(BF16)
    heads = range(GDN_HEADS)
    out_gain = on_ref[...]
    for r0 in range(0, q_ref.shape[0], c_len):
        _gdn_chunk(pl.ds(r0, c_len), q_ref, k_ref, v_ref, z_ref, gb_ref, o_ref, state_ref,
                   row, col, causal, strict, tri, heads, out_gain)


def _gdn_chunk(rs, q_ref, k_ref, v_ref, z_ref, gb_ref, o_ref, state_ref, row, col, causal, strict, tri, heads,
               out_gain):
    c_len = GDN_CHUNK
    gb = gb_ref[rs, :]
    gb_hi = gb.astype(BF16)
    gb_lo = (gb - gb_hi.astype(F32)).astype(BF16)
    gcum = (jnp.dot(tri, gb_hi, preferred_element_type=F32)
            + jnp.dot(tri, gb_lo, preferred_element_type=F32))
    gcum_t = gcum.T

    def head_slices(ref):
        return [ref[rs, i * GDN_DK:(i + 1) * GDN_DK].astype(F32) for i in heads]

    qn, kn, v_all = head_slices(q_ref), head_slices(k_ref), head_slices(v_ref)
    beta = [gb[:, i:i + 1] for i in heads]
    gcol = [gcum[:, GDN_HEADS + i:GDN_HEADS + i + 1] for i in heads]
    grow = [gcum_t[GDN_HEADS + i:GDN_HEADS + i + 1, :] for i in heads]
    decay = [jnp.exp(jnp.where(causal, gc - gr, MASK_VALUE)) for gc, gr in zip(gcol, grow)]
    e_col = [jnp.exp(gc) for gc in gcol]
    g_last = [gc[c_len - 1:c_len, :] for gc in gcol]
    kb = [k * b for k, b in zip(kn, beta)]
    kk_qk = [_bdot_nt(jnp.concatenate([kb[i], qn[i]], axis=0), kn[i]) for i in heads]
    low = [jnp.where(strict, kk_qk[i][:c_len] * decay[i], 0.0) for i in heads]
    intra = [kk_qk[i][c_len:] * decay[i] for i in heads]
    inv = _unit_lower_inverse(low, row, col)
    sol = [_bdot(inv[i], jnp.concatenate([v_all[i] * beta[i], kb[i] * e_col[i]], axis=1)) for i in heads]
    state = [state_ref[i] for i in heads]
    ws_qs = [_bdot(jnp.concatenate([sol[i][:, GDN_DV:], qn[i] * e_col[i]], axis=0), state[i]) for i in heads]
    v_new = [sol[i][:, :GDN_DV] - ws_qs[i][:c_len] for i in heads]
    o = [ws_qs[i][c_len:] + _bdot(intra[i], v_new[i]) for i in heads]
    for i in heads:
        k_dec = kn[i] * jnp.exp(g_last[i] - gcol[i])
        state_ref[i] = state[i] * jnp.exp(g_last[i]) + _bdot_tn(k_dec, v_new[i])
    for i in heads:
        hs = slice(i * GDN_DV, (i + 1) * GDN_DV)
        o_ref[rs, hs] = (_rms(o[i], out_gain) * z_ref[rs, hs].astype(F32)).astype(o_ref.dtype)


def _gdn(proj, gb, out_norm, *, rows):
    s = proj.shape[0]
    assert rows % GDN_CHUNK == 0
    c = rows
    hb = GDN_HEADS
    ng = GDN_HEADS // hb
    bw = hb * GDN_DK

    def cur(sec):
        return pl.BlockSpec((c, bw), lambda g, t, sec=sec: (t, sec * ng + g))

    return pl.pallas_call(
        _gdn_kernel,
        grid=(ng, s // c),
        in_specs=[cur(0), cur(1), cur(2), cur(3),
                  pl.BlockSpec((c, LANES), lambda g, t: (t, g)),
                  pl.BlockSpec((1, GDN_DV), lambda g, t: (0, 0))],
        out_specs=pl.BlockSpec((c, bw), lambda g, t: (t, g)),
        out_shape=jax.ShapeDtypeStruct((s, GDN_V_W), BF16),
        scratch_shapes=[pltpu.VMEM((hb, GDN_DK, GDN_DV), F32)],
        compiler_params=_compiler_params(("parallel", "arbitrary")),
        name="gdn_delta_rule",
    )(proj, proj, proj, proj, gb, out_norm.reshape(1, GDN_DV))


def _overlapped_row_chunks(n_chunks, lhs_rows, w_ref, finish_rows):
    done = []
    prev = None
    for c in range(n_chunks + 1):
        cur = None
        if c < n_chunks:
            lhs = lhs_rows(c)
            if c >= 2:
                lhs = _ordered_after(lhs, done[c - 2])
            cur = jnp.dot(lhs, w_ref[...], preferred_element_type=F32)
        if c >= 1:
            done.append(finish_rows(c - 1, prev))
        prev = cur


def _matmul_norm_res_kernel(a_ref, w_ref, g_ref, x_ref, o_ref, *, row_chunk):
    def rows(c):
        return pl.ds(c * row_chunk, row_chunk)

    def finish(c, m):
        out = x_ref[rows(c), :] + _rms(m, g_ref[...])
        o_ref[rows(c), :] = out
        return out

    _overlapped_row_chunks(a_ref.shape[0] // row_chunk, lambda c: a_ref[rows(c), :], w_ref, finish)


def _matmul_norm_res(a, w, gain, x, layer, *, tm, row_chunk):
    s, k = a.shape
    d = w.shape[2]
    return pl.pallas_call(
        functools.partial(_matmul_norm_res_kernel, row_chunk=row_chunk),
        grid=(s // tm,),
        in_specs=[pl.BlockSpec((tm, k), lambda i: (i, 0)),
                  pl.BlockSpec((None, k, d), lambda i: (layer, 0, 0)),
                  pl.BlockSpec((1, d), lambda i: (0, 0)),
                  pl.BlockSpec((tm, d), lambda i: (i, 0))],
        out_specs=pl.BlockSpec((tm, d), lambda i: (i, 0)),
        out_shape=jax.ShapeDtypeStruct((s, d), F32),
        compiler_params=_compiler_params(("parallel",)),
        name="matmul_norm_res",
    )(a, w, gain.reshape(1, d), x)


def _ffn_kernel(x_ref, gpre_ref, wg_ref, wu_ref, wd_ref, gpost_ref, o_ref, xn_ref, *, row_chunk):
    j = pl.program_id(1)
    last = pl.num_programs(1) - 1

    @pl.when(j == 0)
    def _():
        xn_ref[...] = _rms(x_ref[...], gpre_ref[...]).astype(BF16)
        o_ref[...] = jnp.zeros_like(o_ref)

    def activations():
        xn = xn_ref[...]
        gate = jnp.dot(xn, wg_ref[...], preferred_element_type=F32)
        up = jnp.dot(xn, wu_ref[...], preferred_element_type=F32)
        return (_silu(gate) * up).astype(BF16)

    @pl.when(j < last)
    def _():
        o_ref[...] += jnp.dot(activations(), wd_ref[...], preferred_element_type=F32)

    @pl.when(j == last)
    def _():
        act = activations()

        def rows(c):
            return pl.ds(c * row_chunk, row_chunk)

        def finish(c, m):
            out = x_ref[rows(c), :] + _rms(o_ref[rows(c), :] + m, gpost_ref[...])
            o_ref[rows(c), :] = out
            return out

        _overlapped_row_chunks(o_ref.shape[0] // row_chunk,
                               lambda c: act[c * row_chunk:(c + 1) * row_chunk, :], wd_ref, finish)


def _ffn(x, g_pre, w_gate_up, w_down, g_post, layer, *, tm, tf, row_chunk):
    s, d = x.shape
    d_ff = w_down.shape[1]
    nf = d_ff // tf
    return pl.pallas_call(
        functools.partial(_ffn_kernel, row_chunk=row_chunk),
        grid=(s // tm, nf),
        in_specs=[pl.BlockSpec((tm, d), lambda i, j: (i, 0)),
                  pl.BlockSpec((1, d), lambda i, j: (0, 0)),
                  pl.BlockSpec((None, d, tf), lambda i, j: (layer, 0, j)),
                  pl.BlockSpec((None, d, tf), lambda i, j: (layer, 0, nf + j)),
                  pl.BlockSpec((None, tf, d), lambda i, j: (layer, j, 0)),
                  pl.BlockSpec((1, d), lambda i, j: (0, 0))],
        out_specs=pl.BlockSpec((tm, d), lambda i, j: (i, 0)),
        out_shape=jax.ShapeDtypeStruct((s, d), F32),
        scratch_shapes=[pltpu.VMEM((tm, d), BF16)],
        compiler_params=_compiler_params(("parallel", "arbitrary"), FFN_VMEM_LIMIT_BYTES),
        name="swiglu_ffn",
    )(x, g_pre.reshape(1, d), w_gate_up, w_gate_up, w_down, g_post.reshape(1, d))


def _t5_bucket(dist):
    max_exact = NUM_BUCKETS // 2
    d_f = jnp.maximum(dist, 1).astype(F32)
    large = max_exact + (jnp.log(d_f / max_exact) / math.log(MAX_DISTANCE / max_exact)
                         * (NUM_BUCKETS - max_exact)).astype(jnp.int32)
    large = jnp.minimum(large, NUM_BUCKETS - 1)
    return jnp.where(dist < max_exact, dist, large)


def _tile_offsets(group):
    w = DSA_SPAN
    classes = DSA_RESIDUES // DSA_GROUPS[group][1]
    rows = w // classes
    i = jnp.arange(w)
    return (i % rows) * classes + i // rows


def _band_buckets(group):
    w = DSA_SPAN
    off = _tile_offsets(group)
    rel = off[:, None] + w - jnp.concatenate([off, off + w])[None, :]
    band = (rel >= 0) & (rel <= w)
    return jnp.where(band, _t5_bucket(jnp.clip(rel, 0, w) * DSA_GROUPS[group][1]), -1).astype(jnp.int32)


def _dsa_min_tiles(group):
    classes = DSA_RESIDUES // DSA_GROUPS[group][1]
    return max(1, BF16_SUBLANES * classes // DSA_SPAN)


def _dsa_kernel(tab_ref, bucket_ref, q_ref, kp_ref, kc_ref, vp_ref, vc_ref, o_ref, lse_ref, bias_ref, *, group):
    w = DSA_SPAN
    classes = q_ref.shape[0]
    rows = w // classes
    n_tiles = q_ref.shape[1] // rows
    step = pl.program_id(1)

    @pl.when((pl.program_id(0) == 0) & (step == 0))
    def _():
        bucket = bucket_ref[...]
        key_col = lax.broadcasted_iota(jnp.int32, (w, 2 * w), 1)

        def per_head(h, carry):
            b = jnp.full(bucket.shape, MASK_VALUE, F32)
            for t in range(NUM_BUCKETS):
                b = jnp.where(bucket == t, tab_ref[t, group * DSA_HEADS + h] * LOG2_E, b)
            bias_ref[0, h] = b
            bias_ref[1, h] = jnp.where(key_col < w, MASK_VALUE, b)
            return carry

        lax.fori_loop(0, DSA_HEADS, per_head, 0)

    def tiles(ref, cols):
        blk = ref[:, :, cols]
        if rows % BF16_SUBLANES:
            blk = blk.astype(F32)
        return [blk[:, t * rows:(t + 1) * rows, :].reshape(w, blk.shape[-1]).astype(ref.dtype)
                for t in range(ref.shape[1] // rows)]

    def store(ref, cols, parts):
        parts = [p.reshape(classes, rows, p.shape[-1]) for p in parts]
        ref[:, :, cols] = jnp.concatenate(parts, axis=1).astype(ref.dtype)

    first = jnp.where(step == 0, 1, 0)
    lane = lax.broadcasted_iota(jnp.int32, (w, LANES), 1)
    lse_all = [jnp.zeros((w, LANES), F32) for _ in range(n_tiles)]
    ones = jnp.ones((2 * w, DSA_DH), BF16)
    for h in range(DSA_HEADS):
        hs = slice(h * DSA_DH, (h + 1) * DSA_DH)
        q_t = tiles(q_ref, hs)
        k_t = tiles(kc_ref, hs)
        v_t = tiles(vc_ref, hs)
        k_prev = [tiles(kp_ref, hs)[-1]] + k_t[:-1]
        v_prev = [tiles(vp_ref, hs)[-1]] + v_t[:-1]
        outs = []
        for t in range(n_tiles):
            k = jnp.concatenate([k_prev[t], k_t[t]], axis=0)
            v = jnp.concatenate([v_prev[t], v_t[t]], axis=0)
            s = _bdot_nt(q_t[t], k) + bias_ref[first if t == 0 else 0, h]
            m = jnp.max(s, axis=-1, keepdims=True)
            p = jnp.exp2(s - m)
            acc = _bdot(p, jnp.concatenate([v, ones], axis=1))
            outs.append(acc[:, :DSA_DH])
            lse_all[t] = jnp.where(lane == h, m, jnp.where(lane == DSA_HEADS + h, acc[:, DSA_DH:], lse_all[t]))
        store(o_ref, hs, outs)
    store(lse_ref, slice(None), lse_all)


def _dsa_group(q_all, kv, rel_bias, group, *, tiles_per_step):
    s = q_all.shape[0]
    dilation = DSA_GROUPS[group][1]
    w = DSA_SPAN
    gw = DSA_GROUP_W
    classes = DSA_RESIDUES // dilation
    class_len = s // DSA_RESIDUES
    rows = w // classes
    prev_tiles = _dsa_min_tiles(group)
    n_tiles = max(prev_tiles, min(tiles_per_step, class_len // rows))
    assert n_tiles % prev_tiles == 0 and class_len % (n_tiles * rows) == 0

    def view(a):
        return a.reshape(classes, dilation, class_len, a.shape[-1])

    def cur(width, col):
        return pl.BlockSpec((classes, None, n_tiles * rows, width), lambda r, n: (0, r, n, col))

    def prev(width, col):
        ratio = n_tiles // prev_tiles
        return pl.BlockSpec((classes, None, prev_tiles * rows, width),
                            lambda r, n: (0, r, jnp.maximum(n * ratio - 1, 0), col))

    o, lse = pl.pallas_call(
        functools.partial(_dsa_kernel, group=group),
        grid=(dilation, class_len // (n_tiles * rows)),
        in_specs=[pl.BlockSpec(memory_space=pltpu.SMEM),
                  pl.BlockSpec((w, 2 * w), lambda r, n: (0, 0)),
                  cur(gw, group),
                  prev(gw, group), cur(gw, group),
                  prev(gw, N_GROUPS + group), cur(gw, N_GROUPS + group)],
        out_specs=[cur(gw, 0), cur(LANES, 0)],
        out_shape=[jax.ShapeDtypeStruct((classes, dilation, class_len, gw), BF16),
                   jax.ShapeDtypeStruct((classes, dilation, class_len, LANES), F32)],
        scratch_shapes=[pltpu.VMEM((2, DSA_HEADS, w, 2 * w), F32)],
        compiler_params=_compiler_params(("arbitrary", "arbitrary")),
        name=f"dsa_attention_g{group}",
    )(rel_bias, _band_buckets(group), view(q_all), view(kv), view(kv), view(kv), view(kv))
    return o.reshape(s, gw), lse.reshape(s, LANES)


def _to_residue_major_kernel(x_ref, o_ref):
    rows = o_ref.shape[1]
    x = x_ref[...].reshape(rows, DSA_RESIDUES, x_ref.shape[-1])
    o_ref[...] = pltpu.einshape("lrd->rld", x)


def _from_residue_major_kernel(x_ref, o_ref):
    rows = x_ref.shape[1]
    o_ref[...] = pltpu.einshape("rld->lrd", x_ref[...]).reshape(rows * DSA_RESIDUES, x_ref.shape[-1])


def _residue_major(x, *, rows, inverse):
    s, d = x.shape
    class_len = s // DSA_RESIDUES
    natural = pl.BlockSpec((rows * DSA_RESIDUES, d), lambda i: (i, 0))
    major = pl.BlockSpec((DSA_RESIDUES, rows, d), lambda i: (0, i, 0))
    if inverse:
        body, specs, arg, shape = (_from_residue_major_kernel, (major, natural),
                                   x.reshape(DSA_RESIDUES, class_len, d), (s, d))
    else:
        body, specs, arg, shape = _to_residue_major_kernel, (natural, major), x, (DSA_RESIDUES, class_len, d)
    out = pl.pallas_call(
        body,
        grid=(class_len // rows,),
        in_specs=[specs[0]],
        out_specs=specs[1],
        out_shape=jax.ShapeDtypeStruct(shape, x.dtype),
        compiler_params=_compiler_params(("parallel",)),
        name="from_residue_major" if inverse else "to_residue_major",
    )(arg)
    return out.reshape(s, d)


def _dsa_out_kernel(o0_ref, o1_ref, o2_ref, l0_ref, l1_ref, l2_ref, w_ref, g_ref, x_ref, out_ref, *, row_chunk):
    o_refs = [o0_ref, o1_ref, o2_ref]
    l_refs = [l0_ref, l1_ref, l2_ref]

    def rows(c):
        return pl.ds(c * row_chunk, row_chunk)

    def merged_rows(c):
        stats = [l[rows(c), :] for l in l_refs]
        dens = [pltpu.roll(st, LANES - DSA_HEADS, axis=1) for st in stats]
        top = jnp.maximum(jnp.maximum(stats[0], stats[1]), stats[2])
        es = [jnp.exp2(st - top) for st in stats]
        inv_den = 1.0 / (es[0] * dens[0] + es[1] * dens[1] + es[2] * dens[2])
        wts = [(e * inv_den).astype(BF16) for e in es]
        heads = []
        for h in range(DSA_HEADS):
            hs = slice(h * DSA_DH, (h + 1) * DSA_DH)
            merged = wts[0][:, h:h + 1] * o_refs[0][rows(c), hs]
            for g in range(1, N_GROUPS):
                merged = merged + wts[g][:, h:h + 1] * o_refs[g][rows(c), hs]
            heads.append(merged)
        return jnp.concatenate(heads, axis=1)

    def finish(c, m):
        out = x_ref[rows(c), :] + _rms(m, g_ref[...])
        out_ref[rows(c), :] = out
        return out

    _overlapped_row_chunks(x_ref.shape[0] // row_chunk, merged_rows, w_ref, finish)


def _dsa_out(os_, lses, w, gain, x, layer, *, tm, row_chunk):
    s, k = os_[0].shape
    d = w.shape[2]
    o_spec = pl.BlockSpec((tm, k), lambda i: (i, 0))
    l_spec = pl.BlockSpec((tm, LANES), lambda i: (i, 0))
    return pl.pallas_call(
        functools.partial(_dsa_out_kernel, row_chunk=row_chunk),
        grid=(s // tm,),
        in_specs=[o_spec, o_spec, o_spec, l_spec, l_spec, l_spec,
                  pl.BlockSpec((None, k, d), lambda i: (layer, 0, 0)),
                  pl.BlockSpec((1, d), lambda i: (0, 0)),
                  pl.BlockSpec((tm, d), lambda i: (i, 0))],
        out_specs=pl.BlockSpec((tm, d), lambda i: (i, 0)),
        out_shape=jax.ShapeDtypeStruct((s, d), F32),
        compiler_params=_compiler_params(("parallel",)),
        name="dsa_merge_out",
    )(*os_, *lses, w, gain.reshape(1, d), x)


def _tiles(s):
    tm_proj = min(1024, s)
    eighth = tm_proj // 8
    return dict(tm_proj=tm_proj, tn_proj=2048, tn_gdn=1024, tm_out=min(512, s), out_chunk=128, tm_ffn=min(1024, s), tf_ffn=512, ffn_chunk=min(256, s), dsa_tiles=8, gdn_rows=min(2 * GDN_CHUNK, s),
                perm_rows=32, gdn_proj_chunks=(2 * eighth, 2 * eighth, 2 * eighth, eighth, eighth))


def kernel(x, norm_gains, ffn_w_gate_up, ffn_w_down, gdn_w_in, gdn_conv_w, gdn_a_log, gdn_dt_bias,
           gdn_out_norm, gdn_w_out, kv_norm, kv_w, dsa_w_q, dsa_w_out, rel_bias):
    b, s, d = x.shape
    assert b == 1 and d == D_MODEL and s % (DSA_SPAN * DSA_GROUPS[-1][1]) == 0
    t = _tiles(s)
    depth = norm_gains.shape[0]
    n_a = gdn_w_in.shape[0]
    xs = x.reshape(s, d)
    kv = None
    ffn_w_gate_up, ffn_w_down = ffn_w_gate_up.astype(BF16), ffn_w_down.astype(BF16)
    gdn_w_in, gdn_w_out = gdn_w_in.astype(BF16), gdn_w_out.astype(BF16)
    dsa_w_q, dsa_w_out = dsa_w_q.astype(BF16), dsa_w_out.astype(BF16)
    for layer in range(depth):
        gains = norm_gains[layer]
        if layer < n_a:
            proj, gb = _gdn_in_proj(xs, gains[0], gdn_w_in, gdn_conv_w, layer, gdn_a_log[layer], gdn_dt_bias[layer],
                                    tm=t["tm_proj"], tn=t["tn_gdn"], row_chunks=t["gdn_proj_chunks"])
            o = _gdn(proj, gb, gdn_out_norm[layer], rows=t["gdn_rows"])
            xs = _matmul_norm_res(o, gdn_w_out, gains[1], xs, layer, tm=t["tm_out"], row_chunk=t["out_chunk"])
        else:
            j = layer - n_a
            q_all = _norm_matmul(xs, gains[0], dsa_w_q, j, tm=t["tm_proj"], tn=t["tn_proj"], out_dtype=BF16,
                                 out_scale=DSA_DH ** -0.5 * LOG2_E)
            outs = [_dsa_group(q_all, kv, rel_bias, g, tiles_per_step=t["dsa_tiles"]) for g in range(N_GROUPS)]
            xs = _dsa_out([o for o, _ in outs], [l for _, l in outs], dsa_w_out, gains[1], xs, j,
                          tm=t["tm_out"], row_chunk=t["out_chunk"])
        xs = _ffn(xs, gains[2], ffn_w_gate_up, ffn_w_down, gains[3], layer, tm=t["tm_ffn"], tf=t["tf_ffn"],
                  row_chunk=t["ffn_chunk"])
        if layer == n_a - 1 and depth > n_a:
            xs = _residue_major(xs, rows=t["perm_rows"], inverse=False)
            kv = _norm_matmul(xs, kv_norm, kv_w.astype(BF16)[None], 0, tm=t["tm_proj"], tn=t["tn_proj"],
                              out_dtype=BF16)
    if depth > n_a:
        xs = _residue_major(xs, rows=t["perm_rows"], inverse=True)
    return xs.reshape(b, s, d)
```

```python
import functools
import math

import jax
import jax.numpy as jnp
from jax import lax
from jax.experimental import pallas as pl
from jax.experimental.pallas import tpu as pltpu

F32 = jnp.float32
BF16 = jnp.bfloat16

RMS_EPS = 1e-6
D_MODEL = 2048

GDN_HEADS = 16
GDN_DK = 128
GDN_DV = 128
GDN_CONV = 4
GDN_QK_W = GDN_HEADS * GDN_DK
GDN_V_W = GDN_HEADS * GDN_DV
GDN_CONV_W = 2 * GDN_QK_W + GDN_V_W
GDN_CHUNK = 128
GDN_INV_BASE = 8
CONV_HALO_ROWS = 8

DSA_GROUPS = ((128, 1), (512, 4), (2048, 16))
N_GROUPS = len(DSA_GROUPS)
DSA_HEADS = 16
DSA_DH = 128
DSA_SPAN = 128
DSA_GROUP_W = DSA_HEADS * DSA_DH
NUM_BUCKETS = 32
MAX_DISTANCE = 2048
MASK_VALUE = -1e30
LOG2_E = math.log2(math.e)

DSA_RESIDUES = DSA_GROUPS[-1][1]

LANES = 128
BF16_SUBLANES = 16
VMEM_BYTES = 64 * 1024 * 1024
VMEM_LIMIT_BYTES = VMEM_BYTES * 7 // 8
FFN_VMEM_LIMIT_BYTES = VMEM_BYTES * 31 // 32


def _compiler_params(semantics, vmem_limit_bytes=VMEM_LIMIT_BYTES):
    return pltpu.CompilerParams(dimension_semantics=semantics, vmem_limit_bytes=vmem_limit_bytes)


def _rms(x, gain):
    ms = jnp.mean(x * x, axis=-1, keepdims=True)
    return x * lax.rsqrt(ms + RMS_EPS) * gain


def _sigmoid(x):
    return 1.0 / (1.0 + jnp.exp(-x))


def _silu(x):
    h = 0.5 * x
    return h + h * jnp.tanh(h)


def _bdot(a, b):
    return jnp.dot(a.astype(BF16), b.astype(BF16), preferred_element_type=F32)


def _bdot_nt(a, b):
    return lax.dot_general(a.astype(BF16), b.astype(BF16), (((1,), (1,)), ((), ())),
                           preferred_element_type=F32)


def _bdot_tn(a, b):
    return lax.dot_general(a.astype(BF16), b.astype(BF16), (((0,), (0,)), ((), ())),
                           preferred_element_type=F32)


def _norm_matmul_kernel(x_ref, g_ref, w_ref, o_ref, xn_ref, *, out_scale):
    @pl.when(pl.program_id(1) == 0)
    def _():
        xn_ref[...] = _rms(x_ref[...], g_ref[...]).astype(BF16)

    y = jnp.dot(xn_ref[...], w_ref[...], preferred_element_type=F32)
    if out_scale != 1.0:
        y = y * out_scale
    o_ref[...] = y.astype(o_ref.dtype)


def _norm_matmul(x, gain, w, layer, *, tm, tn, out_dtype, out_scale=1.0):
    s, d = x.shape
    n = w.shape[2]
    return pl.pallas_call(
        functools.partial(_norm_matmul_kernel, out_scale=out_scale),
        grid=(s // tm, n // tn),
        in_specs=[pl.BlockSpec((tm, d), lambda i, j: (i, 0)),
                  pl.BlockSpec((1, d), lambda i, j: (0, 0)),
                  pl.BlockSpec((None, d, tn), lambda i, j: (layer, 0, j))],
        out_specs=pl.BlockSpec((tm, tn), lambda i, j: (i, j)),
        out_shape=jax.ShapeDtypeStruct((s, n), out_dtype),
        scratch_shapes=[pltpu.VMEM((tm, d), BF16)],
        compiler_params=_compiler_params(("parallel", "arbitrary")),
        name="norm_matmul",
    )(x, gain.reshape(1, d), w)


def _ordered_after(x, anchor):
    rows, lanes = BF16_SUBLANES, LANES
    a = lax.bitcast_convert_type(jnp.abs(anchor[:rows, :lanes].astype(F32)), jnp.int32)
    zero = lax.shift_right_logical(a, 31)
    tile = lax.bitcast_convert_type(x[:rows, :lanes].astype(F32), jnp.int32) | zero
    tile = lax.bitcast_convert_type(tile, F32).astype(x.dtype)
    top = jnp.concatenate([tile, x[:rows, lanes:]], axis=1)
    return jnp.concatenate([top, x[rows:, :]], axis=0)


def _gdn_in_proj_kernel(x_ref, g_ref, w_ref, cw_ref, wba_ref, alog_ref, dtb_ref, o_ref, gb_ref,
                        xn_ref, halo_ref, *, qk_blocks, v_blocks, row_chunks):
    i = pl.program_id(0)
    j = pl.program_id(1)
    tm = x_ref.shape[0]
    tn = w_ref.shape[1]

    @pl.when(j == 0)
    def _():
        xn = _rms(x_ref[...], g_ref[...]).astype(BF16)
        xn_ref[...] = xn
        gb_ref[...] = _gdn_gate_values(jnp.dot(xn, wba_ref[...], preferred_element_type=F32),
                                       alog_ref[...], dtb_ref[...])

    @pl.when((i == 0) & (j == 0))
    def _():
        halo_ref[...] = jnp.zeros_like(halo_ref)

    chunks = []
    for rows in row_chunks:
        chunks.append((sum(r for _, r in chunks), rows))
    assert sum(row_chunks) == tm
    n_chunks = len(chunks)

    def conv_silu(raw, halo):
        rows = raw.shape[0]
        cw = cw_ref[...]
        groups = jnp.concatenate([halo, raw], axis=0).reshape(rows // CONV_HALO_ROWS + 1, CONV_HALO_ROWS, tn)
        in_group = lax.broadcasted_iota(jnp.int32, (rows // CONV_HALO_ROWS, CONV_HALO_ROWS, tn), 1)
        y = cw[GDN_CONV - 1:GDN_CONV, :] * raw
        for t in range(GDN_CONV - 1):
            back = GDN_CONV - 1 - t
            rot = pltpu.roll(groups, back, axis=1)
            shifted = jnp.where(in_group < back, rot[:-1], rot[1:]).reshape(rows, tn)
            y = y + cw[t:t + 1, :] * shifted
        return _silu(y)

    def finish_qk(raw, halo):
        y = conv_silu(raw, halo)
        c = jnp.where(j < qk_blocks, float(GDN_DK), 1.0).astype(F32)
        out = []
        for h in range(tn // GDN_DK):
            yh = y[:, h * GDN_DK:(h + 1) * GDN_DK]
            ss = jnp.sum(yh * yh, axis=-1, keepdims=True)
            out.append((yh * lax.rsqrt(ss * c + RMS_EPS * c)).astype(o_ref.dtype))
        return jnp.concatenate(out, axis=1)

    def finish_v(raw, halo):
        return conv_silu(raw, halo).astype(o_ref.dtype)

    def finish_z(raw, halo):
        return _silu(raw).astype(o_ref.dtype)

    def run(finish, conv):
        done = []
        prev = None
        halo = jnp.where(i > 0, halo_ref[j], 0.0) if conv else None
        for c in range(n_chunks + 1):
            cur = None
            if c < n_chunks:
                lhs = xn_ref[pl.ds(*chunks[c]), :]
                if c >= 2:
                    lhs = _ordered_after(lhs, done[c - 2])
                cur = jnp.dot(lhs, w_ref[...], preferred_element_type=F32)
            if c >= 1:
                out = finish(prev, halo)
                o_ref[pl.ds(*chunks[c - 1]), :] = out
                done.append(out)
                halo = prev[prev.shape[0] - CONV_HALO_ROWS:, :]
            prev = cur
        if conv:
            halo_ref[j] = halo

    @pl.when(j < 2 * qk_blocks)
    def _():
        run(finish_qk, True)

    @pl.when((j >= 2 * qk_blocks) & (j < 2 * qk_blocks + v_blocks))
    def _():
        run(finish_v, True)

    @pl.when(j >= 2 * qk_blocks + v_blocks)
    def _():
        run(finish_z, False)


def _gdn_in_proj(x, gain, w_in, conv_w, layer, a_log, dt_bias, *, tm, tn, row_chunks):
    s, d = x.shape
    n = GDN_CONV_W + GDN_V_W
    assert n % LANES == 0 and w_in.shape[2] - n == 2 * GDN_HEADS
    conv_blocks = GDN_CONV_W // tn
    alog, dtb = _gdn_gate_params(a_log, dt_bias)
    lane_row = pl.BlockSpec((1, LANES), lambda i, j: (0, 0))
    return pl.pallas_call(
        functools.partial(_gdn_in_proj_kernel, qk_blocks=GDN_QK_W // tn, v_blocks=GDN_V_W // tn,
                          row_chunks=row_chunks),
        grid=(s // tm, n // tn),
        in_specs=[pl.BlockSpec((tm, d), lambda i, j: (i, 0)),
                  pl.BlockSpec((1, d), lambda i, j: (0, 0)),
                  pl.BlockSpec((None, d, tn), lambda i, j: (layer, 0, j)),
                  pl.BlockSpec((None, GDN_CONV, tn), lambda i, j: (layer, 0, jnp.minimum(j, conv_blocks - 1))),
                  pl.BlockSpec((None, d, LANES), lambda i, j: (layer, 0, n // LANES)), lane_row, lane_row],
        out_specs=[pl.BlockSpec((tm, tn), lambda i, j: (i, j)),
                   pl.BlockSpec((tm, LANES), lambda i, j: (i, 0))],
        out_shape=[jax.ShapeDtypeStruct((s, n), BF16), jax.ShapeDtypeStruct((s, LANES), F32)],
        scratch_shapes=[pltpu.VMEM((tm, d), BF16),
                        pltpu.VMEM((conv_blocks, CONV_HALO_ROWS, tn), F32)],
        compiler_params=_compiler_params(("arbitrary", "arbitrary")),
        name="gdn_in_proj",
    )(x, gain.reshape(1, d), w_in, conv_w, w_in, alog, dtb)


def _gdn_gate_values(p, alog, dtb):
    lane = lax.broadcasted_iota(jnp.int32, p.shape, 1)
    beta = _sigmoid(p)
    a = p + dtb
    softplus = jnp.maximum(a, 0.0) + jnp.log(1.0 + jnp.exp(-jnp.abs(a)))
    g = -jnp.exp(alog) * softplus
    return jnp.where(lane < GDN_HEADS, beta, jnp.where(lane < 2 * GDN_HEADS, g, 0.0))


def _gdn_gate_params(a_log, dt_bias):
    pad = LANES - 2 * GDN_HEADS
    zeros_h = jnp.zeros((GDN_HEADS,), F32)
    alog = jnp.pad(jnp.concatenate([zeros_h, a_log.astype(F32)]), (0, pad)).reshape(1, LANES)
    dtb = jnp.pad(jnp.concatenate([zeros_h, dt_bias.astype(F32)]), (0, pad)).reshape(1, LANES)
    return alog, dtb


def _unit_lower_inverse(lows, row, col):
    c = lows[0].shape[0]
    eye = (row == col).astype(F32)
    blk = GDN_INV_BASE
    diag_mask = (row // blk) == (col // blk)
    powers = [jnp.where(diag_mask, low, 0.0) for low in lows]
    invs = [eye - p for p in powers]
    span = 2
    while span < blk:
        powers = [_bdot(p, p) for p in powers]
        invs = [x + _bdot(x, p) for x, p in zip(invs, powers)]
        span *= 2
    while blk < c:
        off_mask = ((row // (2 * blk)) == (col // (2 * blk))) & ((row // blk) != (col // blk))
        offs = [_bdot(jnp.where(off_mask, low, 0.0), x) for low, x in zip(lows, invs)]
        invs = [x - _bdot(x, t) for x, t in zip(invs, offs)]
        blk *= 2
    return invs


def _gdn_kernel(q_ref, k_ref, v_ref, z_ref, gb_ref, on_ref, o_ref, state_ref):
    c_len = GDN_CHUNK
    step = pl.program_id(1)

    @pl.when(step == 0)
    def _():
        state_ref[...] = jnp.zeros_like(state_ref)

    row = lax.broadcasted_iota(jnp.int32, (c_len, c_len), 0)
    col = lax.broadcasted_iota(jnp.int32, (c_len, c_len), 1)
    causal = row >= col
    strict = row > col

    tri = causal.astype(BF16)
    heads = range(GDN_HEADS)
    out_gain = on_ref[...]
    for r0 in range(0, q_ref.shape[0], c_len):
        _gdn_chunk(pl.ds(r0, c_len), q_ref, k_ref, v_ref, z_ref, gb_ref, o_ref, state_ref,
                   row, col, causal, strict, tri, heads, out_gain)


def _gdn_chunk(rs, q_ref, k_ref, v_ref, z_ref, gb_ref, o_ref, state_ref, row, col, causal, strict, tri, heads,
               out_gain):
    c_len = GDN_CHUNK
    gb = gb_ref[rs, :]
    gb_hi = gb.astype(BF16)
    gb_lo = (gb - gb_hi.astype(F32)).astype(BF16)
    gcum = (jnp.dot(tri, gb_hi, preferred_element_type=F32)
            + jnp.dot(tri, gb_lo, preferred_element_type=F32))
    gcum_t = gcum.T

    def head_slices(ref):
        return [ref[rs, i * GDN_DK:(i + 1) * GDN_DK].astype(F32) for i in heads]

    qn, kn, v_all = head_slices(q_ref), head_slices(k_ref), head_slices(v_ref)
    beta = [gb[:, i:i + 1] for i in heads]
    gcol = [gcum[:, GDN_HEADS + i:GDN_HEADS + i + 1] for i in heads]
    grow = [gcum_t[GDN_HEADS + i:GDN_HEADS + i + 1, :] for i in heads]
    decay = [jnp.exp(jnp.where(causal, gc - gr, MASK_VALUE)) for gc, gr in zip(gcol, grow)]
    e_col = [jnp.exp(gc) for gc in gcol]
    g_last = [gc[c_len - 1:c_len, :] for gc in gcol]
    kb = [k * b for k, b in zip(kn, beta)]
    kk_qk = [_bdot_nt(jnp.concatenate([kb[i], qn[i]], axis=0), kn[i]) for i in heads]
    low = [jnp.where(strict, kk_qk[i][:c_len] * decay[i], 0.0) for i in heads]
    intra = [kk_qk[i][c_len:] * decay[i] for i in heads]
    inv = _unit_lower_inverse(low, row, col)
    sol = [_bdot(inv[i], jnp.concatenate([v_all[i] * beta[i], kb[i] * e_col[i]], axis=1)) for i in heads]
    state = [state_ref[i] for i in heads]
    ws_qs = [_bdot(jnp.concatenate([sol[i][:, GDN_DV:], qn[i] * e_col[i]], axis=0), state[i]) for i in heads]
    v_new = [sol[i][:, :GDN_DV] - ws_qs[i][:c_len] for i in heads]
    o = [ws_qs[i][c_len:] + _bdot(intra[i], v_new[i]) for i in heads]
    for i in heads:
        k_dec = kn[i] * jnp.exp(g_last[i] - gcol[i])
        state_ref[i] = state[i] * jnp.exp(g_last[i]) + _bdot_tn(k_dec, v_new[i])
    for i in heads:
        hs = slice(i * GDN_DV, (i + 1) * GDN_DV)
        o_ref[rs, hs] = (_rms(o[i], out_gain) * z_ref[rs, hs].astype(F32)).astype(o_ref.dtype)


def _gdn(proj, gb, out_norm, *, rows):
    s = proj.shape[0]
    assert rows % GDN_CHUNK == 0
    c = rows
    hb = GDN_HEADS
    ng = GDN_HEADS // hb
    bw = hb * GDN_DK

    def cur(sec):
        return pl.BlockSpec((c, bw), lambda g, t, sec=sec: (t, sec * ng + g))

    return pl.pallas_call(
        _gdn_kernel,
        grid=(ng, s // c),
        in_specs=[cur(0), cur(1), cur(2), cur(3),
                  pl.BlockSpec((c, LANES), lambda g, t: (t, g)),
                  pl.BlockSpec((1, GDN_DV), lambda g, t: (0, 0))],
        out_specs=pl.BlockSpec((c, bw), lambda g, t: (t, g)),
        out_shape=jax.ShapeDtypeStruct((s, GDN_V_W), BF16),
        scratch_shapes=[pltpu.VMEM((hb, GDN_DK, GDN_DV), F32)],
        compiler_params=_compiler_params(("parallel", "arbitrary")),
        name="gdn_delta_rule",
    )(proj, proj, proj, proj, gb, out_norm.reshape(1, GDN_DV))


def _overlapped_row_chunks(n_chunks, lhs_rows, w_ref, finish_rows):
    done = []
    prev = None
    for c in range(n_chunks + 1):
        cur = None
        if c < n_chunks:
            lhs = lhs_rows(c)
            if c >= 2:
                lhs = _ordered_after(lhs, done[c - 2])
            cur = jnp.dot(lhs, w_ref[...], preferred_element_type=F32)
        if c >= 1:
            done.append(finish_rows(c - 1, prev))
        prev = cur


def _matmul_norm_res_kernel(a_ref, w_ref, g_ref, x_ref, o_ref, *, row_chunk):
    def rows(c):
        return pl.ds(c * row_chunk, row_chunk)

    def finish(c, m):
        out = x_ref[rows(c), :] + _rms(m, g_ref[...])
        o_ref[rows(c), :] = out
        return out

    _overlapped_row_chunks(a_ref.shape[0] // row_chunk, lambda c: a_ref[rows(c), :], w_ref, finish)


def _matmul_norm_res(a, w, gain, x, layer, *, tm, row_chunk):
    s, k = a.shape
    d = w.shape[2]
    return pl.pallas_call(
        functools.partial(_matmul_norm_res_kernel, row_chunk=row_chunk),
        grid=(s // tm,),
        in_specs=[pl.BlockSpec((tm, k), lambda i: (i, 0)),
                  pl.BlockSpec((None, k, d), lambda i: (layer, 0, 0)),
                  pl.BlockSpec((1, d), lambda i: (0, 0)),
                  pl.BlockSpec((tm, d), lambda i: (i, 0))],
        out_specs=pl.BlockSpec((tm, d), lambda i: (i, 0)),
        out_shape=jax.ShapeDtypeStruct((s, d), F32),
        compiler_params=_compiler_params(("parallel",)),
        name="matmul_norm_res",
    )(a, w, gain.reshape(1, d), x)


def _ffn_kernel(x_ref, gpre_ref, wg_ref, wu_ref, wd_ref, gpost_ref, o_ref, xn_ref, *, row_chunk):
    j = pl.program_id(1)
    last = pl.num_programs(1) - 1

    @pl.when(j == 0)
    def _():
        xn_ref[...] = _rms(x_ref[...], gpre_ref[...]).astype(BF16)
        o_ref[...] = jnp.zeros_like(o_ref)

    def activations():
        xn = xn_ref[...]
        gate = jnp.dot(xn, wg_ref[...], preferred_element_type=F32)
        up = jnp.dot(xn, wu_ref[...], preferred_element_type=F32)
        return (_silu(gate) * up).astype(BF16)

    @pl.when(j < last)
    def _():
        o_ref[...] += jnp.dot(activations(), wd_ref[...], preferred_element_type=F32)

    @pl.when(j == last)
    def _():
        act = activations()

        def rows(c):
            return pl.ds(c * row_chunk, row_chunk)

        def finish(c, m):
            out = x_ref[rows(c), :] + _rms(o_ref[rows(c), :] + m, gpost_ref[...])
            o_ref[rows(c), :] = out
            return out

        _overlapped_row_chunks(o_ref.shape[0] // row_chunk,
                               lambda c: act[c * row_chunk:(c + 1) * row_chunk, :], wd_ref, finish)


def _ffn(x, g_pre, w_gate_up, w_down, g_post, layer, *, tm, tf, row_chunk):
    s, d = x.shape
    d_ff = w_down.shape[1]
    nf = d_ff // tf
    return pl.pallas_call(
        functools.partial(_ffn_kernel, row_chunk=row_chunk),
        grid=(s // tm, nf),
        in_specs=[pl.BlockSpec((tm, d), lambda i, j: (i, 0)),
                  pl.BlockSpec((1, d), lambda i, j: (0, 0)),
                  pl.BlockSpec((None, d, tf), lambda i, j: (layer, 0, j)),
                  pl.BlockSpec((None, d, tf), lambda i, j: (layer, 0, nf + j)),
                  pl.BlockSpec((None, tf, d), lambda i, j: (layer, j, 0)),
                  pl.BlockSpec((1, d), lambda i, j: (0, 0))],
        out_specs=pl.BlockSpec((tm, d), lambda i, j: (i, 0)),
        out_shape=jax.ShapeDtypeStruct((s, d), F32),
        scratch_shapes=[pltpu.VMEM((tm, d), BF16)],
        compiler_params=_compiler_params(("parallel", "arbitrary"), FFN_VMEM_LIMIT_BYTES),
        name="swiglu_ffn",
    )(x, g_pre.reshape(1, d), w_gate_up, w_gate_up, w_down, g_post.reshape(1, d))


def _t5_bucket(dist):
    max_exact = NUM_BUCKETS // 2
    d_f = jnp.maximum(dist, 1).astype(F32)
    large = max_exact + (jnp.log(d_f / max_exact) / math.log(MAX_DISTANCE / max_exact)
                         * (NUM_BUCKETS - max_exact)).astype(jnp.int32)
    large = jnp.minimum(large, NUM_BUCKETS - 1)
    return jnp.where(dist < max_exact, dist, large)


def _tile_offsets(group):
    w = DSA_SPAN
    classes = DSA_RESIDUES // DSA_GROUPS[group][1]
    rows = w // classes
    i = jnp.arange(w)
    return (i % rows) * classes + i // rows


def _band_buckets(group):
    w = DSA_SPAN
    off = _tile_offsets(group)
    rel = off[:, None] + w - jnp.concatenate([off, off + w])[None, :]
    band = (rel >= 0) & (rel <= w)
    return jnp.where(band, _t5_bucket(jnp.clip(rel, 0, w) * DSA_GROUPS[group][1]), -1).astype(jnp.int32)


def _dsa_min_tiles(group):
    classes = DSA_RESIDUES // DSA_GROUPS[group][1]
    return max(1, BF16_SUBLANES * classes // DSA_SPAN)


def _dsa_kernel(tab_ref, bucket_ref, q_ref, kp_ref, kc_ref, vp_ref, vc_ref, o_ref, lse_ref, bias_ref, *, group):
    w = DSA_SPAN
    classes = q_ref.shape[0]
    rows = w // classes
    n_tiles = q_ref.shape[1] // rows
    step = pl.program_id(1)

    @pl.when((pl.program_id(0) == 0) & (step == 0))
    def _():
        bucket = bucket_ref[...]
        key_col = lax.broadcasted_iota(jnp.int32, (w, 2 * w), 1)

        def per_head(h, carry):
            b = jnp.full(bucket.shape, MASK_VALUE, F32)
            for t in range(NUM_BUCKETS):
                b = jnp.where(bucket == t, tab_ref[t, group * DSA_HEADS + h] * LOG2_E, b)
            bias_ref[0, h] = b
            bias_ref[1, h] = jnp.where(key_col < w, MASK_VALUE, b)
            return carry

        lax.fori_loop(0, DSA_HEADS, per_head, 0)

    def tiles(ref, cols):
        blk = ref[:, :, cols]
        if rows % BF16_SUBLANES:
            blk = blk.astype(F32)
        return [blk[:, t * rows:(t + 1) * rows, :].reshape(w, blk.shape[-1]).astype(ref.dtype)
                for t in range(ref.shape[1] // rows)]

    def store(ref, cols, parts):
        parts = [p.reshape(classes, rows, p.shape[-1]) for p in parts]
        ref[:, :, cols] = jnp.concatenate(parts, axis=1).astype(ref.dtype)

    first = jnp.where(step == 0, 1, 0)
    lane = lax.broadcasted_iota(jnp.int32, (w, LANES), 1)
    lse_all = [jnp.zeros((w, LANES), F32) for _ in range(n_tiles)]
    ones = jnp.ones((2 * w, DSA_DH), BF16)
    for h in range(DSA_HEADS):
        hs = slice(h * DSA_DH, (h + 1) * DSA_DH)
        q_t = tiles(q_ref, hs)
        k_t = tiles(kc_ref, hs)
        v_t = tiles(vc_ref, hs)
        k_prev = [tiles(kp_ref, hs)[-1]] + k_t[:-1]
        v_prev = [tiles(vp_ref, hs)[-1]] + v_t[:-1]
        outs = []
        for t in range(n_tiles):
            k = jnp.concatenate([k_prev[t], k_t[t]], axis=0)
            v = jnp.concatenate([v_prev[t], v_t[t]], axis=0)
            s = _bdot_nt(q_t[t], k) + bias_ref[first if t == 0 else 0, h]
            m = jnp.max(s, axis=-1, keepdims=True)
            p = jnp.exp2(s - m)
            acc = _bdot(p, jnp.concatenate([v, ones], axis=1))
            outs.append(acc[:, :DSA_DH])
            lse_all[t] = jnp.where(lane == h, m, jnp.where(lane == DSA_HEADS + h, acc[:, DSA_DH:], lse_all[t]))
        store(o_ref, hs, outs)
    store(lse_ref, slice(None), lse_all)


def _dsa_group(q_all, kv, rel_bias, group, *, tiles_per_step):
    s = q_all.shape[0]
    dilation = DSA_GROUPS[group][1]
    w = DSA_SPAN
    gw = DSA_GROUP_W
    classes = DSA_RESIDUES // dilation
    class_len = s // DSA_RESIDUES
    rows = w // classes
    prev_tiles = _dsa_min_tiles(group)
    n_tiles = max(prev_tiles, min(tiles_per_step, class_len // rows))
    assert n_tiles % prev_tiles == 0 and class_len % (n_tiles * rows) == 0

    def view(a):
        return a.reshape(classes, dilation, class_len, a.shape[-1])

    def cur(width, col):
        return pl.BlockSpec((classes, None, n_tiles * rows, width), lambda r, n: (0, r, n, col))

    def prev(width, col):
        ratio = n_tiles // prev_tiles
        return pl.BlockSpec((classes, None, prev_tiles * rows, width),
                            lambda r, n: (0, r, jnp.maximum(n * ratio - 1, 0), col))

    o, lse = pl.pallas_call(
        functools.partial(_dsa_kernel, group=group),
        grid=(dilation, class_len // (n_tiles * rows)),
        in_specs=[pl.BlockSpec(memory_space=pltpu.SMEM),
                  pl.BlockSpec((w, 2 * w), lambda r, n: (0, 0)),
                  cur(gw, group),
                  prev(gw, group), cur(gw, group),
                  prev(gw, N_GROUPS + group), cur(gw, N_GROUPS + group)],
        out_specs=[cur(gw, 0), cur(LANES, 0)],
        out_shape=[jax.ShapeDtypeStruct((classes, dilation, class_len, gw), BF16),
                   jax.ShapeDtypeStruct((classes, dilation, class_len, LANES), F32)],
        scratch_shapes=[pltpu.VMEM((2, DSA_HEADS, w, 2 * w), F32)],
        compiler_params=_compiler_params(("arbitrary", "arbitrary")),
        name=f"dsa_attention_g{group}",
    )(rel_bias, _band_buckets(group), view(q_all), view(kv), view(kv), view(kv), view(kv))
    return o.reshape(s, gw), lse.reshape(s, LANES)


def _to_residue_major_kernel(x_ref, o_ref):
    rows = o_ref.shape[1]
    x = x_ref[...].reshape(rows, DSA_RESIDUES, x_ref.shape[-1])
    o_ref[...] = pltpu.einshape("lrd->rld", x)


def _from_residue_major_kernel(x_ref, o_ref):
    rows = x_ref.shape[1]
    o_ref[...] = pltpu.einshape("rld->lrd", x_ref[...]).reshape(rows * DSA_RESIDUES, x_ref.shape[-1])


def _residue_major(x, *, rows, inverse):
    s, d = x.shape
    class_len = s // DSA_RESIDUES
    natural = pl.BlockSpec((rows * DSA_RESIDUES, d), lambda i: (i, 0))
    major = pl.BlockSpec((DSA_RESIDUES, rows, d), lambda i: (0, i, 0))
    if inverse:
        body, specs, arg, shape = (_from_residue_major_kernel, (major, natural),
                                   x.reshape(DSA_RESIDUES, class_len, d), (s, d))
    else:
        body, specs, arg, shape = _to_residue_major_kernel, (natural, major), x, (DSA_RESIDUES, class_len, d)
    out = pl.pallas_call(
        body,
        grid=(class_len // rows,),
        in_specs=[specs[0]],
        out_specs=specs[1],
        out_shape=jax.ShapeDtypeStruct(shape, x.dtype),
        compiler_params=_compiler_params(("parallel",)),
        name="from_residue_major" if inverse else "to_residue_major",
    )(arg)
    return out.reshape(s, d)


def _dsa_out_kernel(o0_ref, o1_ref, o2_ref, l0_ref, l1_ref, l2_ref, w_ref, g_ref, x_ref, out_ref, *, row_chunk):
    o_refs = [o0_ref, o1_ref, o2_ref]
    l_refs = [l0_ref, l1_ref, l2_ref]

    def rows(c):
        return pl.ds(c * row_chunk, row_chunk)

    def merged_rows(c):
        stats = [l[rows(c), :] for l in l_refs]
        dens = [pltpu.roll(st, LANES - DSA_HEADS, axis=1) for st in stats]
        top = jnp.maximum(jnp.maximum(stats[0], stats[1]), stats[2])
        es = [jnp.exp2(st - top) for st in stats]
        inv_den = 1.0 / (es[0] * dens[0] + es[1] * dens[1] + es[2] * dens[2])
        wts = [(e * inv_den).astype(BF16) for e in es]
        heads = []
        for h in range(DSA_HEADS):
            hs = slice(h * DSA_DH, (h + 1) * DSA_DH)
            merged = wts[0][:, h:h + 1] * o_refs[0][rows(c), hs]
            for g in range(1, N_GROUPS):
                merged = merged + wts[g][:, h:h + 1] * o_refs[g][rows(c), hs]
            heads.append(merged)
        return jnp.concatenate(heads, axis=1)

    def finish(c, m):
        out = x_ref[rows(c), :] + _rms(m, g_ref[...])
        out_ref[rows(c), :] = out
        return out

    _overlapped_row_chunks(x_ref.shape[0] // row_chunk, merged_rows, w_ref, finish)


def _dsa_out(os_, lses, w, gain, x, layer, *, tm, row_chunk):
    s, k = os_[0].shape
    d = w.shape[2]
    o_spec = pl.BlockSpec((tm, k), lambda i: (i, 0))
    l_spec = pl.BlockSpec((tm, LANES), lambda i: (i, 0))
    return pl.pallas_call(
        functools.partial(_dsa_out_kernel, row_chunk=row_chunk),
        grid=(s // tm,),
        in_specs=[o_spec, o_spec, o_spec, l_spec, l_spec, l_spec,
                  pl.BlockSpec((None, k, d), lambda i: (layer, 0, 0)),
                  pl.BlockSpec((1, d), lambda i: (0, 0)),
                  pl.BlockSpec((tm, d), lambda i: (i, 0))],
        out_specs=pl.BlockSpec((tm, d), lambda i: (i, 0)),
        out_shape=jax.ShapeDtypeStruct((s, d), F32),
        compiler_params=_compiler_params(("parallel",)),
        name="dsa_merge_out",
    )(*os_, *lses, w, gain.reshape(1, d), x)


def _tiles(s):
    tm_proj = min(1024, s)
    eighth = tm_proj // 8
    return dict(tm_proj=tm_proj, tn_proj=2048, tn_gdn=1024, tm_out=min(512, s), out_chunk=128, tm_ffn=min(1024, s), tf_ffn=512, ffn_chunk=min(256, s), dsa_tiles=8, gdn_rows=min(4 * GDN_CHUNK, s),
                perm_rows=32, gdn_proj_chunks=(2 * eighth, 2 * eighth, 2 * eighth, eighth, eighth))


def kernel(x, norm_gains, ffn_w_gate_up, ffn_w_down, gdn_w_in, gdn_conv_w, gdn_a_log, gdn_dt_bias,
           gdn_out_norm, gdn_w_out, kv_norm, kv_w, dsa_w_q, dsa_w_out, rel_bias):
    b, s, d = x.shape
    assert b == 1 and d == D_MODEL and s % (DSA_SPAN * DSA_GROUPS[-1][1]) == 0
    t = _tiles(s)
    depth = norm_gains.shape[0]
    n_a = gdn_w_in.shape[0]
    xs = x.reshape(s, d)
    kv = None
    ffn_w_gate_up, ffn_w_down = ffn_w_gate_up.astype(BF16), ffn_w_down.astype(BF16)
    gdn_w_in, gdn_w_out = gdn_w_in.astype(BF16), gdn_w_out.astype(BF16)
    dsa_w_q, dsa_w_out = dsa_w_q.astype(BF16), dsa_w_out.astype(BF16)
    for layer in range(depth):
        gains = norm_gains[layer]
        if layer < n_a:
            proj, gb = _gdn_in_proj(xs, gains[0], gdn_w_in, gdn_conv_w, layer, gdn_a_log[layer], gdn_dt_bias[layer],
                                    tm=t["tm_proj"], tn=t["tn_gdn"], row_chunks=t["gdn_proj_chunks"])
            o = _gdn(proj, gb, gdn_out_norm[layer], rows=t["gdn_rows"])
            xs = _matmul_norm_res(o, gdn_w_out, gains[1], xs, layer, tm=t["tm_out"], row_chunk=t["out_chunk"])
        else:
            j = layer - n_a
            q_all = _norm_matmul(xs, gains[0], dsa_w_q, j, tm=t["tm_proj"], tn=t["tn_proj"], out_dtype=BF16,
                                 out_scale=DSA_DH ** -0.5 * LOG2_E)
            outs = [_dsa_group(q_all, kv, rel_bias, g, tiles_per_step=t["dsa_tiles"]) for g in range(N_GROUPS)]
            xs = _dsa_out([o for o, _ in outs], [l for _, l in outs], dsa_w_out, gains[1], xs, j,
                          tm=t["tm_out"], row_chunk=t["out_chunk"])
        xs = _ffn(xs, gains[2], ffn_w_gate_up, ffn_w_down, gains[3], layer, tm=t["tm_ffn"], tf=t["tf_ffn"],
                  row_chunk=t["ffn_chunk"])
        if layer == n_a - 1 and depth > n_a:
            xs = _residue_major(xs, rows=t["perm_rows"], inverse=False)
            kv = _norm_matmul(xs, kv_norm, kv_w.astype(BF16)[None], 0, tm=t["tm_proj"], tn=t["tn_proj"],
                              out_dtype=BF16)
    if depth > n_a:
        xs = _residue_major(xs, rows=t["perm_rows"], inverse=True)
    return xs.reshape(b, s, d)
```
